```python
import jax
import jax.numpy as jnp
from jax import lax
import numpy as np

D_MODEL = 1024
BATCH = 4
SEQ = 8192
DEPTH = 2

HEAD_DIM = 64
BLOCK = 128
SWA_Q_HEADS = 8
SWA_KV_HEADS = 2
SWA_GROUP = SWA_Q_HEADS // SWA_KV_HEADS
WINDOW = 128
SWA_WIDTH = SWA_Q_HEADS * HEAD_DIM
KV_WIDTH = SWA_KV_HEADS * HEAD_DIM
LRU_WIDTH = D_MODEL // 4
LRU_BLOCKS = 8
LRU_BLOCK_W = LRU_WIDTH // LRU_BLOCKS
CONV_WIDTH = 4
LRU_C = 8.0
FOX_HEADS = 4
FOX_WIDTH = FOX_HEADS * HEAD_DIM
D_MIX = SWA_WIDTH + LRU_WIDTH + FOX_WIDTH
OFF_KA = SWA_WIDTH
OFF_VA = OFF_KA + KV_WIDTH
OFF_XB = OFF_VA + KV_WIDTH
OFF_GB = OFF_XB + LRU_WIDTH
OFF_QC = OFF_GB + LRU_WIDTH
OFF_KC = OFF_QC + FOX_WIDTH
OFF_VC = OFF_KC + FOX_WIDTH
OFF_FC = OFF_VC + FOX_WIDTH
IN_COLS = OFF_FC + FOX_HEADS
IN_SPLITS = (OFF_KA, OFF_VA, OFF_XB, OFF_GB, OFF_QC, OFF_KC, OFF_VC, OFF_FC)
N_GROUPS = 4
EXPERTS_PER_GROUP = 4
N_EXPERTS = N_GROUPS * EXPERTS_PER_GROUP
TOP_K = 2
D_EXPERT = 256
EPS = 1e-6
F32 = jnp.float32

kernel_name = 'hymba_style_hybrid_swa_rglru_fox_hmoe_adaln'


def rms_normalize(x):
    xf = x.astype(F32)
    return xf * lax.rsqrt(jnp.mean(xf * xf, axis=-1, keepdims=True) + EPS)


def rms_norm(x, gain):
    return (rms_normalize(x) * gain.astype(F32)).astype(x.dtype)


def modulate(h, shift, scale):
    return h * (1.0 + scale[:, None, :]) + shift[:, None, :]


def alibi_slopes(n_heads):
    return jnp.exp2(-8.0 * jnp.arange(1, n_heads + 1, dtype=F32) / n_heads)


def sliding_window_sink_attention(q, k, v, sinks):
    b, s, _, dh = q.shape
    nb = s // BLOCK
    qb = q.reshape(b, nb, BLOCK, SWA_KV_HEADS, SWA_GROUP, dh)

    def band(t):
        tb = t.reshape(b, nb, BLOCK, SWA_KV_HEADS, dh)
        prev = jnp.pad(tb[:, :-1], ((0, 0), (1, 0), (0, 0), (0, 0), (0, 0)))
        return jnp.concatenate([prev, tb], axis=2)

    kb, vb = band(k), band(v)
    scores = jnp.einsum('bnqhgd,bnkhd->bnhgqk', qb, kb).astype(F32) * (dh ** -0.5)
    q_pos = BLOCK + jnp.arange(BLOCK)
    k_pos = jnp.arange(2 * BLOCK)
    dist = q_pos[:, None] - k_pos[None, :]
    in_window = (dist >= 0) & (dist < WINDOW)
    block_exists = (jnp.arange(nb)[:, None, None] > 0) | (k_pos[None, None, :] >= BLOCK)
    valid = in_window[None] & block_exists
    slopes = alibi_slopes(SWA_Q_HEADS).reshape(SWA_KV_HEADS, SWA_GROUP)
    scores = scores - slopes[:, :, None, None] * dist.astype(F32)
    scores = jnp.where(valid[None, :, None, None], scores, -jnp.inf)
    sink = jnp.broadcast_to(sinks.astype(F32).reshape(SWA_KV_HEADS, SWA_GROUP, 1, 1),
                            scores.shape[:-1] + (1,))
    probs = jax.nn.softmax(jnp.concatenate([scores, sink], axis=-1), axis=-1)[..., :-1]
    out = jnp.einsum('bnhgqk,bnkhd->bnqhgd', probs.astype(v.dtype), vb)
    return out.reshape(b, s, SWA_Q_HEADS * dh)


def causal_depthwise_conv(x, w, bias):
    s = x.shape[1]
    xp = jnp.pad(x, ((0, 0), (CONV_WIDTH - 1, 0), (0, 0)))
    y = bias
    for k in range(CONV_WIDTH):
        y = y + xp[:, k:k + s] * w[k]
    return y


def rg_lru(x, w_a, b_a, w_x, b_x, lam):
    b, s, w = x.shape
    xb = x.reshape(b, s, LRU_BLOCKS, LRU_BLOCK_W)
    gate_a = jnp.einsum('bsnc,ncd->bsnd', xb, w_a).reshape(b, s, w) + b_a
    gate_x = jnp.einsum('bsnc,ncd->bsnd', xb, w_x).reshape(b, s, w) + b_x
    r = jax.nn.sigmoid(gate_a.astype(F32))
    i = jax.nn.sigmoid(gate_x.astype(F32))
    log_a = -LRU_C * r * jax.nn.softplus(-lam.astype(F32))
    a = jnp.exp(log_a)
    u = jnp.sqrt(-jnp.expm1(2.0 * log_a)) * (i * x.astype(F32))

    def combine(left, right):
        a_l, u_l = left
        a_r, u_r = right
        return a_l * a_r, a_r * u_l + u_r

    _, hs = lax.associative_scan(combine, (a, u), axis=1)
    return hs.astype(x.dtype)


def forgetting_attention(q, k, v, log_f):
    b, s, h, dh = q.shape
    nb = s // BLOCK
    cum = jnp.cumsum(log_f, axis=1).transpose(0, 2, 1)
    q_blocks = q.reshape(b, nb, BLOCK, h, dh).transpose(1, 0, 2, 3, 4)
    cum_q = cum.reshape(b, h, nb, BLOCK).transpose(2, 0, 1, 3)
    k_pos = jnp.arange(s)

    def attend_block(args):
        n, q_n, cq_n = args
        logits = jnp.einsum('bqhd,bkhd->bhqk', q_n, k).astype(F32) * (dh ** -0.5)
        logits = logits + (cq_n[..., :, None] - cum[:, :, None, :])
        q_pos = n * BLOCK + jnp.arange(BLOCK)
        causal = k_pos[None, :] <= q_pos[:, None]
        logits = jnp.where(causal, logits, -jnp.inf)
        probs = jax.nn.softmax(logits, axis=-1)
        return jnp.einsum('bhqk,bkhd->bqhd', probs.astype(v.dtype), v)

    out = lax.map(attend_block, (jnp.arange(nb), q_blocks, cum_q))
    return out.transpose(1, 0, 2, 3, 4).reshape(b, s, h * dh)


def hybrid_mixer(h, w_in, w_out, out_gain, sinks, conv_w, conv_b,
                 lru_wa, lru_ba, lru_wx, lru_bx, lru_lam, fox_bf):
    b, s, _ = h.shape
    proj = jnp.einsum('bsd,de->bse', h, w_in)
    q_a, k_a, v_a, x_b, g_b, q_c, k_c, v_c, f_c = jnp.split(proj, IN_SPLITS, axis=-1)
    y_a = sliding_window_sink_attention(
        q_a.reshape(b, s, SWA_Q_HEADS, HEAD_DIM),
        k_a.reshape(b, s, SWA_KV_HEADS, HEAD_DIM),
        v_a.reshape(b, s, SWA_KV_HEADS, HEAD_DIM), sinks)
    x_conv = causal_depthwise_conv(x_b, conv_w, conv_b)
    y_b = rg_lru(x_conv, lru_wa, lru_ba, lru_wx, lru_bx, lru_lam) * jax.nn.gelu(g_b)
    log_f = jax.nn.log_sigmoid(f_c.astype(F32) + fox_bf.astype(F32))
    y_c = forgetting_attention(
        q_c.reshape(b, s, FOX_HEADS, HEAD_DIM),
        k_c.reshape(b, s, FOX_HEADS, HEAD_DIM),
        v_c.reshape(b, s, FOX_HEADS, HEAD_DIM), log_f)
    y = jnp.concatenate([rms_normalize(y_a), rms_normalize(y_b), rms_normalize(y_c)], axis=-1)
    y = (y * out_gain.astype(F32)).astype(h.dtype)
    return jnp.einsum('bse,ed->bsd', y, w_out)


def hierarchical_moe(h, w_rg, b_rg, w_re, b_re, w_gate, w_up, w_down):
    b, s, _ = h.shape
    group_logits = (jnp.einsum('bsd,dg->bsg', h, w_rg) + b_rg).astype(F32)
    group_prob = jax.nn.softmax(group_logits, axis=-1)
    group_p, group_idx = lax.top_k(group_prob, 1)
    expert_logits = (jnp.einsum('bsd,de->bse', h, w_re) + b_re).astype(F32)
    expert_logits = expert_logits.reshape(b, s, N_GROUPS, EXPERTS_PER_GROUP)
    in_group = jnp.take_along_axis(expert_logits, group_idx[..., None], axis=2)[:, :, 0]
    expert_prob = jax.nn.softmax(in_group, axis=-1)
    top_p, top_local = lax.top_k(expert_prob, TOP_K)
    weights = group_p * top_p / jnp.sum(top_p, axis=-1, keepdims=True)
    expert_idx = group_idx * EXPERTS_PER_GROUP + top_local
    combine = jnp.sum(jax.nn.one_hot(expert_idx, N_EXPERTS, dtype=F32) * weights[..., None], axis=-2)
    combine = combine.astype(h.dtype)
    y = jnp.zeros_like(h)
    for e in range(N_EXPERTS):
        hidden = jax.nn.silu(h @ w_gate[e]) * (h @ w_up[e])
        y = y + combine[..., e:e + 1] * (hidden @ w_down[e])
    return y


def setup_inputs(seed: int = 0) -> dict:
    key = jax.random.key(seed)
    ks = jax.random.split(key, 26)
    L, D = DEPTH, D_MODEL

    def nrm(k, shape, scale):
        return scale * jax.random.normal(k, shape, F32)

    gate_offset = jnp.concatenate([jnp.zeros((2 * D,), F32), jnp.ones((D,), F32),
                                   jnp.zeros((2 * D,), F32), jnp.ones((D,), F32)])
    u = jax.random.uniform(ks[16], (L, LRU_WIDTH), F32, 0.9, 0.999)
    root = u ** (1.0 / LRU_C)
    lru_lam = jnp.log(root) - jnp.log1p(-root)
    return {
        'x': nrm(ks[0], (BATCH, SEQ, D), 1.0),
        'c': nrm(ks[1], (BATCH, D), 1.0),
        'w_mod': nrm(ks[2], (L, D, 6 * D), 0.25 * D ** -0.5),
        'b_mod': nrm(ks[3], (L, 6 * D), 0.1) + gate_offset,
        'norm_mix': 1.0 + nrm(ks[4], (L, D), 0.1),
        'norm_ffn': 1.0 + nrm(ks[5], (L, D), 0.1),
        'w_in': nrm(ks[6], (L, D, IN_COLS), D ** -0.5),
        'w_out': nrm(ks[7], (L, D_MIX, D), D_MIX ** -0.5),
        'out_gain': 1.0 + nrm(ks[8], (L, D_MIX), 0.1),
        'sinks': nrm(ks[9], (L, SWA_Q_HEADS), 0.5),
        'conv_w': nrm(ks[10], (L, CONV_WIDTH, LRU_WIDTH), CONV_WIDTH ** -0.5),
        'conv_b': nrm(ks[11], (L, LRU_WIDTH), 0.02),
        'lru_wa': nrm(ks[12], (L, LRU_BLOCKS, LRU_BLOCK_W, LRU_BLOCK_W), LRU_BLOCK_W ** -0.5),
        'lru_ba': nrm(ks[13], (L, LRU_WIDTH), 0.02),
        'lru_wx': nrm(ks[14], (L, LRU_BLOCKS, LRU_BLOCK_W, LRU_BLOCK_W), LRU_BLOCK_W ** -0.5),
        'lru_bx': nrm(ks[15], (L, LRU_WIDTH), 0.02),
        'lru_lam': lru_lam,
        'fox_bf': jax.random.uniform(ks[17], (L, FOX_HEADS), F32, 1.0, 5.0),
        'w_router_group': nrm(ks[18], (L, D, N_GROUPS), D ** -0.5),
        'b_router_group': nrm(ks[19], (L, N_GROUPS), 0.01),
        'w_router_expert': nrm(ks[20], (L, D, N_EXPERTS), D ** -0.5),
        'b_router_expert': nrm(ks[21], (L, N_EXPERTS), 0.01),
        'w_gate': nrm(ks[22], (L, N_EXPERTS, D, D_EXPERT), D ** -0.5),
        'w_up': nrm(ks[23], (L, N_EXPERTS, D, D_EXPERT), D ** -0.5),
        'w_down': nrm(ks[24], (L, N_EXPERTS, D_EXPERT, D), D_EXPERT ** -0.5),
        'norm_final': 1.0 + nrm(ks[25], (D,), 0.1),
    }


def reference(x, c, w_mod, b_mod, norm_mix, norm_ffn, w_in, w_out, out_gain, sinks,
              conv_w, conv_b, lru_wa, lru_ba, lru_wx, lru_bx, lru_lam, fox_bf,
              w_router_group, b_router_group, w_router_expert, b_router_expert,
              w_gate, w_up, w_down, norm_final):
    c_act = jax.nn.silu(c)
    for l in range(DEPTH):
        mod = c_act @ w_mod[l] + b_mod[l]
        shift_m, scale_m, gate_m, shift_f, scale_f, gate_f = jnp.split(mod, 6, axis=-1)
        h = modulate(rms_norm(x, norm_mix[l]), shift_m, scale_m)
        x = x + gate_m[:, None, :] * hybrid_mixer(
            h, w_in[l], w_out[l], out_gain[l], sinks[l], conv_w[l], conv_b[l],
            lru_wa[l], lru_ba[l], lru_wx[l], lru_bx[l], lru_lam[l], fox_bf[l])
        h = modulate(rms_norm(x, norm_ffn[l]), shift_f, scale_f)
        x = x + gate_f[:, None, :] * hierarchical_moe(
            h, w_router_group[l], b_router_group[l], w_router_expert[l], b_router_expert[l],
            w_gate[l], w_up[l], w_down[l])
    return rms_norm(x, norm_final)
```

```python
import functools

import jax
import jax.numpy as jnp
from jax import lax
from jax.experimental import pallas as pl
from jax.experimental.pallas import tpu as pltpu

F32 = jnp.float32
BF16 = jnp.bfloat16

D_MODEL = 1024
HEAD_DIM = 64
PAIR = 2 * HEAD_DIM
BLOCK = 128
SWA_Q_HEADS = 8
SWA_WIDTH = SWA_Q_HEADS * HEAD_DIM
KV_WIDTH = 2 * HEAD_DIM
LRU_WIDTH = 256
LRU_BLOCKS = 8
CONV_WIDTH = 4
LRU_C = 8.0
FOX_HEADS = 4
FOX_WIDTH = FOX_HEADS * HEAD_DIM
N_GROUPS = 4
EXPERTS_PER_GROUP = 4
N_EXPERTS = 16
D_EXPERT = 256
EPS = 1e-6
NEG = -1e30

OFF_KA = SWA_WIDTH
OFF_VA = OFF_KA + KV_WIDTH
OFF_XB = OFF_VA + KV_WIDTH
OFF_GB = OFF_XB + LRU_WIDTH
OFF_QC = OFF_GB + LRU_WIDTH
OFF_KC = OFF_QC + FOX_WIDTH
OFF_VC = OFF_KC + FOX_WIDTH
OFF_FC = OFF_VC + FOX_WIDTH

ROUTER_ROWS = 32
FORGET_ROWS = 16
VMEM_LIMIT_BYTES = 48 * 1024 * 1024

TM_PRE = 512
TQ_SWA = 512
TS_LRU = 512
T_FOX = 512
TM_POST = 512
TM_MOE = 1024
CUMSUM_CHUNK = 256


def _params(*semantics):
    return pltpu.CompilerParams(dimension_semantics=semantics, vmem_limit_bytes=VMEM_LIMIT_BYTES)


def _split_bf16(v):
    hi = v.astype(BF16)
    lo = (v - hi.astype(F32)).astype(BF16)
    return hi, lo


def _dot(a, b):
    return jnp.dot(a, b, preferred_element_type=F32)


def _dot_nt(a, b):
    return lax.dot_general(a, b, (((1,), (1,)), ((), ())), preferred_element_type=F32)


def _rms_normalize(v):
    return v * lax.rsqrt(jnp.mean(v * v, axis=-1, keepdims=True) + EPS)


def _mod_kernel(c_ref, w_ref, b_ref, o_ref):
    c = c_ref[...]
    ca = c * jax.nn.sigmoid(c)
    w = w_ref[0]
    a_hi, a_lo = _split_bf16(ca)
    w_hi, w_lo = _split_bf16(w)
    o_ref[0] = _dot(a_hi, w_hi) + _dot(a_lo, w_hi) + _dot(a_hi, w_lo) + b_ref[0]


def _modulation(c, w_mod, b_mod):
    depth, d, d6 = w_mod.shape
    b = c.shape[0]
    n = d6 // d
    return pl.pallas_call(
        _mod_kernel,
        grid=(depth, n),
        in_specs=[
            pl.BlockSpec((b, d), lambda l, j: (0, 0)),
            pl.BlockSpec((1, d, d), lambda l, j: (l, 0, j)),
            pl.BlockSpec((1, 1, d), lambda l, j: (l, 0, j)),
        ],
        out_specs=pl.BlockSpec((1, b, d), lambda l, j: (l, 0, j)),
        out_shape=jax.ShapeDtypeStruct((depth, b, d6), F32),
        compiler_params=_params("arbitrary", "arbitrary"),
        name="modulation",
    )(c, w_mod, b_mod.reshape(depth, 1, d6))


def _log_sigmoid(z):
    return jnp.minimum(z, 0.0) - jnp.log1p(jnp.exp(-jnp.abs(z)))


def _pre_kernel(x_ref, mod_ref, g_ref, w_ref, wf_ref, fb_ref,
                qa_ref, ka_ref, va_ref, xb_ref, gb_ref, qc_ref, kc_ref, vc_ref, lf_ref):
    x = x_ref[0]
    h = (_rms_normalize(x) * g_ref[...]) * (1.0 + mod_ref[0, 1:2, :]) + mod_ref[0, 0:1, :]
    hb = h.astype(BF16)

    def proj(lo, width):
        return _dot(hb, w_ref[:, lo:lo + width])

    scale = HEAD_DIM ** -0.5
    qa_ref[0] = (proj(0, SWA_WIDTH) * scale).astype(BF16)
    ka_ref[0] = proj(OFF_KA, KV_WIDTH).astype(BF16)
    va_ref[0] = proj(OFF_VA, KV_WIDTH).astype(BF16)
    xb_ref[0] = proj(OFF_XB, LRU_WIDTH).astype(BF16)
    gb_ref[0] = proj(OFF_GB, LRU_WIDTH).astype(BF16)
    qc_ref[0] = (proj(OFF_QC, FOX_WIDTH) * scale).astype(BF16)
    kc_ref[0] = proj(OFF_KC, FOX_WIDTH).astype(BF16)
    vc_ref[0] = proj(OFF_VC, FOX_WIDTH).astype(BF16)
    f_t = _dot_nt(wf_ref[...], hb)
    lf_ref[0] = _log_sigmoid(f_t + fb_ref[...])


def _pre_mixer(x, mod, gain, w_in, wf_t, fb):
    b, s, d = x.shape
    tm = TM_PRE
    ncols = w_in.shape[1]

    def tok(width):
        return pl.BlockSpec((1, tm, width), lambda bi, i: (bi, i, 0))

    def tok_shape(width):
        return jax.ShapeDtypeStruct((b, s, width), BF16)

    return pl.pallas_call(
        _pre_kernel,
        grid=(b, s // tm),
        in_specs=[
            tok(d),
            pl.BlockSpec((1, 6, d), lambda bi, i: (bi, 0, 0)),
            pl.BlockSpec((1, d), lambda bi, i: (0, 0)),
            pl.BlockSpec((d, ncols), lambda bi, i: (0, 0)),
            pl.BlockSpec((FORGET_ROWS, d), lambda bi, i: (0, 0)),
            pl.BlockSpec((FORGET_ROWS, 1), lambda bi, i: (0, 0)),
        ],
        out_specs=[
            tok(SWA_WIDTH), tok(KV_WIDTH), tok(KV_WIDTH), tok(LRU_WIDTH), tok(LRU_WIDTH),
            tok(FOX_WIDTH), tok(FOX_WIDTH), tok(FOX_WIDTH),
            pl.BlockSpec((1, FORGET_ROWS, tm), lambda bi, i: (bi, 0, i)),
        ],
        out_shape=[
            tok_shape(SWA_WIDTH), tok_shape(KV_WIDTH), tok_shape(KV_WIDTH), tok_shape(LRU_WIDTH),
            tok_shape(LRU_WIDTH), tok_shape(FOX_WIDTH), tok_shape(FOX_WIDTH), tok_shape(FOX_WIDTH),
            jax.ShapeDtypeStruct((b, FORGET_ROWS, s), F32),
        ],
        compiler_params=_params("parallel", "parallel"),
        name="pre_mixer",
    )(x, mod, gain, w_in, wf_t, fb)


def _cumsum_kernel(lf_ref, o_ref):
    n = CUMSUM_CHUNK
    s = lf_ref.shape[2]
    r = lax.broadcasted_iota(jnp.int32, (n, n), 0)
    c = lax.broadcasted_iota(jnp.int32, (n, n), 1)
    tri = jnp.where(r <= c, 1.0, 0.0).astype(BF16)
    carry = jnp.zeros((lf_ref.shape[1], 1), F32)
    for j in range(s // n):
        seg = lf_ref[0, :, j * n:(j + 1) * n]
        hi = seg.astype(BF16)
        rem = seg - hi.astype(F32)
        mid = rem.astype(BF16)
        lo = (rem - mid.astype(F32)).astype(BF16)
        cs = (_dot(hi, tri) + _dot(mid, tri)) + _dot(lo, tri) + carry
        o_ref[0, :, j * n:(j + 1) * n] = cs
        carry = cs[:, n - 1:n]


def _cumsum(lf):
    b, r, s = lf.shape
    return pl.pallas_call(
        _cumsum_kernel,
        grid=(b,),
        in_specs=[pl.BlockSpec((1, r, s), lambda bi: (bi, 0, 0))],
        out_specs=pl.BlockSpec((1, r, s), lambda bi: (bi, 0, 0)),
        out_shape=jax.ShapeDtypeStruct((b, r, s), F32),
        compiler_params=_params("parallel"),
        name="cumsum",
    )(lf)


def _swa_kernel(sink_ref, q_ref, kc_ref, vc_ref, kp_ref, vp_ref, o_ref, kf_ref, vf_ref, bias_ref):
    i = pl.program_id(1)
    nsub = q_ref.shape[1] // BLOCK
    npair = SWA_Q_HEADS // 2

    @pl.when((pl.program_id(0) == 0) & (i == 0))
    def _():
        qpos = BLOCK + lax.broadcasted_iota(jnp.int32, (BLOCK, 2 * BLOCK), 0)
        kpos = lax.broadcasted_iota(jnp.int32, (BLOCK, 2 * BLOCK), 1)
        dist = qpos - kpos
        valid = (dist >= 0) & (dist < BLOCK)
        distf = dist.astype(F32)
        for hd in range(SWA_Q_HEADS):
            slope = 2.0 ** (-8.0 * (hd + 1) / SWA_Q_HEADS)
            bias_ref[hd] = jnp.where(valid, -slope * distf, NEG)

    kf_ref[0:BLOCK, :] = kp_ref[0]
    kf_ref[BLOCK:, :] = kc_ref[0]
    vf_ref[0:BLOCK, :] = vp_ref[0]
    vf_ref[BLOCK:, :] = vc_ref[0]

    lane = lax.broadcasted_iota(jnp.int32, (2 * BLOCK, PAIR), 1)
    lane_o = lax.broadcasted_iota(jnp.int32, (BLOCK, PAIR), 1)
    col = lax.broadcasted_iota(jnp.int32, (BLOCK, 2 * BLOCK), 1)

    for j in range(nsub):
        k2 = kf_ref[j * BLOCK:(j + 2) * BLOCK, :]
        v2 = vf_ref[j * BLOCK:(j + 2) * BLOCK, :]
        k_half = (jnp.where(lane < HEAD_DIM, k2, jnp.zeros_like(k2)),
                  jnp.where(lane >= HEAD_DIM, k2, jnp.zeros_like(k2)))
        for p in range(npair):
            qp = q_ref[0, j * BLOCK:(j + 1) * BLOCK, p * PAIR:(p + 1) * PAIR]
            outs = []
            for a in range(2):
                hd = p + a * npair
                sc = _dot_nt(qp, k_half[a]) + bias_ref[hd]
                if j == 0:
                    sc = sc + jnp.where(col < BLOCK, jnp.where(i == 0, NEG, 0.0), 0.0)
                sink = sink_ref[0, hd]
                m = jnp.maximum(jnp.max(sc, axis=-1, keepdims=True), sink)
                pr = jnp.exp(sc - m)
                den = jnp.sum(pr, axis=-1, keepdims=True) + jnp.exp(sink - m)
                outs.append(_dot(pr.astype(BF16), v2) / den)
            o_ref[0, j * BLOCK:(j + 1) * BLOCK, p * PAIR:(p + 1) * PAIR] = jnp.where(
                lane_o < HEAD_DIM, outs[0], outs[1]).astype(BF16)


def _swa(qa, ka, va, sinks):
    b, s, _ = qa.shape
    tq = TQ_SWA
    nsub = tq // BLOCK
    prev = lambda bi, i: (bi, jnp.maximum(i * nsub - 1, 0), 0)
    return pl.pallas_call(
        _swa_kernel,
        grid=(b, s // tq),
        in_specs=[
            pl.BlockSpec(memory_space=pltpu.SMEM),
            pl.BlockSpec((1, tq, SWA_WIDTH), lambda bi, i: (bi, i, 0)),
            pl.BlockSpec((1, tq, KV_WIDTH), lambda bi, i: (bi, i, 0)),
            pl.BlockSpec((1, tq, KV_WIDTH), lambda bi, i: (bi, i, 0)),
            pl.BlockSpec((1, BLOCK, KV_WIDTH), prev),
            pl.BlockSpec((1, BLOCK, KV_WIDTH), prev),
        ],
        out_specs=pl.BlockSpec((1, tq, SWA_WIDTH), lambda bi, i: (bi, i, 0)),
        out_shape=jax.ShapeDtypeStruct((b, s, SWA_WIDTH), BF16),
        scratch_shapes=[
            pltpu.VMEM((tq + BLOCK, KV_WIDTH), BF16),
            pltpu.VMEM((tq + BLOCK, KV_WIDTH), BF16),
            pltpu.VMEM((SWA_Q_HEADS, BLOCK, 2 * BLOCK), F32),
        ],
        compiler_params=_params("arbitrary", "arbitrary"),
        name="swa",
    )(sinks, qa, ka, va, ka, va)


def _gelu_tanh(v):
    return 0.5 * v * (1.0 + jnp.tanh(0.7978845608028654 * (v + 0.044715 * (v * v * v))))


def _lru_kernel(xb_ref, gb_ref, cw_ref, cb_ref, wa_ref, ba_ref, wx_ref, bx_ref, lam_ref,
                o_ref, tail_ref, h_ref):
    ts = xb_ref.shape[1]

    @pl.when(pl.program_id(1) == 0)
    def _():
        tail_ref[...] = jnp.zeros_like(tail_ref)
        h_ref[...] = jnp.zeros_like(h_ref)

    x = xb_ref[0].astype(F32)
    tail = tail_ref[...]
    row8 = lax.broadcasted_iota(jnp.int32, (8, LRU_WIDTH), 0)
    conv = cb_ref[...] + cw_ref[CONV_WIDTH - 1:CONV_WIDTH, :] * x
    for k in range(1, CONV_WIDTH):
        rolled = pltpu.roll(x, k, 0)
        head = jnp.where(row8 < k, pltpu.roll(tail, k, 0), rolled[0:8])
        shifted = jnp.concatenate([head, rolled[8:]], axis=0)
        conv = conv + cw_ref[CONV_WIDTH - 1 - k:CONV_WIDTH - k, :] * shifted
    tail_ref[...] = x[ts - 8:ts]

    cbf = conv.astype(BF16)
    r = jax.nn.sigmoid(_dot(cbf, wa_ref[...]) + ba_ref[...])
    gi = jax.nn.sigmoid(_dot(cbf, wx_ref[...]) + bx_ref[...])
    z = -lam_ref[...]
    softplus = jnp.maximum(z, 0.0) + jnp.log1p(jnp.exp(-jnp.abs(z)))
    log_a = (-LRU_C * softplus) * r
    a = jnp.exp(log_a)
    u = jnp.sqrt(1.0 - a * a) * (gi * conv)

    row = lax.broadcasted_iota(jnp.int32, (ts, LRU_WIDTH), 0)
    step = 1
    while step < ts:
        keep = row >= step
        u = jnp.where(keep, a * pltpu.roll(u, step, 0) + u, u)
        a = jnp.where(keep, a * pltpu.roll(a, step, 0), a)
        step *= 2
    h = u + a * h_ref[7:8, :]
    h_ref[...] = h[ts - 8:ts]
    o_ref[0] = (h * _gelu_tanh(gb_ref[0].astype(F32))).astype(BF16)


def _lru(xb, gb, conv_w, conv_b, wa, ba, wx, bx, lam):
    b, s, w = xb.shape
    ts = TS_LRU
    tok = pl.BlockSpec((1, ts, w), lambda bi, i: (bi, i, 0))
    full = lambda shape: pl.BlockSpec(shape, lambda bi, i: (0,) * len(shape))
    return pl.pallas_call(
        _lru_kernel,
        grid=(b, s // ts),
        in_specs=[tok, tok, full((CONV_WIDTH, w)), full((1, w)), full((w, w)), full((1, w)),
                  full((w, w)), full((1, w)), full((1, w))],
        out_specs=tok,
        out_shape=jax.ShapeDtypeStruct((b, s, w), BF16),
        scratch_shapes=[pltpu.VMEM((8, w), F32), pltpu.VMEM((8, w), F32)],
        compiler_params=_params("arbitrary", "arbitrary"),
        name="lru",
    )(xb, gb, conv_w, conv_b, wa, ba, wx, bx, lam)


def _fox_kernel(q_ref, k_ref, v_ref, cum_ref, o_ref, m_ref, l_ref, acc_ref):
    p = pl.program_id(1)
    qi = pl.program_id(2)
    t = q_ref.shape[1]
    q = q_ref[0]
    lane = lax.broadcasted_iota(jnp.int32, (t, PAIR), 1)
    q_half = (jnp.where(lane < HEAD_DIM, q, jnp.zeros_like(q)),
              jnp.where(lane >= HEAD_DIM, q, jnp.zeros_like(q)))
    causal = (lax.broadcasted_iota(jnp.int32, (t, t), 1) <= lax.broadcasted_iota(jnp.int32, (t, t), 0))
    outs = []
    for a in range(2):
        hd = 2 * p + a
        qm = q_half[a]
        base = cum_ref[0, hd, pl.ds(qi, 1), :][:, 0:1]
        m_ref[...] = jnp.full_like(m_ref, NEG)
        l_ref[...] = jnp.zeros_like(l_ref)
        acc_ref[...] = jnp.zeros_like(acc_ref)

        def step(ki, masked):
            start = pl.multiple_of(ki * t, t)
            k = k_ref[0, pl.ds(start, t), :]
            v = v_ref[0, pl.ds(start, t), :]
            sc = _dot_nt(qm, k) + (base - cum_ref[0, hd, pl.ds(ki, 1), :])
            if masked:
                sc = jnp.where(causal, sc, NEG)
            m_old = m_ref[...]
            m_new = jnp.maximum(m_old, jnp.max(sc, axis=-1, keepdims=True))
            alpha = jnp.exp(m_old - m_new)
            pr = jnp.exp(sc - m_new)
            l_ref[...] = alpha * l_ref[...] + jnp.sum(pr, axis=-1, keepdims=True)
            acc_ref[...] = alpha * acc_ref[...] + _dot(pr.astype(BF16), v)
            m_ref[...] = m_new

        def body(ki, carry):
            step(ki, False)
            return carry

        lax.fori_loop(0, qi, body, 0)
        step(qi, True)
        outs.append(acc_ref[...] / l_ref[...])
    o_ref[0] = jnp.where(lane < HEAD_DIM, outs[0], outs[1]).astype(BF16)


def _fox(qc, kc, vc, cum):
    b, s, w = qc.shape
    t = T_FOX
    nt = s // t
    npair = w // PAIR
    cum4 = cum.reshape(b, cum.shape[1], nt, t)
    return pl.pallas_call(
        _fox_kernel,
        grid=(b, npair, nt),
        in_specs=[
            pl.BlockSpec((1, t, PAIR), lambda bi, p, qi: (bi, qi, p)),
            pl.BlockSpec((1, s, PAIR), lambda bi, p, qi: (bi, 0, p)),
            pl.BlockSpec((1, s, PAIR), lambda bi, p, qi: (bi, 0, p)),
            pl.BlockSpec((1, cum.shape[1], nt, t), lambda bi, p, qi: (bi, 0, 0, 0)),
        ],
        out_specs=pl.BlockSpec((1, t, PAIR), lambda bi, p, qi: (bi, qi, p)),
        out_shape=jax.ShapeDtypeStruct((b, s, w), BF16),
        scratch_shapes=[pltpu.VMEM((t, 1), F32), pltpu.VMEM((t, 1), F32), pltpu.VMEM((t, PAIR), F32)],
        compiler_params=_params("parallel", "parallel", "arbitrary"),
        name="fox",
    )(qc, kc, vc, cum4)


def _first_max(vals):
    best = vals[0]
    for v in vals[1:]:
        best = jnp.maximum(best, v)
    idx = jnp.full(best.shape, len(vals) - 1, jnp.int32)
    for j in range(len(vals) - 2, -1, -1):
        idx = jnp.where(vals[j] == best, j, idx)
    return best, idx


def _softmax_rows(vals):
    top = vals[0]
    for v in vals[1:]:
        top = jnp.maximum(top, v)
    ex = [jnp.exp(v - top) for v in vals]
    tot = ex[0]
    for e in ex[1:]:
        tot = tot + e
    return [e / tot for e in ex]


def _post_kernel(ya_ref, yb_ref, yc_ref, x_ref, mod_ref, ga_ref, gb_ref, gc_ref, wo_ref, nf_ref,
                 wr_hi_ref, wr_lo_ref, br_ref, xo_ref, h2_ref, comb_ref):
    ya = _rms_normalize(ya_ref[0].astype(F32)) * ga_ref[...]
    yb = _rms_normalize(yb_ref[0].astype(F32)) * gb_ref[...]
    yc = _rms_normalize(yc_ref[0].astype(F32)) * gc_ref[...]
    y = jnp.concatenate([ya, yb, yc], axis=-1).astype(BF16)
    x = x_ref[0] + mod_ref[0, 2:3, :] * _dot(y, wo_ref[...])
    xo_ref[0] = x
    h = (_rms_normalize(x) * nf_ref[...]) * (1.0 + mod_ref[0, 4:5, :]) + mod_ref[0, 3:4, :]
    h_hi, h_lo = _split_bf16(h)
    h2_ref[0] = h_hi

    w_hi = wr_hi_ref[...]
    logits = (_dot_nt(w_hi, h_hi) + _dot_nt(w_hi, h_lo)) + _dot_nt(wr_lo_ref[...], h_hi) + br_ref[...]
    rows = [logits[r:r + 1, :] for r in range(8 + N_EXPERTS)]
    group_prob = _softmax_rows(rows[0:N_GROUPS])
    group_p, group_idx = _first_max(group_prob)
    in_group = []
    for j in range(EXPERTS_PER_GROUP):
        v = rows[8 + (N_GROUPS - 1) * EXPERTS_PER_GROUP + j]
        for g in range(N_GROUPS - 2, -1, -1):
            v = jnp.where(group_idx == g, rows[8 + g * EXPERTS_PER_GROUP + j], v)
        in_group.append(v)
    expert_prob = _softmax_rows(in_group)
    p1, i1 = _first_max(expert_prob)
    rest = [jnp.where(i1 == j, -1.0, expert_prob[j]) for j in range(EXPERTS_PER_GROUP)]
    p2, i2 = _first_max(rest)
    w1 = group_p * p1 / (p1 + p2)
    w2 = group_p * p2 / (p1 + p2)
    for j in range(EXPERTS_PER_GROUP):
        wj = jnp.where(i1 == j, w1, jnp.where(i2 == j, w2, 0.0))
        for g in range(N_GROUPS):
            e = g * EXPERTS_PER_GROUP + j
            comb_ref[0, e:e + 1, :] = jnp.where(group_idx == g, wj, 0.0)


def _post_mixer(ya, yb, yc, x, mod, gain_a, gain_b, gain_c, w_out, norm_ffn, wr_hi, wr_lo, br):
    b, s, d = x.shape
    tm = TM_POST
    tok = lambda width: pl.BlockSpec((1, tm, width), lambda bi, i: (bi, i, 0))
    full = lambda shape: pl.BlockSpec(shape, lambda bi, i: (0,) * len(shape))
    return pl.pallas_call(
        _post_kernel,
        grid=(b, s // tm),
        in_specs=[
            tok(SWA_WIDTH), tok(LRU_WIDTH), tok(FOX_WIDTH), tok(d),
            pl.BlockSpec((1, 6, d), lambda bi, i: (bi, 0, 0)),
            full((1, SWA_WIDTH)), full((1, LRU_WIDTH)), full((1, FOX_WIDTH)),
            full((d, d)), full((1, d)),
            full((ROUTER_ROWS, d)), full((ROUTER_ROWS, d)), full((ROUTER_ROWS, 1)),
        ],
        out_specs=[
            tok(d), tok(d),
            pl.BlockSpec((1, N_EXPERTS, tm), lambda bi, i: (bi, 0, i)),
        ],
        out_shape=[
            jax.ShapeDtypeStruct((b, s, d), F32),
            jax.ShapeDtypeStruct((b, s, d), BF16),
            jax.ShapeDtypeStruct((b, N_EXPERTS, s), F32),
        ],
        compiler_params=_params("parallel", "parallel"),
        name="post_mixer",
    )(ya, yb, yc, x, mod, gain_a, gain_b, gain_c, w_out, norm_ffn, wr_hi, wr_lo, br)


def _moe_kernel(h_ref, comb_ref, wg_ref, wu_ref, wd_ref, x_ref, mod_ref, nf_ref, o_ref, acc_ref,
                *, final):
    e = pl.program_id(2)

    @pl.when(e == 0)
    def _():
        acc_ref[...] = jnp.zeros_like(acc_ref)

    h = h_ref[0]
    gate = _dot(h, wg_ref[0])
    hidden = (gate * jax.nn.sigmoid(gate)) * _dot(h, wu_ref[0])
    comb = comb_ref[0]
    lane = lax.broadcasted_iota(jnp.int32, comb.shape, 1)
    ce = jnp.sum(jnp.where(lane == e, comb, 0.0), axis=-1, keepdims=True)
    acc_ref[...] += _dot((hidden * ce).astype(BF16), wd_ref[0])

    @pl.when(e == N_EXPERTS - 1)
    def _():
        x = x_ref[0] + mod_ref[0, 5:6, :] * acc_ref[...]
        if final:
            x = _rms_normalize(x) * nf_ref[...]
        o_ref[0] = x


def _moe(h2, comb, w_gate, w_up, w_down, x, mod, norm_final, final):
    b, s, d = x.shape
    tm = TM_MOE
    tok = lambda width: pl.BlockSpec((1, tm, width), lambda bi, i, e: (bi, i, 0))
    return pl.pallas_call(
        functools.partial(_moe_kernel, final=final),
        grid=(b, s // tm, N_EXPERTS),
        in_specs=[
            tok(d), tok(N_EXPERTS),
            pl.BlockSpec((1, d, D_EXPERT), lambda bi, i, e: (e, 0, 0)),
            pl.BlockSpec((1, d, D_EXPERT), lambda bi, i, e: (e, 0, 0)),
            pl.BlockSpec((1, D_EXPERT, d), lambda bi, i, e: (e, 0, 0)),
            tok(d),
            pl.BlockSpec((1, 6, d), lambda bi, i, e: (bi, 0, 0)),
            pl.BlockSpec((1, d), lambda bi, i, e: (0, 0)),
        ],
        out_specs=tok(d),
        out_shape=jax.ShapeDtypeStruct((b, s, d), F32),
        scratch_shapes=[pltpu.VMEM((tm, d), F32)],
        compiler_params=_params("parallel", "parallel", "arbitrary"),
        name="moe_final" if final else "moe",
    )(h2, comb, w_gate, w_up, w_down, x, mod, norm_final)


def _block_diag(w):
    nb, c, _ = w.shape
    eye = jnp.eye(nb, dtype=w.dtype)
    return (eye[:, None, :, None] * w[:, :, None, :]).reshape(nb * c, nb * c)


def _swa_head_perm():
    cols = []
    half = SWA_Q_HEADS // 2
    for p in range(half):
        for hd in (p, p + half):
            cols.extend(range(hd * HEAD_DIM, (hd + 1) * HEAD_DIM))
    return jnp.asarray(cols, jnp.int32)


def kernel(x, c, w_mod, b_mod, norm_mix, norm_ffn, w_in, w_out, out_gain, sinks, conv_w, conv_b,
           lru_wa, lru_ba, lru_wx, lru_bx, lru_lam, fox_bf, w_router_group, b_router_group,
           w_router_expert, b_router_expert, w_gate, w_up, w_down, norm_final):
    depth = w_mod.shape[0]
    b, s, d = x.shape
    assert d == D_MODEL and s % max(TM_PRE, TQ_SWA, TS_LRU, T_FOX, TM_POST, TM_MOE) == 0
    c_rows = jnp.zeros((16, d), F32).at[:b].set(c)
    mod_all = _modulation(c_rows, w_mod, b_mod)[:, :b]
    perm = _swa_head_perm()

    for l in range(depth):
        mod = mod_all[l].reshape(b, 6, d)
        w_in_l = w_in[l]
        w_cols = jnp.concatenate([w_in_l[:, :SWA_WIDTH][:, perm], w_in_l[:, SWA_WIDTH:OFF_FC]], axis=1)
        wf_t = jnp.zeros((FORGET_ROWS, d), F32).at[:FOX_HEADS].set(w_in_l[:, OFF_FC:].T).astype(BF16)
        fb = jnp.zeros((FORGET_ROWS, 1), F32).at[:FOX_HEADS, 0].set(fox_bf[l])
        gain = out_gain[l]
        gain_a = gain[:SWA_WIDTH][perm].reshape(1, SWA_WIDTH)
        gain_b = gain[SWA_WIDTH:SWA_WIDTH + LRU_WIDTH].reshape(1, LRU_WIDTH)
        gain_c = gain[SWA_WIDTH + LRU_WIDTH:].reshape(1, FOX_WIDTH)
        w_out_l = jnp.concatenate([w_out[l][:SWA_WIDTH][perm], w_out[l][SWA_WIDTH:]], axis=0).astype(BF16)
        wr = jnp.zeros((ROUTER_ROWS, d), F32)
        wr = wr.at[:N_GROUPS].set(w_router_group[l].T).at[8:8 + N_EXPERTS].set(w_router_expert[l].T)
        wr_hi = wr.astype(BF16)
        wr_lo = (wr - wr_hi.astype(F32)).astype(BF16)
        br = jnp.zeros((ROUTER_ROWS, 1), F32)
        br = br.at[:N_GROUPS, 0].set(b_router_group[l]).at[8:8 + N_EXPERTS, 0].set(b_router_expert[l])

        qa, ka, va, xb, gb, qc, kc, vc, lf = _pre_mixer(
            x, mod, norm_mix[l].reshape(1, d), w_cols.astype(BF16), wf_t, fb)
        cum = _cumsum(lf)
        ya = _swa(qa, ka, va, sinks[l].reshape(1, SWA_Q_HEADS))
        yb = _lru(xb, gb, conv_w[l], conv_b[l].reshape(1, -1),
                  _block_diag(lru_wa[l]).astype(BF16), lru_ba[l].reshape(1, -1),
                  _block_diag(lru_wx[l]).astype(BF16), lru_bx[l].reshape(1, -1),
                  lru_lam[l].reshape(1, -1))
        yc = _fox(qc, kc, vc, cum)
        x, h2, comb_t = _post_mixer(ya, yb, yc, x, mod, gain_a, gain_b, gain_c, w_out_l,
                                    norm_ffn[l].reshape(1, d), wr_hi, wr_lo, br)
        comb = jnp.swapaxes(comb_t, 1, 2)
        x = _moe(h2, comb, w_gate[l].astype(BF16), w_up[l].astype(BF16), w_down[l].astype(BF16),
                 x, mod, norm_final.reshape(1, d), final=(l == depth - 1))
    return x
```

```python
import functools

import jax
import jax.numpy as jnp
from jax import lax
from jax.experimental import pallas as pl
from jax.experimental.pallas import tpu as pltpu

F32 = jnp.float32
BF16 = jnp.bfloat16

D_MODEL = 1024
HEAD_DIM = 64
PAIR = 2 * HEAD_DIM
BLOCK = 128
SWA_Q_HEADS = 8
SWA_WIDTH = SWA_Q_HEADS * HEAD_DIM
KV_WIDTH = 2 * HEAD_DIM
LRU_WIDTH = 256
LRU_BLOCKS = 8
CONV_WIDTH = 4
LRU_C = 8.0
FOX_HEADS = 4
FOX_WIDTH = FOX_HEADS * HEAD_DIM
N_GROUPS = 4
EXPERTS_PER_GROUP = 4
N_EXPERTS = 16
D_EXPERT = 256
EPS = 1e-6
NEG = -1e30
LOG2E = 1.4426950408889634

OFF_KA = SWA_WIDTH
OFF_VA = OFF_KA + KV_WIDTH
OFF_XB = OFF_VA + KV_WIDTH
OFF_GB = OFF_XB + LRU_WIDTH
OFF_QC = OFF_GB + LRU_WIDTH
OFF_KC = OFF_QC + FOX_WIDTH
OFF_VC = OFF_KC + FOX_WIDTH
OFF_FC = OFF_VC + FOX_WIDTH

ROUTER_ROWS = 32
FORGET_ROWS = 16
VMEM_LIMIT_BYTES = 48 * 1024 * 1024

TM_PRE = 512
TQ_SWA = 512
TS_LRU = 512
T_FOX = 512
TM_POST = 512
TM_MOE = 1024
CUMSUM_CHUNK = 256


def _params(*semantics):
    return pltpu.CompilerParams(dimension_semantics=semantics, vmem_limit_bytes=VMEM_LIMIT_BYTES)


def _split_bf16(v):
    hi = v.astype(BF16)
    lo = (v - hi.astype(F32)).astype(BF16)
    return hi, lo


def _dot(a, b):
    return jnp.dot(a, b, preferred_element_type=F32)


def _dot_nt(a, b):
    return lax.dot_general(a, b, (((1,), (1,)), ((), ())), preferred_element_type=F32)


def _rms_normalize(v):
    return v * lax.rsqrt(jnp.mean(v * v, axis=-1, keepdims=True) + EPS)


def _mod_kernel(c_ref, w_ref, b_ref, o_ref):
    c = c_ref[...]
    ca = c * jax.nn.sigmoid(c)
    w = w_ref[0]
    a_hi, a_lo = _split_bf16(ca)
    w_hi, w_lo = _split_bf16(w)
    o_ref[0] = _dot(a_hi, w_hi) + _dot(a_lo, w_hi) + _dot(a_hi, w_lo) + b_ref[0]


def _modulation(c, w_mod, b_mod):
    depth, d, d6 = w_mod.shape
    b = c.shape[0]
    n = d6 // d
    return pl.pallas_call(
        _mod_kernel,
        grid=(depth, n),
        in_specs=[
            pl.BlockSpec((b, d), lambda l, j: (0, 0)),
            pl.BlockSpec((1, d, d), lambda l, j: (l, 0, j)),
            pl.BlockSpec((1, 1, d), lambda l, j: (l, 0, j)),
        ],
        out_specs=pl.BlockSpec((1, b, d), lambda l, j: (l, 0, j)),
        out_shape=jax.ShapeDtypeStruct((depth, b, d6), F32),
        compiler_params=_params("arbitrary", "arbitrary"),
        name="modulation",
    )(c, w_mod, b_mod.reshape(depth, 1, d6))


def _log_sigmoid(z):
    return jnp.minimum(z, 0.0) - jnp.log1p(jnp.exp(-jnp.abs(z)))


def _other_half(a):
    return HEAD_DIM * (1 - a)


def _pre_kernel(x_ref, mod_ref, g_ref, w_ref, wt_ref, fb_ref,
                qa_ref, ka_ref, va_ref, xb_ref, gb_ref, kc_ref, qct_ref, vct_ref, lf_ref):
    x = x_ref[0]
    h = (_rms_normalize(x) * g_ref[...]) * (1.0 + mod_ref[0, 1:2, :]) + mod_ref[0, 0:1, :]
    hb = h.astype(BF16)

    def proj(lo, width):
        return _dot(hb, w_ref[:, lo:lo + width])

    scale = HEAD_DIM ** -0.5
    qa_ref[0] = (proj(0, SWA_WIDTH) * scale).astype(BF16)
    ka_ref[0] = proj(OFF_KA, KV_WIDTH).astype(BF16)
    va_ref[0] = proj(OFF_VA, KV_WIDTH).astype(BF16)
    xb_ref[0] = proj(OFF_XB, LRU_WIDTH).astype(BF16)
    gb_ref[0] = proj(OFF_GB, LRU_WIDTH).astype(BF16)
    kc_ref[0] = proj(OFF_QC, FOX_WIDTH).astype(BF16)

    t_all = _dot_nt(wt_ref[...], hb)
    qct_ref[0, 0] = (t_all[0:FOX_WIDTH] * (scale * LOG2E)).astype(BF16)
    row = lax.broadcasted_iota(jnp.int32, (PAIR, t_all.shape[1]), 0)
    for p in range(FOX_WIDTH // PAIR):
        v_pair = t_all[FOX_WIDTH + p * PAIR:FOX_WIDTH + (p + 1) * PAIR]
        for a in range(2):
            vct_ref[0, 0, 2 * p + a] = jnp.where(row == _other_half(a), 1.0, v_pair).astype(BF16)
    lf_ref[0] = _log_sigmoid(t_all[2 * FOX_WIDTH:] + fb_ref[...])


def _pre_mixer(x, mod, gain, w_tok, w_t, fb):
    b, s, d = x.shape
    tm = TM_PRE
    nt = s // tm

    def tok(width):
        return pl.BlockSpec((1, tm, width), lambda bi, i: (bi, i, 0))

    def tok_shape(width):
        return jax.ShapeDtypeStruct((b, s, width), BF16)

    return pl.pallas_call(
        _pre_kernel,
        grid=(b, nt),
        in_specs=[
            tok(d),
            pl.BlockSpec((1, 6, d), lambda bi, i: (bi, 0, 0)),
            pl.BlockSpec((1, d), lambda bi, i: (0, 0)),
            pl.BlockSpec(w_tok.shape, lambda bi, i: (0, 0)),
            pl.BlockSpec(w_t.shape, lambda bi, i: (0, 0)),
            pl.BlockSpec((FORGET_ROWS, 1), lambda bi, i: (0, 0)),
        ],
        out_specs=[
            tok(SWA_WIDTH), tok(KV_WIDTH), tok(KV_WIDTH), tok(LRU_WIDTH), tok(LRU_WIDTH),
            tok(FOX_WIDTH),
            pl.BlockSpec((1, 1, FOX_WIDTH, tm), lambda bi, i: (bi, i, 0, 0)),
            pl.BlockSpec((1, 1, FOX_HEADS, PAIR, tm), lambda bi, i: (bi, i, 0, 0, 0)),
            pl.BlockSpec((1, FORGET_ROWS, tm), lambda bi, i: (bi, 0, i)),
        ],
        out_shape=[
            tok_shape(SWA_WIDTH), tok_shape(KV_WIDTH), tok_shape(KV_WIDTH), tok_shape(LRU_WIDTH),
            tok_shape(LRU_WIDTH), tok_shape(FOX_WIDTH),
            jax.ShapeDtypeStruct((b, nt, FOX_WIDTH, tm), BF16),
            jax.ShapeDtypeStruct((b, nt, FOX_HEADS, PAIR, tm), BF16),
            jax.ShapeDtypeStruct((b, FORGET_ROWS, s), F32),
        ],
        compiler_params=_params("parallel", "parallel"),
        name="pre_mixer",
    )(x, mod, gain, w_tok, w_t, fb)


def _cumsum_kernel(lf_ref, o_ref):
    n = CUMSUM_CHUNK
    s = lf_ref.shape[2]
    r = lax.broadcasted_iota(jnp.int32, (n, n), 0)
    c = lax.broadcasted_iota(jnp.int32, (n, n), 1)
    tri = jnp.where(r <= c, 1.0, 0.0).astype(BF16)
    carry = jnp.zeros((lf_ref.shape[1], 1), F32)
    for j in range(s // n):
        seg = lf_ref[0, :, j * n:(j + 1) * n]
        hi = seg.astype(BF16)
        rem = seg - hi.astype(F32)
        mid = rem.astype(BF16)
        lo = (rem - mid.astype(F32)).astype(BF16)
        cs = (_dot(hi, tri) + _dot(mid, tri)) + _dot(lo, tri) + carry
        o_ref[0, :, j * n:(j + 1) * n] = cs
        carry = cs[:, n - 1:n]


def _cumsum(lf):
    b, r, s = lf.shape
    return pl.pallas_call(
        _cumsum_kernel,
        grid=(b,),
        in_specs=[pl.BlockSpec((1, r, s), lambda bi: (bi, 0, 0))],
        out_specs=pl.BlockSpec((1, r, s), lambda bi: (bi, 0, 0)),
        out_shape=jax.ShapeDtypeStruct((b, r, s), F32),
        compiler_params=_params("parallel"),
        name="cumsum",
    )(lf)


def _swa_kernel(sink_ref, q_ref, kc_ref, vc_ref, kp_ref, vp_ref, o_ref, kf_ref, vf_ref, bias_ref):
    i = pl.program_id(1)
    nsub = q_ref.shape[1] // BLOCK
    npair = SWA_Q_HEADS // 2

    @pl.when((pl.program_id(0) == 0) & (i == 0))
    def _():
        qpos = BLOCK + lax.broadcasted_iota(jnp.int32, (BLOCK, 2 * BLOCK), 0)
        kpos = lax.broadcasted_iota(jnp.int32, (BLOCK, 2 * BLOCK), 1)
        dist = qpos - kpos
        valid = (dist >= 0) & (dist < BLOCK)
        distf = dist.astype(F32)
        for hd in range(SWA_Q_HEADS):
            slope = 2.0 ** (-8.0 * (hd + 1) / SWA_Q_HEADS)
            bias_ref[hd] = jnp.where(valid, -slope * distf, NEG)

    kf_ref[0:BLOCK, :] = kp_ref[0]
    kf_ref[BLOCK:, :] = kc_ref[0]
    vf_ref[0:BLOCK, :] = vp_ref[0]
    vf_ref[BLOCK:, :] = vc_ref[0]

    lane = lax.broadcasted_iota(jnp.int32, (2 * BLOCK, PAIR), 1)
    lane_o = lax.broadcasted_iota(jnp.int32, (BLOCK, PAIR), 1)
    col = lax.broadcasted_iota(jnp.int32, (BLOCK, 2 * BLOCK), 1)

    for j in range(nsub):
        k2 = kf_ref[j * BLOCK:(j + 2) * BLOCK, :]
        v2 = vf_ref[j * BLOCK:(j + 2) * BLOCK, :]
        k_half = (jnp.where(lane < HEAD_DIM, k2, jnp.zeros_like(k2)),
                  jnp.where(lane >= HEAD_DIM, k2, jnp.zeros_like(k2)))
        for p in range(npair):
            qp = q_ref[0, j * BLOCK:(j + 1) * BLOCK, p * PAIR:(p + 1) * PAIR]
            outs = []
            for a in range(2):
                hd = p + a * npair
                sc = _dot_nt(qp, k_half[a]) + bias_ref[hd]
                if j == 0:
                    sc = sc + jnp.where(col < BLOCK, jnp.where(i == 0, NEG, 0.0), 0.0)
                sink = sink_ref[0, hd]
                m = jnp.maximum(jnp.max(sc, axis=-1, keepdims=True), sink)
                pr = jnp.exp(sc - m)
                den = jnp.sum(pr, axis=-1, keepdims=True) + jnp.exp(sink - m)
                outs.append(_dot(pr.astype(BF16), v2) / den)
            o_ref[0, j * BLOCK:(j + 1) * BLOCK, p * PAIR:(p + 1) * PAIR] = jnp.where(
                lane_o < HEAD_DIM, outs[0], outs[1]).astype(BF16)


def _swa(qa, ka, va, sinks):
    b, s, _ = qa.shape
    tq = TQ_SWA
    nsub = tq // BLOCK
    prev = lambda bi, i: (bi, jnp.maximum(i * nsub - 1, 0), 0)
    return pl.pallas_call(
        _swa_kernel,
        grid=(b, s // tq),
        in_specs=[
            pl.BlockSpec(memory_space=pltpu.SMEM),
            pl.BlockSpec((1, tq, SWA_WIDTH), lambda bi, i: (bi, i, 0)),
            pl.BlockSpec((1, tq, KV_WIDTH), lambda bi, i: (bi, i, 0)),
            pl.BlockSpec((1, tq, KV_WIDTH), lambda bi, i: (bi, i, 0)),
            pl.BlockSpec((1, BLOCK, KV_WIDTH), prev),
            pl.BlockSpec((1, BLOCK, KV_WIDTH), prev),
        ],
        out_specs=pl.BlockSpec((1, tq, SWA_WIDTH), lambda bi, i: (bi, i, 0)),
        out_shape=jax.ShapeDtypeStruct((b, s, SWA_WIDTH), BF16),
        scratch_shapes=[
            pltpu.VMEM((tq + BLOCK, KV_WIDTH), BF16),
            pltpu.VMEM((tq + BLOCK, KV_WIDTH), BF16),
            pltpu.VMEM((SWA_Q_HEADS, BLOCK, 2 * BLOCK), F32),
        ],
        compiler_params=_params("arbitrary", "arbitrary"),
        name="swa",
    )(sinks, qa, ka, va, ka, va)


def _gelu_tanh(v):
    return 0.5 * v * (1.0 + jnp.tanh(0.7978845608028654 * (v + 0.044715 * (v * v * v))))


def _lru_kernel(xb_ref, gb_ref, cw_ref, cb_ref, wa_ref, ba_ref, wx_ref, bx_ref, lam_ref,
                o_ref, tail_ref, h_ref):
    ts = xb_ref.shape[1]

    @pl.when(pl.program_id(1) == 0)
    def _():
        tail_ref[...] = jnp.zeros_like(tail_ref)
        h_ref[...] = jnp.zeros_like(h_ref)

    x = xb_ref[0].astype(F32)
    tail = tail_ref[...]
    row8 = lax.broadcasted_iota(jnp.int32, (8, LRU_WIDTH), 0)
    conv = cb_ref[...] + cw_ref[CONV_WIDTH - 1:CONV_WIDTH, :] * x
    for k in range(1, CONV_WIDTH):
        rolled = pltpu.roll(x, k, 0)
        head = jnp.where(row8 < k, pltpu.roll(tail, k, 0), rolled[0:8])
        shifted = jnp.concatenate([head, rolled[8:]], axis=0)
        conv = conv + cw_ref[CONV_WIDTH - 1 - k:CONV_WIDTH - k, :] * shifted
    tail_ref[...] = x[ts - 8:ts]

    cbf = conv.astype(BF16)
    r = jax.nn.sigmoid(_dot(cbf, wa_ref[...]) + ba_ref[...])
    gi = jax.nn.sigmoid(_dot(cbf, wx_ref[...]) + bx_ref[...])
    z = -lam_ref[...]
    softplus = jnp.maximum(z, 0.0) + jnp.log1p(jnp.exp(-jnp.abs(z)))
    log_a = (-LRU_C * softplus) * r
    a = jnp.exp(log_a)
    u = jnp.sqrt(1.0 - a * a) * (gi * conv)

    row = lax.broadcasted_iota(jnp.int32, (ts, LRU_WIDTH), 0)
    step = 1
    while step < ts:
        keep = row >= step
        u = jnp.where(keep, a * pltpu.roll(u, step, 0) + u, u)
        a = jnp.where(keep, a * pltpu.roll(a, step, 0), a)
        step *= 2
    h = u + a * h_ref[7:8, :]
    h_ref[...] = h[ts - 8:ts]
    o_ref[0] = (h * _gelu_tanh(gb_ref[0].astype(F32))).astype(BF16)


def _lru(xb, gb, conv_w, conv_b, wa, ba, wx, bx, lam):
    b, s, w = xb.shape
    ts = TS_LRU
    tok = pl.BlockSpec((1, ts, w), lambda bi, i: (bi, i, 0))
    full = lambda shape: pl.BlockSpec(shape, lambda bi, i: (0,) * len(shape))
    return pl.pallas_call(
        _lru_kernel,
        grid=(b, s // ts),
        in_specs=[tok, tok, full((CONV_WIDTH, w)), full((1, w)), full((w, w)), full((1, w)),
                  full((w, w)), full((1, w)), full((1, w))],
        out_specs=tok,
        out_shape=jax.ShapeDtypeStruct((b, s, w), BF16),
        scratch_shapes=[pltpu.VMEM((8, w), F32), pltpu.VMEM((8, w), F32)],
        compiler_params=_params("arbitrary", "arbitrary"),
        name="lru",
    )(xb, gb, conv_w, conv_b, wa, ba, wx, bx, lam)


BIAS_ROWS = 6


def _split3(v):
    hi = v.astype(BF16).astype(F32)
    rem = v - hi
    mid = rem.astype(BF16).astype(F32)
    lo = (rem - mid).astype(BF16).astype(F32)
    return hi, mid, lo


def _fox_prep_kernel(k_ref, c_ref, o_ref):
    tm = k_ref.shape[1]
    lane = lax.broadcasted_iota(jnp.int32, (tm, PAIR), 1)
    for p in range(FOX_WIDTH // PAIR):
        kp = k_ref[0, :, p * PAIR:(p + 1) * PAIR].astype(F32)
        for a in range(2):
            hd = 2 * p + a
            c_hi, c_mid, c_lo = _split3(c_ref[0, :, hd:hd + 1] * LOG2E)
            rel = lane - _other_half(a)
            aug = jnp.where(rel == 0, c_hi, jnp.where(rel == 1, c_mid, jnp.where(
                rel == 2, c_lo, jnp.where(rel < BIAS_ROWS, 1.0, 0.0))))
            in_half = (lane >= HEAD_DIM * a) & (lane < HEAD_DIM * (a + 1))
            o_ref[0, hd] = jnp.where(in_half, kp, aug).astype(BF16)


def _fox_prep(kc, cum_col):
    b, s, w = kc.shape
    tm = TM_PRE
    return pl.pallas_call(
        _fox_prep_kernel,
        grid=(b, s // tm),
        in_specs=[
            pl.BlockSpec((1, tm, w), lambda bi, i: (bi, i, 0)),
            pl.BlockSpec((1, tm, cum_col.shape[2]), lambda bi, i: (bi, i, 0)),
        ],
        out_specs=pl.BlockSpec((1, FOX_HEADS, tm, PAIR), lambda bi, i: (bi, 0, i, 0)),
        out_shape=jax.ShapeDtypeStruct((b, FOX_HEADS, s, PAIR), BF16),
        compiler_params=_params("parallel", "parallel"),
        name="fox_prep",
    )(kc, cum_col)


def _fox_kernel(base_ref, qt_ref, ka_ref, vt_ref, o_ref, acc_ref, s0_ref, s1_ref):
    bi = pl.program_id(0)
    p = pl.program_id(1)
    qi = pl.program_id(2)
    nq = pl.num_programs(2)
    t = qt_ref.shape[3]
    qt = qt_ref[0, 0].astype(F32)
    row = lax.broadcasted_iota(jnp.int32, (PAIR, t), 0)
    causal = (lax.broadcasted_iota(jnp.int32, (t, t), 0) <= lax.broadcasted_iota(jnp.int32, (t, t), 1))
    q_aug = []
    for a in range(2):
        hd = 2 * p + a
        base = jnp.full((1, t), base_ref[(bi * FOX_HEADS + hd) * nq + qi], F32) * LOG2E
        b_hi, b_mid, b_lo = _split3(base)
        rel = row - _other_half(a)
        aug = jnp.where(rel < 3, -1.0, jnp.where(rel == 3, b_hi, jnp.where(
            rel == 4, b_mid, jnp.where(rel == 5, b_lo, 0.0))))
        in_half = (row >= HEAD_DIM * a) & (row < HEAD_DIM * (a + 1))
        q_aug.append(jnp.where(in_half, qt, aug).astype(BF16))
        acc_ref[a] = jnp.zeros((PAIR, t), F32)

    slots = (s0_ref, s1_ref)

    def scores(ki, slot, masked):
        tile_max = []
        for a in range(2):
            k = ka_ref[0, a, pl.ds(pl.multiple_of(ki * t, t), t), :]
            st = _dot(k, q_aug[a])
            if masked:
                st = jnp.where(causal, st, NEG)
            slots[slot][a] = st
            tile_max.append(jnp.max(st, axis=0, keepdims=True))
        return tuple(tile_max)

    def consume(ki, slot, m_old, tile_max):
        m_new = []
        for a in range(2):
            m_a = jnp.maximum(m_old[a], tile_max[a])
            alpha = jnp.exp2(m_old[a] - m_a)
            pr = jnp.exp2(slots[slot][a] - m_a).astype(BF16)
            acc_ref[a] = alpha * acc_ref[a] + _dot(vt_ref[0, ki, a], pr)
            m_new.append(m_a)
        return tuple(m_new)

    def advance(j, slot, carry):
        m_old, tile_max, prev = carry
        next_max = scores(j, 1 - slot, False)
        return consume(prev, slot, m_old, tile_max), next_max, j

    def body(i, carry):
        return advance(2 * i + 1, 1, advance(2 * i, 0, carry))

    m0 = jnp.full((1, t), NEG, F32)
    carry = ((m0, m0), scores(qi, 0, True), qi)
    carry = lax.fori_loop(0, qi // 2, body, carry)

    def finish_odd(carry):
        m, tile_max, prev = advance(qi - 1, 0, carry)
        consume(prev, 1, m, tile_max)
        return 0

    def finish_even(carry):
        m, tile_max, prev = carry
        consume(prev, 0, m, tile_max)
        return 0

    lax.cond(qi % 2 == 1, finish_odd, finish_even, carry)
    outs = []
    for a in range(2):
        acc = acc_ref[a]
        ob = _other_half(a)
        outs.append(acc / acc[ob:ob + 1, :])
    o_ref[0] = jnp.where(row < HEAD_DIM, outs[0], outs[1]).T.astype(BF16)


def _fox(qct, kaug, vaug, base):
    b, nt, w, t = qct.shape
    s = nt * t
    npair = w // PAIR
    return pl.pallas_call(
        _fox_kernel,
        grid=(b, npair, nt),
        in_specs=[
            pl.BlockSpec(memory_space=pltpu.SMEM),
            pl.BlockSpec((1, 1, PAIR, t), lambda bi, p, qi: (bi, qi, p, 0)),
            pl.BlockSpec((1, 2, s, PAIR), lambda bi, p, qi: (bi, p, 0, 0)),
            pl.BlockSpec((1, nt, 2, PAIR, t), lambda bi, p, qi: (bi, 0, p, 0, 0)),
        ],
        out_specs=pl.BlockSpec((1, t, PAIR), lambda bi, p, qi: (bi, qi, p)),
        out_shape=jax.ShapeDtypeStruct((b, s, w), BF16),
        scratch_shapes=[pltpu.VMEM((2, PAIR, t), F32), pltpu.VMEM((2, t, t), F32),
                        pltpu.VMEM((2, t, t), F32)],
        compiler_params=_params("parallel", "parallel", "arbitrary"),
        name="fox",
    )(base, qct, kaug, vaug)


def _first_max(vals):
    best = vals[0]
    for v in vals[1:]:
        best = jnp.maximum(best, v)
    idx = jnp.full(best.shape, len(vals) - 1, jnp.int32)
    for j in range(len(vals) - 2, -1, -1):
        idx = jnp.where(vals[j] == best, j, idx)
    return best, idx


def _softmax_rows(vals):
    top = vals[0]
    for v in vals[1:]:
        top = jnp.maximum(top, v)
    ex = [jnp.exp(v - top) for v in vals]
    tot = ex[0]
    for e in ex[1:]:
        tot = tot + e
    return [e / tot for e in ex]


def _post_kernel(ya_ref, yb_ref, yc_ref, x_ref, mod_ref, ga_ref, gb_ref, gc_ref, wo_ref, nf_ref,
                 wr_hi_ref, wr_lo_ref, br_ref, xo_ref, h2_ref, comb_ref):
    ya = _rms_normalize(ya_ref[0].astype(F32)) * ga_ref[...]
    yb = _rms_normalize(yb_ref[0].astype(F32)) * gb_ref[...]
    yc = _rms_normalize(yc_ref[0].astype(F32)) * gc_ref[...]
    y = jnp.concatenate([ya, yb, yc], axis=-1).astype(BF16)
    x = x_ref[0] + mod_ref[0, 2:3, :] * _dot(y, wo_ref[...])
    xo_ref[0] = x
    h = (_rms_normalize(x) * nf_ref[...]) * (1.0 + mod_ref[0, 4:5, :]) + mod_ref[0, 3:4, :]
    h_hi, h_lo = _split_bf16(h)
    h2_ref[0] = h_hi

    w_hi = wr_hi_ref[...]
    logits = (_dot_nt(w_hi, h_hi) + _dot_nt(w_hi, h_lo)) + _dot_nt(wr_lo_ref[...], h_hi) + br_ref[...]
    rows = [logits[r:r + 1, :] for r in range(8 + N_EXPERTS)]
    group_prob = _softmax_rows(rows[0:N_GROUPS])
    group_p, group_idx = _first_max(group_prob)
    in_group = []
    for j in range(EXPERTS_PER_GROUP):
        v = rows[8 + (N_GROUPS - 1) * EXPERTS_PER_GROUP + j]
        for g in range(N_GROUPS - 2, -1, -1):
            v = jnp.where(group_idx == g, rows[8 + g * EXPERTS_PER_GROUP + j], v)
        in_group.append(v)
    expert_prob = _softmax_rows(in_group)
    p1, i1 = _first_max(expert_prob)
    rest = [jnp.where(i1 == j, -1.0, expert_prob[j]) for j in range(EXPERTS_PER_GROUP)]
    p2, i2 = _first_max(rest)
    w1 = group_p * p1 / (p1 + p2)
    w2 = group_p * p2 / (p1 + p2)
    for j in range(EXPERTS_PER_GROUP):
        wj = jnp.where(i1 == j, w1, jnp.where(i2 == j, w2, 0.0))
        for g in range(N_GROUPS):
            e = g * EXPERTS_PER_GROUP + j
            comb_ref[0, e:e + 1, :] = jnp.where(group_idx == g, wj, 0.0)


def _post_mixer(ya, yb, yc, x, mod, gain_a, gain_b, gain_c, w_out, norm_ffn, wr_hi, wr_lo, br):
    b, s, d = x.shape
    tm = TM_POST
    tok = lambda width: pl.BlockSpec((1, tm, width), lambda bi, i: (bi, i, 0))
    full = lambda shape: pl.BlockSpec(shape, lambda bi, i: (0,) * len(shape))
    return pl.pallas_call(
        _post_kernel,
        grid=(b, s // tm),
        in_specs=[
            tok(SWA_WIDTH), tok(LRU_WIDTH), tok(FOX_WIDTH), tok(d),
            pl.BlockSpec((1, 6, d), lambda bi, i: (bi, 0, 0)),
            full((1, SWA_WIDTH)), full((1, LRU_WIDTH)), full((1, FOX_WIDTH)),
            full((d, d)), full((1, d)),
            full((ROUTER_ROWS, d)), full((ROUTER_ROWS, d)), full((ROUTER_ROWS, 1)),
        ],
        out_specs=[
            tok(d), tok(d),
            pl.BlockSpec((1, N_EXPERTS, tm), lambda bi, i: (bi, 0, i)),
        ],
        out_shape=[
            jax.ShapeDtypeStruct((b, s, d), F32),
            jax.ShapeDtypeStruct((b, s, d), BF16),
            jax.ShapeDtypeStruct((b, N_EXPERTS, s), F32),
        ],
        compiler_params=_params("parallel", "parallel"),
        name="post_mixer",
    )(ya, yb, yc, x, mod, gain_a, gain_b, gain_c, w_out, norm_ffn, wr_hi, wr_lo, br)


def _moe_kernel(h_ref, comb_ref, wg_ref, wu_ref, wd_ref, x_ref, mod_ref, nf_ref, o_ref, acc_ref,
                *, final):
    e = pl.program_id(2)

    @pl.when(e == 0)
    def _():
        acc_ref[...] = jnp.zeros_like(acc_ref)

    h = h_ref[0]
    gate = _dot(h, wg_ref[0])
    hidden = (gate * jax.nn.sigmoid(gate)) * _dot(h, wu_ref[0])
    comb = comb_ref[0]
    lane = lax.broadcasted_iota(jnp.int32, comb.shape, 1)
    ce = jnp.sum(jnp.where(lane == e, comb, 0.0), axis=-1, keepdims=True)
    acc_ref[...] += _dot((hidden * ce).astype(BF16), wd_ref[0])

    @pl.when(e == N_EXPERTS - 1)
    def _():
        x = x_ref[0] + mod_ref[0, 5:6, :] * acc_ref[...]
        if final:
            x = _rms_normalize(x) * nf_ref[...]
        o_ref[0] = x


def _moe(h2, comb, w_gate, w_up, w_down, x, mod, norm_final, final):
    b, s, d = x.shape
    tm = TM_MOE
    tok = lambda width: pl.BlockSpec((1, tm, width), lambda bi, i, e: (bi, i, 0))
    return pl.pallas_call(
        functools.partial(_moe_kernel, final=final),
        grid=(b, s // tm, N_EXPERTS),
        in_specs=[
            tok(d), tok(N_EXPERTS),
            pl.BlockSpec((1, d, D_EXPERT), lambda bi, i, e: (e, 0, 0)),
            pl.BlockSpec((1, d, D_EXPERT), lambda bi, i, e: (e, 0, 0)),
            pl.BlockSpec((1, D_EXPERT, d), lambda bi, i, e: (e, 0, 0)),
            tok(d),
            pl.BlockSpec((1, 6, d), lambda bi, i, e: (bi, 0, 0)),
            pl.BlockSpec((1, d), lambda bi, i, e: (0, 0)),
        ],
        out_specs=tok(d),
        out_shape=jax.ShapeDtypeStruct((b, s, d), F32),
        scratch_shapes=[pltpu.VMEM((tm, d), F32)],
        compiler_params=_params("parallel", "parallel", "arbitrary"),
        name="moe_final" if final else "moe",
    )(h2, comb, w_gate, w_up, w_down, x, mod, norm_final)


def _block_diag(w):
    nb, c, _ = w.shape
    eye = jnp.eye(nb, dtype=w.dtype)
    return (eye[:, None, :, None] * w[:, :, None, :]).reshape(nb * c, nb * c)


def _swa_head_perm():
    cols = []
    half = SWA_Q_HEADS // 2
    for p in range(half):
        for hd in (p, p + half):
            cols.extend(range(hd * HEAD_DIM, (hd + 1) * HEAD_DIM))
    return jnp.asarray(cols, jnp.int32)


def kernel(x, c, w_mod, b_mod, norm_mix, norm_ffn, w_in, w_out, out_gain, sinks, conv_w, conv_b,
           lru_wa, lru_ba, lru_wx, lru_bx, lru_lam, fox_bf, w_router_group, b_router_group,
           w_router_expert, b_router_expert, w_gate, w_up, w_down, norm_final):
    depth = w_mod.shape[0]
    b, s, d = x.shape
    assert d == D_MODEL and s % max(TM_PRE, TQ_SWA, TS_LRU, T_FOX, TM_POST, TM_MOE) == 0
    assert TM_PRE == T_FOX
    c_rows = jnp.zeros((16, d), F32).at[:b].set(c)
    mod_all = _modulation(c_rows, w_mod, b_mod)[:, :b]
    perm = _swa_head_perm()

    for l in range(depth):
        mod = mod_all[l].reshape(b, 6, d)
        w_in_l = w_in[l]
        w_tok = jnp.concatenate([w_in_l[:, :SWA_WIDTH][:, perm], w_in_l[:, SWA_WIDTH:OFF_QC],
                                 w_in_l[:, OFF_KC:OFF_VC]], axis=1).astype(BF16)
        w_t = jnp.concatenate([w_in_l[:, OFF_QC:OFF_KC].T, w_in_l[:, OFF_VC:OFF_FC].T,
                               jnp.zeros((FORGET_ROWS, d), F32).at[:FOX_HEADS].set(w_in_l[:, OFF_FC:].T)],
                              axis=0).astype(BF16)
        fb = jnp.zeros((FORGET_ROWS, 1), F32).at[:FOX_HEADS, 0].set(fox_bf[l])
        gain = out_gain[l]
        gain_a = gain[:SWA_WIDTH][perm].reshape(1, SWA_WIDTH)
        gain_b = gain[SWA_WIDTH:SWA_WIDTH + LRU_WIDTH].reshape(1, LRU_WIDTH)
        gain_c = gain[SWA_WIDTH + LRU_WIDTH:].reshape(1, FOX_WIDTH)
        w_out_l = jnp.concatenate([w_out[l][:SWA_WIDTH][perm], w_out[l][SWA_WIDTH:]], axis=0).astype(BF16)
        wr = jnp.zeros((ROUTER_ROWS, d), F32)
        wr = wr.at[:N_GROUPS].set(w_router_group[l].T).at[8:8 + N_EXPERTS].set(w_router_expert[l].T)
        wr_hi = wr.astype(BF16)
        wr_lo = (wr - wr_hi.astype(F32)).astype(BF16)
        br = jnp.zeros((ROUTER_ROWS, 1), F32)
        br = br.at[:N_GROUPS, 0].set(b_router_group[l]).at[8:8 + N_EXPERTS, 0].set(b_router_expert[l])

        qa, ka, va, xb, gb, kc, qct, vaug, lf = _pre_mixer(
            x, mod, norm_mix[l].reshape(1, d), w_tok, w_t, fb)
        cum = _cumsum(lf)
        kaug = _fox_prep(kc, jnp.swapaxes(cum, 1, 2))
        base = cum[:, :FOX_HEADS, ::T_FOX].reshape(-1)
        ya = _swa(qa, ka, va, sinks[l].reshape(1, SWA_Q_HEADS))
        yb = _lru(xb, gb, conv_w[l], conv_b[l].reshape(1, -1),
                  _block_diag(lru_wa[l]).astype(BF16), lru_ba[l].reshape(1, -1),
                  _block_diag(lru_wx[l]).astype(BF16), lru_bx[l].reshape(1, -1),
                  lru_lam[l].reshape(1, -1))
        yc = _fox(qct, kaug, vaug, base)
        x, h2, comb_t = _post_mixer(ya, yb, yc, x, mod, gain_a, gain_b, gain_c, w_out_l,
                                    norm_ffn[l].reshape(1, d), wr_hi, wr_lo, br)
        comb = jnp.swapaxes(comb_t, 1, 2)
        x = _moe(h2, comb, w_gate[l].astype(BF16), w_up[l].astype(BF16), w_down[l].astype(BF16),
                 x, mod, norm_final.reshape(1, d), final=(l == depth - 1))
    return x
```

```python
import functools

import jax
import jax.numpy as jnp
from jax import lax
from jax.experimental import pallas as pl
from jax.experimental.pallas import tpu as pltpu

F32 = jnp.float32
BF16 = jnp.bfloat16

D_MODEL = 1024
HEAD_DIM = 64
PAIR = 2 * HEAD_DIM
BLOCK = 128
SWA_Q_HEADS = 8
SWA_WIDTH = SWA_Q_HEADS * HEAD_DIM
KV_WIDTH = 2 * HEAD_DIM
LRU_WIDTH = 256
LRU_BLOCKS = 8
CONV_WIDTH = 4
LRU_C = 8.0
FOX_HEADS = 4
FOX_WIDTH = FOX_HEADS * HEAD_DIM
N_GROUPS = 4
EXPERTS_PER_GROUP = 4
N_EXPERTS = 16
D_EXPERT = 256
EPS = 1e-6
NEG = -1e30
LOG2E = 1.4426950408889634

OFF_KA = SWA_WIDTH
OFF_VA = OFF_KA + KV_WIDTH
OFF_XB = OFF_VA + KV_WIDTH
OFF_GB = OFF_XB + LRU_WIDTH
OFF_QC = OFF_GB + LRU_WIDTH
OFF_KC = OFF_QC + FOX_WIDTH
OFF_VC = OFF_KC + FOX_WIDTH
OFF_FC = OFF_VC + FOX_WIDTH

ROUTER_ROWS = 32
FORGET_ROWS = 16
VMEM_LIMIT_BYTES = 48 * 1024 * 1024

TM_PRE = 512
TQ_SWA = 512
TS_LRU = 512
T_FOX = 512
TM_POST = 512
TM_MOE = 1024
CUMSUM_CHUNK = 256


def _params(*semantics):
    return pltpu.CompilerParams(dimension_semantics=semantics, vmem_limit_bytes=VMEM_LIMIT_BYTES)


def _split_bf16(v):
    hi = v.astype(BF16)
    lo = (v - hi.astype(F32)).astype(BF16)
    return hi, lo


def _dot(a, b):
    return jnp.dot(a, b, preferred_element_type=F32)


def _dot_nt(a, b):
    return lax.dot_general(a, b, (((1,), (1,)), ((), ())), preferred_element_type=F32)


def _rms_normalize(v):
    return v * lax.rsqrt(jnp.mean(v * v, axis=-1, keepdims=True) + EPS)


def _mod_kernel(c_ref, w_ref, b_ref, o_ref):
    c = c_ref[...]
    ca = c * jax.nn.sigmoid(c)
    w = w_ref[0]
    a_hi, a_lo = _split_bf16(ca)
    w_hi, w_lo = _split_bf16(w)
    o_ref[0] = _dot(a_hi, w_hi) + _dot(a_lo, w_hi) + _dot(a_hi, w_lo) + b_ref[0]


def _modulation(c, w_mod, b_mod):
    depth, d, d6 = w_mod.shape
    b = c.shape[0]
    n = d6 // d
    return pl.pallas_call(
        _mod_kernel,
        grid=(depth, n),
        in_specs=[
            pl.BlockSpec((b, d), lambda l, j: (0, 0)),
            pl.BlockSpec((1, d, d), lambda l, j: (l, 0, j)),
            pl.BlockSpec((1, 1, d), lambda l, j: (l, 0, j)),
        ],
        out_specs=pl.BlockSpec((1, b, d), lambda l, j: (l, 0, j)),
        out_shape=jax.ShapeDtypeStruct((depth, b, d6), F32),
        compiler_params=_params("arbitrary", "arbitrary"),
        name="modulation",
    )(c, w_mod, b_mod.reshape(depth, 1, d6))


def _log_sigmoid(z):
    return jnp.minimum(z, 0.0) - jnp.log1p(jnp.exp(-jnp.abs(z)))


def _other_half(a):
    return HEAD_DIM * (1 - a)


def _pre_kernel(x_ref, mod_ref, g_ref, w_ref, wt_ref, fb_ref,
                qa_ref, ka_ref, va_ref, xb_ref, gb_ref, kc_ref, qct_ref, vct_ref, lf_ref):
    x = x_ref[0]
    h = (_rms_normalize(x) * g_ref[...]) * (1.0 + mod_ref[0, 1:2, :]) + mod_ref[0, 0:1, :]
    hb = h.astype(BF16)

    def proj(lo, width):
        return _dot(hb, w_ref[:, lo:lo + width])

    scale = HEAD_DIM ** -0.5
    qa_ref[0] = (proj(0, SWA_WIDTH) * scale).astype(BF16)
    ka_ref[0] = proj(OFF_KA, KV_WIDTH).astype(BF16)
    va_ref[0] = proj(OFF_VA, KV_WIDTH).astype(BF16)
    xb_ref[0] = proj(OFF_XB, LRU_WIDTH).astype(BF16)
    gb_ref[0] = proj(OFF_GB, LRU_WIDTH).astype(BF16)
    kc_ref[0] = proj(OFF_QC, FOX_WIDTH).astype(BF16)

    t_all = _dot_nt(wt_ref[...], hb)
    qct_ref[0, 0] = (t_all[0:FOX_WIDTH] * (scale * LOG2E)).astype(BF16)
    row = lax.broadcasted_iota(jnp.int32, (PAIR, t_all.shape[1]), 0)
    for p in range(FOX_WIDTH // PAIR):
        v_pair = t_all[FOX_WIDTH + p * PAIR:FOX_WIDTH + (p + 1) * PAIR]
        for a in range(2):
            vct_ref[0, 0, 2 * p + a] = jnp.where(row == _other_half(a), 1.0, v_pair).astype(BF16)
    lf_ref[0] = _log_sigmoid(t_all[2 * FOX_WIDTH:] + fb_ref[...])


def _pre_mixer(x, mod, gain, w_tok, w_t, fb):
    b, s, d = x.shape
    tm = TM_PRE
    nt = s // tm

    def tok(width):
        return pl.BlockSpec((1, tm, width), lambda bi, i: (bi, i, 0))

    def tok_shape(width):
        return jax.ShapeDtypeStruct((b, s, width), BF16)

    return pl.pallas_call(
        _pre_kernel,
        grid=(b, nt),
        in_specs=[
            tok(d),
            pl.BlockSpec((1, 6, d), lambda bi, i: (bi, 0, 0)),
            pl.BlockSpec((1, d), lambda bi, i: (0, 0)),
            pl.BlockSpec(w_tok.shape, lambda bi, i: (0, 0)),
            pl.BlockSpec(w_t.shape, lambda bi, i: (0, 0)),
            pl.BlockSpec((FORGET_ROWS, 1), lambda bi, i: (0, 0)),
        ],
        out_specs=[
            tok(SWA_WIDTH), tok(KV_WIDTH), tok(KV_WIDTH), tok(LRU_WIDTH), tok(LRU_WIDTH),
            tok(FOX_WIDTH),
            pl.BlockSpec((1, 1, FOX_WIDTH, tm), lambda bi, i: (bi, i, 0, 0)),
            pl.BlockSpec((1, 1, FOX_HEADS, PAIR, tm), lambda bi, i: (bi, i, 0, 0, 0)),
            pl.BlockSpec((1, FORGET_ROWS, tm), lambda bi, i: (bi, 0, i)),
        ],
        out_shape=[
            tok_shape(SWA_WIDTH), tok_shape(KV_WIDTH), tok_shape(KV_WIDTH), tok_shape(LRU_WIDTH),
            tok_shape(LRU_WIDTH), tok_shape(FOX_WIDTH),
            jax.ShapeDtypeStruct((b, nt, FOX_WIDTH, tm), BF16),
            jax.ShapeDtypeStruct((b, nt, FOX_HEADS, PAIR, tm), BF16),
            jax.ShapeDtypeStruct((b, FORGET_ROWS, s), F32),
        ],
        compiler_params=_params("parallel", "parallel"),
        name="pre_mixer",
    )(x, mod, gain, w_tok, w_t, fb)


def _cumsum_kernel(lf_ref, o_ref):
    n = CUMSUM_CHUNK
    s = lf_ref.shape[2]
    r = lax.broadcasted_iota(jnp.int32, (n, n), 0)
    c = lax.broadcasted_iota(jnp.int32, (n, n), 1)
    tri = jnp.where(r <= c, 1.0, 0.0).astype(BF16)
    carry = jnp.zeros((lf_ref.shape[1], 1), F32)
    for j in range(s // n):
        seg = lf_ref[0, :, j * n:(j + 1) * n]
        hi = seg.astype(BF16)
        rem = seg - hi.astype(F32)
        mid = rem.astype(BF16)
        lo = (rem - mid.astype(F32)).astype(BF16)
        cs = (_dot(hi, tri) + _dot(mid, tri)) + _dot(lo, tri) + carry
        o_ref[0, :, j * n:(j + 1) * n] = cs
        carry = cs[:, n - 1:n]


def _cumsum(lf):
    b, r, s = lf.shape
    return pl.pallas_call(
        _cumsum_kernel,
        grid=(b,),
        in_specs=[pl.BlockSpec((1, r, s), lambda bi: (bi, 0, 0))],
        out_specs=pl.BlockSpec((1, r, s), lambda bi: (bi, 0, 0)),
        out_shape=jax.ShapeDtypeStruct((b, r, s), F32),
        compiler_params=_params("parallel"),
        name="cumsum",
    )(lf)


def _swa_kernel(sink_ref, q_ref, kc_ref, vc_ref, kp_ref, vp_ref, o_ref, kf_ref, vf_ref, bias_ref):
    i = pl.program_id(1)
    nsub = q_ref.shape[1] // BLOCK
    npair = SWA_Q_HEADS // 2

    @pl.when((pl.program_id(0) == 0) & (i == 0))
    def _():
        qpos = BLOCK + lax.broadcasted_iota(jnp.int32, (BLOCK, 2 * BLOCK), 0)
        kpos = lax.broadcasted_iota(jnp.int32, (BLOCK, 2 * BLOCK), 1)
        dist = qpos - kpos
        valid = (dist >= 0) & (dist < BLOCK)
        distf = dist.astype(F32)
        for hd in range(SWA_Q_HEADS):
            slope = 2.0 ** (-8.0 * (hd + 1) / SWA_Q_HEADS)
            bias_ref[hd] = jnp.where(valid, -slope * distf, NEG)

    kf_ref[0:BLOCK, :] = kp_ref[0]
    kf_ref[BLOCK:, :] = kc_ref[0]
    vf_ref[0:BLOCK, :] = vp_ref[0]
    vf_ref[BLOCK:, :] = vc_ref[0]

    lane = lax.broadcasted_iota(jnp.int32, (2 * BLOCK, PAIR), 1)
    lane_o = lax.broadcasted_iota(jnp.int32, (BLOCK, PAIR), 1)
    col = lax.broadcasted_iota(jnp.int32, (BLOCK, 2 * BLOCK), 1)

    for j in range(nsub):
        k2 = kf_ref[j * BLOCK:(j + 2) * BLOCK, :]
        v2 = vf_ref[j * BLOCK:(j + 2) * BLOCK, :]
        k_half = (jnp.where(lane < HEAD_DIM, k2, jnp.zeros_like(k2)),
                  jnp.where(lane >= HEAD_DIM, k2, jnp.zeros_like(k2)))
        for p in range(npair):
            qp = q_ref[0, j * BLOCK:(j + 1) * BLOCK, p * PAIR:(p + 1) * PAIR]
            outs = []
            for a in range(2):
                hd = p + a * npair
                sc = _dot_nt(qp, k_half[a]) + bias_ref[hd]
                if j == 0:
                    sc = sc + jnp.where(col < BLOCK, jnp.where(i == 0, NEG, 0.0), 0.0)
                sink = sink_ref[0, hd]
                m = jnp.maximum(jnp.max(sc, axis=-1, keepdims=True), sink)
                pr = jnp.exp(sc - m)
                den = jnp.sum(pr, axis=-1, keepdims=True) + jnp.exp(sink - m)
                outs.append(_dot(pr.astype(BF16), v2) / den)
            o_ref[0, j * BLOCK:(j + 1) * BLOCK, p * PAIR:(p + 1) * PAIR] = jnp.where(
                lane_o < HEAD_DIM, outs[0], outs[1]).astype(BF16)


def _swa(qa, ka, va, sinks):
    b, s, _ = qa.shape
    tq = TQ_SWA
    nsub = tq // BLOCK
    prev = lambda bi, i: (bi, jnp.maximum(i * nsub - 1, 0), 0)
    return pl.pallas_call(
        _swa_kernel,
        grid=(b, s // tq),
        in_specs=[
            pl.BlockSpec(memory_space=pltpu.SMEM),
            pl.BlockSpec((1, tq, SWA_WIDTH), lambda bi, i: (bi, i, 0)),
            pl.BlockSpec((1, tq, KV_WIDTH), lambda bi, i: (bi, i, 0)),
            pl.BlockSpec((1, tq, KV_WIDTH), lambda bi, i: (bi, i, 0)),
            pl.BlockSpec((1, BLOCK, KV_WIDTH), prev),
            pl.BlockSpec((1, BLOCK, KV_WIDTH), prev),
        ],
        out_specs=pl.BlockSpec((1, tq, SWA_WIDTH), lambda bi, i: (bi, i, 0)),
        out_shape=jax.ShapeDtypeStruct((b, s, SWA_WIDTH), BF16),
        scratch_shapes=[
            pltpu.VMEM((tq + BLOCK, KV_WIDTH), BF16),
            pltpu.VMEM((tq + BLOCK, KV_WIDTH), BF16),
            pltpu.VMEM((SWA_Q_HEADS, BLOCK, 2 * BLOCK), F32),
        ],
        compiler_params=_params("arbitrary", "arbitrary"),
        name="swa",
    )(sinks, qa, ka, va, ka, va)


def _gelu_tanh(v):
    return 0.5 * v * (1.0 + jnp.tanh(0.7978845608028654 * (v + 0.044715 * (v * v * v))))


def _lru_kernel(xb_ref, gb_ref, cw_ref, cb_ref, wa_ref, ba_ref, wx_ref, bx_ref, lam_ref,
                o_ref, tail_ref, h_ref):
    ts = xb_ref.shape[1]

    @pl.when(pl.program_id(1) == 0)
    def _():
        tail_ref[...] = jnp.zeros_like(tail_ref)
        h_ref[...] = jnp.zeros_like(h_ref)

    x = xb_ref[0].astype(F32)
    tail = tail_ref[...]
    row8 = lax.broadcasted_iota(jnp.int32, (8, LRU_WIDTH), 0)
    conv = cb_ref[...] + cw_ref[CONV_WIDTH - 1:CONV_WIDTH, :] * x
    for k in range(1, CONV_WIDTH):
        rolled = pltpu.roll(x, k, 0)
        head = jnp.where(row8 < k, pltpu.roll(tail, k, 0), rolled[0:8])
        shifted = jnp.concatenate([head, rolled[8:]], axis=0)
        conv = conv + cw_ref[CONV_WIDTH - 1 - k:CONV_WIDTH - k, :] * shifted
    tail_ref[...] = x[ts - 8:ts]

    cbf = conv.astype(BF16)
    r = jax.nn.sigmoid(_dot(cbf, wa_ref[...]) + ba_ref[...])
    gi = jax.nn.sigmoid(_dot(cbf, wx_ref[...]) + bx_ref[...])
    z = -lam_ref[...]
    softplus = jnp.maximum(z, 0.0) + jnp.log1p(jnp.exp(-jnp.abs(z)))
    log_a = (-LRU_C * softplus) * r
    a = jnp.exp(log_a)
    u = jnp.sqrt(1.0 - a * a) * (gi * conv)

    row = lax.broadcasted_iota(jnp.int32, (ts, LRU_WIDTH), 0)
    step = 1
    while step < ts:
        keep = row >= step
        u = jnp.where(keep, a * pltpu.roll(u, step, 0) + u, u)
        a = jnp.where(keep, a * pltpu.roll(a, step, 0), a)
        step *= 2
    h = u + a * h_ref[7:8, :]
    h_ref[...] = h[ts - 8:ts]
    o_ref[0] = (h * _gelu_tanh(gb_ref[0].astype(F32))).astype(BF16)


def _lru(xb, gb, conv_w, conv_b, wa, ba, wx, bx, lam):
    b, s, w = xb.shape
    ts = TS_LRU
    tok = pl.BlockSpec((1, ts, w), lambda bi, i: (bi, i, 0))
    full = lambda shape: pl.BlockSpec(shape, lambda bi, i: (0,) * len(shape))
    return pl.pallas_call(
        _lru_kernel,
        grid=(b, s // ts),
        in_specs=[tok, tok, full((CONV_WIDTH, w)), full((1, w)), full((w, w)), full((1, w)),
                  full((w, w)), full((1, w)), full((1, w))],
        out_specs=tok,
        out_shape=jax.ShapeDtypeStruct((b, s, w), BF16),
        scratch_shapes=[pltpu.VMEM((8, w), F32), pltpu.VMEM((8, w), F32)],
        compiler_params=_params("arbitrary", "arbitrary"),
        name="lru",
    )(xb, gb, conv_w, conv_b, wa, ba, wx, bx, lam)


BIAS_ROWS = 6


def _split3(v):
    hi = v.astype(BF16).astype(F32)
    rem = v - hi
    mid = rem.astype(BF16).astype(F32)
    lo = (rem - mid).astype(BF16).astype(F32)
    return hi, mid, lo


def _fox_prep_kernel(k_ref, c_ref, o_ref):
    tm = k_ref.shape[1]
    lane = lax.broadcasted_iota(jnp.int32, (tm, PAIR), 1)
    for p in range(FOX_WIDTH // PAIR):
        kp = k_ref[0, :, p * PAIR:(p + 1) * PAIR].astype(F32)
        for a in range(2):
            hd = 2 * p + a
            c_hi, c_mid, c_lo = _split3(c_ref[0, :, hd:hd + 1] * LOG2E)
            rel = lane - _other_half(a)
            aug = jnp.where(rel == 0, c_hi, jnp.where(rel == 1, c_mid, jnp.where(
                rel == 2, c_lo, jnp.where(rel < BIAS_ROWS, 1.0, 0.0))))
            in_half = (lane >= HEAD_DIM * a) & (lane < HEAD_DIM * (a + 1))
            o_ref[0, hd] = jnp.where(in_half, kp, aug).astype(BF16)


def _fox_prep(kc, cum_col):
    b, s, w = kc.shape
    tm = TM_PRE
    return pl.pallas_call(
        _fox_prep_kernel,
        grid=(b, s // tm),
        in_specs=[
            pl.BlockSpec((1, tm, w), lambda bi, i: (bi, i, 0)),
            pl.BlockSpec((1, tm, cum_col.shape[2]), lambda bi, i: (bi, i, 0)),
        ],
        out_specs=pl.BlockSpec((1, FOX_HEADS, tm, PAIR), lambda bi, i: (bi, 0, i, 0)),
        out_shape=jax.ShapeDtypeStruct((b, FOX_HEADS, s, PAIR), BF16),
        compiler_params=_params("parallel", "parallel"),
        name="fox_prep",
    )(kc, cum_col)


def _fox_kernel(base_ref, qt_ref, ka_ref, vt_ref, o_ref, acc_ref, s0_ref, s1_ref):
    bi = pl.program_id(0)
    p = pl.program_id(1)
    qi = pl.program_id(2)
    nq = pl.num_programs(2)
    t = qt_ref.shape[3]
    qt = qt_ref[0, 0].astype(F32)
    row = lax.broadcasted_iota(jnp.int32, (PAIR, t), 0)
    causal = (lax.broadcasted_iota(jnp.int32, (t, t), 0) <= lax.broadcasted_iota(jnp.int32, (t, t), 1))
    q_aug = []
    for a in range(2):
        hd = 2 * p + a
        base = jnp.full((1, t), base_ref[(bi * FOX_HEADS + hd) * nq + qi], F32) * LOG2E
        b_hi, b_mid, b_lo = _split3(base)
        rel = row - _other_half(a)
        aug = jnp.where(rel < 3, -1.0, jnp.where(rel == 3, b_hi, jnp.where(
            rel == 4, b_mid, jnp.where(rel == 5, b_lo, 0.0))))
        in_half = (row >= HEAD_DIM * a) & (row < HEAD_DIM * (a + 1))
        q_aug.append(jnp.where(in_half, qt, aug).astype(BF16))
        acc_ref[a] = jnp.zeros((PAIR, t), F32)

    slots = (s0_ref, s1_ref)

    def scores(ki, slot, masked):
        tile_max = []
        for a in range(2):
            k = ka_ref[0, a, pl.ds(pl.multiple_of(ki * t, t), t), :]
            st = _dot(k, q_aug[a])
            if masked:
                st = jnp.where(causal, st, NEG)
            slots[slot][a] = st
            tile_max.append(jnp.max(st, axis=0, keepdims=True))
        return tuple(tile_max)

    def consume(ki, slot, m_old, tile_max):
        m_new = []
        for a in range(2):
            m_a = jnp.maximum(m_old[a], tile_max[a])
            alpha = jnp.exp2(m_old[a] - m_a)
            pr = jnp.exp2(slots[slot][a] - m_a).astype(BF16)
            acc_ref[a] = alpha * acc_ref[a] + _dot(vt_ref[0, ki, a], pr)
            m_new.append(m_a)
        return tuple(m_new)

    def advance(j, slot, carry):
        m_old, tile_max, prev = carry
        next_max = scores(j, 1 - slot, False)
        return consume(prev, slot, m_old, tile_max), next_max, j

    def body(i, carry):
        return advance(2 * i + 1, 1, advance(2 * i, 0, carry))

    m0 = jnp.full((1, t), NEG, F32)
    carry = ((m0, m0), scores(qi, 0, True), qi)
    carry = lax.fori_loop(0, qi // 2, body, carry)

    def finish_odd(carry):
        m, tile_max, prev = advance(qi - 1, 0, carry)
        consume(prev, 1, m, tile_max)
        return 0

    def finish_even(carry):
        m, tile_max, prev = carry
        consume(prev, 0, m, tile_max)
        return 0

    lax.cond(qi % 2 == 1, finish_odd, finish_even, carry)
    outs = []
    for a in range(2):
        acc = acc_ref[a]
        ob = _other_half(a)
        outs.append(acc / acc[ob:ob + 1, :])
    o_ref[0] = jnp.where(row < HEAD_DIM, outs[0], outs[1]).T.astype(BF16)


def _fox(qct, kaug, vaug, base):
    b, nt, w, t = qct.shape
    s = nt * t
    npair = w // PAIR
    return pl.pallas_call(
        _fox_kernel,
        grid=(b, npair, nt),
        in_specs=[
            pl.BlockSpec(memory_space=pltpu.SMEM),
            pl.BlockSpec((1, 1, PAIR, t), lambda bi, p, qi: (bi, qi, p, 0)),
            pl.BlockSpec((1, 2, s, PAIR), lambda bi, p, qi: (bi, p, 0, 0)),
            pl.BlockSpec((1, nt, 2, PAIR, t), lambda bi, p, qi: (bi, 0, p, 0, 0)),
        ],
        out_specs=pl.BlockSpec((1, t, PAIR), lambda bi, p, qi: (bi, qi, p)),
        out_shape=jax.ShapeDtypeStruct((b, s, w), BF16),
        scratch_shapes=[pltpu.VMEM((2, PAIR, t), F32), pltpu.VMEM((2, t, t), F32),
                        pltpu.VMEM((2, t, t), F32)],
        compiler_params=_params("parallel", "parallel", "arbitrary"),
        name="fox",
    )(base, qct, kaug, vaug)


def _first_max(vals):
    best = vals[0]
    for v in vals[1:]:
        best = jnp.maximum(best, v)
    idx = jnp.full(best.shape, len(vals) - 1, jnp.int32)
    for j in range(len(vals) - 2, -1, -1):
        idx = jnp.where(vals[j] == best, j, idx)
    return best, idx


def _softmax_rows(vals):
    top = vals[0]
    for v in vals[1:]:
        top = jnp.maximum(top, v)
    ex = [jnp.exp(v - top) for v in vals]
    tot = ex[0]
    for e in ex[1:]:
        tot = tot + e
    return [e / tot for e in ex]


def _post_kernel(ya_ref, yb_ref, yc_ref, x_ref, mod_ref, ga_ref, gb_ref, gc_ref, wo_ref, nf_ref,
                 wr_hi_ref, wr_lo_ref, br_ref, xo_ref, h2_ref, comb_ref):
    ya = _rms_normalize(ya_ref[0].astype(F32)) * ga_ref[...]
    yb = _rms_normalize(yb_ref[0].astype(F32)) * gb_ref[...]
    yc = _rms_normalize(yc_ref[0].astype(F32)) * gc_ref[...]
    y = jnp.concatenate([ya, yb, yc], axis=-1).astype(BF16)
    x = x_ref[0] + mod_ref[0, 2:3, :] * _dot(y, wo_ref[...])
    xo_ref[0] = x
    h = (_rms_normalize(x) * nf_ref[...]) * (1.0 + mod_ref[0, 4:5, :]) + mod_ref[0, 3:4, :]
    h_hi, h_lo = _split_bf16(h)
    h2_ref[0] = h_hi

    w_hi = wr_hi_ref[...]
    logits = (_dot_nt(w_hi, h_hi) + _dot_nt(w_hi, h_lo)) + _dot_nt(wr_lo_ref[...], h_hi) + br_ref[...]
    rows = [logits[r:r + 1, :] for r in range(8 + N_EXPERTS)]
    group_prob = _softmax_rows(rows[0:N_GROUPS])
    group_p, group_idx = _first_max(group_prob)
    in_group = []
    for j in range(EXPERTS_PER_GROUP):
        v = rows[8 + (N_GROUPS - 1) * EXPERTS_PER_GROUP + j]
        for g in range(N_GROUPS - 2, -1, -1):
            v = jnp.where(group_idx == g, rows[8 + g * EXPERTS_PER_GROUP + j], v)
        in_group.append(v)
    expert_prob = _softmax_rows(in_group)
    p1, i1 = _first_max(expert_prob)
    rest = [jnp.where(i1 == j, -1.0, expert_prob[j]) for j in range(EXPERTS_PER_GROUP)]
    p2, i2 = _first_max(rest)
    w1 = group_p * p1 / (p1 + p2)
    w2 = group_p * p2 / (p1 + p2)
    for j in range(EXPERTS_PER_GROUP):
        wj = jnp.where(i1 == j, w1, jnp.where(i2 == j, w2, 0.0))
        for g in range(N_GROUPS):
            e = g * EXPERTS_PER_GROUP + j
            comb_ref[0, e:e + 1, :] = jnp.where(group_idx == g, wj, 0.0)


def _post_mixer(ya, yb, yc, x, mod, gain_a, gain_b, gain_c, w_out, norm_ffn, wr_hi, wr_lo, br):
    b, s, d = x.shape
    tm = TM_POST
    tok = lambda width: pl.BlockSpec((1, tm, width), lambda bi, i: (bi, i, 0))
    full = lambda shape: pl.BlockSpec(shape, lambda bi, i: (0,) * len(shape))
    return pl.pallas_call(
        _post_kernel,
        grid=(b, s // tm),
        in_specs=[
            tok(SWA_WIDTH), tok(LRU_WIDTH), tok(FOX_WIDTH), tok(d),
            pl.BlockSpec((1, 6, d), lambda bi, i: (bi, 0, 0)),
            full((1, SWA_WIDTH)), full((1, LRU_WIDTH)), full((1, FOX_WIDTH)),
            full((d, d)), full((1, d)),
            full((ROUTER_ROWS, d)), full((ROUTER_ROWS, d)), full((ROUTER_ROWS, 1)),
        ],
        out_specs=[
            tok(d), tok(d),
            pl.BlockSpec((1, N_EXPERTS, tm), lambda bi, i: (bi, 0, i)),
        ],
        out_shape=[
            jax.ShapeDtypeStruct((b, s, d), F32),
            jax.ShapeDtypeStruct((b, s, d), BF16),
            jax.ShapeDtypeStruct((b, N_EXPERTS, s), F32),
        ],
        compiler_params=_params("parallel", "parallel"),
        name="post_mixer",
    )(ya, yb, yc, x, mod, gain_a, gain_b, gain_c, w_out, norm_ffn, wr_hi, wr_lo, br)


def _moe_kernel(h_ref, comb_ref, wg_ref, wu_ref, wd_ref, x_ref, mod_ref, nf_ref, o_ref, acc_ref,
                hid_ref, *, final):
    g = pl.program_id(2)

    @pl.when(g == 0)
    def _():
        acc_ref[...] = jnp.zeros_like(acc_ref)

    h = h_ref[0]
    comb = comb_ref[0]
    lane = lax.broadcasted_iota(jnp.int32, comb.shape, 1)
    for j in range(EXPERTS_PER_GROUP):
        cols = slice(j * D_EXPERT, (j + 1) * D_EXPERT)
        gate = _dot(h, wg_ref[0, :, cols])
        hidden = (gate * jax.nn.sigmoid(gate)) * _dot(h, wu_ref[0, :, cols])
        ce = jnp.sum(jnp.where(lane == g * EXPERTS_PER_GROUP + j, comb, 0.0), axis=-1, keepdims=True)
        hid_ref[:, cols] = (hidden * ce).astype(BF16)
    acc_ref[...] += _dot(hid_ref[...], wd_ref[0])

    @pl.when(g == N_GROUPS - 1)
    def _():
        x = x_ref[0] + mod_ref[0, 5:6, :] * acc_ref[...]
        if final:
            x = _rms_normalize(x) * nf_ref[...]
        o_ref[0] = x


def _moe(h2, comb, w_gate, w_up, w_down, x, mod, norm_final, final):
    b, s, d = x.shape
    tm = TM_MOE
    gw = EXPERTS_PER_GROUP * D_EXPERT
    tok = lambda width: pl.BlockSpec((1, tm, width), lambda bi, i, g: (bi, i, 0))
    return pl.pallas_call(
        functools.partial(_moe_kernel, final=final),
        grid=(b, s // tm, N_GROUPS),
        in_specs=[
            tok(d), tok(N_EXPERTS),
            pl.BlockSpec((1, d, gw), lambda bi, i, g: (g, 0, 0)),
            pl.BlockSpec((1, d, gw), lambda bi, i, g: (g, 0, 0)),
            pl.BlockSpec((1, gw, d), lambda bi, i, g: (g, 0, 0)),
            tok(d),
            pl.BlockSpec((1, 6, d), lambda bi, i, g: (bi, 0, 0)),
            pl.BlockSpec((1, d), lambda bi, i, g: (0, 0)),
        ],
        out_specs=tok(d),
        out_shape=jax.ShapeDtypeStruct((b, s, d), F32),
        scratch_shapes=[pltpu.VMEM((tm, d), F32), pltpu.VMEM((tm, gw), BF16)],
        compiler_params=_params("parallel", "parallel", "arbitrary"),
        name="moe_final" if final else "moe",
    )(h2, comb, w_gate, w_up, w_down, x, mod, norm_final)


def _block_diag(w):
    nb, c, _ = w.shape
    eye = jnp.eye(nb, dtype=w.dtype)
    return (eye[:, None, :, None] * w[:, :, None, :]).reshape(nb * c, nb * c)


def _group_columns(w):
    e, d, f = w.shape
    w = w.reshape(N_GROUPS, EXPERTS_PER_GROUP, d, f).transpose(0, 2, 1, 3)
    return w.reshape(N_GROUPS, d, EXPERTS_PER_GROUP * f).astype(BF16)


def _swa_head_perm():
    cols = []
    half = SWA_Q_HEADS // 2
    for p in range(half):
        for hd in (p, p + half):
            cols.extend(range(hd * HEAD_DIM, (hd + 1) * HEAD_DIM))
    return jnp.asarray(cols, jnp.int32)


def kernel(x, c, w_mod, b_mod, norm_mix, norm_ffn, w_in, w_out, out_gain, sinks, conv_w, conv_b,
           lru_wa, lru_ba, lru_wx, lru_bx, lru_lam, fox_bf, w_router_group, b_router_group,
           w_router_expert, b_router_expert, w_gate, w_up, w_down, norm_final):
    depth = w_mod.shape[0]
    b, s, d = x.shape
    assert d == D_MODEL and s % max(TM_PRE, TQ_SWA, TS_LRU, T_FOX, TM_POST, TM_MOE) == 0
    assert TM_PRE == T_FOX
    c_rows = jnp.zeros((16, d), F32).at[:b].set(c)
    mod_all = _modulation(c_rows, w_mod, b_mod)[:, :b]
    perm = _swa_head_perm()

    for l in range(depth):
        mod = mod_all[l].reshape(b, 6, d)
        w_in_l = w_in[l]
        w_tok = jnp.concatenate([w_in_l[:, :SWA_WIDTH][:, perm], w_in_l[:, SWA_WIDTH:OFF_QC],
                                 w_in_l[:, OFF_KC:OFF_VC]], axis=1).astype(BF16)
        w_t = jnp.concatenate([w_in_l[:, OFF_QC:OFF_KC].T, w_in_l[:, OFF_VC:OFF_FC].T,
                               jnp.zeros((FORGET_ROWS, d), F32).at[:FOX_HEADS].set(w_in_l[:, OFF_FC:].T)],
                              axis=0).astype(BF16)
        fb = jnp.zeros((FORGET_ROWS, 1), F32).at[:FOX_HEADS, 0].set(fox_bf[l])
        gain = out_gain[l]
        gain_a = gain[:SWA_WIDTH][perm].reshape(1, SWA_WIDTH)
        gain_b = gain[SWA_WIDTH:SWA_WIDTH + LRU_WIDTH].reshape(1, LRU_WIDTH)
        gain_c = gain[SWA_WIDTH + LRU_WIDTH:].reshape(1, FOX_WIDTH)
        w_out_l = jnp.concatenate([w_out[l][:SWA_WIDTH][perm], w_out[l][SWA_WIDTH:]], axis=0).astype(BF16)
        wr = jnp.zeros((ROUTER_ROWS, d), F32)
        wr = wr.at[:N_GROUPS].set(w_router_group[l].T).at[8:8 + N_EXPERTS].set(w_router_expert[l].T)
        wr_hi = wr.astype(BF16)
        wr_lo = (wr - wr_hi.astype(F32)).astype(BF16)
        br = jnp.zeros((ROUTER_ROWS, 1), F32)
        br = br.at[:N_GROUPS, 0].set(b_router_group[l]).at[8:8 + N_EXPERTS, 0].set(b_router_expert[l])

        qa, ka, va, xb, gb, kc, qct, vaug, lf = _pre_mixer(
            x, mod, norm_mix[l].reshape(1, d), w_tok, w_t, fb)
        cum = _cumsum(lf)
        kaug = _fox_prep(kc, jnp.swapaxes(cum, 1, 2))
        base = cum[:, :FOX_HEADS, ::T_FOX].reshape(-1)
        ya = _swa(qa, ka, va, sinks[l].reshape(1, SWA_Q_HEADS))
        yb = _lru(xb, gb, conv_w[l], conv_b[l].reshape(1, -1),
                  _block_diag(lru_wa[l]).astype(BF16), lru_ba[l].reshape(1, -1),
                  _block_diag(lru_wx[l]).astype(BF16), lru_bx[l].reshape(1, -1),
                  lru_lam[l].reshape(1, -1))
        yc = _fox(qct, kaug, vaug, base)
        x, h2, comb_t = _post_mixer(ya, yb, yc, x, mod, gain_a, gain_b, gain_c, w_out_l,
                                    norm_ffn[l].reshape(1, d), wr_hi, wr_lo, br)
        comb = jnp.swapaxes(comb_t, 1, 2)
        x = _moe(h2, comb, _group_columns(w_gate[l]), _group_columns(w_up[l]),
                 w_down[l].reshape(N_GROUPS, EXPERTS_PER_GROUP * D_EXPERT, d).astype(BF16),
                 x, mod, norm_final.reshape(1, d), final=(l == depth - 1))
    return x
```

```python
import functools

import jax
import jax.numpy as jnp
from jax import lax
from jax.experimental import pallas as pl
from jax.experimental.pallas import tpu as pltpu

F32 = jnp.float32
BF16 = jnp.bfloat16

D_MODEL = 1024
HEAD_DIM = 64
PAIR = 2 * HEAD_DIM
BLOCK = 128
SWA_Q_HEADS = 8
SWA_WIDTH = SWA_Q_HEADS * HEAD_DIM
KV_WIDTH = 2 * HEAD_DIM
LRU_WIDTH = 256
LRU_BLOCKS = 8
CONV_WIDTH = 4
LRU_C = 8.0
FOX_HEADS = 4
FOX_WIDTH = FOX_HEADS * HEAD_DIM
N_GROUPS = 4
EXPERTS_PER_GROUP = 4
N_EXPERTS = 16
D_EXPERT = 256
EPS = 1e-6
NEG = -1e30
LOG2E = 1.4426950408889634

OFF_KA = SWA_WIDTH
OFF_VA = OFF_KA + KV_WIDTH
OFF_XB = OFF_VA + KV_WIDTH
OFF_GB = OFF_XB + LRU_WIDTH
OFF_QC = OFF_GB + LRU_WIDTH
OFF_KC = OFF_QC + FOX_WIDTH
OFF_VC = OFF_KC + FOX_WIDTH
OFF_FC = OFF_VC + FOX_WIDTH

ROUTER_ROWS = 32
FORGET_ROWS = 16
VMEM_LIMIT_BYTES = 48 * 1024 * 1024

TM_PRE = 512
TQ_SWA = 512
TS_LRU = 512
T_FOX = 512
TM_POST = 512
CUMSUM_CHUNK = 256


def _params(*semantics):
    return pltpu.CompilerParams(dimension_semantics=semantics, vmem_limit_bytes=VMEM_LIMIT_BYTES)


def _split_bf16(v):
    hi = v.astype(BF16)
    lo = (v - hi.astype(F32)).astype(BF16)
    return hi, lo


def _dot(a, b):
    return jnp.dot(a, b, preferred_element_type=F32)


def _dot_nt(a, b):
    return lax.dot_general(a, b, (((1,), (1,)), ((), ())), preferred_element_type=F32)


def _rms_normalize(v):
    return v * lax.rsqrt(jnp.mean(v * v, axis=-1, keepdims=True) + EPS)


def _mod_kernel(c_ref, w_ref, b_ref, o_ref):
    c = c_ref[...]
    ca = c * jax.nn.sigmoid(c)
    w = w_ref[0]
    a_hi, a_lo = _split_bf16(ca)
    w_hi, w_lo = _split_bf16(w)
    o_ref[0] = _dot(a_hi, w_hi) + _dot(a_lo, w_hi) + _dot(a_hi, w_lo) + b_ref[0]


def _modulation(c, w_mod, b_mod):
    depth, d, d6 = w_mod.shape
    b = c.shape[0]
    n = d6 // d
    return pl.pallas_call(
        _mod_kernel,
        grid=(depth, n),
        in_specs=[
            pl.BlockSpec((b, d), lambda l, j: (0, 0)),
            pl.BlockSpec((1, d, d), lambda l, j: (l, 0, j)),
            pl.BlockSpec((1, 1, d), lambda l, j: (l, 0, j)),
        ],
        out_specs=pl.BlockSpec((1, b, d), lambda l, j: (l, 0, j)),
        out_shape=jax.ShapeDtypeStruct((depth, b, d6), F32),
        compiler_params=_params("arbitrary", "arbitrary"),
        name="modulation",
    )(c, w_mod, b_mod.reshape(depth, 1, d6))


def _log_sigmoid(z):
    return jnp.minimum(z, 0.0) - jnp.log1p(jnp.exp(-jnp.abs(z)))


def _other_half(a):
    return HEAD_DIM * (1 - a)


def _pre_kernel(x_ref, mod_ref, g_ref, w_ref, wt_ref, fb_ref,
                qa_ref, ka_ref, va_ref, xb_ref, gb_ref, kc_ref, qct_ref, vct_ref, lf_ref):
    x = x_ref[0]
    h = (_rms_normalize(x) * g_ref[...]) * (1.0 + mod_ref[0, 1:2, :]) + mod_ref[0, 0:1, :]
    hb = h.astype(BF16)

    def proj(lo, width):
        return _dot(hb, w_ref[:, lo:lo + width])

    scale = HEAD_DIM ** -0.5
    qa_ref[0] = (proj(0, SWA_WIDTH) * scale).astype(BF16)
    ka_ref[0] = proj(OFF_KA, KV_WIDTH).astype(BF16)
    va_ref[0] = proj(OFF_VA, KV_WIDTH).astype(BF16)
    xb_ref[0] = proj(OFF_XB, LRU_WIDTH).astype(BF16)
    gb_ref[0] = proj(OFF_GB, LRU_WIDTH).astype(BF16)
    kc_ref[0] = proj(OFF_QC, FOX_WIDTH).astype(BF16)

    t_all = _dot_nt(wt_ref[...], hb)
    qct_ref[0, 0] = (t_all[0:FOX_WIDTH] * (scale * LOG2E)).astype(BF16)
    row = lax.broadcasted_iota(jnp.int32, (PAIR, t_all.shape[1]), 0)
    for p in range(FOX_WIDTH // PAIR):
        v_pair = t_all[FOX_WIDTH + p * PAIR:FOX_WIDTH + (p + 1) * PAIR]
        for a in range(2):
            vct_ref[0, 0, 2 * p + a] = jnp.where(row == _other_half(a), 1.0, v_pair).astype(BF16)
    lf_ref[0] = _log_sigmoid(t_all[2 * FOX_WIDTH:] + fb_ref[...])


def _pre_mixer(x, mod, gain, w_tok, w_t, fb):
    b, s, d = x.shape
    tm = TM_PRE
    nt = s // tm

    def tok(width):
        return pl.BlockSpec((1, tm, width), lambda bi, i: (bi, i, 0))

    def tok_shape(width):
        return jax.ShapeDtypeStruct((b, s, width), BF16)

    return pl.pallas_call(
        _pre_kernel,
        grid=(b, nt),
        in_specs=[
            tok(d),
            pl.BlockSpec((1, 6, d), lambda bi, i: (bi, 0, 0)),
            pl.BlockSpec((1, d), lambda bi, i: (0, 0)),
            pl.BlockSpec(w_tok.shape, lambda bi, i: (0, 0)),
            pl.BlockSpec(w_t.shape, lambda bi, i: (0, 0)),
            pl.BlockSpec((FORGET_ROWS, 1), lambda bi, i: (0, 0)),
        ],
        out_specs=[
            tok(SWA_WIDTH), tok(KV_WIDTH), tok(KV_WIDTH), tok(LRU_WIDTH), tok(LRU_WIDTH),
            tok(FOX_WIDTH),
            pl.BlockSpec((1, 1, FOX_WIDTH, tm), lambda bi, i: (bi, i, 0, 0)),
            pl.BlockSpec((1, 1, FOX_HEADS, PAIR, tm), lambda bi, i: (bi, i, 0, 0, 0)),
            pl.BlockSpec((1, FORGET_ROWS, tm), lambda bi, i: (bi, 0, i)),
        ],
        out_shape=[
            tok_shape(SWA_WIDTH), tok_shape(KV_WIDTH), tok_shape(KV_WIDTH), tok_shape(LRU_WIDTH),
            tok_shape(LRU_WIDTH), tok_shape(FOX_WIDTH),
            jax.ShapeDtypeStruct((b, nt, FOX_WIDTH, tm), BF16),
            jax.ShapeDtypeStruct((b, nt, FOX_HEADS, PAIR, tm), BF16),
            jax.ShapeDtypeStruct((b, FORGET_ROWS, s), F32),
        ],
        compiler_params=_params("parallel", "parallel"),
        name="pre_mixer",
    )(x, mod, gain, w_tok, w_t, fb)


def _cumsum_kernel(lf_ref, o_ref):
    n = CUMSUM_CHUNK
    s = lf_ref.shape[2]
    r = lax.broadcasted_iota(jnp.int32, (n, n), 0)
    c = lax.broadcasted_iota(jnp.int32, (n, n), 1)
    tri = jnp.where(r <= c, 1.0, 0.0).astype(BF16)
    carry = jnp.zeros((lf_ref.shape[1], 1), F32)
    for j in range(s // n):
        seg = lf_ref[0, :, j * n:(j + 1) * n]
        hi = seg.astype(BF16)
        rem = seg - hi.astype(F32)
        mid = rem.astype(BF16)
        lo = (rem - mid.astype(F32)).astype(BF16)
        cs = (_dot(hi, tri) + _dot(mid, tri)) + _dot(lo, tri) + carry
        o_ref[0, :, j * n:(j + 1) * n] = cs
        carry = cs[:, n - 1:n]


def _cumsum(lf):
    b, r, s = lf.shape
    return pl.pallas_call(
        _cumsum_kernel,
        grid=(b,),
        in_specs=[pl.BlockSpec((1, r, s), lambda bi: (bi, 0, 0))],
        out_specs=pl.BlockSpec((1, r, s), lambda bi: (bi, 0, 0)),
        out_shape=jax.ShapeDtypeStruct((b, r, s), F32),
        compiler_params=_params("parallel"),
        name="cumsum",
    )(lf)


def _swa_kernel(sink_ref, q_ref, kc_ref, vc_ref, kp_ref, vp_ref, o_ref, kf_ref, vf_ref, bias_ref):
    i = pl.program_id(1)
    nsub = q_ref.shape[1] // BLOCK
    npair = SWA_Q_HEADS // 2

    @pl.when((pl.program_id(0) == 0) & (i == 0))
    def _():
        qpos = BLOCK + lax.broadcasted_iota(jnp.int32, (BLOCK, 2 * BLOCK), 0)
        kpos = lax.broadcasted_iota(jnp.int32, (BLOCK, 2 * BLOCK), 1)
        dist = qpos - kpos
        valid = (dist >= 0) & (dist < BLOCK)
        distf = dist.astype(F32)
        for hd in range(SWA_Q_HEADS):
            slope = 2.0 ** (-8.0 * (hd + 1) / SWA_Q_HEADS)
            bias_ref[hd] = jnp.where(valid, -slope * distf, NEG)

    kf_ref[0:BLOCK, :] = kp_ref[0]
    kf_ref[BLOCK:, :] = kc_ref[0]
    vf_ref[0:BLOCK, :] = vp_ref[0]
    vf_ref[BLOCK:, :] = vc_ref[0]

    lane = lax.broadcasted_iota(jnp.int32, (2 * BLOCK, PAIR), 1)
    lane_o = lax.broadcasted_iota(jnp.int32, (BLOCK, PAIR), 1)
    col = lax.broadcasted_iota(jnp.int32, (BLOCK, 2 * BLOCK), 1)

    for j in range(nsub):
        k2 = kf_ref[j * BLOCK:(j + 2) * BLOCK, :]
        v2 = vf_ref[j * BLOCK:(j + 2) * BLOCK, :]
        k_half = (jnp.where(lane < HEAD_DIM, k2, jnp.zeros_like(k2)),
                  jnp.where(lane >= HEAD_DIM, k2, jnp.zeros_like(k2)))
        for p in range(npair):
            qp = q_ref[0, j * BLOCK:(j + 1) * BLOCK, p * PAIR:(p + 1) * PAIR]
            outs = []
            for a in range(2):
                hd = p + a * npair
                sc = _dot_nt(qp, k_half[a]) + bias_ref[hd]
                if j == 0:
                    sc = sc + jnp.where(col < BLOCK, jnp.where(i == 0, NEG, 0.0), 0.0)
                sink = sink_ref[0, hd]
                m = jnp.maximum(jnp.max(sc, axis=-1, keepdims=True), sink)
                pr = jnp.exp(sc - m)
                den = jnp.sum(pr, axis=-1, keepdims=True) + jnp.exp(sink - m)
                outs.append(_dot(pr.astype(BF16), v2) / den)
            o_ref[0, j * BLOCK:(j + 1) * BLOCK, p * PAIR:(p + 1) * PAIR] = jnp.where(
                lane_o < HEAD_DIM, outs[0], outs[1]).astype(BF16)


def _swa(qa, ka, va, sinks):
    b, s, _ = qa.shape
    tq = TQ_SWA
    nsub = tq // BLOCK
    prev = lambda bi, i: (bi, jnp.maximum(i * nsub - 1, 0), 0)
    return pl.pallas_call(
        _swa_kernel,
        grid=(b, s // tq),
        in_specs=[
            pl.BlockSpec(memory_space=pltpu.SMEM),
            pl.BlockSpec((1, tq, SWA_WIDTH), lambda bi, i: (bi, i, 0)),
            pl.BlockSpec((1, tq, KV_WIDTH), lambda bi, i: (bi, i, 0)),
            pl.BlockSpec((1, tq, KV_WIDTH), lambda bi, i: (bi, i, 0)),
            pl.BlockSpec((1, BLOCK, KV_WIDTH), prev),
            pl.BlockSpec((1, BLOCK, KV_WIDTH), prev),
        ],
        out_specs=pl.BlockSpec((1, tq, SWA_WIDTH), lambda bi, i: (bi, i, 0)),
        out_shape=jax.ShapeDtypeStruct((b, s, SWA_WIDTH), BF16),
        scratch_shapes=[
            pltpu.VMEM((tq + BLOCK, KV_WIDTH), BF16),
            pltpu.VMEM((tq + BLOCK, KV_WIDTH), BF16),
            pltpu.VMEM((SWA_Q_HEADS, BLOCK, 2 * BLOCK), F32),
        ],
        compiler_params=_params("arbitrary", "arbitrary"),
        name="swa",
    )(sinks, qa, ka, va, ka, va)


def _gelu_tanh(v):
    return 0.5 * v * (1.0 + jnp.tanh(0.7978845608028654 * (v + 0.044715 * (v * v * v))))


def _lru_kernel(xb_ref, gb_ref, cw_ref, cb_ref, wa_ref, ba_ref, wx_ref, bx_ref, lam_ref,
                o_ref, tail_ref, h_ref):
    ts = xb_ref.shape[1]

    @pl.when(pl.program_id(1) == 0)
    def _():
        tail_ref[...] = jnp.zeros_like(tail_ref)
        h_ref[...] = jnp.zeros_like(h_ref)

    x = xb_ref[0].astype(F32)
    tail = tail_ref[...]
    row8 = lax.broadcasted_iota(jnp.int32, (8, LRU_WIDTH), 0)
    conv = cb_ref[...] + cw_ref[CONV_WIDTH - 1:CONV_WIDTH, :] * x
    for k in range(1, CONV_WIDTH):
        rolled = pltpu.roll(x, k, 0)
        head = jnp.where(row8 < k, pltpu.roll(tail, k, 0), rolled[0:8])
        shifted = jnp.concatenate([head, rolled[8:]], axis=0)
        conv = conv + cw_ref[CONV_WIDTH - 1 - k:CONV_WIDTH - k, :] * shifted
    tail_ref[...] = x[ts - 8:ts]

    cbf = conv.astype(BF16)
    r = jax.nn.sigmoid(_dot(cbf, wa_ref[...]) + ba_ref[...])
    gi = jax.nn.sigmoid(_dot(cbf, wx_ref[...]) + bx_ref[...])
    z = -lam_ref[...]
    softplus = jnp.maximum(z, 0.0) + jnp.log1p(jnp.exp(-jnp.abs(z)))
    log_a = (-LRU_C * softplus) * r
    a = jnp.exp(log_a)
    u = jnp.sqrt(1.0 - a * a) * (gi * conv)

    row = lax.broadcasted_iota(jnp.int32, (ts, LRU_WIDTH), 0)
    step = 1
    while step < ts:
        keep = row >= step
        u = jnp.where(keep, a * pltpu.roll(u, step, 0) + u, u)
        a = jnp.where(keep, a * pltpu.roll(a, step, 0), a)
        step *= 2
    h = u + a * h_ref[7:8, :]
    h_ref[...] = h[ts - 8:ts]
    o_ref[0] = (h * _gelu_tanh(gb_ref[0].astype(F32))).astype(BF16)


def _lru(xb, gb, conv_w, conv_b, wa, ba, wx, bx, lam):
    b, s, w = xb.shape
    ts = TS_LRU
    tok = pl.BlockSpec((1, ts, w), lambda bi, i: (bi, i, 0))
    full = lambda shape: pl.BlockSpec(shape, lambda bi, i: (0,) * len(shape))
    return pl.pallas_call(
        _lru_kernel,
        grid=(b, s // ts),
        in_specs=[tok, tok, full((CONV_WIDTH, w)), full((1, w)), full((w, w)), full((1, w)),
                  full((w, w)), full((1, w)), full((1, w))],
        out_specs=tok,
        out_shape=jax.ShapeDtypeStruct((b, s, w), BF16),
        scratch_shapes=[pltpu.VMEM((8, w), F32), pltpu.VMEM((8, w), F32)],
        compiler_params=_params("arbitrary", "arbitrary"),
        name="lru",
    )(xb, gb, conv_w, conv_b, wa, ba, wx, bx, lam)


BIAS_ROWS = 6


def _split3(v):
    hi = v.astype(BF16).astype(F32)
    rem = v - hi
    mid = rem.astype(BF16).astype(F32)
    lo = (rem - mid).astype(BF16).astype(F32)
    return hi, mid, lo


def _fox_prep_kernel(k_ref, c_ref, o_ref):
    tm = k_ref.shape[1]
    lane = lax.broadcasted_iota(jnp.int32, (tm, PAIR), 1)
    for p in range(FOX_WIDTH // PAIR):
        kp = k_ref[0, :, p * PAIR:(p + 1) * PAIR].astype(F32)
        for a in range(2):
            hd = 2 * p + a
            c_hi, c_mid, c_lo = _split3(c_ref[0, :, hd:hd + 1] * LOG2E)
            rel = lane - _other_half(a)
            aug = jnp.where(rel == 0, c_hi, jnp.where(rel == 1, c_mid, jnp.where(
                rel == 2, c_lo, jnp.where(rel < BIAS_ROWS, 1.0, 0.0))))
            in_half = (lane >= HEAD_DIM * a) & (lane < HEAD_DIM * (a + 1))
            o_ref[0, hd] = jnp.where(in_half, kp, aug).astype(BF16)


def _fox_prep(kc, cum_col):
    b, s, w = kc.shape
    tm = TM_PRE
    return pl.pallas_call(
        _fox_prep_kernel,
        grid=(b, s // tm),
        in_specs=[
            pl.BlockSpec((1, tm, w), lambda bi, i: (bi, i, 0)),
            pl.BlockSpec((1, tm, cum_col.shape[2]), lambda bi, i: (bi, i, 0)),
        ],
        out_specs=pl.BlockSpec((1, FOX_HEADS, tm, PAIR), lambda bi, i: (bi, 0, i, 0)),
        out_shape=jax.ShapeDtypeStruct((b, FOX_HEADS, s, PAIR), BF16),
        compiler_params=_params("parallel", "parallel"),
        name="fox_prep",
    )(kc, cum_col)


def _fox_kernel(base_ref, qt_ref, ka_ref, vt_ref, o_ref, acc_ref, s0_ref, s1_ref):
    bi = pl.program_id(0)
    p = pl.program_id(1)
    qi = pl.program_id(2)
    nq = pl.num_programs(2)
    t = qt_ref.shape[3]
    qt = qt_ref[0, 0].astype(F32)
    row = lax.broadcasted_iota(jnp.int32, (PAIR, t), 0)
    causal = (lax.broadcasted_iota(jnp.int32, (t, t), 0) <= lax.broadcasted_iota(jnp.int32, (t, t), 1))
    q_aug = []
    for a in range(2):
        hd = 2 * p + a
        base = jnp.full((1, t), base_ref[(bi * FOX_HEADS + hd) * nq + qi], F32) * LOG2E
        b_hi, b_mid, b_lo = _split3(base)
        rel = row - _other_half(a)
        aug = jnp.where(rel < 3, -1.0, jnp.where(rel == 3, b_hi, jnp.where(
            rel == 4, b_mid, jnp.where(rel == 5, b_lo, 0.0))))
        in_half = (row >= HEAD_DIM * a) & (row < HEAD_DIM * (a + 1))
        q_aug.append(jnp.where(in_half, qt, aug).astype(BF16))
        acc_ref[a] = jnp.zeros((PAIR, t), F32)

    slots = (s0_ref, s1_ref)

    def scores(ki, slot, masked):
        tile_max = []
        for a in range(2):
            k = ka_ref[0, a, pl.ds(pl.multiple_of(ki * t, t), t), :]
            st = _dot(k, q_aug[a])
            if masked:
                st = jnp.where(causal, st, NEG)
            slots[slot][a] = st
            tile_max.append(jnp.max(st, axis=0, keepdims=True))
        return tuple(tile_max)

    def consume(ki, slot, m_old, tile_max):
        m_new = []
        for a in range(2):
            m_a = jnp.maximum(m_old[a], tile_max[a])
            alpha = jnp.exp2(m_old[a] - m_a)
            pr = jnp.exp2(slots[slot][a] - m_a).astype(BF16)
            acc_ref[a] = alpha * acc_ref[a] + _dot(vt_ref[0, ki, a], pr)
            m_new.append(m_a)
        return tuple(m_new)

    def advance(j, slot, carry):
        m_old, tile_max, prev = carry
        next_max = scores(j, 1 - slot, False)
        return consume(prev, slot, m_old, tile_max), next_max, j

    def body(i, carry):
        return advance(2 * i + 1, 1, advance(2 * i, 0, carry))

    m0 = jnp.full((1, t), NEG, F32)
    carry = ((m0, m0), scores(qi, 0, True), qi)
    carry = lax.fori_loop(0, qi // 2, body, carry)

    def finish_odd(carry):
        m, tile_max, prev = advance(qi - 1, 0, carry)
        consume(prev, 1, m, tile_max)
        return 0

    def finish_even(carry):
        m, tile_max, prev = carry
        consume(prev, 0, m, tile_max)
        return 0

    lax.cond(qi % 2 == 1, finish_odd, finish_even, carry)
    outs = []
    for a in range(2):
        acc = acc_ref[a]
        ob = _other_half(a)
        outs.append(acc / acc[ob:ob + 1, :])
    o_ref[0] = jnp.where(row < HEAD_DIM, outs[0], outs[1]).T.astype(BF16)


def _fox(qct, kaug, vaug, base):
    b, nt, w, t = qct.shape
    s = nt * t
    npair = w // PAIR
    return pl.pallas_call(
        _fox_kernel,
        grid=(b, npair, nt),
        in_specs=[
            pl.BlockSpec(memory_space=pltpu.SMEM),
            pl.BlockSpec((1, 1, PAIR, t), lambda bi, p, qi: (bi, qi, p, 0)),
            pl.BlockSpec((1, 2, s, PAIR), lambda bi, p, qi: (bi, p, 0, 0)),
            pl.BlockSpec((1, nt, 2, PAIR, t), lambda bi, p, qi: (bi, 0, p, 0, 0)),
        ],
        out_specs=pl.BlockSpec((1, t, PAIR), lambda bi, p, qi: (bi, qi, p)),
        out_shape=jax.ShapeDtypeStruct((b, s, w), BF16),
        scratch_shapes=[pltpu.VMEM((2, PAIR, t), F32), pltpu.VMEM((2, t, t), F32),
                        pltpu.VMEM((2, t, t), F32)],
        compiler_params=_params("parallel", "parallel", "arbitrary"),
        name="fox",
    )(base, qct, kaug, vaug)


def _first_max(vals):
    best = vals[0]
    for v in vals[1:]:
        best = jnp.maximum(best, v)
    idx = jnp.full(best.shape, len(vals) - 1, jnp.int32)
    for j in range(len(vals) - 2, -1, -1):
        idx = jnp.where(vals[j] == best, j, idx)
    return best, idx


def _softmax_rows(vals):
    top = vals[0]
    for v in vals[1:]:
        top = jnp.maximum(top, v)
    ex = [jnp.exp(v - top) for v in vals]
    tot = ex[0]
    for e in ex[1:]:
        tot = tot + e
    return [e / tot for e in ex]


def _post_kernel(ya_ref, yb_ref, yc_ref, x_ref, mod_ref, ga_ref, gb_ref, gc_ref, wo_ref, nf_ref,
                 wr_hi_ref, wr_lo_ref, br_ref, xo_ref, h2_ref, ri_ref, rw_ref, cnt_ref):
    ya = _rms_normalize(ya_ref[0].astype(F32)) * ga_ref[...]
    yb = _rms_normalize(yb_ref[0].astype(F32)) * gb_ref[...]
    yc = _rms_normalize(yc_ref[0].astype(F32)) * gc_ref[...]
    y = jnp.concatenate([ya, yb, yc], axis=-1).astype(BF16)
    x = x_ref[0] + mod_ref[0, 2:3, :] * _dot(y, wo_ref[...])
    xo_ref[0] = x
    h = (_rms_normalize(x) * nf_ref[...]) * (1.0 + mod_ref[0, 4:5, :]) + mod_ref[0, 3:4, :]
    h_hi, h_lo = _split_bf16(h)
    h2_ref[0] = h_hi

    w_hi = wr_hi_ref[...]
    logits = (_dot_nt(w_hi, h_hi) + _dot_nt(w_hi, h_lo)) + _dot_nt(wr_lo_ref[...], h_hi) + br_ref[...]
    rows = [logits[r:r + 1, :] for r in range(8 + N_EXPERTS)]
    group_prob = _softmax_rows(rows[0:N_GROUPS])
    group_p, group_idx = _first_max(group_prob)
    in_group = []
    for j in range(EXPERTS_PER_GROUP):
        v = rows[8 + (N_GROUPS - 1) * EXPERTS_PER_GROUP + j]
        for g in range(N_GROUPS - 2, -1, -1):
            v = jnp.where(group_idx == g, rows[8 + g * EXPERTS_PER_GROUP + j], v)
        in_group.append(v)
    expert_prob = _softmax_rows(in_group)
    p1, i1 = _first_max(expert_prob)
    rest = [jnp.where(i1 == j, -1.0, expert_prob[j]) for j in range(EXPERTS_PER_GROUP)]
    p2, i2 = _first_max(rest)
    e1 = group_idx * EXPERTS_PER_GROUP + i1
    e2 = group_idx * EXPERTS_PER_GROUP + i2
    tm = e1.shape[1]
    ri_ref[0, 0] = jnp.zeros((8, tm), jnp.int32)
    ri_ref[0, 0, 0:1, :] = e1
    ri_ref[0, 0, 1:2, :] = e2
    rw_ref[0, 0] = jnp.zeros((8, tm), F32)
    rw_ref[0, 0, 0:1, :] = group_p * p1 / (p1 + p2)
    rw_ref[0, 0, 1:2, :] = group_p * p2 / (p1 + p2)
    erow = lax.broadcasted_iota(jnp.int32, (N_EXPERTS, tm), 0)
    sel = jnp.where((erow == e1) | (erow == e2), 1.0, 0.0).astype(BF16)
    cnt_ref[0, 0] = _dot(sel, jnp.ones((tm, 128), BF16))


def _post_mixer(ya, yb, yc, x, mod, gain_a, gain_b, gain_c, w_out, norm_ffn, wr_hi, wr_lo, br):
    b, s, d = x.shape
    tm = TM_POST
    tok = lambda width: pl.BlockSpec((1, tm, width), lambda bi, i: (bi, i, 0))
    full = lambda shape: pl.BlockSpec(shape, lambda bi, i: (0,) * len(shape))
    return pl.pallas_call(
        _post_kernel,
        grid=(b, s // tm),
        in_specs=[
            tok(SWA_WIDTH), tok(LRU_WIDTH), tok(FOX_WIDTH), tok(d),
            pl.BlockSpec((1, 6, d), lambda bi, i: (bi, 0, 0)),
            full((1, SWA_WIDTH)), full((1, LRU_WIDTH)), full((1, FOX_WIDTH)),
            full((d, d)), full((1, d)),
            full((ROUTER_ROWS, d)), full((ROUTER_ROWS, d)), full((ROUTER_ROWS, 1)),
        ],
        out_specs=[
            tok(d), tok(d),
            pl.BlockSpec((1, 1, 8, tm), lambda bi, i: (bi, i, 0, 0)),
            pl.BlockSpec((1, 1, 8, tm), lambda bi, i: (bi, i, 0, 0)),
            pl.BlockSpec((1, 1, N_EXPERTS, 128), lambda bi, i: (bi, i, 0, 0)),
        ],
        out_shape=[
            jax.ShapeDtypeStruct((b, s, d), F32),
            jax.ShapeDtypeStruct((b, s, d), BF16),
            jax.ShapeDtypeStruct((b, s // tm, 8, tm), jnp.int32),
            jax.ShapeDtypeStruct((b, s // tm, 8, tm), F32),
            jax.ShapeDtypeStruct((b, s // tm, N_EXPERTS, 128), F32),
        ],
        compiler_params=_params("parallel", "parallel"),
        name="post_mixer",
    )(ya, yb, yc, x, mod, gain_a, gain_b, gain_c, w_out, norm_ffn, wr_hi, wr_lo, br)


TMS = TM_POST
SEG = 16
SORT_ROWS = 1280
ET = 512
SEG_PIECES = (512, 256, 128, 64, 32, 16)
TAIL_PIECES = (256, 128, 64, 32, 16)
TOTAL_PIECES = (1024, 512, 256, 128, 64, 32, 16)
RARE_ROWS = 128
assert SORT_ROWS >= 2 * TMS + N_EXPERTS * (SEG - 1) and SEG_PIECES[0] == TMS and TAIL_PIECES[0] * 2 == ET


def _for_each_piece(n16_ref, lo_ref, goff_ref, tile, fn):
    for e in range(N_EXPERTS):
        n = n16_ref[tile * N_EXPERTS + e]
        lo = lo_ref[tile * N_EXPERTS + e]
        go = goff_ref[tile * N_EXPERTS + e]

        def pieces(sizes, n=n, lo=lo, go=go):
            for rows in sizes:
                done = (n // (2 * rows)) * (2 * rows)

                @pl.when((n & rows) != 0)
                def _(done=done, rows=rows):
                    fn(pl.multiple_of(lo + done, SEG), pl.multiple_of(go + done, SEG), rows)

        @pl.when(n >= RARE_ROWS)
        def _():
            pieces([r for r in SEG_PIECES if r >= RARE_ROWS])

        pieces([r for r in SEG_PIECES if r < RARE_ROWS])


def _for_each_total_piece(total_ref, tile, fn):
    n = total_ref[tile]
    for rows in TOTAL_PIECES:
        @pl.when((n & rows) != 0)
        def _(rows=rows):
            fn(rows)


def _dispatch_kernel(n16_ref, lo_ref, goff_ref, total_ref, tail_row_ref, tail_n_ref, n_valid_ref,
                     h_ref, ri_ref, tri_ref, xs_ref, dest_ref, sorted_ref, zero_ref, sem):
    i = pl.program_id(0)
    last = pl.num_programs(0) - 1
    slot = i % 2
    tms = h_ref.shape[0]

    ri = ri_ref[0]
    e1 = ri[0:1, :]
    e2 = ri[1:2, :]
    erow = lax.broadcasted_iota(jnp.int32, (N_EXPERTS, tms), 0)
    m1 = erow == e1
    m2 = erow == e2
    sel = jnp.where(m1 | m2, 1.0, 0.0).astype(BF16)
    rank = _dot(sel, tri_ref[...]).astype(jnp.int32)
    start = jnp.zeros((N_EXPERTS, tms), jnp.int32)
    for e in range(N_EXPERTS):
        start = jnp.where(erow == e, lo_ref[i * N_EXPERTS + e], start)
    dest = rank + start
    r1 = jnp.sum(jnp.where(m1, dest, 0), axis=0, keepdims=True)
    r2 = jnp.sum(jnp.where(m2, dest, 0), axis=0, keepdims=True)
    dest_ref[0] = jnp.zeros((8, tms), jnp.int32)
    dest_ref[0, 0:1, :] = r1
    dest_ref[0, 1:2, :] = r2

    rows = lax.broadcasted_iota(jnp.int32, (SORT_ROWS, tms), 0)
    perm = jnp.where((rows == r1) | (rows == r2), 1.0, 0.0).astype(BF16)
    sorted_ref[slot] = _dot(perm, h_ref[...]).astype(BF16)

    def copy(src_slot, local_row, global_row, n):
        return pltpu.make_async_copy(sorted_ref.at[src_slot, pl.ds(local_row, n)],
                                     xs_ref.at[pl.ds(global_row, n)], sem)

    @pl.when(i > 0)
    def _():
        _for_each_total_piece(total_ref, i - 1, lambda n: copy(1 - slot, 0, 0, n).wait())

    _for_each_piece(n16_ref, lo_ref, goff_ref, i, lambda lr, gr, n: copy(slot, lr, gr, n).start())

    @pl.when(i == last)
    def _():
        _for_each_total_piece(total_ref, i, lambda n: copy(slot, 0, 0, n).wait())
        zero_ref[...] = jnp.zeros_like(zero_ref)

        def tail_copies(fn):
            for e in range(N_EXPERTS):
                n = tail_n_ref[e]
                row = tail_row_ref[e]
                for rows_ in TAIL_PIECES:
                    done = (n // (2 * rows_)) * (2 * rows_)

                    @pl.when((n & rows_) != 0)
                    def _(row=row, done=done, rows_=rows_):
                        fn(pltpu.make_async_copy(
                            zero_ref.at[pl.ds(0, rows_)],
                            xs_ref.at[pl.ds(pl.multiple_of(row + done, SEG), rows_)], sem))

        tail_copies(lambda c: c.start())
        tail_copies(lambda c: c.wait())

        def unused_tiles(fn):
            def body(j, carry):
                for part in range(ET // TAIL_PIECES[0]):
                    row = pl.multiple_of(j * ET + part * TAIL_PIECES[0], SEG)
                    fn(pltpu.make_async_copy(zero_ref, xs_ref.at[pl.ds(row, TAIL_PIECES[0])], sem))
                return carry
            lax.fori_loop(n_valid_ref[0], xs_ref.shape[0] // ET, body, 0)

        unused_tiles(lambda c: c.start())
        unused_tiles(lambda c: c.wait())


def _dispatch(h2, route_i, tri, tables, rows_max):
    t, d = h2.shape
    nt = t // TMS
    grid_spec = pltpu.PrefetchScalarGridSpec(
        num_scalar_prefetch=7,
        grid=(nt,),
        in_specs=[
            pl.BlockSpec((TMS, d), lambda i, *_: (i, 0)),
            pl.BlockSpec((1, 8, TMS), lambda i, *_: (i, 0, 0)),
            pl.BlockSpec((TMS, TMS), lambda i, *_: (0, 0)),
        ],
        out_specs=[
            pl.BlockSpec(memory_space=pl.ANY),
            pl.BlockSpec((1, 8, TMS), lambda i, *_: (i, 0, 0)),
        ],
        scratch_shapes=[
            pltpu.VMEM((2, SORT_ROWS, d), BF16),
            pltpu.VMEM((TAIL_PIECES[0], d), BF16),
            pltpu.SemaphoreType.DMA,
        ],
    )
    return pl.pallas_call(
        _dispatch_kernel,
        grid_spec=grid_spec,
        out_shape=[
            jax.ShapeDtypeStruct((rows_max, d), BF16),
            jax.ShapeDtypeStruct((nt, 8, TMS), jnp.int32),
        ],
        compiler_params=_params("arbitrary"),
        name="moe_dispatch",
    )(*tables, h2, route_i, tri)


def _experts_kernel(te_ref, nv_ref, xs_ref, wg_ref, wu_ref, wd_ref, ys_ref):
    j = pl.program_id(0)

    @pl.when(j < nv_ref[0])
    def _():
        h = xs_ref[...]
        gate = _dot(h, wg_ref[0])
        hidden = (gate * jax.nn.sigmoid(gate)) * _dot(h, wu_ref[0])
        ys_ref[...] = _dot(hidden.astype(BF16), wd_ref[0]).astype(BF16)

    @pl.when(j >= nv_ref[0])
    def _():
        ys_ref[...] = jnp.zeros_like(ys_ref)


def _experts(xs, tile_expert, n_valid, w_gate, w_up, w_down):
    rows, d = xs.shape
    row_tile = lambda j, te, nv: (jnp.minimum(j, nv[0] - 1), 0)
    weight = lambda j, te, nv: (te[j], 0, 0)
    grid_spec = pltpu.PrefetchScalarGridSpec(
        num_scalar_prefetch=2,
        grid=(rows // ET,),
        in_specs=[
            pl.BlockSpec((ET, d), row_tile),
            pl.BlockSpec((1, d, D_EXPERT), weight),
            pl.BlockSpec((1, d, D_EXPERT), weight),
            pl.BlockSpec((1, D_EXPERT, d), weight),
        ],
        out_specs=pl.BlockSpec((ET, d), lambda j, te, nv: (j, 0)),
    )
    return pl.pallas_call(
        _experts_kernel,
        grid_spec=grid_spec,
        out_shape=jax.ShapeDtypeStruct((rows, d), BF16),
        compiler_params=_params("arbitrary"),
        name="moe_experts",
    )(tile_expert, n_valid, xs, w_gate, w_up, w_down)


def _combine_kernel(n16_ref, lo_ref, goff_ref, total_ref, ys_ref, dest_ref, w_ref, x_ref, mod_ref,
                    nf_ref, o_ref, buf_ref, sems, *, final):
    i = pl.program_id(0)
    nt = pl.num_programs(0)
    slot = i % 2

    def copy(dst_slot, local_row, global_row, n):
        return pltpu.make_async_copy(ys_ref.at[pl.ds(global_row, n)],
                                     buf_ref.at[dst_slot, pl.ds(local_row, n)], sems.at[dst_slot])

    @pl.when(i == 0)
    def _():
        buf_ref[...] = jnp.zeros_like(buf_ref)
        _for_each_piece(n16_ref, lo_ref, goff_ref, 0, lambda lr, gr, n: copy(0, lr, gr, n).start())

    @pl.when(i + 1 < nt)
    def _():
        _for_each_piece(n16_ref, lo_ref, goff_ref, i + 1,
                        lambda lr, gr, n: copy(1 - slot, lr, gr, n).start())

    _for_each_total_piece(total_ref, i, lambda n: copy(slot, 0, 0, n).wait())

    dest = dest_ref[...]
    w = w_ref[...]
    col = lax.broadcasted_iota(jnp.int32, (dest.shape[0], SORT_ROWS), 1)
    wmat = (jnp.where(col == dest[:, 0:1], w[:, 0:1], 0.0)
            + jnp.where(col == dest[:, 1:2], w[:, 1:2], 0.0))
    w_hi, w_lo = _split_bf16(wmat)
    ys = buf_ref[slot]
    y = _dot(w_hi, ys) + _dot(w_lo, ys)
    x = x_ref[...] + mod_ref[0, 5:6, :] * y
    if final:
        x = _rms_normalize(x) * nf_ref[...]
    o_ref[...] = x


def _combine(ys, dest_col, w_col, x, mod, norm_final, tables, tiles_per_batch, final):
    t, d = x.shape
    nt = t // TMS
    grid_spec = pltpu.PrefetchScalarGridSpec(
        num_scalar_prefetch=4,
        grid=(nt,),
        in_specs=[
            pl.BlockSpec(memory_space=pl.ANY),
            pl.BlockSpec((TMS, 8), lambda i, *_: (i, 0)),
            pl.BlockSpec((TMS, 8), lambda i, *_: (i, 0)),
            pl.BlockSpec((TMS, d), lambda i, *_: (i, 0)),
            pl.BlockSpec((1, 6, d), lambda i, *_: (i // tiles_per_batch, 0, 0)),
            pl.BlockSpec((1, d), lambda i, *_: (0, 0)),
        ],
        out_specs=pl.BlockSpec((TMS, d), lambda i, *_: (i, 0)),
        scratch_shapes=[pltpu.VMEM((2, SORT_ROWS, d), BF16), pltpu.SemaphoreType.DMA((2,))],
    )
    return pl.pallas_call(
        functools.partial(_combine_kernel, final=final),
        grid_spec=grid_spec,
        out_shape=jax.ShapeDtypeStruct((t, d), F32),
        compiler_params=_params("arbitrary"),
        name="moe_combine_final" if final else "moe_combine",
    )(*tables, ys, dest_col, w_col, x, mod, norm_final)


def _route_tables(counts, n_expert_tiles):
    n16 = (counts + (SEG - 1)) // SEG * SEG
    lo = jnp.cumsum(n16, axis=1) - n16
    total = jnp.sum(n16, axis=0)
    region = (total + (ET - 1)) // ET * ET
    ends = jnp.cumsum(region)
    base = ends - region
    goff = base[None, :] + jnp.cumsum(n16, axis=0) - n16
    tile_row = jnp.arange(n_expert_tiles, dtype=jnp.int32) * ET
    tile_expert = jnp.minimum(jnp.sum(tile_row[:, None] >= ends[None, :], axis=1), N_EXPERTS - 1)
    n_valid = (ends[-1] // ET).reshape(1)
    i32 = lambda a: a.astype(jnp.int32).reshape(-1)
    return (i32(n16), i32(lo), i32(goff), i32(jnp.sum(n16, axis=1))), \
        (i32(base + total), i32(region - total)), i32(tile_expert), i32(n_valid)


def _moe(h2, route_i, route_w, cnt, w_gate, w_up, w_down, x, mod, norm_final, final):
    b, s, d = x.shape
    t = b * s
    nt = t // TMS
    n_expert_tiles = -(-(2 * t + nt * N_EXPERTS * (SEG - 1) + N_EXPERTS * (ET - 1)) // ET)
    counts = jnp.round(cnt[:, :, :, 0]).astype(jnp.int32).reshape(nt, N_EXPERTS)
    seg_tables, tail_tables, tile_expert, n_valid = _route_tables(counts, n_expert_tiles)
    tri = jnp.triu(jnp.ones((TMS, TMS), BF16), k=1)
    xs, dest = _dispatch(h2.reshape(t, d), route_i.reshape(nt, 8, TMS), tri,
                         seg_tables + tail_tables + (n_valid,), n_expert_tiles * ET)
    ys = _experts(xs, tile_expert, n_valid, w_gate, w_up, w_down)
    dest_col = jnp.swapaxes(dest, 1, 2).reshape(t, 8)
    w_col = jnp.swapaxes(route_w.reshape(nt, 8, TMS), 1, 2).reshape(t, 8)
    out = _combine(ys, dest_col, w_col, x.reshape(t, d), mod, norm_final, seg_tables, s // TMS, final)
    return out.reshape(b, s, d)


def _block_diag(w):
    nb, c, _ = w.shape
    eye = jnp.eye(nb, dtype=w.dtype)
    return (eye[:, None, :, None] * w[:, :, None, :]).reshape(nb * c, nb * c)


def _swa_head_perm():
    cols = []
    half = SWA_Q_HEADS // 2
    for p in range(half):
        for hd in (p, p + half):
            cols.extend(range(hd * HEAD_DIM, (hd + 1) * HEAD_DIM))
    return jnp.asarray(cols, jnp.int32)


def kernel(x, c, w_mod, b_mod, norm_mix, norm_ffn, w_in, w_out, out_gain, sinks, conv_w, conv_b,
           lru_wa, lru_ba, lru_wx, lru_bx, lru_lam, fox_bf, w_router_group, b_router_group,
           w_router_expert, b_router_expert, w_gate, w_up, w_down, norm_final):
    depth = w_mod.shape[0]
    b, s, d = x.shape
    assert d == D_MODEL and s % max(TM_PRE, TQ_SWA, TS_LRU, T_FOX, TM_POST) == 0
    assert TM_PRE == T_FOX
    c_rows = jnp.zeros((16, d), F32).at[:b].set(c)
    mod_all = _modulation(c_rows, w_mod, b_mod)[:, :b]
    perm = _swa_head_perm()

    for l in range(depth):
        mod = mod_all[l].reshape(b, 6, d)
        w_in_l = w_in[l]
        w_tok = jnp.concatenate([w_in_l[:, :SWA_WIDTH][:, perm], w_in_l[:, SWA_WIDTH:OFF_QC],
                                 w_in_l[:, OFF_KC:OFF_VC]], axis=1).astype(BF16)
        w_t = jnp.concatenate([w_in_l[:, OFF_QC:OFF_KC].T, w_in_l[:, OFF_VC:OFF_FC].T,
                               jnp.zeros((FORGET_ROWS, d), F32).at[:FOX_HEADS].set(w_in_l[:, OFF_FC:].T)],
                              axis=0).astype(BF16)
        fb = jnp.zeros((FORGET_ROWS, 1), F32).at[:FOX_HEADS, 0].set(fox_bf[l])
        gain = out_gain[l]
        gain_a = gain[:SWA_WIDTH][perm].reshape(1, SWA_WIDTH)
        gain_b = gain[SWA_WIDTH:SWA_WIDTH + LRU_WIDTH].reshape(1, LRU_WIDTH)
        gain_c = gain[SWA_WIDTH + LRU_WIDTH:].reshape(1, FOX_WIDTH)
        w_out_l = jnp.concatenate([w_out[l][:SWA_WIDTH][perm], w_out[l][SWA_WIDTH:]], axis=0).astype(BF16)
        wr = jnp.zeros((ROUTER_ROWS, d), F32)
        wr = wr.at[:N_GROUPS].set(w_router_group[l].T).at[8:8 + N_EXPERTS].set(w_router_expert[l].T)
        wr_hi = wr.astype(BF16)
        wr_lo = (wr - wr_hi.astype(F32)).astype(BF16)
        br = jnp.zeros((ROUTER_ROWS, 1), F32)
        br = br.at[:N_GROUPS, 0].set(b_router_group[l]).at[8:8 + N_EXPERTS, 0].set(b_router_expert[l])

        qa, ka, va, xb, gb, kc, qct, vaug, lf = _pre_mixer(
            x, mod, norm_mix[l].reshape(1, d), w_tok, w_t, fb)
        cum = _cumsum(lf)
        kaug = _fox_prep(kc, jnp.swapaxes(cum, 1, 2))
        base = cum[:, :FOX_HEADS, ::T_FOX].reshape(-1)
        ya = _swa(qa, ka, va, sinks[l].reshape(1, SWA_Q_HEADS))
        yb = _lru(xb, gb, conv_w[l], conv_b[l].reshape(1, -1),
                  _block_diag(lru_wa[l]).astype(BF16), lru_ba[l].reshape(1, -1),
                  _block_diag(lru_wx[l]).astype(BF16), lru_bx[l].reshape(1, -1),
                  lru_lam[l].reshape(1, -1))
        yc = _fox(qct, kaug, vaug, base)
        x, h2, route_i, route_w, cnt = _post_mixer(
            ya, yb, yc, x, mod, gain_a, gain_b, gain_c, w_out_l,
            norm_ffn[l].reshape(1, d), wr_hi, wr_lo, br)
        x = _moe(h2, route_i, route_w, cnt, w_gate[l].astype(BF16), w_up[l].astype(BF16),
                 w_down[l].astype(BF16), x, mod, norm_final.reshape(1, d), final=(l == depth - 1))
    return x
```

```python
import functools

import jax
import jax.numpy as jnp
from jax import lax
from jax.experimental import pallas as pl
from jax.experimental.pallas import tpu as pltpu

F32 = jnp.float32
BF16 = jnp.bfloat16

D_MODEL = 1024
HEAD_DIM = 64
PAIR = 2 * HEAD_DIM
BLOCK = 128
SWA_Q_HEADS = 8
SWA_WIDTH = SWA_Q_HEADS * HEAD_DIM
KV_WIDTH = 2 * HEAD_DIM
LRU_WIDTH = 256
LRU_BLOCKS = 8
CONV_WIDTH = 4
LRU_C = 8.0
FOX_HEADS = 4
FOX_WIDTH = FOX_HEADS * HEAD_DIM
N_GROUPS = 4
EXPERTS_PER_GROUP = 4
N_EXPERTS = 16
D_EXPERT = 256
EPS = 1e-6
NEG = -1e30
LOG2E = 1.4426950408889634

OFF_KA = SWA_WIDTH
OFF_VA = OFF_KA + KV_WIDTH
OFF_XB = OFF_VA + KV_WIDTH
OFF_GB = OFF_XB + LRU_WIDTH
OFF_QC = OFF_GB + LRU_WIDTH
OFF_KC = OFF_QC + FOX_WIDTH
OFF_VC = OFF_KC + FOX_WIDTH
OFF_FC = OFF_VC + FOX_WIDTH

ROUTER_ROWS = 32
FORGET_ROWS = 16
VMEM_LIMIT_BYTES = 48 * 1024 * 1024

TM_PRE = 512
TQ_SWA = 512
TS_LRU = 512
T_FOX = 512
TM_POST = 512
CUMSUM_CHUNK = 256


def _params(*semantics):
    return pltpu.CompilerParams(dimension_semantics=semantics, vmem_limit_bytes=VMEM_LIMIT_BYTES)


def _split_bf16(v):
    hi = v.astype(BF16)
    lo = (v - hi.astype(F32)).astype(BF16)
    return hi, lo


def _dot(a, b):
    return jnp.dot(a, b, preferred_element_type=F32)


def _dot_nt(a, b):
    return lax.dot_general(a, b, (((1,), (1,)), ((), ())), preferred_element_type=F32)


def _rms_normalize(v):
    return v * lax.rsqrt(jnp.mean(v * v, axis=-1, keepdims=True) + EPS)


def _mod_kernel(c_ref, w_ref, b_ref, o_ref):
    c = c_ref[...]
    ca = c * jax.nn.sigmoid(c)
    w = w_ref[0]
    a_hi, a_lo = _split_bf16(ca)
    w_hi, w_lo = _split_bf16(w)
    o_ref[0] = _dot(a_hi, w_hi) + _dot(a_lo, w_hi) + _dot(a_hi, w_lo) + b_ref[0]


def _modulation(c, w_mod, b_mod):
    depth, d, d6 = w_mod.shape
    b = c.shape[0]
    n = d6 // d
    return pl.pallas_call(
        _mod_kernel,
        grid=(depth, n),
        in_specs=[
            pl.BlockSpec((b, d), lambda l, j: (0, 0)),
            pl.BlockSpec((1, d, d), lambda l, j: (l, 0, j)),
            pl.BlockSpec((1, 1, d), lambda l, j: (l, 0, j)),
        ],
        out_specs=pl.BlockSpec((1, b, d), lambda l, j: (l, 0, j)),
        out_shape=jax.ShapeDtypeStruct((depth, b, d6), F32),
        compiler_params=_params("arbitrary", "arbitrary"),
        name="modulation",
    )(c, w_mod, b_mod.reshape(depth, 1, d6))


def _log_sigmoid(z):
    return jnp.minimum(z, 0.0) - jnp.log1p(jnp.exp(-jnp.abs(z)))


def _other_half(a):
    return HEAD_DIM * (1 - a)


def _pre_kernel(x_ref, mod_ref, g_ref, w_ref, wt_ref, fb_ref,
                qa_ref, ka_ref, va_ref, xb_ref, gb_ref, kc_ref, qct_ref, vct_ref, lf_ref):
    x = x_ref[0]
    h = (_rms_normalize(x) * g_ref[...]) * (1.0 + mod_ref[0, 1:2, :]) + mod_ref[0, 0:1, :]
    hb = h.astype(BF16)

    def proj(lo, width):
        return _dot(hb, w_ref[:, lo:lo + width])

    scale = HEAD_DIM ** -0.5
    qa_ref[0] = (proj(0, SWA_WIDTH) * scale).astype(BF16)
    ka_ref[0] = proj(OFF_KA, KV_WIDTH).astype(BF16)
    va_ref[0] = proj(OFF_VA, KV_WIDTH).astype(BF16)
    xb_ref[0] = proj(OFF_XB, LRU_WIDTH).astype(BF16)
    gb_ref[0] = proj(OFF_GB, LRU_WIDTH).astype(BF16)
    kc_ref[0] = proj(OFF_QC, FOX_WIDTH).astype(BF16)

    t_all = _dot_nt(wt_ref[...], hb)
    qct_ref[0, 0] = (t_all[0:FOX_WIDTH] * (scale * LOG2E)).astype(BF16)
    row = lax.broadcasted_iota(jnp.int32, (PAIR, t_all.shape[1]), 0)
    for p in range(FOX_WIDTH // PAIR):
        v_pair = t_all[FOX_WIDTH + p * PAIR:FOX_WIDTH + (p + 1) * PAIR]
        for a in range(2):
            vct_ref[0, 0, 2 * p + a] = jnp.where(row == _other_half(a), 1.0, v_pair).astype(BF16)
    lf_ref[0] = _log_sigmoid(t_all[2 * FOX_WIDTH:] + fb_ref[...])


def _pre_mixer(x, mod, gain, w_tok, w_t, fb):
    b, s, d = x.shape
    tm = TM_PRE
    nt = s // tm

    def tok(width):
        return pl.BlockSpec((1, tm, width), lambda bi, i: (bi, i, 0))

    def tok_shape(width):
        return jax.ShapeDtypeStruct((b, s, width), BF16)

    return pl.pallas_call(
        _pre_kernel,
        grid=(b, nt),
        in_specs=[
            tok(d),
            pl.BlockSpec((1, 6, d), lambda bi, i: (bi, 0, 0)),
            pl.BlockSpec((1, d), lambda bi, i: (0, 0)),
            pl.BlockSpec(w_tok.shape, lambda bi, i: (0, 0)),
            pl.BlockSpec(w_t.shape, lambda bi, i: (0, 0)),
            pl.BlockSpec((FORGET_ROWS, 1), lambda bi, i: (0, 0)),
        ],
        out_specs=[
            tok(SWA_WIDTH), tok(KV_WIDTH), tok(KV_WIDTH), tok(LRU_WIDTH), tok(LRU_WIDTH),
            tok(FOX_WIDTH),
            pl.BlockSpec((1, 1, FOX_WIDTH, tm), lambda bi, i: (bi, i, 0, 0)),
            pl.BlockSpec((1, 1, FOX_HEADS, PAIR, tm), lambda bi, i: (bi, i, 0, 0, 0)),
            pl.BlockSpec((1, FORGET_ROWS, tm), lambda bi, i: (bi, 0, i)),
        ],
        out_shape=[
            tok_shape(SWA_WIDTH), tok_shape(KV_WIDTH), tok_shape(KV_WIDTH), tok_shape(LRU_WIDTH),
            tok_shape(LRU_WIDTH), tok_shape(FOX_WIDTH),
            jax.ShapeDtypeStruct((b, nt, FOX_WIDTH, tm), BF16),
            jax.ShapeDtypeStruct((b, nt, FOX_HEADS, PAIR, tm), BF16),
            jax.ShapeDtypeStruct((b, FORGET_ROWS, s), F32),
        ],
        compiler_params=_params("parallel", "parallel"),
        name="pre_mixer",
    )(x, mod, gain, w_tok, w_t, fb)


def _cumsum_kernel(lf_ref, o_ref):
    n = CUMSUM_CHUNK
    s = lf_ref.shape[2]
    r = lax.broadcasted_iota(jnp.int32, (n, n), 0)
    c = lax.broadcasted_iota(jnp.int32, (n, n), 1)
    tri = jnp.where(r <= c, 1.0, 0.0).astype(BF16)
    carry = jnp.zeros((lf_ref.shape[1], 1), F32)
    for j in range(s // n):
        seg = lf_ref[0, :, j * n:(j + 1) * n]
        hi = seg.astype(BF16)
        rem = seg - hi.astype(F32)
        mid = rem.astype(BF16)
        lo = (rem - mid.astype(F32)).astype(BF16)
        cs = (_dot(hi, tri) + _dot(mid, tri)) + _dot(lo, tri) + carry
        o_ref[0, :, j * n:(j + 1) * n] = cs
        carry = cs[:, n - 1:n]


def _cumsum(lf):
    b, r, s = lf.shape
    return pl.pallas_call(
        _cumsum_kernel,
        grid=(b,),
        in_specs=[pl.BlockSpec((1, r, s), lambda bi: (bi, 0, 0))],
        out_specs=pl.BlockSpec((1, r, s), lambda bi: (bi, 0, 0)),
        out_shape=jax.ShapeDtypeStruct((b, r, s), F32),
        compiler_params=_params("parallel"),
        name="cumsum",
    )(lf)


def _swa_kernel(sink_ref, q_ref, kc_ref, vc_ref, kp_ref, vp_ref, o_ref, kf_ref, vf_ref, bias_ref):
    i = pl.program_id(1)
    nsub = q_ref.shape[1] // BLOCK
    npair = SWA_Q_HEADS // 2

    @pl.when((pl.program_id(0) == 0) & (i == 0))
    def _():
        qpos = BLOCK + lax.broadcasted_iota(jnp.int32, (BLOCK, 2 * BLOCK), 0)
        kpos = lax.broadcasted_iota(jnp.int32, (BLOCK, 2 * BLOCK), 1)
        dist = qpos - kpos
        valid = (dist >= 0) & (dist < BLOCK)
        distf = dist.astype(F32)
        for hd in range(SWA_Q_HEADS):
            slope = 2.0 ** (-8.0 * (hd + 1) / SWA_Q_HEADS)
            bias_ref[hd] = jnp.where(valid, -slope * distf, NEG)

    kf_ref[0:BLOCK, :] = kp_ref[0]
    kf_ref[BLOCK:, :] = kc_ref[0]
    vf_ref[0:BLOCK, :] = vp_ref[0]
    vf_ref[BLOCK:, :] = vc_ref[0]

    lane = lax.broadcasted_iota(jnp.int32, (2 * BLOCK, PAIR), 1)
    lane_o = lax.broadcasted_iota(jnp.int32, (BLOCK, PAIR), 1)
    col = lax.broadcasted_iota(jnp.int32, (BLOCK, 2 * BLOCK), 1)

    for j in range(nsub):
        k2 = kf_ref[j * BLOCK:(j + 2) * BLOCK, :]
        v2 = vf_ref[j * BLOCK:(j + 2) * BLOCK, :]
        k_half = (jnp.where(lane < HEAD_DIM, k2, jnp.zeros_like(k2)),
                  jnp.where(lane >= HEAD_DIM, k2, jnp.zeros_like(k2)))
        for p in range(npair):
            qp = q_ref[0, j * BLOCK:(j + 1) * BLOCK, p * PAIR:(p + 1) * PAIR]
            outs = []
            for a in range(2):
                hd = p + a * npair
                sc = _dot_nt(qp, k_half[a]) + bias_ref[hd]
                if j == 0:
                    sc = sc + jnp.where(col < BLOCK, jnp.where(i == 0, NEG, 0.0), 0.0)
                sink = sink_ref[0, hd]
                m = jnp.maximum(jnp.max(sc, axis=-1, keepdims=True), sink)
                pr = jnp.exp(sc - m)
                den = jnp.sum(pr, axis=-1, keepdims=True) + jnp.exp(sink - m)
                outs.append(_dot(pr.astype(BF16), v2) / den)
            o_ref[0, j * BLOCK:(j + 1) * BLOCK, p * PAIR:(p + 1) * PAIR] = jnp.where(
                lane_o < HEAD_DIM, outs[0], outs[1]).astype(BF16)


def _swa(qa, ka, va, sinks):
    b, s, _ = qa.shape
    tq = TQ_SWA
    nsub = tq // BLOCK
    prev = lambda bi, i: (bi, jnp.maximum(i * nsub - 1, 0), 0)
    return pl.pallas_call(
        _swa_kernel,
        grid=(b, s // tq),
        in_specs=[
            pl.BlockSpec(memory_space=pltpu.SMEM),
            pl.BlockSpec((1, tq, SWA_WIDTH), lambda bi, i: (bi, i, 0)),
            pl.BlockSpec((1, tq, KV_WIDTH), lambda bi, i: (bi, i, 0)),
            pl.BlockSpec((1, tq, KV_WIDTH), lambda bi, i: (bi, i, 0)),
            pl.BlockSpec((1, BLOCK, KV_WIDTH), prev),
            pl.BlockSpec((1, BLOCK, KV_WIDTH), prev),
        ],
        out_specs=pl.BlockSpec((1, tq, SWA_WIDTH), lambda bi, i: (bi, i, 0)),
        out_shape=jax.ShapeDtypeStruct((b, s, SWA_WIDTH), BF16),
        scratch_shapes=[
            pltpu.VMEM((tq + BLOCK, KV_WIDTH), BF16),
            pltpu.VMEM((tq + BLOCK, KV_WIDTH), BF16),
            pltpu.VMEM((SWA_Q_HEADS, BLOCK, 2 * BLOCK), F32),
        ],
        compiler_params=_params("arbitrary", "arbitrary"),
        name="swa",
    )(sinks, qa, ka, va, ka, va)


def _gelu_tanh(v):
    return 0.5 * v * (1.0 + jnp.tanh(0.7978845608028654 * (v + 0.044715 * (v * v * v))))


def _lru_kernel(xb_ref, gb_ref, cw_ref, cb_ref, wa_ref, ba_ref, wx_ref, bx_ref, lam_ref,
                o_ref, win_ref, h_ref):
    ts = xb_ref.shape[1]

    @pl.when(pl.program_id(1) == 0)
    def _():
        win_ref[0:8, :] = jnp.zeros((8, LRU_WIDTH), F32)
        h_ref[...] = jnp.zeros_like(h_ref)

    x = xb_ref[0].astype(F32)
    win_ref[8:, :] = x
    conv = cb_ref[...] + cw_ref[CONV_WIDTH - 1:CONV_WIDTH, :] * x
    for k in range(1, CONV_WIDTH):
        conv = conv + cw_ref[CONV_WIDTH - 1 - k:CONV_WIDTH - k, :] * win_ref[8 - k:8 - k + ts, :]
    win_ref[0:8, :] = x[ts - 8:ts]

    cbf = conv.astype(BF16)
    r = jax.nn.sigmoid(_dot(cbf, wa_ref[...]) + ba_ref[...])
    gi = jax.nn.sigmoid(_dot(cbf, wx_ref[...]) + bx_ref[...])
    z = -lam_ref[...]
    softplus = jnp.maximum(z, 0.0) + jnp.log1p(jnp.exp(-jnp.abs(z)))
    log_a = (-LRU_C * softplus) * r
    a = jnp.exp(log_a)
    v = 1.0 - a * a
    u = jnp.where(v == 0.0, 0.0, v * lax.rsqrt(v)) * (gi * conv)

    a = a.reshape(ts // 8, 8, LRU_WIDTH)
    u = u.reshape(ts // 8, 8, LRU_WIDTH)
    sub = lax.broadcasted_iota(jnp.int32, a.shape, 1)
    for step in (1, 2, 4):
        keep = sub >= step
        u = jnp.where(keep, a * pltpu.roll(u, step, 1) + u, u)
        a = jnp.where(keep, a * pltpu.roll(a, step, 1), a)
    carry = h_ref[7:8, :]
    groups = []
    for g in range(ts // 8):
        hg = u[g] + a[g] * carry
        groups.append(hg)
        carry = hg[7:8, :]
    h = jnp.concatenate(groups, axis=0)
    h_ref[...] = groups[-1]
    o_ref[0] = (h * _gelu_tanh(gb_ref[0].astype(F32))).astype(BF16)


def _lru(xb, gb, conv_w, conv_b, wa, ba, wx, bx, lam):
    b, s, w = xb.shape
    ts = TS_LRU
    tok = pl.BlockSpec((1, ts, w), lambda bi, i: (bi, i, 0))
    full = lambda shape: pl.BlockSpec(shape, lambda bi, i: (0,) * len(shape))
    return pl.pallas_call(
        _lru_kernel,
        grid=(b, s // ts),
        in_specs=[tok, tok, full((CONV_WIDTH, w)), full((1, w)), full((w, w)), full((1, w)),
                  full((w, w)), full((1, w)), full((1, w))],
        out_specs=tok,
        out_shape=jax.ShapeDtypeStruct((b, s, w), BF16),
        scratch_shapes=[pltpu.VMEM((8 + ts, w), F32), pltpu.VMEM((8, w), F32)],
        compiler_params=_params("arbitrary", "arbitrary"),
        name="lru",
    )(xb, gb, conv_w, conv_b, wa, ba, wx, bx, lam)


BIAS_ROWS = 6


def _split3(v):
    hi = v.astype(BF16).astype(F32)
    rem = v - hi
    mid = rem.astype(BF16).astype(F32)
    lo = (rem - mid).astype(BF16).astype(F32)
    return hi, mid, lo


def _fox_prep_kernel(k_ref, c_ref, o_ref):
    tm = k_ref.shape[1]
    lane = lax.broadcasted_iota(jnp.int32, (tm, PAIR), 1)
    for p in range(FOX_WIDTH // PAIR):
        kp = k_ref[0, :, p * PAIR:(p + 1) * PAIR].astype(F32)
        for a in range(2):
            hd = 2 * p + a
            c_hi, c_mid, c_lo = _split3(c_ref[0, :, hd:hd + 1] * LOG2E)
            rel = lane - _other_half(a)
            aug = jnp.where(rel == 0, c_hi, jnp.where(rel == 1, c_mid, jnp.where(
                rel == 2, c_lo, jnp.where(rel < BIAS_ROWS, 1.0, 0.0))))
            in_half = (lane >= HEAD_DIM * a) & (lane < HEAD_DIM * (a + 1))
            o_ref[0, hd] = jnp.where(in_half, kp, aug).astype(BF16)


def _fox_prep(kc, cum_col):
    b, s, w = kc.shape
    tm = TM_PRE
    return pl.pallas_call(
        _fox_prep_kernel,
        grid=(b, s // tm),
        in_specs=[
            pl.BlockSpec((1, tm, w), lambda bi, i: (bi, i, 0)),
            pl.BlockSpec((1, tm, cum_col.shape[2]), lambda bi, i: (bi, i, 0)),
        ],
        out_specs=pl.BlockSpec((1, FOX_HEADS, tm, PAIR), lambda bi, i: (bi, 0, i, 0)),
        out_shape=jax.ShapeDtypeStruct((b, FOX_HEADS, s, PAIR), BF16),
        compiler_params=_params("parallel", "parallel"),
        name="fox_prep",
    )(kc, cum_col)


def _fox_kernel(base_ref, qt_ref, ka_ref, vt_ref, o_ref, acc_ref, s0_ref, s1_ref):
    bi = pl.program_id(0)
    p = pl.program_id(1)
    qi = pl.program_id(2)
    nq = pl.num_programs(2)
    t = qt_ref.shape[3]
    qt = qt_ref[0, 0].astype(F32)
    row = lax.broadcasted_iota(jnp.int32, (PAIR, t), 0)
    causal = (lax.broadcasted_iota(jnp.int32, (t, t), 0) <= lax.broadcasted_iota(jnp.int32, (t, t), 1))
    q_aug = []
    for a in range(2):
        hd = 2 * p + a
        base = jnp.full((1, t), base_ref[(bi * FOX_HEADS + hd) * nq + qi], F32) * LOG2E
        b_hi, b_mid, b_lo = _split3(base)
        rel = row - _other_half(a)
        aug = jnp.where(rel < 3, -1.0, jnp.where(rel == 3, b_hi, jnp.where(
            rel == 4, b_mid, jnp.where(rel == 5, b_lo, 0.0))))
        in_half = (row >= HEAD_DIM * a) & (row < HEAD_DIM * (a + 1))
        q_aug.append(jnp.where(in_half, qt, aug).astype(BF16))
        acc_ref[a] = jnp.zeros((PAIR, t), F32)

    slots = (s0_ref, s1_ref)

    def scores(ki, slot, masked):
        tile_max = []
        for a in range(2):
            k = ka_ref[0, a, pl.ds(pl.multiple_of(ki * t, t), t), :]
            st = _dot(k, q_aug[a])
            if masked:
                st = jnp.where(causal, st, NEG)
            slots[slot][a] = st
            tile_max.append(jnp.max(st, axis=0, keepdims=True))
        return tuple(tile_max)

    def consume(ki, slot, m_old, tile_max):
        m_new = []
        for a in range(2):
            m_a = jnp.maximum(m_old[a], tile_max[a])
            alpha = jnp.exp2(m_old[a] - m_a)
            pr = jnp.exp2(slots[slot][a] - m_a).astype(BF16)
            acc_ref[a] = alpha * acc_ref[a] + _dot(vt_ref[0, ki, a], pr)
            m_new.append(m_a)
        return tuple(m_new)

    def advance(j, slot, carry):
        m_old, tile_max, prev = carry
        next_max = scores(j, 1 - slot, False)
        return consume(prev, slot, m_old, tile_max), next_max, j

    def body(i, carry):
        return advance(2 * i + 1, 1, advance(2 * i, 0, carry))

    m0 = jnp.full((1, t), NEG, F32)
    carry = ((m0, m0), scores(qi, 0, True), qi)
    carry = lax.fori_loop(0, qi // 2, body, carry)

    def finish_odd(carry):
        m, tile_max, prev = advance(qi - 1, 0, carry)
        consume(prev, 1, m, tile_max)
        return 0

    def finish_even(carry):
        m, tile_max, prev = carry
        consume(prev, 0, m, tile_max)
        return 0

    lax.cond(qi % 2 == 1, finish_odd, finish_even, carry)
    outs = []
    for a in range(2):
        acc = acc_ref[a]
        ob = _other_half(a)
        outs.append(acc / acc[ob:ob + 1, :])
    o_ref[0] = jnp.where(row < HEAD_DIM, outs[0], outs[1]).T.astype(BF16)


def _fox(qct, kaug, vaug, base):
    b, nt, w, t = qct.shape
    s = nt * t
    npair = w // PAIR
    return pl.pallas_call(
        _fox_kernel,
        grid=(b, npair, nt),
        in_specs=[
            pl.BlockSpec(memory_space=pltpu.SMEM),
            pl.BlockSpec((1, 1, PAIR, t), lambda bi, p, qi: (bi, qi, p, 0)),
            pl.BlockSpec((1, 2, s, PAIR), lambda bi, p, qi: (bi, p, 0, 0)),
            pl.BlockSpec((1, nt, 2, PAIR, t), lambda bi, p, qi: (bi, 0, p, 0, 0)),
        ],
        out_specs=pl.BlockSpec((1, t, PAIR), lambda bi, p, qi: (bi, qi, p)),
        out_shape=jax.ShapeDtypeStruct((b, s, w), BF16),
        scratch_shapes=[pltpu.VMEM((2, PAIR, t), F32), pltpu.VMEM((2, t, t), F32),
                        pltpu.VMEM((2, t, t), F32)],
        compiler_params=_params("parallel", "parallel", "arbitrary"),
        name="fox",
    )(base, qct, kaug, vaug)


def _first_max(vals):
    best = vals[0]
    for v in vals[1:]:
        best = jnp.maximum(best, v)
    idx = jnp.full(best.shape, len(vals) - 1, jnp.int32)
    for j in range(len(vals) - 2, -1, -1):
        idx = jnp.where(vals[j] == best, j, idx)
    return best, idx


def _softmax_rows(vals):
    top = vals[0]
    for v in vals[1:]:
        top = jnp.maximum(top, v)
    ex = [jnp.exp(v - top) for v in vals]
    tot = ex[0]
    for e in ex[1:]:
        tot = tot + e
    return [e / tot for e in ex]


def _post_kernel(ya_ref, yb_ref, yc_ref, x_ref, mod_ref, ga_ref, gb_ref, gc_ref, wo_ref, nf_ref,
                 wr_hi_ref, wr_lo_ref, br_ref, xo_ref, h2_ref, ri_ref, rw_ref, cnt_ref):
    ya = _rms_normalize(ya_ref[0].astype(F32)) * ga_ref[...]
    yb = _rms_normalize(yb_ref[0].astype(F32)) * gb_ref[...]
    yc = _rms_normalize(yc_ref[0].astype(F32)) * gc_ref[...]
    y = jnp.concatenate([ya, yb, yc], axis=-1).astype(BF16)
    x = x_ref[0] + mod_ref[0, 2:3, :] * _dot(y, wo_ref[...])
    xo_ref[0] = x
    h = (_rms_normalize(x) * nf_ref[...]) * (1.0 + mod_ref[0, 4:5, :]) + mod_ref[0, 3:4, :]
    h_hi, h_lo = _split_bf16(h)
    h2_ref[0] = h_hi

    w_hi = wr_hi_ref[...]
    logits = (_dot_nt(w_hi, h_hi) + _dot_nt(w_hi, h_lo)) + _dot_nt(wr_lo_ref[...], h_hi) + br_ref[...]
    rows = [logits[r:r + 1, :] for r in range(8 + N_EXPERTS)]
    group_prob = _softmax_rows(rows[0:N_GROUPS])
    group_p, group_idx = _first_max(group_prob)
    in_group = []
    for j in range(EXPERTS_PER_GROUP):
        v = rows[8 + (N_GROUPS - 1) * EXPERTS_PER_GROUP + j]
        for g in range(N_GROUPS - 2, -1, -1):
            v = jnp.where(group_idx == g, rows[8 + g * EXPERTS_PER_GROUP + j], v)
        in_group.append(v)
    expert_prob = _softmax_rows(in_group)
    p1, i1 = _first_max(expert_prob)
    rest = [jnp.where(i1 == j, -1.0, expert_prob[j]) for j in range(EXPERTS_PER_GROUP)]
    p2, i2 = _first_max(rest)
    e1 = group_idx * EXPERTS_PER_GROUP + i1
    e2 = group_idx * EXPERTS_PER_GROUP + i2
    tm = e1.shape[1]
    ri_ref[0, 0] = jnp.zeros((8, tm), jnp.int32)
    ri_ref[0, 0, 0:1, :] = e1
    ri_ref[0, 0, 1:2, :] = e2
    rw_ref[0, 0] = jnp.zeros((8, tm), F32)
    rw_ref[0, 0, 0:1, :] = group_p * p1 / (p1 + p2)
    rw_ref[0, 0, 1:2, :] = group_p * p2 / (p1 + p2)
    erow = lax.broadcasted_iota(jnp.int32, (N_EXPERTS, tm), 0)
    sel = jnp.where((erow == e1) | (erow == e2), 1.0, 0.0).astype(BF16)
    cnt_ref[0, 0] = _dot(sel, jnp.ones((tm, 128), BF16))


def _post_mixer(ya, yb, yc, x, mod, gain_a, gain_b, gain_c, w_out, norm_ffn, wr_hi, wr_lo, br):
    b, s, d = x.shape
    tm = TM_POST
    tok = lambda width: pl.BlockSpec((1, tm, width), lambda bi, i: (bi, i, 0))
    full = lambda shape: pl.BlockSpec(shape, lambda bi, i: (0,) * len(shape))
    return pl.pallas_call(
        _post_kernel,
        grid=(b, s // tm),
        in_specs=[
            tok(SWA_WIDTH), tok(LRU_WIDTH), tok(FOX_WIDTH), tok(d),
            pl.BlockSpec((1, 6, d), lambda bi, i: (bi, 0, 0)),
            full((1, SWA_WIDTH)), full((1, LRU_WIDTH)), full((1, FOX_WIDTH)),
            full((d, d)), full((1, d)),
            full((ROUTER_ROWS, d)), full((ROUTER_ROWS, d)), full((ROUTER_ROWS, 1)),
        ],
        out_specs=[
            tok(d), tok(d),
            pl.BlockSpec((1, 1, 8, tm), lambda bi, i: (bi, i, 0, 0)),
            pl.BlockSpec((1, 1, 8, tm), lambda bi, i: (bi, i, 0, 0)),
            pl.BlockSpec((1, 1, N_EXPERTS, 128), lambda bi, i: (bi, i, 0, 0)),
        ],
        out_shape=[
            jax.ShapeDtypeStruct((b, s, d), F32),
            jax.ShapeDtypeStruct((b, s, d), BF16),
            jax.ShapeDtypeStruct((b, s // tm, 8, tm), jnp.int32),
            jax.ShapeDtypeStruct((b, s // tm, 8, tm), F32),
            jax.ShapeDtypeStruct((b, s // tm, N_EXPERTS, 128), F32),
        ],
        compiler_params=_params("parallel", "parallel"),
        name="post_mixer",
    )(ya, yb, yc, x, mod, gain_a, gain_b, gain_c, w_out, norm_ffn, wr_hi, wr_lo, br)


TMS = TM_POST
SEG = 16
SORT_ROWS = 1280
ET = 512
SEG_PIECES = (512, 256, 128, 64, 32, 16)
TAIL_PIECES = (256, 128, 64, 32, 16)
TOTAL_PIECES = (1024, 512, 256, 128, 64, 32, 16)
RARE_ROWS = 128
assert SORT_ROWS >= 2 * TMS + N_EXPERTS * (SEG - 1) and SEG_PIECES[0] == TMS and TAIL_PIECES[0] * 2 == ET


def _for_each_piece(n16_ref, lo_ref, goff_ref, tile, fn):
    for e in range(N_EXPERTS):
        n = n16_ref[tile * N_EXPERTS + e]
        lo = lo_ref[tile * N_EXPERTS + e]
        go = goff_ref[tile * N_EXPERTS + e]

        def pieces(sizes, n=n, lo=lo, go=go):
            for rows in sizes:
                done = (n // (2 * rows)) * (2 * rows)

                @pl.when((n & rows) != 0)
                def _(done=done, rows=rows):
                    fn(pl.multiple_of(lo + done, SEG), pl.multiple_of(go + done, SEG), rows)

        @pl.when(n >= RARE_ROWS)
        def _():
            pieces([r for r in SEG_PIECES if r >= RARE_ROWS])

        pieces([r for r in SEG_PIECES if r < RARE_ROWS])


def _for_each_total_piece(total_ref, tile, fn):
    n = total_ref[tile]
    for rows in TOTAL_PIECES:
        @pl.when((n & rows) != 0)
        def _(rows=rows):
            fn(rows)


def _dispatch_kernel(n16_ref, lo_ref, goff_ref, total_ref, tail_row_ref, tail_n_ref, n_valid_ref,
                     h_ref, ri_ref, tri_ref, xs_ref, dest_ref, sorted_ref, zero_ref, sem):
    i = pl.program_id(0)
    last = pl.num_programs(0) - 1
    slot = i % 2
    tms = h_ref.shape[0]

    ri = ri_ref[0]
    e1 = ri[0:1, :]
    e2 = ri[1:2, :]
    erow = lax.broadcasted_iota(jnp.int32, (N_EXPERTS, tms), 0)
    m1 = erow == e1
    m2 = erow == e2
    sel = jnp.where(m1 | m2, 1.0, 0.0).astype(BF16)
    rank = _dot(sel, tri_ref[...]).astype(jnp.int32)
    start = jnp.zeros((N_EXPERTS, tms), jnp.int32)
    for e in range(N_EXPERTS):
        start = jnp.where(erow == e, lo_ref[i * N_EXPERTS + e], start)
    dest = rank + start
    r1 = jnp.sum(jnp.where(m1, dest, 0), axis=0, keepdims=True)
    r2 = jnp.sum(jnp.where(m2, dest, 0), axis=0, keepdims=True)
    dest_ref[0] = jnp.zeros((8, tms), jnp.int32)
    dest_ref[0, 0:1, :] = r1
    dest_ref[0, 1:2, :] = r2

    rows = lax.broadcasted_iota(jnp.int32, (SORT_ROWS, tms), 0)
    perm = jnp.where((rows == r1) | (rows == r2), 1.0, 0.0).astype(BF16)
    sorted_ref[slot] = _dot(perm, h_ref[...]).astype(BF16)

    def copy(src_slot, local_row, global_row, n):
        return pltpu.make_async_copy(sorted_ref.at[src_slot, pl.ds(local_row, n)],
                                     xs_ref.at[pl.ds(global_row, n)], sem)

    @pl.when(i > 0)
    def _():
        _for_each_total_piece(total_ref, i - 1, lambda n: copy(1 - slot, 0, 0, n).wait())

    _for_each_piece(n16_ref, lo_ref, goff_ref, i, lambda lr, gr, n: copy(slot, lr, gr, n).start())

    @pl.when(i == last)
    def _():
        _for_each_total_piece(total_ref, i, lambda n: copy(slot, 0, 0, n).wait())
        zero_ref[...] = jnp.zeros_like(zero_ref)

        def tail_copies(fn):
            for e in range(N_EXPERTS):
                n = tail_n_ref[e]
                row = tail_row_ref[e]
                for rows_ in TAIL_PIECES:
                    done = (n // (2 * rows_)) * (2 * rows_)

                    @pl.when((n & rows_) != 0)
                    def _(row=row, done=done, rows_=rows_):
                        fn(pltpu.make_async_copy(
                            zero_ref.at[pl.ds(0, rows_)],
                            xs_ref.at[pl.ds(pl.multiple_of(row + done, SEG), rows_)], sem))

        tail_copies(lambda c: c.start())
        tail_copies(lambda c: c.wait())

        def unused_tiles(fn):
            def body(j, carry):
                for part in range(ET // TAIL_PIECES[0]):
                    row = pl.multiple_of(j * ET + part * TAIL_PIECES[0], SEG)
                    fn(pltpu.make_async_copy(zero_ref, xs_ref.at[pl.ds(row, TAIL_PIECES[0])], sem))
                return carry
            lax.fori_loop(n_valid_ref[0], xs_ref.shape[0] // ET, body, 0)

        unused_tiles(lambda c: c.start())
        unused_tiles(lambda c: c.wait())


def _dispatch(h2, route_i, tri, tables, rows_max):
    t, d = h2.shape
    nt = t // TMS
    grid_spec = pltpu.PrefetchScalarGridSpec(
        num_scalar_prefetch=7,
        grid=(nt,),
        in_specs=[
            pl.BlockSpec((TMS, d), lambda i, *_: (i, 0)),
            pl.BlockSpec((1, 8, TMS), lambda i, *_: (i, 0, 0)),
            pl.BlockSpec((TMS, TMS), lambda i, *_: (0, 0)),
        ],
        out_specs=[
            pl.BlockSpec(memory_space=pl.ANY),
            pl.BlockSpec((1, 8, TMS), lambda i, *_: (i, 0, 0)),
        ],
        scratch_shapes=[
            pltpu.VMEM((2, SORT_ROWS, d), BF16),
            pltpu.VMEM((TAIL_PIECES[0], d), BF16),
            pltpu.SemaphoreType.DMA,
        ],
    )
    return pl.pallas_call(
        _dispatch_kernel,
        grid_spec=grid_spec,
        out_shape=[
            jax.ShapeDtypeStruct((rows_max, d), BF16),
            jax.ShapeDtypeStruct((nt, 8, TMS), jnp.int32),
        ],
        compiler_params=_params("arbitrary"),
        name="moe_dispatch",
    )(*tables, h2, route_i, tri)


def _experts_kernel(te_ref, nv_ref, xs_ref, wg_ref, wu_ref, wd_ref, ys_ref):
    j = pl.program_id(0)

    @pl.when(j < nv_ref[0])
    def _():
        h = xs_ref[...]
        gate = _dot(h, wg_ref[0])
        hidden = (gate * jax.nn.sigmoid(gate)) * _dot(h, wu_ref[0])
        ys_ref[...] = _dot(hidden.astype(BF16), wd_ref[0]).astype(BF16)

    @pl.when(j >= nv_ref[0])
    def _():
        ys_ref[...] = jnp.zeros_like(ys_ref)


def _experts(xs, tile_expert, n_valid, w_gate, w_up, w_down):
    rows, d = xs.shape
    row_tile = lambda j, te, nv: (jnp.minimum(j, nv[0] - 1), 0)
    weight = lambda j, te, nv: (te[j], 0, 0)
    grid_spec = pltpu.PrefetchScalarGridSpec(
        num_scalar_prefetch=2,
        grid=(rows // ET,),
        in_specs=[
            pl.BlockSpec((ET, d), row_tile),
            pl.BlockSpec((1, d, D_EXPERT), weight),
            pl.BlockSpec((1, d, D_EXPERT), weight),
            pl.BlockSpec((1, D_EXPERT, d), weight),
        ],
        out_specs=pl.BlockSpec((ET, d), lambda j, te, nv: (j, 0)),
    )
    return pl.pallas_call(
        _experts_kernel,
        grid_spec=grid_spec,
        out_shape=jax.ShapeDtypeStruct((rows, d), BF16),
        compiler_params=_params("arbitrary"),
        name="moe_experts",
    )(tile_expert, n_valid, xs, w_gate, w_up, w_down)


def _combine_kernel(n16_ref, lo_ref, goff_ref, total_ref, ys_ref, dest_ref, w_ref, x_ref, mod_ref,
                    nf_ref, o_ref, buf_ref, sems, *, final):
    i = pl.program_id(0)
    nt = pl.num_programs(0)
    slot = i % 2

    def copy(dst_slot, local_row, global_row, n):
        return pltpu.make_async_copy(ys_ref.at[pl.ds(global_row, n)],
                                     buf_ref.at[dst_slot, pl.ds(local_row, n)], sems.at[dst_slot])

    @pl.when(i == 0)
    def _():
        buf_ref[...] = jnp.zeros_like(buf_ref)
        _for_each_piece(n16_ref, lo_ref, goff_ref, 0, lambda lr, gr, n: copy(0, lr, gr, n).start())

    @pl.when(i + 1 < nt)
    def _():
        _for_each_piece(n16_ref, lo_ref, goff_ref, i + 1,
                        lambda lr, gr, n: copy(1 - slot, lr, gr, n).start())

    _for_each_total_piece(total_ref, i, lambda n: copy(slot, 0, 0, n).wait())

    dest = dest_ref[...]
    w = w_ref[...]
    col = lax.broadcasted_iota(jnp.int32, (dest.shape[0], SORT_ROWS), 1)
    wmat = (jnp.where(col == dest[:, 0:1], w[:, 0:1], 0.0)
            + jnp.where(col == dest[:, 1:2], w[:, 1:2], 0.0))
    y = _dot(wmat.astype(BF16), buf_ref[slot])
    x = x_ref[...] + mod_ref[0, 5:6, :] * y
    if final:
        x = _rms_normalize(x) * nf_ref[...]
    o_ref[...] = x


def _combine(ys, dest_col, w_col, x, mod, norm_final, tables, tiles_per_batch, final):
    t, d = x.shape
    nt = t // TMS
    grid_spec = pltpu.PrefetchScalarGridSpec(
        num_scalar_prefetch=4,
        grid=(nt,),
        in_specs=[
            pl.BlockSpec(memory_space=pl.ANY),
            pl.BlockSpec((TMS, 8), lambda i, *_: (i, 0)),
            pl.BlockSpec((TMS, 8), lambda i, *_: (i, 0)),
            pl.BlockSpec((TMS, d), lambda i, *_: (i, 0)),
            pl.BlockSpec((1, 6, d), lambda i, *_: (i // tiles_per_batch, 0, 0)),
            pl.BlockSpec((1, d), lambda i, *_: (0, 0)),
        ],
        out_specs=pl.BlockSpec((TMS, d), lambda i, *_: (i, 0)),
        scratch_shapes=[pltpu.VMEM((2, SORT_ROWS, d), BF16), pltpu.SemaphoreType.DMA((2,))],
    )
    return pl.pallas_call(
        functools.partial(_combine_kernel, final=final),
        grid_spec=grid_spec,
        out_shape=jax.ShapeDtypeStruct((t, d), F32),
        compiler_params=_params("arbitrary"),
        name="moe_combine_final" if final else "moe_combine",
    )(*tables, ys, dest_col, w_col, x, mod, norm_final)


def _route_tables(counts, n_expert_tiles):
    n16 = (counts + (SEG - 1)) // SEG * SEG
    lo = jnp.cumsum(n16, axis=1) - n16
    total = jnp.sum(n16, axis=0)
    region = (total + (ET - 1)) // ET * ET
    ends = jnp.cumsum(region)
    base = ends - region
    goff = base[None, :] + jnp.cumsum(n16, axis=0) - n16
    tile_row = jnp.arange(n_expert_tiles, dtype=jnp.int32) * ET
    tile_expert = jnp.minimum(jnp.sum(tile_row[:, None] >= ends[None, :], axis=1), N_EXPERTS - 1)
    n_valid = (ends[-1] // ET).reshape(1)
    i32 = lambda a: a.astype(jnp.int32).reshape(-1)
    return (i32(n16), i32(lo), i32(goff), i32(jnp.sum(n16, axis=1))), \
        (i32(base + total), i32(region - total)), i32(tile_expert), i32(n_valid)


def _moe(h2, route_i, route_w, cnt, w_gate, w_up, w_down, x, mod, norm_final, final):
    b, s, d = x.shape
    t = b * s
    nt = t // TMS
    n_expert_tiles = -(-(2 * t + nt * N_EXPERTS * (SEG - 1) + N_EXPERTS * (ET - 1)) // ET)
    counts = jnp.round(cnt[:, :, :, 0]).astype(jnp.int32).reshape(nt, N_EXPERTS)
    seg_tables, tail_tables, tile_expert, n_valid = _route_tables(counts, n_expert_tiles)
    tri = jnp.triu(jnp.ones((TMS, TMS), BF16), k=1)
    xs, dest = _dispatch(h2.reshape(t, d), route_i.reshape(nt, 8, TMS), tri,
                         seg_tables + tail_tables + (n_valid,), n_expert_tiles * ET)
    ys = _experts(xs, tile_expert, n_valid, w_gate, w_up, w_down)
    dest_col = jnp.swapaxes(dest, 1, 2).reshape(t, 8)
    w_col = jnp.swapaxes(route_w.reshape(nt, 8, TMS), 1, 2).reshape(t, 8)
    out = _combine(ys, dest_col, w_col, x.reshape(t, d), mod, norm_final, seg_tables, s // TMS, final)
    return out.reshape(b, s, d)


def _block_diag(w):
    nb, c, _ = w.shape
    eye = jnp.eye(nb, dtype=w.dtype)
    return (eye[:, None, :, None] * w[:, :, None, :]).reshape(nb * c, nb * c)


def _swa_head_perm():
    cols = []
    half = SWA_Q_HEADS // 2
    for p in range(half):
        for hd in (p, p + half):
            cols.extend(range(hd * HEAD_DIM, (hd + 1) * HEAD_DIM))
    return jnp.asarray(cols, jnp.int32)


def kernel(x, c, w_mod, b_mod, norm_mix, norm_ffn, w_in, w_out, out_gain, sinks, conv_w, conv_b,
           lru_wa, lru_ba, lru_wx, lru_bx, lru_lam, fox_bf, w_router_group, b_router_group,
           w_router_expert, b_router_expert, w_gate, w_up, w_down, norm_final):
    depth = w_mod.shape[0]
    b, s, d = x.shape
    assert d == D_MODEL and s % max(TM_PRE, TQ_SWA, TS_LRU, T_FOX, TM_POST) == 0
    assert TM_PRE == T_FOX
    c_rows = jnp.zeros((16, d), F32).at[:b].set(c)
    mod_all = _modulation(c_rows, w_mod, b_mod)[:, :b]
    perm = _swa_head_perm()

    for l in range(depth):
        mod = mod_all[l].reshape(b, 6, d)
        w_in_l = w_in[l]
        w_tok = jnp.concatenate([w_in_l[:, :SWA_WIDTH][:, perm], w_in_l[:, SWA_WIDTH:OFF_QC],
                                 w_in_l[:, OFF_KC:OFF_VC]], axis=1).astype(BF16)
        w_t = jnp.concatenate([w_in_l[:, OFF_QC:OFF_KC].T, w_in_l[:, OFF_VC:OFF_FC].T,
                               jnp.zeros((FORGET_ROWS, d), F32).at[:FOX_HEADS].set(w_in_l[:, OFF_FC:].T)],
                              axis=0).astype(BF16)
        fb = jnp.zeros((FORGET_ROWS, 1), F32).at[:FOX_HEADS, 0].set(fox_bf[l])
        gain = out_gain[l]
        gain_a = gain[:SWA_WIDTH][perm].reshape(1, SWA_WIDTH)
        gain_b = gain[SWA_WIDTH:SWA_WIDTH + LRU_WIDTH].reshape(1, LRU_WIDTH)
        gain_c = gain[SWA_WIDTH + LRU_WIDTH:].reshape(1, FOX_WIDTH)
        w_out_l = jnp.concatenate([w_out[l][:SWA_WIDTH][perm], w_out[l][SWA_WIDTH:]], axis=0).astype(BF16)
        wr = jnp.zeros((ROUTER_ROWS, d), F32)
        wr = wr.at[:N_GROUPS].set(w_router_group[l].T).at[8:8 + N_EXPERTS].set(w_router_expert[l].T)
        wr_hi = wr.astype(BF16)
        wr_lo = (wr - wr_hi.astype(F32)).astype(BF16)
        br = jnp.zeros((ROUTER_ROWS, 1), F32)
        br = br.at[:N_GROUPS, 0].set(b_router_group[l]).at[8:8 + N_EXPERTS, 0].set(b_router_expert[l])

        qa, ka, va, xb, gb, kc, qct, vaug, lf = _pre_mixer(
            x, mod, norm_mix[l].reshape(1, d), w_tok, w_t, fb)
        cum = _cumsum(lf)
        kaug = _fox_prep(kc, jnp.swapaxes(cum, 1, 2))
        base = cum[:, :FOX_HEADS, ::T_FOX].reshape(-1)
        ya = _swa(qa, ka, va, sinks[l].reshape(1, SWA_Q_HEADS))
        yb = _lru(xb, gb, conv_w[l], conv_b[l].reshape(1, -1),
                  _block_diag(lru_wa[l]).astype(BF16), lru_ba[l].reshape(1, -1),
                  _block_diag(lru_wx[l]).astype(BF16), lru_bx[l].reshape(1, -1),
                  lru_lam[l].reshape(1, -1))
        yc = _fox(qct, kaug, vaug, base)
        x, h2, route_i, route_w, cnt = _post_mixer(
            ya, yb, yc, x, mod, gain_a, gain_b, gain_c, w_out_l,
            norm_ffn[l].reshape(1, d), wr_hi, wr_lo, br)
        x = _moe(h2, route_i, route_w, cnt, w_gate[l].astype(BF16), w_up[l].astype(BF16),
                 w_down[l].astype(BF16), x, mod, norm_final.reshape(1, d), final=(l == depth - 1))
    return x
```

```python
import functools

import jax
import jax.numpy as jnp
from jax import lax
from jax.experimental import pallas as pl
from jax.experimental.pallas import tpu as pltpu

F32 = jnp.float32
BF16 = jnp.bfloat16

D_MODEL = 1024
HEAD_DIM = 64
PAIR = 2 * HEAD_DIM
BLOCK = 128
SWA_Q_HEADS = 8
SWA_WIDTH = SWA_Q_HEADS * HEAD_DIM
KV_WIDTH = 2 * HEAD_DIM
LRU_WIDTH = 256
LRU_BLOCKS = 8
CONV_WIDTH = 4
LRU_C = 8.0
FOX_HEADS = 4
FOX_WIDTH = FOX_HEADS * HEAD_DIM
N_GROUPS = 4
EXPERTS_PER_GROUP = 4
N_EXPERTS = 16
D_EXPERT = 256
EPS = 1e-6
NEG = -1e30
LOG2E = 1.4426950408889634

OFF_KA = SWA_WIDTH
OFF_VA = OFF_KA + KV_WIDTH
OFF_XB = OFF_VA + KV_WIDTH
OFF_GB = OFF_XB + LRU_WIDTH
OFF_QC = OFF_GB + LRU_WIDTH
OFF_KC = OFF_QC + FOX_WIDTH
OFF_VC = OFF_KC + FOX_WIDTH
OFF_FC = OFF_VC + FOX_WIDTH

ROUTER_ROWS = 32
FORGET_ROWS = 16
V_ROWS = HEAD_DIM + 16
VMEM_LIMIT_BYTES = 48 * 1024 * 1024

TM_PRE = 512
TQ_SWA = 512
TS_LRU = 512
T_FOX = 512
TM_POST = 512
CUMSUM_CHUNK = 256


def _params(*semantics):
    return pltpu.CompilerParams(dimension_semantics=semantics, vmem_limit_bytes=VMEM_LIMIT_BYTES)


def _split_bf16(v):
    hi = v.astype(BF16)
    lo = (v - hi.astype(F32)).astype(BF16)
    return hi, lo


def _dot(a, b):
    return jnp.dot(a, b, preferred_element_type=F32)


def _dot_nt(a, b):
    return lax.dot_general(a, b, (((1,), (1,)), ((), ())), preferred_element_type=F32)


def _rms_normalize(v):
    return v * lax.rsqrt(jnp.mean(v * v, axis=-1, keepdims=True) + EPS)


def _mod_kernel(c_ref, w_ref, b_ref, o_ref):
    c = c_ref[...]
    ca = c * jax.nn.sigmoid(c)
    w = w_ref[0]
    a_hi, a_lo = _split_bf16(ca)
    w_hi, w_lo = _split_bf16(w)
    o_ref[0] = _dot(a_hi, w_hi) + _dot(a_lo, w_hi) + _dot(a_hi, w_lo) + b_ref[0]


def _modulation(c, w_mod, b_mod):
    depth, d, d6 = w_mod.shape
    b = c.shape[0]
    n = d6 // d
    return pl.pallas_call(
        _mod_kernel,
        grid=(depth, n),
        in_specs=[
            pl.BlockSpec((b, d), lambda l, j: (0, 0)),
            pl.BlockSpec((1, d, d), lambda l, j: (l, 0, j)),
            pl.BlockSpec((1, 1, d), lambda l, j: (l, 0, j)),
        ],
        out_specs=pl.BlockSpec((1, b, d), lambda l, j: (l, 0, j)),
        out_shape=jax.ShapeDtypeStruct((depth, b, d6), F32),
        compiler_params=_params("arbitrary", "arbitrary"),
        name="modulation",
    )(c, w_mod, b_mod.reshape(depth, 1, d6))


def _log_sigmoid(z):
    return jnp.minimum(z, 0.0) - jnp.log1p(jnp.exp(-jnp.abs(z)))


def _other_half(a):
    return HEAD_DIM * (1 - a)


def _pre_kernel(x_ref, mod_ref, g_ref, w_ref, wt_ref, fb_ref,
                qa_ref, ka_ref, va_ref, xb_ref, gb_ref, kc_ref, qct_ref, vct_ref, lf_ref):
    x = x_ref[0]
    h = (_rms_normalize(x) * g_ref[...]) * (1.0 + mod_ref[0, 1:2, :]) + mod_ref[0, 0:1, :]
    hb = h.astype(BF16)

    def proj(lo, width):
        return _dot(hb, w_ref[:, lo:lo + width])

    scale = HEAD_DIM ** -0.5
    qa_ref[0] = (proj(0, SWA_WIDTH) * scale).astype(BF16)
    ka_ref[0] = proj(OFF_KA, KV_WIDTH).astype(BF16)
    va_ref[0] = proj(OFF_VA, KV_WIDTH).astype(BF16)
    xb_ref[0] = proj(OFF_XB, LRU_WIDTH).astype(BF16)
    gb_ref[0] = proj(OFF_GB, LRU_WIDTH).astype(BF16)
    kc_ref[0] = proj(OFF_QC, FOX_WIDTH).astype(BF16)

    t_all = _dot_nt(wt_ref[...], hb)
    qct_ref[0, 0] = (t_all[0:FOX_WIDTH] * (scale * LOG2E)).astype(BF16)
    ones_row = jnp.where(lax.broadcasted_iota(jnp.int32, (V_ROWS - HEAD_DIM, t_all.shape[1]), 0) == 0,
                         1.0, 0.0)
    for hd in range(FOX_HEADS):
        v_head = t_all[FOX_WIDTH + hd * HEAD_DIM:FOX_WIDTH + (hd + 1) * HEAD_DIM]
        vct_ref[0, 0, hd] = jnp.concatenate([v_head, ones_row], axis=0).astype(BF16)
    lf_ref[0] = _log_sigmoid(t_all[2 * FOX_WIDTH:] + fb_ref[...])


def _pre_mixer(x, mod, gain, w_tok, w_t, fb):
    b, s, d = x.shape
    tm = TM_PRE
    nt = s // tm

    def tok(width):
        return pl.BlockSpec((1, tm, width), lambda bi, i: (bi, i, 0))

    def tok_shape(width):
        return jax.ShapeDtypeStruct((b, s, width), BF16)

    return pl.pallas_call(
        _pre_kernel,
        grid=(b, nt),
        in_specs=[
            tok(d),
            pl.BlockSpec((1, 6, d), lambda bi, i: (bi, 0, 0)),
            pl.BlockSpec((1, d), lambda bi, i: (0, 0)),
            pl.BlockSpec(w_tok.shape, lambda bi, i: (0, 0)),
            pl.BlockSpec(w_t.shape, lambda bi, i: (0, 0)),
            pl.BlockSpec((FORGET_ROWS, 1), lambda bi, i: (0, 0)),
        ],
        out_specs=[
            tok(SWA_WIDTH), tok(KV_WIDTH), tok(KV_WIDTH), tok(LRU_WIDTH), tok(LRU_WIDTH),
            tok(FOX_WIDTH),
            pl.BlockSpec((1, 1, FOX_WIDTH, tm), lambda bi, i: (bi, i, 0, 0)),
            pl.BlockSpec((1, 1, FOX_HEADS, V_ROWS, tm), lambda bi, i: (bi, i, 0, 0, 0)),
            pl.BlockSpec((1, FORGET_ROWS, tm), lambda bi, i: (bi, 0, i)),
        ],
        out_shape=[
            tok_shape(SWA_WIDTH), tok_shape(KV_WIDTH), tok_shape(KV_WIDTH), tok_shape(LRU_WIDTH),
            tok_shape(LRU_WIDTH), tok_shape(FOX_WIDTH),
            jax.ShapeDtypeStruct((b, nt, FOX_WIDTH, tm), BF16),
            jax.ShapeDtypeStruct((b, nt, FOX_HEADS, V_ROWS, tm), BF16),
            jax.ShapeDtypeStruct((b, FORGET_ROWS, s), F32),
        ],
        compiler_params=_params("parallel", "parallel"),
        name="pre_mixer",
    )(x, mod, gain, w_tok, w_t, fb)


def _cumsum_kernel(lf_ref, o_ref):
    n = CUMSUM_CHUNK
    s = lf_ref.shape[2]
    r = lax.broadcasted_iota(jnp.int32, (n, n), 0)
    c = lax.broadcasted_iota(jnp.int32, (n, n), 1)
    tri = jnp.where(r <= c, 1.0, 0.0).astype(BF16)
    carry = jnp.zeros((lf_ref.shape[1], 1), F32)
    for j in range(s // n):
        seg = lf_ref[0, :, j * n:(j + 1) * n]
        hi = seg.astype(BF16)
        rem = seg - hi.astype(F32)
        mid = rem.astype(BF16)
        lo = (rem - mid.astype(F32)).astype(BF16)
        cs = (_dot(hi, tri) + _dot(mid, tri)) + _dot(lo, tri) + carry
        o_ref[0, :, j * n:(j + 1) * n] = cs
        carry = cs[:, n - 1:n]


def _cumsum(lf):
    b, r, s = lf.shape
    return pl.pallas_call(
        _cumsum_kernel,
        grid=(b,),
        in_specs=[pl.BlockSpec((1, r, s), lambda bi: (bi, 0, 0))],
        out_specs=pl.BlockSpec((1, r, s), lambda bi: (bi, 0, 0)),
        out_shape=jax.ShapeDtypeStruct((b, r, s), F32),
        compiler_params=_params("parallel"),
        name="cumsum",
    )(lf)


def _swa_kernel(sink_ref, q_ref, kc_ref, vc_ref, kp_ref, vp_ref, o_ref, kf_ref, vf_ref, bias_ref):
    i = pl.program_id(1)
    nsub = q_ref.shape[1] // BLOCK
    npair = SWA_Q_HEADS // 2

    @pl.when((pl.program_id(0) == 0) & (i == 0))
    def _():
        qpos = BLOCK + lax.broadcasted_iota(jnp.int32, (BLOCK, 2 * BLOCK), 0)
        kpos = lax.broadcasted_iota(jnp.int32, (BLOCK, 2 * BLOCK), 1)
        dist = qpos - kpos
        valid = (dist >= 0) & (dist < BLOCK)
        distf = dist.astype(F32)
        for hd in range(SWA_Q_HEADS):
            slope = 2.0 ** (-8.0 * (hd + 1) / SWA_Q_HEADS)
            bias_ref[hd] = jnp.where(valid, -slope * distf, NEG)

    kf_ref[0:BLOCK, :] = kp_ref[0]
    kf_ref[BLOCK:, :] = kc_ref[0]
    vf_ref[0:BLOCK, :] = vp_ref[0]
    vf_ref[BLOCK:, :] = vc_ref[0]

    lane = lax.broadcasted_iota(jnp.int32, (2 * BLOCK, PAIR), 1)
    lane_o = lax.broadcasted_iota(jnp.int32, (BLOCK, PAIR), 1)
    col = lax.broadcasted_iota(jnp.int32, (BLOCK, 2 * BLOCK), 1)

    for j in range(nsub):
        k2 = kf_ref[j * BLOCK:(j + 2) * BLOCK, :]
        v2 = vf_ref[j * BLOCK:(j + 2) * BLOCK, :]
        k_half = (jnp.where(lane < HEAD_DIM, k2, jnp.zeros_like(k2)),
                  jnp.where(lane >= HEAD_DIM, k2, jnp.zeros_like(k2)))
        for p in range(npair):
            qp = q_ref[0, j * BLOCK:(j + 1) * BLOCK, p * PAIR:(p + 1) * PAIR]
            outs = []
            for a in range(2):
                hd = p + a * npair
                sc = _dot_nt(qp, k_half[a]) + bias_ref[hd]
                if j == 0:
                    sc = sc + jnp.where(col < BLOCK, jnp.where(i == 0, NEG, 0.0), 0.0)
                sink = sink_ref[0, hd]
                m = jnp.maximum(jnp.max(sc, axis=-1, keepdims=True), sink)
                pr = jnp.exp(sc - m)
                den = jnp.sum(pr, axis=-1, keepdims=True) + jnp.exp(sink - m)
                outs.append(_dot(pr.astype(BF16), v2) / den)
            o_ref[0, j * BLOCK:(j + 1) * BLOCK, p * PAIR:(p + 1) * PAIR] = jnp.where(
                lane_o < HEAD_DIM, outs[0], outs[1]).astype(BF16)


def _swa(qa, ka, va, sinks):
    b, s, _ = qa.shape
    tq = TQ_SWA
    nsub = tq // BLOCK
    prev = lambda bi, i: (bi, jnp.maximum(i * nsub - 1, 0), 0)
    return pl.pallas_call(
        _swa_kernel,
        grid=(b, s // tq),
        in_specs=[
            pl.BlockSpec(memory_space=pltpu.SMEM),
            pl.BlockSpec((1, tq, SWA_WIDTH), lambda bi, i: (bi, i, 0)),
            pl.BlockSpec((1, tq, KV_WIDTH), lambda bi, i: (bi, i, 0)),
            pl.BlockSpec((1, tq, KV_WIDTH), lambda bi, i: (bi, i, 0)),
            pl.BlockSpec((1, BLOCK, KV_WIDTH), prev),
            pl.BlockSpec((1, BLOCK, KV_WIDTH), prev),
        ],
        out_specs=pl.BlockSpec((1, tq, SWA_WIDTH), lambda bi, i: (bi, i, 0)),
        out_shape=jax.ShapeDtypeStruct((b, s, SWA_WIDTH), BF16),
        scratch_shapes=[
            pltpu.VMEM((tq + BLOCK, KV_WIDTH), BF16),
            pltpu.VMEM((tq + BLOCK, KV_WIDTH), BF16),
            pltpu.VMEM((SWA_Q_HEADS, BLOCK, 2 * BLOCK), F32),
        ],
        compiler_params=_params("arbitrary", "arbitrary"),
        name="swa",
    )(sinks, qa, ka, va, ka, va)


def _gelu_tanh(v):
    return 0.5 * v * (1.0 + jnp.tanh(0.7978845608028654 * (v + 0.044715 * (v * v * v))))


def _lru_kernel(xb_ref, gb_ref, cw_ref, cb_ref, wa_ref, ba_ref, wx_ref, bx_ref, lam_ref,
                o_ref, win_ref, h_ref):
    ts = xb_ref.shape[1]

    @pl.when(pl.program_id(1) == 0)
    def _():
        win_ref[0:8, :] = jnp.zeros((8, LRU_WIDTH), F32)
        h_ref[...] = jnp.zeros_like(h_ref)

    x = xb_ref[0].astype(F32)
    win_ref[8:, :] = x
    conv = cb_ref[...] + cw_ref[CONV_WIDTH - 1:CONV_WIDTH, :] * x
    for k in range(1, CONV_WIDTH):
        conv = conv + cw_ref[CONV_WIDTH - 1 - k:CONV_WIDTH - k, :] * win_ref[8 - k:8 - k + ts, :]
    win_ref[0:8, :] = x[ts - 8:ts]

    cbf = conv.astype(BF16)
    r = jax.nn.sigmoid(_dot(cbf, wa_ref[...]) + ba_ref[...])
    gi = jax.nn.sigmoid(_dot(cbf, wx_ref[...]) + bx_ref[...])
    z = -lam_ref[...]
    softplus = jnp.maximum(z, 0.0) + jnp.log1p(jnp.exp(-jnp.abs(z)))
    log_a = (-LRU_C * softplus) * r
    a = jnp.exp(log_a)
    v = 1.0 - a * a
    u = jnp.where(v == 0.0, 0.0, v * lax.rsqrt(v)) * (gi * conv)

    a = a.reshape(ts // 8, 8, LRU_WIDTH)
    u = u.reshape(ts // 8, 8, LRU_WIDTH)
    sub = lax.broadcasted_iota(jnp.int32, a.shape, 1)
    for step in (1, 2, 4):
        keep = sub >= step
        u = jnp.where(keep, a * pltpu.roll(u, step, 1) + u, u)
        a = jnp.where(keep, a * pltpu.roll(a, step, 1), a)
    carry = h_ref[7:8, :]
    groups = []
    for g in range(ts // 8):
        hg = u[g] + a[g] * carry
        groups.append(hg)
        carry = hg[7:8, :]
    h = jnp.concatenate(groups, axis=0)
    h_ref[...] = groups[-1]
    o_ref[0] = (h * _gelu_tanh(gb_ref[0].astype(F32))).astype(BF16)


def _lru(xb, gb, conv_w, conv_b, wa, ba, wx, bx, lam):
    b, s, w = xb.shape
    ts = TS_LRU
    tok = pl.BlockSpec((1, ts, w), lambda bi, i: (bi, i, 0))
    full = lambda shape: pl.BlockSpec(shape, lambda bi, i: (0,) * len(shape))
    return pl.pallas_call(
        _lru_kernel,
        grid=(b, s // ts),
        in_specs=[tok, tok, full((CONV_WIDTH, w)), full((1, w)), full((w, w)), full((1, w)),
                  full((w, w)), full((1, w)), full((1, w))],
        out_specs=tok,
        out_shape=jax.ShapeDtypeStruct((b, s, w), BF16),
        scratch_shapes=[pltpu.VMEM((8 + ts, w), F32), pltpu.VMEM((8, w), F32)],
        compiler_params=_params("arbitrary", "arbitrary"),
        name="lru",
    )(xb, gb, conv_w, conv_b, wa, ba, wx, bx, lam)


BIAS_ROWS = 6


def _split3(v):
    hi = v.astype(BF16).astype(F32)
    rem = v - hi
    mid = rem.astype(BF16).astype(F32)
    lo = (rem - mid).astype(BF16).astype(F32)
    return hi, mid, lo


def _fox_prep_kernel(k_ref, c_ref, o_ref):
    tm = k_ref.shape[1]
    lane = lax.broadcasted_iota(jnp.int32, (tm, PAIR), 1)
    for p in range(FOX_WIDTH // PAIR):
        kp = k_ref[0, :, p * PAIR:(p + 1) * PAIR].astype(F32)
        for a in range(2):
            hd = 2 * p + a
            c_hi, c_mid, c_lo = _split3(c_ref[0, :, hd:hd + 1] * LOG2E)
            rel = lane - _other_half(a)
            aug = jnp.where(rel == 0, c_hi, jnp.where(rel == 1, c_mid, jnp.where(
                rel == 2, c_lo, jnp.where(rel < BIAS_ROWS, 1.0, 0.0))))
            in_half = (lane >= HEAD_DIM * a) & (lane < HEAD_DIM * (a + 1))
            o_ref[0, hd] = jnp.where(in_half, kp, aug).astype(BF16)


def _fox_prep(kc, cum_col):
    b, s, w = kc.shape
    tm = TM_PRE
    return pl.pallas_call(
        _fox_prep_kernel,
        grid=(b, s // tm),
        in_specs=[
            pl.BlockSpec((1, tm, w), lambda bi, i: (bi, i, 0)),
            pl.BlockSpec((1, tm, cum_col.shape[2]), lambda bi, i: (bi, i, 0)),
        ],
        out_specs=pl.BlockSpec((1, FOX_HEADS, tm, PAIR), lambda bi, i: (bi, 0, i, 0)),
        out_shape=jax.ShapeDtypeStruct((b, FOX_HEADS, s, PAIR), BF16),
        compiler_params=_params("parallel", "parallel"),
        name="fox_prep",
    )(kc, cum_col)


def _fox_kernel(base_ref, qt_ref, ka_ref, vt_ref, o_ref, acc_ref, s0_ref, s1_ref):
    bi = pl.program_id(0)
    p = pl.program_id(1)
    qi = pl.program_id(2)
    nq = pl.num_programs(2)
    t = qt_ref.shape[3]
    qt = qt_ref[0, 0].astype(F32)
    row = lax.broadcasted_iota(jnp.int32, (PAIR, t), 0)
    causal = (lax.broadcasted_iota(jnp.int32, (t, t), 0) <= lax.broadcasted_iota(jnp.int32, (t, t), 1))
    q_aug = []
    for a in range(2):
        hd = 2 * p + a
        base = jnp.full((1, t), base_ref[(bi * FOX_HEADS + hd) * nq + qi], F32) * LOG2E
        b_hi, b_mid, b_lo = _split3(base)
        rel = row - _other_half(a)
        aug = jnp.where(rel < 3, -1.0, jnp.where(rel == 3, b_hi, jnp.where(
            rel == 4, b_mid, jnp.where(rel == 5, b_lo, 0.0))))
        in_half = (row >= HEAD_DIM * a) & (row < HEAD_DIM * (a + 1))
        q_aug.append(jnp.where(in_half, qt, aug).astype(BF16))
        acc_ref[a] = jnp.zeros((V_ROWS, t), F32)

    slots = (s0_ref, s1_ref)

    def scores(ki, slot, masked):
        tile_max = []
        for a in range(2):
            k = ka_ref[0, a, pl.ds(pl.multiple_of(ki * t, t), t), :]
            st = _dot(k, q_aug[a])
            if masked:
                st = jnp.where(causal, st, NEG)
            slots[slot][a] = st
            tile_max.append(jnp.max(st, axis=0, keepdims=True))
        return tuple(tile_max)

    def consume(ki, slot, m_old, tile_max):
        m_new = []
        for a in range(2):
            m_a = jnp.maximum(m_old[a], tile_max[a])
            alpha = jnp.exp2(m_old[a] - m_a)
            pr = jnp.exp2(slots[slot][a] - m_a).astype(BF16)
            acc_ref[a] = alpha * acc_ref[a] + _dot(vt_ref[0, ki, a], pr)
            m_new.append(m_a)
        return tuple(m_new)

    def advance(j, slot, carry):
        m_old, tile_max, prev = carry
        next_max = scores(j, 1 - slot, False)
        return consume(prev, slot, m_old, tile_max), next_max, j

    def body(i, carry):
        return advance(2 * i + 1, 1, advance(2 * i, 0, carry))

    m0 = jnp.full((1, t), NEG, F32)
    carry = ((m0, m0), scores(qi, 0, True), qi)
    carry = lax.fori_loop(0, qi // 2, body, carry)

    def finish_odd(carry):
        m, tile_max, prev = advance(qi - 1, 0, carry)
        consume(prev, 1, m, tile_max)
        return 0

    def finish_even(carry):
        m, tile_max, prev = carry
        consume(prev, 0, m, tile_max)
        return 0

    lax.cond(qi % 2 == 1, finish_odd, finish_even, carry)
    outs = []
    for a in range(2):
        acc = acc_ref[a]
        outs.append(acc[0:HEAD_DIM] / acc[HEAD_DIM:HEAD_DIM + 1, :])
    o_ref[0] = jnp.concatenate(outs, axis=0).T.astype(BF16)


def _fox(qct, kaug, vaug, base):
    b, nt, w, t = qct.shape
    s = nt * t
    npair = w // PAIR
    return pl.pallas_call(
        _fox_kernel,
        grid=(b, npair, nt),
        in_specs=[
            pl.BlockSpec(memory_space=pltpu.SMEM),
            pl.BlockSpec((1, 1, PAIR, t), lambda bi, p, qi: (bi, qi, p, 0)),
            pl.BlockSpec((1, 2, s, PAIR), lambda bi, p, qi: (bi, p, 0, 0)),
            pl.BlockSpec((1, nt, 2, V_ROWS, t), lambda bi, p, qi: (bi, 0, p, 0, 0)),
        ],
        out_specs=pl.BlockSpec((1, t, PAIR), lambda bi, p, qi: (bi, qi, p)),
        out_shape=jax.ShapeDtypeStruct((b, s, w), BF16),
        scratch_shapes=[pltpu.VMEM((2, V_ROWS, t), F32), pltpu.VMEM((2, t, t), F32),
                        pltpu.VMEM((2, t, t), F32)],
        compiler_params=_params("parallel", "parallel", "arbitrary"),
        name="fox",
    )(base, qct, kaug, vaug)


def _first_max(vals):
    best = vals[0]
    for v in vals[1:]:
        best = jnp.maximum(best, v)
    idx = jnp.full(best.shape, len(vals) - 1, jnp.int32)
    for j in range(len(vals) - 2, -1, -1):
        idx = jnp.where(vals[j] == best, j, idx)
    return best, idx


def _softmax_rows(vals):
    top = vals[0]
    for v in vals[1:]:
        top = jnp.maximum(top, v)
    ex = [jnp.exp(v - top) for v in vals]
    tot = ex[0]
    for e in ex[1:]:
        tot = tot + e
    return [e / tot for e in ex]


def _post_kernel(ya_ref, yb_ref, yc_ref, x_ref, mod_ref, ga_ref, gb_ref, gc_ref, wo_ref, nf_ref,
                 wr_hi_ref, wr_lo_ref, br_ref, xo_ref, h2_ref, ri_ref, rw_ref, cnt_ref):
    ya = _rms_normalize(ya_ref[0].astype(F32)) * ga_ref[...]
    yb = _rms_normalize(yb_ref[0].astype(F32)) * gb_ref[...]
    yc = _rms_normalize(yc_ref[0].astype(F32)) * gc_ref[...]
    y = jnp.concatenate([ya, yb, yc], axis=-1).astype(BF16)
    x = x_ref[0] + mod_ref[0, 2:3, :] * _dot(y, wo_ref[...])
    xo_ref[0] = x
    h = (_rms_normalize(x) * nf_ref[...]) * (1.0 + mod_ref[0, 4:5, :]) + mod_ref[0, 3:4, :]
    h_hi, h_lo = _split_bf16(h)
    h2_ref[0] = h_hi

    w_hi = wr_hi_ref[...]
    logits = (_dot_nt(w_hi, h_hi) + _dot_nt(w_hi, h_lo)) + _dot_nt(wr_lo_ref[...], h_hi) + br_ref[...]
    rows = [logits[r:r + 1, :] for r in range(8 + N_EXPERTS)]
    group_prob = _softmax_rows(rows[0:N_GROUPS])
    group_p, group_idx = _first_max(group_prob)
    in_group = []
    for j in range(EXPERTS_PER_GROUP):
        v = rows[8 + (N_GROUPS - 1) * EXPERTS_PER_GROUP + j]
        for g in range(N_GROUPS - 2, -1, -1):
            v = jnp.where(group_idx == g, rows[8 + g * EXPERTS_PER_GROUP + j], v)
        in_group.append(v)
    expert_prob = _softmax_rows(in_group)
    p1, i1 = _first_max(expert_prob)
    rest = [jnp.where(i1 == j, -1.0, expert_prob[j]) for j in range(EXPERTS_PER_GROUP)]
    p2, i2 = _first_max(rest)
    e1 = group_idx * EXPERTS_PER_GROUP + i1
    e2 = group_idx * EXPERTS_PER_GROUP + i2
    tm = e1.shape[1]
    ri_ref[0, 0] = jnp.zeros((8, tm), jnp.int32)
    ri_ref[0, 0, 0:1, :] = e1
    ri_ref[0, 0, 1:2, :] = e2
    rw_ref[0, 0] = jnp.zeros((8, tm), F32)
    rw_ref[0, 0, 0:1, :] = group_p * p1 / (p1 + p2)
    rw_ref[0, 0, 1:2, :] = group_p * p2 / (p1 + p2)
    erow = lax.broadcasted_iota(jnp.int32, (N_EXPERTS, tm), 0)
    sel = jnp.where((erow == e1) | (erow == e2), 1.0, 0.0).astype(BF16)
    cnt_ref[0, 0] = _dot(sel, jnp.ones((tm, 128), BF16))


def _post_mixer(ya, yb, yc, x, mod, gain_a, gain_b, gain_c, w_out, norm_ffn, wr_hi, wr_lo, br):
    b, s, d = x.shape
    tm = TM_POST
    tok = lambda width: pl.BlockSpec((1, tm, width), lambda bi, i: (bi, i, 0))
    full = lambda shape: pl.BlockSpec(shape, lambda bi, i: (0,) * len(shape))
    return pl.pallas_call(
        _post_kernel,
        grid=(b, s // tm),
        in_specs=[
            tok(SWA_WIDTH), tok(LRU_WIDTH), tok(FOX_WIDTH), tok(d),
            pl.BlockSpec((1, 6, d), lambda bi, i: (bi, 0, 0)),
            full((1, SWA_WIDTH)), full((1, LRU_WIDTH)), full((1, FOX_WIDTH)),
            full((d, d)), full((1, d)),
            full((ROUTER_ROWS, d)), full((ROUTER_ROWS, d)), full((ROUTER_ROWS, 1)),
        ],
        out_specs=[
            tok(d), tok(d),
            pl.BlockSpec((1, 1, 8, tm), lambda bi, i: (bi, i, 0, 0)),
            pl.BlockSpec((1, 1, 8, tm), lambda bi, i: (bi, i, 0, 0)),
            pl.BlockSpec((1, 1, N_EXPERTS, 128), lambda bi, i: (bi, i, 0, 0)),
        ],
        out_shape=[
            jax.ShapeDtypeStruct((b, s, d), F32),
            jax.ShapeDtypeStruct((b, s, d), BF16),
            jax.ShapeDtypeStruct((b, s // tm, 8, tm), jnp.int32),
            jax.ShapeDtypeStruct((b, s // tm, 8, tm), F32),
            jax.ShapeDtypeStruct((b, s // tm, N_EXPERTS, 128), F32),
        ],
        compiler_params=_params("parallel", "parallel"),
        name="post_mixer",
    )(ya, yb, yc, x, mod, gain_a, gain_b, gain_c, w_out, norm_ffn, wr_hi, wr_lo, br)


TMS = TM_POST
SEG = 16
SORT_ROWS = 1280
ET = 1024
SEG_PIECES = (512, 256, 128, 64, 32, 16)
TAIL_PIECES = (512, 256, 128, 64, 32, 16)
TOTAL_PIECES = (1024, 512, 256, 128, 64, 32, 16)
RARE_ROWS = 128
assert SORT_ROWS >= 2 * TMS + N_EXPERTS * (SEG - 1) and SEG_PIECES[0] == TMS and TAIL_PIECES[0] * 2 == ET


def _for_each_piece(n16_ref, lo_ref, goff_ref, tile, fn):
    for e in range(N_EXPERTS):
        n = n16_ref[tile * N_EXPERTS + e]
        lo = lo_ref[tile * N_EXPERTS + e]
        go = goff_ref[tile * N_EXPERTS + e]

        def pieces(sizes, n=n, lo=lo, go=go):
            for rows in sizes:
                done = n & (-2 * rows)

                @pl.when((n & rows) != 0)
                def _(done=done, rows=rows):
                    fn(pl.multiple_of(lo + done, SEG), pl.multiple_of(go + done, SEG), rows)

        @pl.when(n >= RARE_ROWS)
        def _():
            pieces([r for r in SEG_PIECES if r >= RARE_ROWS])

        pieces([r for r in SEG_PIECES if r < RARE_ROWS])


def _for_each_total_piece(total_ref, tile, fn):
    n = total_ref[tile]
    for rows in TOTAL_PIECES:
        @pl.when((n & rows) != 0)
        def _(rows=rows):
            fn(rows)


def _dispatch_kernel(n16_ref, lo_ref, goff_ref, total_ref, tail_row_ref, tail_n_ref, n_valid_ref,
                     h_ref, ri_ref, tri_ref, xs_ref, dest_ref, sorted_ref, zero_ref, sem):
    i = pl.program_id(0)
    last = pl.num_programs(0) - 1
    slot = i & 1
    tms = h_ref.shape[0]

    ri = ri_ref[0]
    e1 = ri[0:1, :]
    e2 = ri[1:2, :]
    erow = lax.broadcasted_iota(jnp.int32, (N_EXPERTS, tms), 0)
    m1 = erow == e1
    m2 = erow == e2
    sel = jnp.where(m1 | m2, 1.0, 0.0).astype(BF16)
    rank = _dot(sel, tri_ref[...]).astype(jnp.int32)
    start = jnp.zeros((N_EXPERTS, tms), jnp.int32)
    for e in range(N_EXPERTS):
        start = jnp.where(erow == e, lo_ref[i * N_EXPERTS + e], start)
    dest = rank + start
    r1 = jnp.sum(jnp.where(m1, dest, 0), axis=0, keepdims=True)
    r2 = jnp.sum(jnp.where(m2, dest, 0), axis=0, keepdims=True)
    dest_ref[0] = jnp.zeros((8, tms), jnp.int32)
    dest_ref[0, 0:1, :] = r1
    dest_ref[0, 1:2, :] = r2

    rows = lax.broadcasted_iota(jnp.int32, (SORT_ROWS, tms), 0)
    perm = jnp.where((rows == r1) | (rows == r2), 1.0, 0.0).astype(BF16)
    sorted_ref[slot] = _dot(perm, h_ref[...]).astype(BF16)

    def copy(src_slot, local_row, global_row, n):
        return pltpu.make_async_copy(sorted_ref.at[src_slot, pl.ds(local_row, n)],
                                     xs_ref.at[pl.ds(global_row, n)], sem)

    @pl.when(i > 0)
    def _():
        _for_each_total_piece(total_ref, i - 1, lambda n: copy(1 - slot, 0, 0, n).wait())

    _for_each_piece(n16_ref, lo_ref, goff_ref, i, lambda lr, gr, n: copy(slot, lr, gr, n).start())

    @pl.when(i == last)
    def _():
        _for_each_total_piece(total_ref, i, lambda n: copy(slot, 0, 0, n).wait())
        zero_ref[...] = jnp.zeros_like(zero_ref)

        def tail_copies(fn):
            for e in range(N_EXPERTS):
                n = tail_n_ref[e]
                row = tail_row_ref[e]
                for rows_ in TAIL_PIECES:
                    done = n & (-2 * rows_)

                    @pl.when((n & rows_) != 0)
                    def _(row=row, done=done, rows_=rows_):
                        fn(pltpu.make_async_copy(
                            zero_ref.at[pl.ds(0, rows_)],
                            xs_ref.at[pl.ds(pl.multiple_of(row + done, SEG), rows_)], sem))

        tail_copies(lambda c: c.start())
        tail_copies(lambda c: c.wait())

        def unused_tiles(fn):
            def body(j, carry):
                for part in range(ET // TAIL_PIECES[0]):
                    row = pl.multiple_of(j * ET + part * TAIL_PIECES[0], SEG)
                    fn(pltpu.make_async_copy(zero_ref, xs_ref.at[pl.ds(row, TAIL_PIECES[0])], sem))
                return carry
            lax.fori_loop(n_valid_ref[0], xs_ref.shape[0] // ET, body, 0)

        unused_tiles(lambda c: c.start())
        unused_tiles(lambda c: c.wait())


def _dispatch(h2, route_i, tri, tables, rows_max):
    t, d = h2.shape
    nt = t // TMS
    grid_spec = pltpu.PrefetchScalarGridSpec(
        num_scalar_prefetch=7,
        grid=(nt,),
        in_specs=[
            pl.BlockSpec((TMS, d), lambda i, *_: (i, 0)),
            pl.BlockSpec((1, 8, TMS), lambda i, *_: (i, 0, 0)),
            pl.BlockSpec((TMS, TMS), lambda i, *_: (0, 0)),
        ],
        out_specs=[
            pl.BlockSpec(memory_space=pl.ANY),
            pl.BlockSpec((1, 8, TMS), lambda i, *_: (i, 0, 0)),
        ],
        scratch_shapes=[
            pltpu.VMEM((2, SORT_ROWS, d), BF16),
            pltpu.VMEM((TAIL_PIECES[0], d), BF16),
            pltpu.SemaphoreType.DMA,
        ],
    )
    return pl.pallas_call(
        _dispatch_kernel,
        grid_spec=grid_spec,
        out_shape=[
            jax.ShapeDtypeStruct((rows_max, d), BF16),
            jax.ShapeDtypeStruct((nt, 8, TMS), jnp.int32),
        ],
        compiler_params=_params("arbitrary"),
        name="moe_dispatch",
    )(*tables, h2, route_i, tri)


def _experts_kernel(te_ref, nv_ref, xs_ref, wg_ref, wu_ref, wd_ref, ys_ref):
    j = pl.program_id(0)

    @pl.when(j < nv_ref[0])
    def _():
        h = xs_ref[...]
        gate = _dot(h, wg_ref[0])
        hidden = (gate * jax.nn.sigmoid(gate)) * _dot(h, wu_ref[0])
        ys_ref[...] = _dot(hidden.astype(BF16), wd_ref[0]).astype(BF16)

    @pl.when(j >= nv_ref[0])
    def _():
        ys_ref[...] = jnp.zeros_like(ys_ref)


def _experts(xs, tile_expert, n_valid, w_gate, w_up, w_down):
    rows, d = xs.shape
    row_tile = lambda j, te, nv: (jnp.minimum(j, nv[0] - 1), 0)
    weight = lambda j, te, nv: (te[j], 0, 0)
    grid_spec = pltpu.PrefetchScalarGridSpec(
        num_scalar_prefetch=2,
        grid=(rows // ET,),
        in_specs=[
            pl.BlockSpec((ET, d), row_tile),
            pl.BlockSpec((1, d, D_EXPERT), weight),
            pl.BlockSpec((1, d, D_EXPERT), weight),
            pl.BlockSpec((1, D_EXPERT, d), weight),
        ],
        out_specs=pl.BlockSpec((ET, d), lambda j, te, nv: (j, 0)),
    )
    return pl.pallas_call(
        _experts_kernel,
        grid_spec=grid_spec,
        out_shape=jax.ShapeDtypeStruct((rows, d), BF16),
        compiler_params=_params("arbitrary"),
        name="moe_experts",
    )(tile_expert, n_valid, xs, w_gate, w_up, w_down)


def _combine_kernel(n16_ref, lo_ref, goff_ref, total_ref, ys_ref, dest_ref, w_ref, x_ref, mod_ref,
                    nf_ref, o_ref, buf_ref, sems, *, final):
    i = pl.program_id(0)
    nt = pl.num_programs(0)
    slot = i & 1

    def copy(dst_slot, local_row, global_row, n):
        return pltpu.make_async_copy(ys_ref.at[pl.ds(global_row, n)],
                                     buf_ref.at[dst_slot, pl.ds(local_row, n)], sems.at[dst_slot])

    @pl.when(i == 0)
    def _():
        buf_ref[...] = jnp.zeros_like(buf_ref)
        _for_each_piece(n16_ref, lo_ref, goff_ref, 0, lambda lr, gr, n: copy(0, lr, gr, n).start())

    @pl.when(i + 1 < nt)
    def _():
        _for_each_piece(n16_ref, lo_ref, goff_ref, i + 1,
                        lambda lr, gr, n: copy(1 - slot, lr, gr, n).start())

    _for_each_total_piece(total_ref, i, lambda n: copy(slot, 0, 0, n).wait())

    dest = dest_ref[...]
    w = w_ref[...]
    col = lax.broadcasted_iota(jnp.int32, (dest.shape[0], SORT_ROWS), 1)
    wmat = (jnp.where(col == dest[:, 0:1], w[:, 0:1], 0.0)
            + jnp.where(col == dest[:, 1:2], w[:, 1:2], 0.0))
    y = _dot(wmat.astype(BF16), buf_ref[slot])
    x = x_ref[...] + mod_ref[0, 5:6, :] * y
    if final:
        x = _rms_normalize(x) * nf_ref[...]
    o_ref[...] = x


def _combine(ys, dest_col, w_col, x, mod, norm_final, tables, tiles_per_batch, final):
    t, d = x.shape
    nt = t // TMS
    grid_spec = pltpu.PrefetchScalarGridSpec(
        num_scalar_prefetch=4,
        grid=(nt,),
        in_specs=[
            pl.BlockSpec(memory_space=pl.ANY),
            pl.BlockSpec((TMS, 8), lambda i, *_: (i, 0)),
            pl.BlockSpec((TMS, 8), lambda i, *_: (i, 0)),
            pl.BlockSpec((TMS, d), lambda i, *_: (i, 0)),
            pl.BlockSpec((1, 6, d), lambda i, *_: (i // tiles_per_batch, 0, 0)),
            pl.BlockSpec((1, d), lambda i, *_: (0, 0)),
        ],
        out_specs=pl.BlockSpec((TMS, d), lambda i, *_: (i, 0)),
        scratch_shapes=[pltpu.VMEM((2, SORT_ROWS, d), BF16), pltpu.SemaphoreType.DMA((2,))],
    )
    return pl.pallas_call(
        functools.partial(_combine_kernel, final=final),
        grid_spec=grid_spec,
        out_shape=jax.ShapeDtypeStruct((t, d), F32),
        compiler_params=_params("arbitrary"),
        name="moe_combine_final" if final else "moe_combine",
    )(*tables, ys, dest_col, w_col, x, mod, norm_final)


def _route_tables(counts, n_expert_tiles):
    n16 = (counts + (SEG - 1)) // SEG * SEG
    lo = jnp.cumsum(n16, axis=1) - n16
    total = jnp.sum(n16, axis=0)
    region = (total + (ET - 1)) // ET * ET
    ends = jnp.cumsum(region)
    base = ends - region
    goff = base[None, :] + jnp.cumsum(n16, axis=0) - n16
    tile_row = jnp.arange(n_expert_tiles, dtype=jnp.int32) * ET
    tile_expert = jnp.minimum(jnp.sum(tile_row[:, None] >= ends[None, :], axis=1), N_EXPERTS - 1)
    n_valid = (ends[-1] // ET).reshape(1)
    i32 = lambda a: a.astype(jnp.int32).reshape(-1)
    return (i32(n16), i32(lo), i32(goff), i32(jnp.sum(n16, axis=1))), \
        (i32(base + total), i32(region - total)), i32(tile_expert), i32(n_valid)


def _moe(h2, route_i, route_w, cnt, w_gate, w_up, w_down, x, mod, norm_final, final):
    b, s, d = x.shape
    t = b * s
    nt = t // TMS
    n_expert_tiles = -(-(2 * t + nt * N_EXPERTS * (SEG - 1) + N_EXPERTS * (ET - 1)) // ET)
    counts = jnp.round(cnt[:, :, :, 0]).astype(jnp.int32).reshape(nt, N_EXPERTS)
    seg_tables, tail_tables, tile_expert, n_valid = _route_tables(counts, n_expert_tiles)
    tri = jnp.triu(jnp.ones((TMS, TMS), BF16), k=1)
    xs, dest = _dispatch(h2.reshape(t, d), route_i.reshape(nt, 8, TMS), tri,
                         seg_tables + tail_tables + (n_valid,), n_expert_tiles * ET)
    ys = _experts(xs, tile_expert, n_valid, w_gate, w_up, w_down)
    dest_col = jnp.swapaxes(dest, 1, 2).reshape(t, 8)
    w_col = jnp.swapaxes(route_w.reshape(nt, 8, TMS), 1, 2).reshape(t, 8)
    out = _combine(ys, dest_col, w_col, x.reshape(t, d), mod, norm_final, seg_tables, s // TMS, final)
    return out.reshape(b, s, d)


def _block_diag(w):
    nb, c, _ = w.shape
    eye = jnp.eye(nb, dtype=w.dtype)
    return (eye[:, None, :, None] * w[:, :, None, :]).reshape(nb * c, nb * c)


def _swa_head_perm():
    cols = []
    half = SWA_Q_HEADS // 2
    for p in range(half):
        for hd in (p, p + half):
            cols.extend(range(hd * HEAD_DIM, (hd + 1) * HEAD_DIM))
    return jnp.asarray(cols, jnp.int32)


def kernel(x, c, w_mod, b_mod, norm_mix, norm_ffn, w_in, w_out, out_gain, sinks, conv_w, conv_b,
           lru_wa, lru_ba, lru_wx, lru_bx, lru_lam, fox_bf, w_router_group, b_router_group,
           w_router_expert, b_router_expert, w_gate, w_up, w_down, norm_final):
    depth = w_mod.shape[0]
    b, s, d = x.shape
    assert d == D_MODEL and s % max(TM_PRE, TQ_SWA, TS_LRU, T_FOX, TM_POST) == 0
    assert TM_PRE == T_FOX
    c_rows = jnp.zeros((16, d), F32).at[:b].set(c)
    mod_all = _modulation(c_rows, w_mod, b_mod)[:, :b]
    perm = _swa_head_perm()

    for l in range(depth):
        mod = mod_all[l].reshape(b, 6, d)
        w_in_l = w_in[l]
        w_tok = jnp.concatenate([w_in_l[:, :SWA_WIDTH][:, perm], w_in_l[:, SWA_WIDTH:OFF_QC],
                                 w_in_l[:, OFF_KC:OFF_VC]], axis=1).astype(BF16)
        w_t = jnp.concatenate([w_in_l[:, OFF_QC:OFF_KC].T, w_in_l[:, OFF_VC:OFF_FC].T,
                               jnp.zeros((FORGET_ROWS, d), F32).at[:FOX_HEADS].set(w_in_l[:, OFF_FC:].T)],
                              axis=0).astype(BF16)
        fb = jnp.zeros((FORGET_ROWS, 1), F32).at[:FOX_HEADS, 0].set(fox_bf[l])
        gain = out_gain[l]
        gain_a = gain[:SWA_WIDTH][perm].reshape(1, SWA_WIDTH)
        gain_b = gain[SWA_WIDTH:SWA_WIDTH + LRU_WIDTH].reshape(1, LRU_WIDTH)
        gain_c = gain[SWA_WIDTH + LRU_WIDTH:].reshape(1, FOX_WIDTH)
        w_out_l = jnp.concatenate([w_out[l][:SWA_WIDTH][perm], w_out[l][SWA_WIDTH:]], axis=0).astype(BF16)
        wr = jnp.zeros((ROUTER_ROWS, d), F32)
        wr = wr.at[:N_GROUPS].set(w_router_group[l].T).at[8:8 + N_EXPERTS].set(w_router_expert[l].T)
        wr_hi = wr.astype(BF16)
        wr_lo = (wr - wr_hi.astype(F32)).astype(BF16)
        br = jnp.zeros((ROUTER_ROWS, 1), F32)
        br = br.at[:N_GROUPS, 0].set(b_router_group[l]).at[8:8 + N_EXPERTS, 0].set(b_router_expert[l])

        qa, ka, va, xb, gb, kc, qct, vaug, lf = _pre_mixer(
            x, mod, norm_mix[l].reshape(1, d), w_tok, w_t, fb)
        cum = _cumsum(lf)
        kaug = _fox_prep(kc, jnp.swapaxes(cum, 1, 2))
        base = cum[:, :FOX_HEADS, ::T_FOX].reshape(-1)
        ya = _swa(qa, ka, va, sinks[l].reshape(1, SWA_Q_HEADS))
        yb = _lru(xb, gb, conv_w[l], conv_b[l].reshape(1, -1),
                  _block_diag(lru_wa[l]).astype(BF16), lru_ba[l].reshape(1, -1),
                  _block_diag(lru_wx[l]).astype(BF16), lru_bx[l].reshape(1, -1),
                  lru_lam[l].reshape(1, -1))
        yc = _fox(qct, kaug, vaug, base)
        x, h2, route_i, route_w, cnt = _post_mixer(
            ya, yb, yc, x, mod, gain_a, gain_b, gain_c, w_out_l,
            norm_ffn[l].reshape(1, d), wr_hi, wr_lo, br)
        x = _moe(h2, route_i, route_w, cnt, w_gate[l].astype(BF16), w_up[l].astype(BF16),
                 w_down[l].astype(BF16), x, mod, norm_final.reshape(1, d), final=(l == depth - 1))
    return x
```

```python
import functools

import jax
import jax.numpy as jnp
from jax import lax
from jax.experimental import pallas as pl
from jax.experimental.pallas import tpu as pltpu

F32 = jnp.float32
BF16 = jnp.bfloat16

D_MODEL = 1024
HEAD_DIM = 64
PAIR = 2 * HEAD_DIM
BLOCK = 128
SWA_Q_HEADS = 8
SWA_WIDTH = SWA_Q_HEADS * HEAD_DIM
KV_WIDTH = 2 * HEAD_DIM
LRU_WIDTH = 256
LRU_BLOCKS = 8
CONV_WIDTH = 4
LRU_C = 8.0
FOX_HEADS = 4
FOX_WIDTH = FOX_HEADS * HEAD_DIM
N_GROUPS = 4
EXPERTS_PER_GROUP = 4
N_EXPERTS = 16
D_EXPERT = 256
EPS = 1e-6
NEG = -1e30
LOG2E = 1.4426950408889634

OFF_KA = SWA_WIDTH
OFF_VA = OFF_KA + KV_WIDTH
OFF_XB = OFF_VA + KV_WIDTH
OFF_GB = OFF_XB + LRU_WIDTH
OFF_QC = OFF_GB + LRU_WIDTH
OFF_KC = OFF_QC + FOX_WIDTH
OFF_VC = OFF_KC + FOX_WIDTH
OFF_FC = OFF_VC + FOX_WIDTH

ROUTER_ROWS = 32
FORGET_ROWS = 16
V_ROWS = HEAD_DIM + 16
VMEM_LIMIT_BYTES = 48 * 1024 * 1024

TM_PRE = 512
TQ_SWA = 512
TS_LRU = 512
T_FOX = 512
TM_POST = 512
CUMSUM_CHUNK = 256


def _params(*semantics):
    return pltpu.CompilerParams(dimension_semantics=semantics, vmem_limit_bytes=VMEM_LIMIT_BYTES)


def _split_bf16(v):
    hi = v.astype(BF16)
    lo = (v - hi.astype(F32)).astype(BF16)
    return hi, lo


def _dot(a, b):
    return jnp.dot(a, b, preferred_element_type=F32)


def _dot_nt(a, b):
    return lax.dot_general(a, b, (((1,), (1,)), ((), ())), preferred_element_type=F32)


def _rms_normalize(v):
    return v * lax.rsqrt(jnp.mean(v * v, axis=-1, keepdims=True) + EPS)


def _mod_kernel(c_ref, w_ref, b_ref, o_ref):
    c = c_ref[...]
    ca = c * jax.nn.sigmoid(c)
    w = w_ref[0]
    a_hi, a_lo = _split_bf16(ca)
    w_hi, w_lo = _split_bf16(w)
    o_ref[0] = _dot(a_hi, w_hi) + _dot(a_lo, w_hi) + _dot(a_hi, w_lo) + b_ref[0]


def _modulation(c, w_mod, b_mod):
    depth, d, d6 = w_mod.shape
    b = c.shape[0]
    n = d6 // d
    return pl.pallas_call(
        _mod_kernel,
        grid=(depth, n),
        in_specs=[
            pl.BlockSpec((b, d), lambda l, j: (0, 0)),
            pl.BlockSpec((1, d, d), lambda l, j: (l, 0, j)),
            pl.BlockSpec((1, 1, d), lambda l, j: (l, 0, j)),
        ],
        out_specs=pl.BlockSpec((1, b, d), lambda l, j: (l, 0, j)),
        out_shape=jax.ShapeDtypeStruct((depth, b, d6), F32),
        compiler_params=_params("arbitrary", "arbitrary"),
        name="modulation",
    )(c, w_mod, b_mod.reshape(depth, 1, d6))


def _log_sigmoid(z):
    return jnp.minimum(z, 0.0) - jnp.log1p(jnp.exp(-jnp.abs(z)))


def _other_half(a):
    return HEAD_DIM * (1 - a)


def _pre_kernel(x_ref, mod_ref, g_ref, w_ref, wt_ref, fb_ref,
                qa_ref, ka_ref, va_ref, xb_ref, gb_ref, kc_ref, qct_ref, vct_ref, lf_ref):
    x = x_ref[0]
    h = (_rms_normalize(x) * g_ref[...]) * (1.0 + mod_ref[0, 1:2, :]) + mod_ref[0, 0:1, :]
    hb = h.astype(BF16)

    def proj(lo, width):
        return _dot(hb, w_ref[:, lo:lo + width])

    scale = HEAD_DIM ** -0.5
    qa_ref[0] = (proj(0, SWA_WIDTH) * scale).astype(BF16)
    ka_ref[0] = proj(OFF_KA, KV_WIDTH).astype(BF16)
    va_ref[0] = proj(OFF_VA, KV_WIDTH).astype(BF16)
    xb_ref[0] = proj(OFF_XB, LRU_WIDTH).astype(BF16)
    gb_ref[0] = proj(OFF_GB, LRU_WIDTH).astype(BF16)
    kc_ref[0] = proj(OFF_QC, FOX_WIDTH).astype(BF16)

    t_all = _dot_nt(wt_ref[...], hb)
    qct_ref[0, 0] = (t_all[0:FOX_WIDTH] * (scale * LOG2E)).astype(BF16)
    ones_row = jnp.where(lax.broadcasted_iota(jnp.int32, (V_ROWS - HEAD_DIM, t_all.shape[1]), 0) == 0,
                         1.0, 0.0)
    for hd in range(FOX_HEADS):
        v_head = t_all[FOX_WIDTH + hd * HEAD_DIM:FOX_WIDTH + (hd + 1) * HEAD_DIM]
        vct_ref[0, 0, hd] = jnp.concatenate([v_head, ones_row], axis=0).astype(BF16)
    lf_ref[0] = _log_sigmoid(t_all[2 * FOX_WIDTH:] + fb_ref[...])


def _pre_mixer(x, mod, gain, w_tok, w_t, fb):
    b, s, d = x.shape
    tm = TM_PRE
    nt = s // tm

    def tok(width):
        return pl.BlockSpec((1, tm, width), lambda bi, i: (bi, i, 0))

    def tok_shape(width):
        return jax.ShapeDtypeStruct((b, s, width), BF16)

    return pl.pallas_call(
        _pre_kernel,
        grid=(b, nt),
        in_specs=[
            tok(d),
            pl.BlockSpec((1, 6, d), lambda bi, i: (bi, 0, 0)),
            pl.BlockSpec((1, d), lambda bi, i: (0, 0)),
            pl.BlockSpec(w_tok.shape, lambda bi, i: (0, 0)),
            pl.BlockSpec(w_t.shape, lambda bi, i: (0, 0)),
            pl.BlockSpec((FORGET_ROWS, 1), lambda bi, i: (0, 0)),
        ],
        out_specs=[
            tok(SWA_WIDTH), tok(KV_WIDTH), tok(KV_WIDTH), tok(LRU_WIDTH), tok(LRU_WIDTH),
            tok(FOX_WIDTH),
            pl.BlockSpec((1, 1, FOX_WIDTH, tm), lambda bi, i: (bi, i, 0, 0)),
            pl.BlockSpec((1, 1, FOX_HEADS, V_ROWS, tm), lambda bi, i: (bi, i, 0, 0, 0)),
            pl.BlockSpec((1, FORGET_ROWS, tm), lambda bi, i: (bi, 0, i)),
        ],
        out_shape=[
            tok_shape(SWA_WIDTH), tok_shape(KV_WIDTH), tok_shape(KV_WIDTH), tok_shape(LRU_WIDTH),
            tok_shape(LRU_WIDTH), tok_shape(FOX_WIDTH),
            jax.ShapeDtypeStruct((b, nt, FOX_WIDTH, tm), BF16),
            jax.ShapeDtypeStruct((b, nt, FOX_HEADS, V_ROWS, tm), BF16),
            jax.ShapeDtypeStruct((b, FORGET_ROWS, s), F32),
        ],
        compiler_params=_params("parallel", "parallel"),
        name="pre_mixer",
    )(x, mod, gain, w_tok, w_t, fb)


def _cumsum_kernel(lf_ref, o_ref, col_ref):
    n = CUMSUM_CHUNK
    rows = lf_ref.shape[1]
    s = lf_ref.shape[2]
    r = lax.broadcasted_iota(jnp.int32, (n, n), 0)
    c = lax.broadcasted_iota(jnp.int32, (n, n), 1)
    tri = jnp.where(r <= c, 1.0, 0.0).astype(BF16)
    carry = jnp.zeros((rows, 1), F32)
    pad = jnp.zeros((col_ref.shape[2] - rows, n), F32)
    for j in range(s // n):
        seg = lf_ref[0, :, j * n:(j + 1) * n]
        hi = seg.astype(BF16)
        rem = seg - hi.astype(F32)
        mid = rem.astype(BF16)
        lo = (rem - mid.astype(F32)).astype(BF16)
        cs = (_dot(hi, tri) + _dot(mid, tri)) + _dot(lo, tri) + carry
        o_ref[0, :, j * n:(j + 1) * n] = cs
        col_ref[0, j * n:(j + 1) * n, :] = jnp.concatenate([cs, pad], axis=0).T
        carry = cs[:, n - 1:n]


def _cumsum(lf):
    b, r, s = lf.shape
    return pl.pallas_call(
        _cumsum_kernel,
        grid=(b,),
        in_specs=[pl.BlockSpec((1, r, s), lambda bi: (bi, 0, 0))],
        out_specs=[pl.BlockSpec((1, r, s), lambda bi: (bi, 0, 0)),
                   pl.BlockSpec((1, s, 128), lambda bi: (bi, 0, 0))],
        out_shape=[jax.ShapeDtypeStruct((b, r, s), F32), jax.ShapeDtypeStruct((b, s, 128), F32)],
        compiler_params=_params("parallel"),
        name="cumsum",
    )(lf)


def _swa_kernel(sink_ref, q_ref, kc_ref, vc_ref, kp_ref, vp_ref, o_ref, kf_ref, vf_ref, bias_ref):
    i = pl.program_id(1)
    nsub = q_ref.shape[1] // BLOCK
    npair = SWA_Q_HEADS // 2

    @pl.when((pl.program_id(0) == 0) & (i == 0))
    def _():
        qpos = BLOCK + lax.broadcasted_iota(jnp.int32, (BLOCK, 2 * BLOCK), 0)
        kpos = lax.broadcasted_iota(jnp.int32, (BLOCK, 2 * BLOCK), 1)
        dist = qpos - kpos
        valid = (dist >= 0) & (dist < BLOCK)
        distf = dist.astype(F32)
        for hd in range(SWA_Q_HEADS):
            slope = 2.0 ** (-8.0 * (hd + 1) / SWA_Q_HEADS)
            bias_ref[hd] = jnp.where(valid, -slope * distf, NEG)

    kf_ref[0:BLOCK, :] = kp_ref[0]
    kf_ref[BLOCK:, :] = kc_ref[0]
    vf_ref[0:BLOCK, :] = vp_ref[0]
    vf_ref[BLOCK:, :] = vc_ref[0]

    lane = lax.broadcasted_iota(jnp.int32, (2 * BLOCK, PAIR), 1)
    lane_o = lax.broadcasted_iota(jnp.int32, (BLOCK, PAIR), 1)
    col = lax.broadcasted_iota(jnp.int32, (BLOCK, 2 * BLOCK), 1)

    for j in range(nsub):
        k2 = kf_ref[j * BLOCK:(j + 2) * BLOCK, :]
        v2 = vf_ref[j * BLOCK:(j + 2) * BLOCK, :]
        k_half = (jnp.where(lane < HEAD_DIM, k2, jnp.zeros_like(k2)),
                  jnp.where(lane >= HEAD_DIM, k2, jnp.zeros_like(k2)))
        for p in range(npair):
            qp = q_ref[0, j * BLOCK:(j + 1) * BLOCK, p * PAIR:(p + 1) * PAIR]
            outs = []
            for a in range(2):
                hd = p + a * npair
                sc = _dot_nt(qp, k_half[a]) + bias_ref[hd]
                if j == 0:
                    sc = sc + jnp.where(col < BLOCK, jnp.where(i == 0, NEG, 0.0), 0.0)
                sink = sink_ref[0, hd]
                m = jnp.maximum(jnp.max(sc, axis=-1, keepdims=True), sink)
                pr = jnp.exp(sc - m)
                den = jnp.sum(pr, axis=-1, keepdims=True) + jnp.exp(sink - m)
                outs.append(_dot(pr.astype(BF16), v2) / den)
            o_ref[0, j * BLOCK:(j + 1) * BLOCK, p * PAIR:(p + 1) * PAIR] = jnp.where(
                lane_o < HEAD_DIM, outs[0], outs[1]).astype(BF16)


def _swa(qa, ka, va, sinks):
    b, s, _ = qa.shape
    tq = TQ_SWA
    nsub = tq // BLOCK
    prev = lambda bi, i: (bi, jnp.maximum(i * nsub - 1, 0), 0)
    return pl.pallas_call(
        _swa_kernel,
        grid=(b, s // tq),
        in_specs=[
            pl.BlockSpec(memory_space=pltpu.SMEM),
            pl.BlockSpec((1, tq, SWA_WIDTH), lambda bi, i: (bi, i, 0)),
            pl.BlockSpec((1, tq, KV_WIDTH), lambda bi, i: (bi, i, 0)),
            pl.BlockSpec((1, tq, KV_WIDTH), lambda bi, i: (bi, i, 0)),
            pl.BlockSpec((1, BLOCK, KV_WIDTH), prev),
            pl.BlockSpec((1, BLOCK, KV_WIDTH), prev),
        ],
        out_specs=pl.BlockSpec((1, tq, SWA_WIDTH), lambda bi, i: (bi, i, 0)),
        out_shape=jax.ShapeDtypeStruct((b, s, SWA_WIDTH), BF16),
        scratch_shapes=[
            pltpu.VMEM((tq + BLOCK, KV_WIDTH), BF16),
            pltpu.VMEM((tq + BLOCK, KV_WIDTH), BF16),
            pltpu.VMEM((SWA_Q_HEADS, BLOCK, 2 * BLOCK), F32),
        ],
        compiler_params=_params("arbitrary", "arbitrary"),
        name="swa",
    )(sinks, qa, ka, va, ka, va)


def _gelu_tanh(v):
    return 0.5 * v * (1.0 + jnp.tanh(0.7978845608028654 * (v + 0.044715 * (v * v * v))))


def _lru_kernel(xb_ref, gb_ref, cw_ref, cb_ref, wa_ref, ba_ref, wx_ref, bx_ref, lam_ref,
                o_ref, win_ref, h_ref):
    ts = xb_ref.shape[1]

    @pl.when(pl.program_id(1) == 0)
    def _():
        win_ref[0:8, :] = jnp.zeros((8, LRU_WIDTH), F32)
        h_ref[...] = jnp.zeros_like(h_ref)

    x = xb_ref[0].astype(F32)
    win_ref[8:, :] = x
    conv = cb_ref[...] + cw_ref[CONV_WIDTH - 1:CONV_WIDTH, :] * x
    for k in range(1, CONV_WIDTH):
        conv = conv + cw_ref[CONV_WIDTH - 1 - k:CONV_WIDTH - k, :] * win_ref[8 - k:8 - k + ts, :]
    win_ref[0:8, :] = x[ts - 8:ts]

    cbf = conv.astype(BF16)
    r = jax.nn.sigmoid(_dot(cbf, wa_ref[...]) + ba_ref[...])
    gi = jax.nn.sigmoid(_dot(cbf, wx_ref[...]) + bx_ref[...])
    z = -lam_ref[...]
    softplus = jnp.maximum(z, 0.0) + jnp.log1p(jnp.exp(-jnp.abs(z)))
    log_a = (-LRU_C * softplus) * r
    a = jnp.exp(log_a)
    v = 1.0 - a * a
    u = jnp.where(v == 0.0, 0.0, v * lax.rsqrt(v)) * (gi * conv)

    a = a.reshape(ts // 8, 8, LRU_WIDTH)
    u = u.reshape(ts // 8, 8, LRU_WIDTH)
    sub = lax.broadcasted_iota(jnp.int32, a.shape, 1)
    for step in (1, 2, 4):
        keep = sub >= step
        u = jnp.where(keep, a * pltpu.roll(u, step, 1) + u, u)
        a = jnp.where(keep, a * pltpu.roll(a, step, 1), a)
    carry = h_ref[7:8, :]
    groups = []
    for g in range(ts // 8):
        hg = u[g] + a[g] * carry
        groups.append(hg)
        carry = hg[7:8, :]
    h = jnp.concatenate(groups, axis=0)
    h_ref[...] = groups[-1]
    o_ref[0] = (h * _gelu_tanh(gb_ref[0].astype(F32))).astype(BF16)


def _lru(xb, gb, conv_w, conv_b, wa, ba, wx, bx, lam):
    b, s, w = xb.shape
    ts = TS_LRU
    tok = pl.BlockSpec((1, ts, w), lambda bi, i: (bi, i, 0))
    full = lambda shape: pl.BlockSpec(shape, lambda bi, i: (0,) * len(shape))
    return pl.pallas_call(
        _lru_kernel,
        grid=(b, s // ts),
        in_specs=[tok, tok, full((CONV_WIDTH, w)), full((1, w)), full((w, w)), full((1, w)),
                  full((w, w)), full((1, w)), full((1, w))],
        out_specs=tok,
        out_shape=jax.ShapeDtypeStruct((b, s, w), BF16),
        scratch_shapes=[pltpu.VMEM((8 + ts, w), F32), pltpu.VMEM((8, w), F32)],
        compiler_params=_params("arbitrary", "arbitrary"),
        name="lru",
    )(xb, gb, conv_w, conv_b, wa, ba, wx, bx, lam)


BIAS_ROWS = 6


def _split3(v):
    hi = v.astype(BF16).astype(F32)
    rem = v - hi
    mid = rem.astype(BF16).astype(F32)
    lo = (rem - mid).astype(BF16).astype(F32)
    return hi, mid, lo


def _fox_prep_kernel(k_ref, c_ref, o_ref):
    tm = k_ref.shape[1]
    lane = lax.broadcasted_iota(jnp.int32, (tm, PAIR), 1)
    for p in range(FOX_WIDTH // PAIR):
        kp = k_ref[0, :, p * PAIR:(p + 1) * PAIR].astype(F32)
        for a in range(2):
            hd = 2 * p + a
            c_hi, c_mid, c_lo = _split3(c_ref[0, :, hd:hd + 1] * LOG2E)
            rel = lane - _other_half(a)
            aug = jnp.where(rel == 0, c_hi, jnp.where(rel == 1, c_mid, jnp.where(
                rel == 2, c_lo, jnp.where(rel < BIAS_ROWS, 1.0, 0.0))))
            in_half = (lane >= HEAD_DIM * a) & (lane < HEAD_DIM * (a + 1))
            o_ref[0, hd] = jnp.where(in_half, kp, aug).astype(BF16)


def _fox_prep(kc, cum_col):
    b, s, w = kc.shape
    tm = TM_PRE
    return pl.pallas_call(
        _fox_prep_kernel,
        grid=(b, s // tm),
        in_specs=[
            pl.BlockSpec((1, tm, w), lambda bi, i: (bi, i, 0)),
            pl.BlockSpec((1, tm, cum_col.shape[2]), lambda bi, i: (bi, i, 0)),
        ],
        out_specs=pl.BlockSpec((1, FOX_HEADS, tm, PAIR), lambda bi, i: (bi, 0, i, 0)),
        out_shape=jax.ShapeDtypeStruct((b, FOX_HEADS, s, PAIR), BF16),
        compiler_params=_params("parallel", "parallel"),
        name="fox_prep",
    )(kc, cum_col)


def _fox_kernel(base_ref, qt_ref, ka_ref, vt_ref, o_ref, acc_ref, s0_ref, s1_ref):
    bi = pl.program_id(0)
    p = pl.program_id(1)
    qi = pl.program_id(2)
    nq = pl.num_programs(2)
    t = qt_ref.shape[3]
    qt = qt_ref[0, 0].astype(F32)
    row = lax.broadcasted_iota(jnp.int32, (PAIR, t), 0)
    causal = (lax.broadcasted_iota(jnp.int32, (t, t), 0) <= lax.broadcasted_iota(jnp.int32, (t, t), 1))
    q_aug = []
    for a in range(2):
        hd = 2 * p + a
        base = jnp.full((1, t), base_ref[(bi * FOX_HEADS + hd) * nq + qi], F32) * LOG2E
        b_hi, b_mid, b_lo = _split3(base)
        rel = row - _other_half(a)
        aug = jnp.where(rel < 3, -1.0, jnp.where(rel == 3, b_hi, jnp.where(
            rel == 4, b_mid, jnp.where(rel == 5, b_lo, 0.0))))
        in_half = (row >= HEAD_DIM * a) & (row < HEAD_DIM * (a + 1))
        q_aug.append(jnp.where(in_half, qt, aug).astype(BF16))
        acc_ref[a] = jnp.zeros((V_ROWS, t), F32)

    slots = (s0_ref, s1_ref)

    def scores(ki, slot, masked):
        tile_max = []
        for a in range(2):
            k = ka_ref[0, a, pl.ds(pl.multiple_of(ki * t, t), t), :]
            st = _dot(k, q_aug[a])
            if masked:
                st = jnp.where(causal, st, NEG)
            slots[slot][a] = st
            tile_max.append(jnp.max(st, axis=0, keepdims=True))
        return tuple(tile_max)

    def consume(ki, slot, m_old, tile_max):
        m_new = []
        for a in range(2):
            m_a = jnp.maximum(m_old[a], tile_max[a])
            alpha = jnp.exp2(m_old[a] - m_a)
            pr = jnp.exp2(slots[slot][a] - m_a).astype(BF16)
            acc_ref[a] = alpha * acc_ref[a] + _dot(vt_ref[0, ki, a], pr)
            m_new.append(m_a)
        return tuple(m_new)

    def advance(j, slot, carry):
        m_old, tile_max, prev = carry
        next_max = scores(j, 1 - slot, False)
        return consume(prev, slot, m_old, tile_max), next_max, j

    def body(i, carry):
        return advance(2 * i + 1, 1, advance(2 * i, 0, carry))

    m0 = jnp.full((1, t), NEG, F32)
    carry = ((m0, m0), scores(qi, 0, True), qi)
    carry = lax.fori_loop(0, qi // 2, body, carry)

    def finish_odd(carry):
        m, tile_max, prev = advance(qi - 1, 0, carry)
        consume(prev, 1, m, tile_max)
        return 0

    def finish_even(carry):
        m, tile_max, prev = carry
        consume(prev, 0, m, tile_max)
        return 0

    lax.cond(qi % 2 == 1, finish_odd, finish_even, carry)
    outs = []
    for a in range(2):
        acc = acc_ref[a]
        outs.append(acc[0:HEAD_DIM] / acc[HEAD_DIM:HEAD_DIM + 1, :])
    o_ref[0] = jnp.concatenate(outs, axis=0).T.astype(BF16)


def _fox(qct, kaug, vaug, base):
    b, nt, w, t = qct.shape
    s = nt * t
    npair = w // PAIR
    return pl.pallas_call(
        _fox_kernel,
        grid=(b, npair, nt),
        in_specs=[
            pl.BlockSpec(memory_space=pltpu.SMEM),
            pl.BlockSpec((1, 1, PAIR, t), lambda bi, p, qi: (bi, qi, p, 0)),
            pl.BlockSpec((1, 2, s, PAIR), lambda bi, p, qi: (bi, p, 0, 0)),
            pl.BlockSpec((1, nt, 2, V_ROWS, t), lambda bi, p, qi: (bi, 0, p, 0, 0)),
        ],
        out_specs=pl.BlockSpec((1, t, PAIR), lambda bi, p, qi: (bi, qi, p)),
        out_shape=jax.ShapeDtypeStruct((b, s, w), BF16),
        scratch_shapes=[pltpu.VMEM((2, V_ROWS, t), F32), pltpu.VMEM((2, t, t), F32),
                        pltpu.VMEM((2, t, t), F32)],
        compiler_params=_params("parallel", "parallel", "arbitrary"),
        name="fox",
    )(base, qct, kaug, vaug)


def _rows_to_columns(rows):
    n = rows[0].shape[1]
    sub = lax.broadcasted_iota(jnp.int32, (128, n), 0)
    block = jnp.zeros((128, n), F32)
    for k, row in enumerate(rows):
        block = jnp.where(sub == k, row, block)
    return block.T


def _first_max(vals):
    best = vals[0]
    for v in vals[1:]:
        best = jnp.maximum(best, v)
    idx = jnp.full(best.shape, len(vals) - 1, jnp.int32)
    for j in range(len(vals) - 2, -1, -1):
        idx = jnp.where(vals[j] == best, j, idx)
    return best, idx


def _softmax_rows(vals):
    top = vals[0]
    for v in vals[1:]:
        top = jnp.maximum(top, v)
    ex = [jnp.exp(v - top) for v in vals]
    tot = ex[0]
    for e in ex[1:]:
        tot = tot + e
    return [e / tot for e in ex]


def _post_kernel(ya_ref, yb_ref, yc_ref, x_ref, mod_ref, ga_ref, gb_ref, gc_ref, wo_ref, nf_ref,
                 wr_hi_ref, wr_lo_ref, br_ref, xo_ref, h2_ref, ri_ref, rw_ref, cnt_ref):
    ya = _rms_normalize(ya_ref[0].astype(F32)) * ga_ref[...]
    yb = _rms_normalize(yb_ref[0].astype(F32)) * gb_ref[...]
    yc = _rms_normalize(yc_ref[0].astype(F32)) * gc_ref[...]
    y = jnp.concatenate([ya, yb, yc], axis=-1).astype(BF16)
    x = x_ref[0] + mod_ref[0, 2:3, :] * _dot(y, wo_ref[...])
    xo_ref[0] = x
    h = (_rms_normalize(x) * nf_ref[...]) * (1.0 + mod_ref[0, 4:5, :]) + mod_ref[0, 3:4, :]
    h_hi, h_lo = _split_bf16(h)
    h2_ref[0] = h_hi

    w_hi = wr_hi_ref[...]
    logits = (_dot_nt(w_hi, h_hi) + _dot_nt(w_hi, h_lo)) + _dot_nt(wr_lo_ref[...], h_hi) + br_ref[...]
    rows = [logits[r:r + 1, :] for r in range(8 + N_EXPERTS)]
    group_prob = _softmax_rows(rows[0:N_GROUPS])
    group_p, group_idx = _first_max(group_prob)
    in_group = []
    for j in range(EXPERTS_PER_GROUP):
        v = rows[8 + (N_GROUPS - 1) * EXPERTS_PER_GROUP + j]
        for g in range(N_GROUPS - 2, -1, -1):
            v = jnp.where(group_idx == g, rows[8 + g * EXPERTS_PER_GROUP + j], v)
        in_group.append(v)
    expert_prob = _softmax_rows(in_group)
    p1, i1 = _first_max(expert_prob)
    rest = [jnp.where(i1 == j, -1.0, expert_prob[j]) for j in range(EXPERTS_PER_GROUP)]
    p2, i2 = _first_max(rest)
    e1 = group_idx * EXPERTS_PER_GROUP + i1
    e2 = group_idx * EXPERTS_PER_GROUP + i2
    tm = e1.shape[1]
    ri_ref[0, 0] = jnp.zeros((8, tm), jnp.int32)
    ri_ref[0, 0, 0:1, :] = e1
    ri_ref[0, 0, 1:2, :] = e2
    rw_ref[0] = _rows_to_columns([group_p * p1 / (p1 + p2), group_p * p2 / (p1 + p2)])
    erow = lax.broadcasted_iota(jnp.int32, (N_EXPERTS, tm), 0)
    sel = jnp.where((erow == e1) | (erow == e2), 1.0, 0.0).astype(BF16)
    cnt_ref[0, 0] = _dot(sel, jnp.ones((tm, 128), BF16))


def _post_mixer(ya, yb, yc, x, mod, gain_a, gain_b, gain_c, w_out, norm_ffn, wr_hi, wr_lo, br):
    b, s, d = x.shape
    tm = TM_POST
    tok = lambda width: pl.BlockSpec((1, tm, width), lambda bi, i: (bi, i, 0))
    full = lambda shape: pl.BlockSpec(shape, lambda bi, i: (0,) * len(shape))
    return pl.pallas_call(
        _post_kernel,
        grid=(b, s // tm),
        in_specs=[
            tok(SWA_WIDTH), tok(LRU_WIDTH), tok(FOX_WIDTH), tok(d),
            pl.BlockSpec((1, 6, d), lambda bi, i: (bi, 0, 0)),
            full((1, SWA_WIDTH)), full((1, LRU_WIDTH)), full((1, FOX_WIDTH)),
            full((d, d)), full((1, d)),
            full((ROUTER_ROWS, d)), full((ROUTER_ROWS, d)), full((ROUTER_ROWS, 1)),
        ],
        out_specs=[
            tok(d), tok(d),
            pl.BlockSpec((1, 1, 8, tm), lambda bi, i: (bi, i, 0, 0)),
            tok(128),
            pl.BlockSpec((1, 1, N_EXPERTS, 128), lambda bi, i: (bi, i, 0, 0)),
        ],
        out_shape=[
            jax.ShapeDtypeStruct((b, s, d), F32),
            jax.ShapeDtypeStruct((b, s, d), BF16),
            jax.ShapeDtypeStruct((b, s // tm, 8, tm), jnp.int32),
            jax.ShapeDtypeStruct((b, s, 128), F32),
            jax.ShapeDtypeStruct((b, s // tm, N_EXPERTS, 128), F32),
        ],
        compiler_params=_params("parallel", "parallel"),
        name="post_mixer",
    )(ya, yb, yc, x, mod, gain_a, gain_b, gain_c, w_out, norm_ffn, wr_hi, wr_lo, br)


TMS = TM_POST
SEG = 16
SORT_ROWS = 1280
ET = 1024
SEG_PIECES = (512, 256, 128, 64, 32, 16)
TAIL_PIECES = (512, 256, 128, 64, 32, 16)
TOTAL_PIECES = (1024, 512, 256, 128, 64, 32, 16)
RARE_ROWS = 128
assert SORT_ROWS >= 2 * TMS + N_EXPERTS * (SEG - 1) and SEG_PIECES[0] == TMS and TAIL_PIECES[0] * 2 == ET


def _for_each_piece(n16_ref, lo_ref, goff_ref, tile, fn):
    for e in range(N_EXPERTS):
        n = n16_ref[tile * N_EXPERTS + e]
        lo = lo_ref[tile * N_EXPERTS + e]
        go = goff_ref[tile * N_EXPERTS + e]

        def pieces(sizes, n=n, lo=lo, go=go):
            for rows in sizes:
                done = n & (-2 * rows)

                @pl.when((n & rows) != 0)
                def _(done=done, rows=rows):
                    fn(pl.multiple_of(lo + done, SEG), pl.multiple_of(go + done, SEG), rows)

        @pl.when(n >= RARE_ROWS)
        def _():
            pieces([r for r in SEG_PIECES if r >= RARE_ROWS])

        pieces([r for r in SEG_PIECES if r < RARE_ROWS])


def _for_each_total_piece(total_ref, tile, fn):
    n = total_ref[tile]
    for rows in TOTAL_PIECES:
        @pl.when((n & rows) != 0)
        def _(rows=rows):
            fn(rows)


def _dispatch_kernel(n16_ref, lo_ref, goff_ref, total_ref, tail_row_ref, tail_n_ref, n_valid_ref,
                     h_ref, ri_ref, tri_ref, xs_ref, dest_ref, sorted_ref, zero_ref, sem):
    i = pl.program_id(0)
    last = pl.num_programs(0) - 1
    slot = i & 1
    tms = h_ref.shape[0]

    ri = ri_ref[0]
    e1 = ri[0:1, :]
    e2 = ri[1:2, :]
    erow = lax.broadcasted_iota(jnp.int32, (N_EXPERTS, tms), 0)
    m1 = erow == e1
    m2 = erow == e2
    sel = jnp.where(m1 | m2, 1.0, 0.0).astype(BF16)
    rank = _dot(sel, tri_ref[...]).astype(jnp.int32)
    start = jnp.zeros((N_EXPERTS, tms), jnp.int32)
    for e in range(N_EXPERTS):
        start = jnp.where(erow == e, lo_ref[i * N_EXPERTS + e], start)
    dest = rank + start
    r1 = jnp.sum(jnp.where(m1, dest, 0), axis=0, keepdims=True)
    r2 = jnp.sum(jnp.where(m2, dest, 0), axis=0, keepdims=True)
    dest_ref[...] = _rows_to_columns([r1.astype(F32), r2.astype(F32)])

    rows = lax.broadcasted_iota(jnp.int32, (SORT_ROWS, tms), 0)
    perm = jnp.where((rows == r1) | (rows == r2), 1.0, 0.0).astype(BF16)
    sorted_ref[slot] = _dot(perm, h_ref[...]).astype(BF16)

    def copy(src_slot, local_row, global_row, n):
        return pltpu.make_async_copy(sorted_ref.at[src_slot, pl.ds(local_row, n)],
                                     xs_ref.at[pl.ds(global_row, n)], sem)

    @pl.when(i > 0)
    def _():
        _for_each_total_piece(total_ref, i - 1, lambda n: copy(1 - slot, 0, 0, n).wait())

    _for_each_piece(n16_ref, lo_ref, goff_ref, i, lambda lr, gr, n: copy(slot, lr, gr, n).start())

    @pl.when(i == last)
    def _():
        _for_each_total_piece(total_ref, i, lambda n: copy(slot, 0, 0, n).wait())
        zero_ref[...] = jnp.zeros_like(zero_ref)

        def tail_copies(fn):
            for e in range(N_EXPERTS):
                n = tail_n_ref[e]
                row = tail_row_ref[e]
                for rows_ in TAIL_PIECES:
                    done = n & (-2 * rows_)

                    @pl.when((n & rows_) != 0)
                    def _(row=row, done=done, rows_=rows_):
                        fn(pltpu.make_async_copy(
                            zero_ref.at[pl.ds(0, rows_)],
                            xs_ref.at[pl.ds(pl.multiple_of(row + done, SEG), rows_)], sem))

        tail_copies(lambda c: c.start())
        tail_copies(lambda c: c.wait())

        def unused_tiles(fn):
            def body(j, carry):
                for part in range(ET // TAIL_PIECES[0]):
                    row = pl.multiple_of(j * ET + part * TAIL_PIECES[0], SEG)
                    fn(pltpu.make_async_copy(zero_ref, xs_ref.at[pl.ds(row, TAIL_PIECES[0])], sem))
                return carry
            lax.fori_loop(n_valid_ref[0], xs_ref.shape[0] // ET, body, 0)

        unused_tiles(lambda c: c.start())
        unused_tiles(lambda c: c.wait())


def _dispatch(h2, route_i, tri, tables, rows_max):
    t, d = h2.shape
    nt = t // TMS
    grid_spec = pltpu.PrefetchScalarGridSpec(
        num_scalar_prefetch=7,
        grid=(nt,),
        in_specs=[
            pl.BlockSpec((TMS, d), lambda i, *_: (i, 0)),
            pl.BlockSpec((1, 8, TMS), lambda i, *_: (i, 0, 0)),
            pl.BlockSpec((TMS, TMS), lambda i, *_: (0, 0)),
        ],
        out_specs=[
            pl.BlockSpec(memory_space=pl.ANY),
            pl.BlockSpec((TMS, 128), lambda i, *_: (i, 0)),
        ],
        scratch_shapes=[
            pltpu.VMEM((2, SORT_ROWS, d), BF16),
            pltpu.VMEM((TAIL_PIECES[0], d), BF16),
            pltpu.SemaphoreType.DMA,
        ],
    )
    return pl.pallas_call(
        _dispatch_kernel,
        grid_spec=grid_spec,
        out_shape=[
            jax.ShapeDtypeStruct((rows_max, d), BF16),
            jax.ShapeDtypeStruct((t, 128), F32),
        ],
        compiler_params=_params("arbitrary"),
        name="moe_dispatch",
    )(*tables, h2, route_i, tri)


def _experts_kernel(te_ref, nv_ref, xs_ref, wg_ref, wu_ref, wd_ref, ys_ref, wg_s, wu_s, wd_s):
    j = pl.program_id(0)

    @pl.when((j == 0) | (te_ref[j] != te_ref[jnp.maximum(j - 1, 0)]))
    def _():
        wg_s[...] = wg_ref[0].astype(BF16)
        wu_s[...] = wu_ref[0].astype(BF16)
        wd_s[...] = wd_ref[0].astype(BF16)

    @pl.when(j < nv_ref[0])
    def _():
        h = xs_ref[...]
        gate = _dot(h, wg_s[...])
        hidden = (gate * jax.nn.sigmoid(gate)) * _dot(h, wu_s[...])
        ys_ref[...] = _dot(hidden.astype(BF16), wd_s[...]).astype(BF16)

    @pl.when(j >= nv_ref[0])
    def _():
        ys_ref[...] = jnp.zeros_like(ys_ref)


def _experts(xs, tile_expert, n_valid, w_gate, w_up, w_down, layer):
    rows, d = xs.shape
    row_tile = lambda j, te, nv: (jnp.minimum(j, nv[0] - 1), 0)
    weight = lambda j, te, nv: (layer, te[j], 0, 0)
    grid_spec = pltpu.PrefetchScalarGridSpec(
        num_scalar_prefetch=2,
        grid=(rows // ET,),
        in_specs=[
            pl.BlockSpec((ET, d), row_tile),
            pl.BlockSpec((None, 1, d, D_EXPERT), weight),
            pl.BlockSpec((None, 1, d, D_EXPERT), weight),
            pl.BlockSpec((None, 1, D_EXPERT, d), weight),
        ],
        out_specs=pl.BlockSpec((ET, d), lambda j, te, nv: (j, 0)),
        scratch_shapes=[pltpu.VMEM((d, D_EXPERT), BF16), pltpu.VMEM((d, D_EXPERT), BF16),
                        pltpu.VMEM((D_EXPERT, d), BF16)],
    )
    return pl.pallas_call(
        _experts_kernel,
        grid_spec=grid_spec,
        out_shape=jax.ShapeDtypeStruct((rows, d), BF16),
        compiler_params=_params("arbitrary"),
        name="moe_experts",
    )(tile_expert, n_valid, xs, w_gate, w_up, w_down)


def _combine_kernel(n16_ref, lo_ref, goff_ref, total_ref, ys_ref, dest_ref, w_ref, x_ref, mod_ref,
                    nf_ref, o_ref, buf_ref, sems, *, final):
    i = pl.program_id(0)
    nt = pl.num_programs(0)
    slot = i & 1

    def copy(dst_slot, local_row, global_row, n):
        return pltpu.make_async_copy(ys_ref.at[pl.ds(global_row, n)],
                                     buf_ref.at[dst_slot, pl.ds(local_row, n)], sems.at[dst_slot])

    @pl.when(i == 0)
    def _():
        buf_ref[...] = jnp.zeros_like(buf_ref)
        _for_each_piece(n16_ref, lo_ref, goff_ref, 0, lambda lr, gr, n: copy(0, lr, gr, n).start())

    @pl.when(i + 1 < nt)
    def _():
        _for_each_piece(n16_ref, lo_ref, goff_ref, i + 1,
                        lambda lr, gr, n: copy(1 - slot, lr, gr, n).start())

    _for_each_total_piece(total_ref, i, lambda n: copy(slot, 0, 0, n).wait())

    dest = dest_ref[:, 0:2].astype(jnp.int32)
    w = w_ref[:, 0:2]
    col = lax.broadcasted_iota(jnp.int32, (dest.shape[0], SORT_ROWS), 1)
    wmat = (jnp.where(col == dest[:, 0:1], w[:, 0:1], 0.0)
            + jnp.where(col == dest[:, 1:2], w[:, 1:2], 0.0))
    y = _dot(wmat.astype(BF16), buf_ref[slot])
    x = x_ref[...] + mod_ref[0, 5:6, :] * y
    if final:
        x = _rms_normalize(x) * nf_ref[...]
    o_ref[...] = x


def _combine(ys, dest_col, w_col, x, mod, norm_final, tables, tiles_per_batch, final):
    t, d = x.shape
    nt = t // TMS
    grid_spec = pltpu.PrefetchScalarGridSpec(
        num_scalar_prefetch=4,
        grid=(nt,),
        in_specs=[
            pl.BlockSpec(memory_space=pl.ANY),
            pl.BlockSpec((TMS, 128), lambda i, *_: (i, 0)),
            pl.BlockSpec((TMS, 128), lambda i, *_: (i, 0)),
            pl.BlockSpec((TMS, d), lambda i, *_: (i, 0)),
            pl.BlockSpec((1, 6, d), lambda i, *_: (i // tiles_per_batch, 0, 0)),
            pl.BlockSpec((1, d), lambda i, *_: (0, 0)),
        ],
        out_specs=pl.BlockSpec((TMS, d), lambda i, *_: (i, 0)),
        scratch_shapes=[pltpu.VMEM((2, SORT_ROWS, d), BF16), pltpu.SemaphoreType.DMA((2,))],
    )
    return pl.pallas_call(
        functools.partial(_combine_kernel, final=final),
        grid_spec=grid_spec,
        out_shape=jax.ShapeDtypeStruct((t, d), F32),
        compiler_params=_params("arbitrary"),
        name="moe_combine_final" if final else "moe_combine",
    )(*tables, ys, dest_col, w_col, x, mod, norm_final)


def _route_tables(counts, n_expert_tiles):
    n16 = (counts + (SEG - 1)) // SEG * SEG
    lo = jnp.cumsum(n16, axis=1) - n16
    total = jnp.sum(n16, axis=0)
    region = (total + (ET - 1)) // ET * ET
    ends = jnp.cumsum(region)
    base = ends - region
    goff = base[None, :] + jnp.cumsum(n16, axis=0) - n16
    tile_row = jnp.arange(n_expert_tiles, dtype=jnp.int32) * ET
    tile_expert = jnp.minimum(jnp.sum(tile_row[:, None] >= ends[None, :], axis=1), N_EXPERTS - 1)
    n_valid = (ends[-1] // ET).reshape(1)
    i32 = lambda a: a.astype(jnp.int32).reshape(-1)
    return (i32(n16), i32(lo), i32(goff), i32(jnp.sum(n16, axis=1))), \
        (i32(base + total), i32(region - total)), i32(tile_expert), i32(n_valid)


def _moe(h2, route_i, route_w, cnt, w_gate, w_up, w_down, layer, x, mod, norm_final, final):
    b, s, d = x.shape
    t = b * s
    nt = t // TMS
    n_expert_tiles = -(-(2 * t + nt * N_EXPERTS * (SEG - 1) + N_EXPERTS * (ET - 1)) // ET)
    counts = jnp.round(cnt[:, :, :, 0]).astype(jnp.int32).reshape(nt, N_EXPERTS)
    seg_tables, tail_tables, tile_expert, n_valid = _route_tables(counts, n_expert_tiles)
    tri = jnp.triu(jnp.ones((TMS, TMS), BF16), k=1)
    xs, dest = _dispatch(h2.reshape(t, d), route_i.reshape(nt, 8, TMS), tri,
                         seg_tables + tail_tables + (n_valid,), n_expert_tiles * ET)
    ys = _experts(xs, tile_expert, n_valid, w_gate, w_up, w_down, layer)
    out = _combine(ys, dest, route_w.reshape(t, 128), x.reshape(t, d), mod, norm_final, seg_tables,
                   s // TMS, final)
    return out.reshape(b, s, d)


def _block_diag(w):
    nb, c, _ = w.shape
    eye = jnp.eye(nb, dtype=w.dtype)
    return (eye[:, None, :, None] * w[:, :, None, :]).reshape(nb * c, nb * c)


def _pair_swa_heads(w, axis):
    half = SWA_Q_HEADS // 2
    shape = w.shape
    w = w.reshape(shape[:axis] + (2, half, HEAD_DIM) + shape[axis + 1:])
    return jnp.swapaxes(w, axis, axis + 1).reshape(shape)


def kernel(x, c, w_mod, b_mod, norm_mix, norm_ffn, w_in, w_out, out_gain, sinks, conv_w, conv_b,
           lru_wa, lru_ba, lru_wx, lru_bx, lru_lam, fox_bf, w_router_group, b_router_group,
           w_router_expert, b_router_expert, w_gate, w_up, w_down, norm_final):
    depth = w_mod.shape[0]
    b, s, d = x.shape
    assert d == D_MODEL and s % max(TM_PRE, TQ_SWA, TS_LRU, T_FOX, TM_POST) == 0
    assert TM_PRE == T_FOX
    c_rows = jnp.zeros((16, d), F32).at[:b].set(c)
    mod_all = _modulation(c_rows, w_mod, b_mod)[:, :b]

    for l in range(depth):
        mod = mod_all[l].reshape(b, 6, d)
        w_in_l = w_in[l]
        w_tok = jnp.concatenate([_pair_swa_heads(w_in_l[:, :SWA_WIDTH], 1), w_in_l[:, SWA_WIDTH:OFF_QC],
                                 w_in_l[:, OFF_KC:OFF_VC]], axis=1).astype(BF16)
        w_t = jnp.concatenate([w_in_l[:, OFF_QC:OFF_KC].T, w_in_l[:, OFF_VC:OFF_FC].T,
                               jnp.zeros((FORGET_ROWS, d), F32).at[:FOX_HEADS].set(w_in_l[:, OFF_FC:].T)],
                              axis=0).astype(BF16)
        fb = jnp.zeros((FORGET_ROWS, 1), F32).at[:FOX_HEADS, 0].set(fox_bf[l])
        gain = out_gain[l]
        gain_a = _pair_swa_heads(gain[:SWA_WIDTH], 0).reshape(1, SWA_WIDTH)
        gain_b = gain[SWA_WIDTH:SWA_WIDTH + LRU_WIDTH].reshape(1, LRU_WIDTH)
        gain_c = gain[SWA_WIDTH + LRU_WIDTH:].reshape(1, FOX_WIDTH)
        w_out_l = jnp.concatenate([_pair_swa_heads(w_out[l][:SWA_WIDTH], 0), w_out[l][SWA_WIDTH:]],
                                  axis=0).astype(BF16)
        wr = jnp.zeros((ROUTER_ROWS, d), F32)
        wr = wr.at[:N_GROUPS].set(w_router_group[l].T).at[8:8 + N_EXPERTS].set(w_router_expert[l].T)
        wr_hi = wr.astype(BF16)
        wr_lo = (wr - wr_hi.astype(F32)).astype(BF16)
        br = jnp.zeros((ROUTER_ROWS, 1), F32)
        br = br.at[:N_GROUPS, 0].set(b_router_group[l]).at[8:8 + N_EXPERTS, 0].set(b_router_expert[l])

        qa, ka, va, xb, gb, kc, qct, vaug, lf = _pre_mixer(
            x, mod, norm_mix[l].reshape(1, d), w_tok, w_t, fb)
        cum, cum_col = _cumsum(lf)
        kaug = _fox_prep(kc, cum_col)
        base = cum[:, :FOX_HEADS, ::T_FOX].reshape(-1)
        ya = _swa(qa, ka, va, sinks[l].reshape(1, SWA_Q_HEADS))
        yb = _lru(xb, gb, conv_w[l], conv_b[l].reshape(1, -1),
                  _block_diag(lru_wa[l]).astype(BF16), lru_ba[l].reshape(1, -1),
                  _block_diag(lru_wx[l]).astype(BF16), lru_bx[l].reshape(1, -1),
                  lru_lam[l].reshape(1, -1))
        yc = _fox(qct, kaug, vaug, base)
        x, h2, route_i, route_w, cnt = _post_mixer(
            ya, yb, yc, x, mod, gain_a, gain_b, gain_c, w_out_l,
            norm_ffn[l].reshape(1, d), wr_hi, wr_lo, br)
        x = _moe(h2, route_i, route_w, cnt, w_gate, w_up, w_down, l, x, mod,
                 norm_final.reshape(1, d), final=(l == depth - 1))
    return x
```

```python
import functools

import jax
import jax.numpy as jnp
from jax import lax
from jax.experimental import pallas as pl
from jax.experimental.pallas import tpu as pltpu

F32 = jnp.float32
BF16 = jnp.bfloat16

D_MODEL = 1024
HEAD_DIM = 64
PAIR = 2 * HEAD_DIM
BLOCK = 128
SWA_Q_HEADS = 8
SWA_WIDTH = SWA_Q_HEADS * HEAD_DIM
KV_WIDTH = 2 * HEAD_DIM
LRU_WIDTH = 256
LRU_BLOCKS = 8
CONV_WIDTH = 4
LRU_C = 8.0
FOX_HEADS = 4
FOX_WIDTH = FOX_HEADS * HEAD_DIM
N_GROUPS = 4
EXPERTS_PER_GROUP = 4
N_EXPERTS = 16
D_EXPERT = 256
EPS = 1e-6
NEG = -1e30
LOG2E = 1.4426950408889634

OFF_KA = SWA_WIDTH
OFF_VA = OFF_KA + KV_WIDTH
OFF_XB = OFF_VA + KV_WIDTH
OFF_GB = OFF_XB + LRU_WIDTH
OFF_QC = OFF_GB + LRU_WIDTH
OFF_KC = OFF_QC + FOX_WIDTH
OFF_VC = OFF_KC + FOX_WIDTH
OFF_FC = OFF_VC + FOX_WIDTH

ROUTER_ROWS = 32
FORGET_ROWS = 16
V_ROWS = HEAD_DIM + 16
VMEM_LIMIT_BYTES = 48 * 1024 * 1024

TM_PRE = 512
TQ_SWA = 512
TS_LRU = 512
T_FOX = 512
TM_POST = 512
CUMSUM_CHUNK = 256


def _params(*semantics):
    return pltpu.CompilerParams(dimension_semantics=semantics, vmem_limit_bytes=VMEM_LIMIT_BYTES)


def _split_bf16(v):
    hi = v.astype(BF16)
    lo = (v - hi.astype(F32)).astype(BF16)
    return hi, lo


def _dot(a, b):
    return jnp.dot(a, b, preferred_element_type=F32)


def _dot_nt(a, b):
    return lax.dot_general(a, b, (((1,), (1,)), ((), ())), preferred_element_type=F32)


def _rms_normalize(v):
    return v * lax.rsqrt(jnp.mean(v * v, axis=-1, keepdims=True) + EPS)


def _mod_kernel(c_ref, w_ref, b_ref, o_ref):
    c = c_ref[...]
    ca = c * jax.nn.sigmoid(c)
    w = w_ref[0]
    a_hi, a_lo = _split_bf16(ca)
    w_hi, w_lo = _split_bf16(w)
    o_ref[0] = _dot(a_hi, w_hi) + _dot(a_lo, w_hi) + _dot(a_hi, w_lo) + b_ref[0]


def _modulation(c, w_mod, b_mod):
    depth, d, d6 = w_mod.shape
    b = c.shape[0]
    n = d6 // d
    return pl.pallas_call(
        _mod_kernel,
        grid=(depth, n),
        in_specs=[
            pl.BlockSpec((b, d), lambda l, j: (0, 0)),
            pl.BlockSpec((1, d, d), lambda l, j: (l, 0, j)),
            pl.BlockSpec((1, 1, d), lambda l, j: (l, 0, j)),
        ],
        out_specs=pl.BlockSpec((1, b, d), lambda l, j: (l, 0, j)),
        out_shape=jax.ShapeDtypeStruct((depth, b, d6), F32),
        compiler_params=_params("arbitrary", "arbitrary"),
        name="modulation",
    )(c, w_mod, b_mod.reshape(depth, 1, d6))


def _log_sigmoid(z):
    return jnp.minimum(z, 0.0) - jnp.log1p(jnp.exp(-jnp.abs(z)))


def _other_half(a):
    return HEAD_DIM * (1 - a)


def _pre_kernel(x_ref, mod_ref, g_ref, w_ref, wt_ref, fb_ref,
                qa_ref, ka_ref, va_ref, xb_ref, gb_ref, kc_ref, qct_ref, vct_ref, lf_ref):
    x = x_ref[0]
    h = (_rms_normalize(x) * g_ref[...]) * (1.0 + mod_ref[0, 1:2, :]) + mod_ref[0, 0:1, :]
    hb = h.astype(BF16)

    def proj(lo, width):
        return _dot(hb, w_ref[:, lo:lo + width])

    scale = HEAD_DIM ** -0.5
    qa_ref[0] = (proj(0, SWA_WIDTH) * (scale * LOG2E)).astype(BF16)
    ka_ref[0] = proj(OFF_KA, KV_WIDTH).astype(BF16)
    va_ref[0] = proj(OFF_VA, KV_WIDTH).astype(BF16)
    xb_ref[0] = proj(OFF_XB, LRU_WIDTH).astype(BF16)
    gb_ref[0] = proj(OFF_GB, LRU_WIDTH).astype(BF16)
    kc_ref[0] = proj(OFF_QC, FOX_WIDTH).astype(BF16)

    t_all = _dot_nt(wt_ref[...], hb)
    qct_ref[0, 0] = (t_all[0:FOX_WIDTH] * (scale * LOG2E)).astype(BF16)
    ones_row = jnp.where(lax.broadcasted_iota(jnp.int32, (V_ROWS - HEAD_DIM, t_all.shape[1]), 0) == 0,
                         1.0, 0.0)
    for hd in range(FOX_HEADS):
        v_head = t_all[FOX_WIDTH + hd * HEAD_DIM:FOX_WIDTH + (hd + 1) * HEAD_DIM]
        vct_ref[0, 0, hd] = jnp.concatenate([v_head, ones_row], axis=0).astype(BF16)
    lf_ref[0] = _log_sigmoid(t_all[2 * FOX_WIDTH:] + fb_ref[...])


def _pre_mixer(x, mod, gain, w_tok, w_t, fb):
    b, s, d = x.shape
    tm = TM_PRE
    nt = s // tm

    def tok(width):
        return pl.BlockSpec((1, tm, width), lambda bi, i: (bi, i, 0))

    def tok_shape(width):
        return jax.ShapeDtypeStruct((b, s, width), BF16)

    return pl.pallas_call(
        _pre_kernel,
        grid=(b, nt),
        in_specs=[
            tok(d),
            pl.BlockSpec((1, 6, d), lambda bi, i: (bi, 0, 0)),
            pl.BlockSpec((1, d), lambda bi, i: (0, 0)),
            pl.BlockSpec(w_tok.shape, lambda bi, i: (0, 0)),
            pl.BlockSpec(w_t.shape, lambda bi, i: (0, 0)),
            pl.BlockSpec((FORGET_ROWS, 1), lambda bi, i: (0, 0)),
        ],
        out_specs=[
            tok(SWA_WIDTH), tok(KV_WIDTH), tok(KV_WIDTH), tok(LRU_WIDTH), tok(LRU_WIDTH),
            tok(FOX_WIDTH),
            pl.BlockSpec((1, 1, FOX_WIDTH, tm), lambda bi, i: (bi, i, 0, 0)),
            pl.BlockSpec((1, 1, FOX_HEADS, V_ROWS, tm), lambda bi, i: (bi, i, 0, 0, 0)),
            pl.BlockSpec((1, FORGET_ROWS, tm), lambda bi, i: (bi, 0, i)),
        ],
        out_shape=[
            tok_shape(SWA_WIDTH), tok_shape(KV_WIDTH), tok_shape(KV_WIDTH), tok_shape(LRU_WIDTH),
            tok_shape(LRU_WIDTH), tok_shape(FOX_WIDTH),
            jax.ShapeDtypeStruct((b, nt, FOX_WIDTH, tm), BF16),
            jax.ShapeDtypeStruct((b, nt, FOX_HEADS, V_ROWS, tm), BF16),
            jax.ShapeDtypeStruct((b, FORGET_ROWS, s), F32),
        ],
        compiler_params=_params("parallel", "parallel"),
        name="pre_mixer",
    )(x, mod, gain, w_tok, w_t, fb)


def _cumsum_kernel(lf_ref, o_ref, col_ref):
    n = CUMSUM_CHUNK
    rows = lf_ref.shape[1]
    s = lf_ref.shape[2]
    r = lax.broadcasted_iota(jnp.int32, (n, n), 0)
    c = lax.broadcasted_iota(jnp.int32, (n, n), 1)
    tri = jnp.where(r <= c, 1.0, 0.0).astype(BF16)
    carry = jnp.zeros((rows, 1), F32)
    for j in range(s // n):
        seg = lf_ref[0, :, j * n:(j + 1) * n]
        hi = seg.astype(BF16)
        rem = seg - hi.astype(F32)
        mid = rem.astype(BF16)
        lo = (rem - mid.astype(F32)).astype(BF16)
        cs = (_dot(hi, tri) + _dot(mid, tri)) + _dot(lo, tri) + carry
        o_ref[0, :, j * n:(j + 1) * n] = cs
        parts = _split3(cs[0:FOX_HEADS] * LOG2E)
        col_ref[0, j * n:(j + 1) * n, :] = _rows_to_columns(
            [parts[k][hd:hd + 1] for hd in range(FOX_HEADS) for k in range(3)])
        carry = cs[:, n - 1:n]


def _cumsum(lf):
    b, r, s = lf.shape
    return pl.pallas_call(
        _cumsum_kernel,
        grid=(b,),
        in_specs=[pl.BlockSpec((1, r, s), lambda bi: (bi, 0, 0))],
        out_specs=[pl.BlockSpec((1, r, s), lambda bi: (bi, 0, 0)),
                   pl.BlockSpec((1, s, 128), lambda bi: (bi, 0, 0))],
        out_shape=[jax.ShapeDtypeStruct((b, r, s), F32), jax.ShapeDtypeStruct((b, s, 128), F32)],
        compiler_params=_params("parallel"),
        name="cumsum",
    )(lf)


def _swa_kernel(sink_ref, q_ref, kc_ref, vc_ref, kp_ref, vp_ref, o_ref, kf_ref, vf_ref, bias_ref):
    i = pl.program_id(1)
    nsub = q_ref.shape[1] // BLOCK
    npair = SWA_Q_HEADS // 2

    @pl.when((pl.program_id(0) == 0) & (i == 0))
    def _():
        qpos = BLOCK + lax.broadcasted_iota(jnp.int32, (BLOCK, 2 * BLOCK), 0)
        kpos = lax.broadcasted_iota(jnp.int32, (BLOCK, 2 * BLOCK), 1)
        dist = qpos - kpos
        valid = (dist >= 0) & (dist < BLOCK)
        distf = dist.astype(F32)
        for hd in range(SWA_Q_HEADS):
            slope = 2.0 ** (-8.0 * (hd + 1) / SWA_Q_HEADS)
            bias_ref[hd] = jnp.where(valid, (-slope * LOG2E) * distf, NEG)

    kf_ref[0:BLOCK, :] = kp_ref[0]
    kf_ref[BLOCK:, :] = kc_ref[0]
    vf_ref[0:BLOCK, :] = vp_ref[0]
    vf_ref[BLOCK:, :] = vc_ref[0]

    lane = lax.broadcasted_iota(jnp.int32, (2 * BLOCK, PAIR), 1)
    lane_o = lax.broadcasted_iota(jnp.int32, (BLOCK, PAIR), 1)
    col = lax.broadcasted_iota(jnp.int32, (BLOCK, 2 * BLOCK), 1)

    for j in range(nsub):
        k2 = kf_ref[j * BLOCK:(j + 2) * BLOCK, :]
        v2 = vf_ref[j * BLOCK:(j + 2) * BLOCK, :]
        k_half = (jnp.where(lane < HEAD_DIM, k2, jnp.zeros_like(k2)),
                  jnp.where(lane >= HEAD_DIM, k2, jnp.zeros_like(k2)))
        for p in range(npair):
            qp = q_ref[0, j * BLOCK:(j + 1) * BLOCK, p * PAIR:(p + 1) * PAIR]
            outs = []
            for a in range(2):
                hd = p + a * npair
                sc = _dot_nt(qp, k_half[a]) + bias_ref[hd]
                if j == 0:
                    sc = sc + jnp.where(col < BLOCK, jnp.where(i == 0, NEG, 0.0), 0.0)
                sink = sink_ref[0, hd] * LOG2E
                m = jnp.maximum(jnp.max(sc, axis=-1, keepdims=True), sink)
                pr = jnp.exp2(sc - m)
                den = jnp.sum(pr, axis=-1, keepdims=True) + jnp.exp2(sink - m)
                outs.append(_dot(pr.astype(BF16), v2) / den)
            o_ref[0, j * BLOCK:(j + 1) * BLOCK, p * PAIR:(p + 1) * PAIR] = jnp.where(
                lane_o < HEAD_DIM, outs[0], outs[1]).astype(BF16)


def _swa(qa, ka, va, sinks):
    b, s, _ = qa.shape
    tq = TQ_SWA
    nsub = tq // BLOCK
    prev = lambda bi, i: (bi, jnp.maximum(i * nsub - 1, 0), 0)
    return pl.pallas_call(
        _swa_kernel,
        grid=(b, s // tq),
        in_specs=[
            pl.BlockSpec(memory_space=pltpu.SMEM),
            pl.BlockSpec((1, tq, SWA_WIDTH), lambda bi, i: (bi, i, 0)),
            pl.BlockSpec((1, tq, KV_WIDTH), lambda bi, i: (bi, i, 0)),
            pl.BlockSpec((1, tq, KV_WIDTH), lambda bi, i: (bi, i, 0)),
            pl.BlockSpec((1, BLOCK, KV_WIDTH), prev),
            pl.BlockSpec((1, BLOCK, KV_WIDTH), prev),
        ],
        out_specs=pl.BlockSpec((1, tq, SWA_WIDTH), lambda bi, i: (bi, i, 0)),
        out_shape=jax.ShapeDtypeStruct((b, s, SWA_WIDTH), BF16),
        scratch_shapes=[
            pltpu.VMEM((tq + BLOCK, KV_WIDTH), BF16),
            pltpu.VMEM((tq + BLOCK, KV_WIDTH), BF16),
            pltpu.VMEM((SWA_Q_HEADS, BLOCK, 2 * BLOCK), F32),
        ],
        compiler_params=_params("arbitrary", "arbitrary"),
        name="swa",
    )(sinks, qa, ka, va, ka, va)


def _gelu_tanh(v):
    return 0.5 * v * (1.0 + jnp.tanh(0.7978845608028654 * (v + 0.044715 * (v * v * v))))


def _lru_kernel(xb_ref, gb_ref, cw_ref, cb_ref, wa_ref, ba_ref, wx_ref, bx_ref, lam_ref,
                o_ref, win_ref, h_ref):
    ts = xb_ref.shape[1]

    @pl.when(pl.program_id(1) == 0)
    def _():
        win_ref[0:8, :] = jnp.zeros((8, LRU_WIDTH), F32)
        h_ref[...] = jnp.zeros_like(h_ref)

    x = xb_ref[0].astype(F32)
    win_ref[8:, :] = x
    conv = cb_ref[...] + cw_ref[CONV_WIDTH - 1:CONV_WIDTH, :] * x
    for k in range(1, CONV_WIDTH):
        conv = conv + cw_ref[CONV_WIDTH - 1 - k:CONV_WIDTH - k, :] * win_ref[8 - k:8 - k + ts, :]
    win_ref[0:8, :] = x[ts - 8:ts]

    cbf = conv.astype(BF16)
    r = jax.nn.sigmoid(_dot(cbf, wa_ref[...]) + ba_ref[...])
    gi = jax.nn.sigmoid(_dot(cbf, wx_ref[...]) + bx_ref[...])
    z = -lam_ref[...]
    softplus = jnp.maximum(z, 0.0) + jnp.log1p(jnp.exp(-jnp.abs(z)))
    log_a = (-LRU_C * softplus) * r
    a = jnp.exp(log_a)
    v = 1.0 - a * a
    u = jnp.where(v == 0.0, 0.0, v * lax.rsqrt(v)) * (gi * conv)

    a = a.reshape(ts // 8, 8, LRU_WIDTH)
    u = u.reshape(ts // 8, 8, LRU_WIDTH)
    sub = lax.broadcasted_iota(jnp.int32, a.shape, 1)
    for step in (1, 2, 4):
        keep = sub >= step
        u = jnp.where(keep, a * pltpu.roll(u, step, 1) + u, u)
        a = jnp.where(keep, a * pltpu.roll(a, step, 1), a)
    carry = h_ref[7:8, :]
    groups = []
    for g in range(ts // 8):
        hg = u[g] + a[g] * carry
        groups.append(hg)
        carry = hg[7:8, :]
    h = jnp.concatenate(groups, axis=0)
    h_ref[...] = groups[-1]
    o_ref[0] = (h * _gelu_tanh(gb_ref[0].astype(F32))).astype(BF16)


def _lru(xb, gb, conv_w, conv_b, wa, ba, wx, bx, lam):
    b, s, w = xb.shape
    ts = TS_LRU
    tok = pl.BlockSpec((1, ts, w), lambda bi, i: (bi, i, 0))
    full = lambda shape: pl.BlockSpec(shape, lambda bi, i: (0,) * len(shape))
    return pl.pallas_call(
        _lru_kernel,
        grid=(b, s // ts),
        in_specs=[tok, tok, full((CONV_WIDTH, w)), full((1, w)), full((w, w)), full((1, w)),
                  full((w, w)), full((1, w)), full((1, w))],
        out_specs=tok,
        out_shape=jax.ShapeDtypeStruct((b, s, w), BF16),
        scratch_shapes=[pltpu.VMEM((8 + ts, w), F32), pltpu.VMEM((8, w), F32)],
        compiler_params=_params("arbitrary", "arbitrary"),
        name="lru",
    )(xb, gb, conv_w, conv_b, wa, ba, wx, bx, lam)


BIAS_ROWS = 6


def _split3(v):
    hi = v.astype(BF16).astype(F32)
    rem = v - hi
    mid = rem.astype(BF16).astype(F32)
    lo = (rem - mid).astype(BF16).astype(F32)
    return hi, mid, lo


def _fox_prep_kernel(k_ref, c_ref, o_ref):
    tm = k_ref.shape[1]
    lane = lax.broadcasted_iota(jnp.int32, (tm, PAIR), 1)
    for p in range(FOX_WIDTH // PAIR):
        kp = k_ref[0, :, p * PAIR:(p + 1) * PAIR].astype(F32)
        for a in range(2):
            hd = 2 * p + a
            c_hi, c_mid, c_lo = (c_ref[0, :, 3 * hd + k:3 * hd + k + 1] for k in range(3))
            rel = lane - _other_half(a)
            aug = jnp.where(rel == 0, c_hi, jnp.where(rel == 1, c_mid, jnp.where(
                rel == 2, c_lo, jnp.where(rel < BIAS_ROWS, 1.0, 0.0))))
            in_half = (lane >= HEAD_DIM * a) & (lane < HEAD_DIM * (a + 1))
            o_ref[0, hd] = jnp.where(in_half, kp, aug).astype(BF16)


def _fox_prep(kc, cum_col):
    b, s, w = kc.shape
    tm = TM_PRE
    return pl.pallas_call(
        _fox_prep_kernel,
        grid=(b, s // tm),
        in_specs=[
            pl.BlockSpec((1, tm, w), lambda bi, i: (bi, i, 0)),
            pl.BlockSpec((1, tm, cum_col.shape[2]), lambda bi, i: (bi, i, 0)),
        ],
        out_specs=pl.BlockSpec((1, FOX_HEADS, tm, PAIR), lambda bi, i: (bi, 0, i, 0)),
        out_shape=jax.ShapeDtypeStruct((b, FOX_HEADS, s, PAIR), BF16),
        compiler_params=_params("parallel", "parallel"),
        name="fox_prep",
    )(kc, cum_col)


def _fox_kernel(base_ref, qt_ref, ka_ref, vt_ref, o_ref, acc_ref, s0_ref, s1_ref):
    bi = pl.program_id(0)
    p = pl.program_id(1)
    qi = pl.program_id(2)
    nq = pl.num_programs(2)
    t = qt_ref.shape[3]
    qt = qt_ref[0, 0].astype(F32)
    row = lax.broadcasted_iota(jnp.int32, (PAIR, t), 0)
    causal = (lax.broadcasted_iota(jnp.int32, (t, t), 0) <= lax.broadcasted_iota(jnp.int32, (t, t), 1))
    q_aug = []
    for a in range(2):
        hd = 2 * p + a
        base = jnp.full((1, t), base_ref[(bi * FOX_HEADS + hd) * nq + qi], F32) * LOG2E
        b_hi, b_mid, b_lo = _split3(base)
        rel = row - _other_half(a)
        aug = jnp.where(rel < 3, -1.0, jnp.where(rel == 3, b_hi, jnp.where(
            rel == 4, b_mid, jnp.where(rel == 5, b_lo, 0.0))))
        in_half = (row >= HEAD_DIM * a) & (row < HEAD_DIM * (a + 1))
        q_aug.append(jnp.where(in_half, qt, aug).astype(BF16))
        acc_ref[a] = jnp.zeros((V_ROWS, t), F32)

    slots = (s0_ref, s1_ref)

    def scores(ki, slot, masked):
        tile_max = []
        for a in range(2):
            k = ka_ref[0, a, pl.ds(pl.multiple_of(ki * t, t), t), :]
            st = _dot(k, q_aug[a])
            if masked:
                st = jnp.where(causal, st, NEG)
            slots[slot][a] = st
            tile_max.append(jnp.max(st, axis=0, keepdims=True))
        return tuple(tile_max)

    def consume(ki, slot, m_old, tile_max):
        m_new = []
        for a in range(2):
            m_a = jnp.maximum(m_old[a], tile_max[a])
            alpha = jnp.exp2(m_old[a] - m_a)
            pr = jnp.exp2(slots[slot][a] - m_a).astype(BF16)
            acc_ref[a] = alpha * acc_ref[a] + _dot(vt_ref[0, ki, a], pr)
            m_new.append(m_a)
        return tuple(m_new)

    def advance(j, slot, carry):
        m_old, tile_max, prev = carry
        next_max = scores(j, 1 - slot, False)
        return consume(prev, slot, m_old, tile_max), next_max, j

    def body(i, carry):
        return advance(2 * i + 1, 1, advance(2 * i, 0, carry))

    m0 = jnp.full((1, t), NEG, F32)
    carry = ((m0, m0), scores(qi, 0, True), qi)
    carry = lax.fori_loop(0, qi // 2, body, carry)

    def finish_odd(carry):
        m, tile_max, prev = advance(qi - 1, 0, carry)
        consume(prev, 1, m, tile_max)
        return 0

    def finish_even(carry):
        m, tile_max, prev = carry
        consume(prev, 0, m, tile_max)
        return 0

    lax.cond(qi % 2 == 1, finish_odd, finish_even, carry)
    outs = []
    for a in range(2):
        acc = acc_ref[a]
        outs.append(acc[0:HEAD_DIM] / acc[HEAD_DIM:HEAD_DIM + 1, :])
    o_ref[0] = jnp.concatenate(outs, axis=0).T.astype(BF16)


def _fox(qct, kaug, vaug, base):
    b, nt, w, t = qct.shape
    s = nt * t
    npair = w // PAIR
    return pl.pallas_call(
        _fox_kernel,
        grid=(b, npair, nt),
        in_specs=[
            pl.BlockSpec(memory_space=pltpu.SMEM),
            pl.BlockSpec((1, 1, PAIR, t), lambda bi, p, qi: (bi, qi, p, 0)),
            pl.BlockSpec((1, 2, s, PAIR), lambda bi, p, qi: (bi, p, 0, 0)),
            pl.BlockSpec((1, nt, 2, V_ROWS, t), lambda bi, p, qi: (bi, 0, p, 0, 0)),
        ],
        out_specs=pl.BlockSpec((1, t, PAIR), lambda bi, p, qi: (bi, qi, p)),
        out_shape=jax.ShapeDtypeStruct((b, s, w), BF16),
        scratch_shapes=[pltpu.VMEM((2, V_ROWS, t), F32), pltpu.VMEM((2, t, t), F32),
                        pltpu.VMEM((2, t, t), F32)],
        compiler_params=_params("parallel", "parallel", "arbitrary"),
        name="fox",
    )(base, qct, kaug, vaug)


def _rows_to_columns(rows):
    n = rows[0].shape[1]
    sub = lax.broadcasted_iota(jnp.int32, (128, n), 0)
    block = jnp.zeros((128, n), F32)
    for k, row in enumerate(rows):
        block = jnp.where(sub == k, row, block)
    return block.T


def _first_max(vals):
    best = vals[0]
    for v in vals[1:]:
        best = jnp.maximum(best, v)
    idx = jnp.full(best.shape, len(vals) - 1, jnp.int32)
    for j in range(len(vals) - 2, -1, -1):
        idx = jnp.where(vals[j] == best, j, idx)
    return best, idx


def _softmax_rows(vals):
    top = vals[0]
    for v in vals[1:]:
        top = jnp.maximum(top, v)
    ex = [jnp.exp(v - top) for v in vals]
    tot = ex[0]
    for e in ex[1:]:
        tot = tot + e
    return [e / tot for e in ex]


def _post_kernel(ya_ref, yb_ref, yc_ref, x_ref, mod_ref, ga_ref, gb_ref, gc_ref, wo_ref, nf_ref,
                 wr_hi_ref, wr_lo_ref, br_ref, xo_ref, h2_ref, ri_ref, rw_ref, cnt_ref):
    ya = _rms_normalize(ya_ref[0].astype(F32)) * ga_ref[...]
    yb = _rms_normalize(yb_ref[0].astype(F32)) * gb_ref[...]
    yc = _rms_normalize(yc_ref[0].astype(F32)) * gc_ref[...]
    y = jnp.concatenate([ya, yb, yc], axis=-1).astype(BF16)
    x = x_ref[0] + mod_ref[0, 2:3, :] * _dot(y, wo_ref[...])
    xo_ref[0] = x
    h = (_rms_normalize(x) * nf_ref[...]) * (1.0 + mod_ref[0, 4:5, :]) + mod_ref[0, 3:4, :]
    h_hi, h_lo = _split_bf16(h)
    h2_ref[0] = h_hi

    w_hi = wr_hi_ref[...]
    logits = (_dot_nt(w_hi, h_hi) + _dot_nt(w_hi, h_lo)) + _dot_nt(wr_lo_ref[...], h_hi) + br_ref[...]
    rows = [logits[r:r + 1, :] for r in range(8 + N_EXPERTS)]
    group_prob = _softmax_rows(rows[0:N_GROUPS])
    group_p, group_idx = _first_max(group_prob)
    in_group = []
    for j in range(EXPERTS_PER_GROUP):
        v = rows[8 + (N_GROUPS - 1) * EXPERTS_PER_GROUP + j]
        for g in range(N_GROUPS - 2, -1, -1):
            v = jnp.where(group_idx == g, rows[8 + g * EXPERTS_PER_GROUP + j], v)
        in_group.append(v)
    expert_prob = _softmax_rows(in_group)
    p1, i1 = _first_max(expert_prob)
    rest = [jnp.where(i1 == j, -1.0, expert_prob[j]) for j in range(EXPERTS_PER_GROUP)]
    p2, i2 = _first_max(rest)
    e1 = group_idx * EXPERTS_PER_GROUP + i1
    e2 = group_idx * EXPERTS_PER_GROUP + i2
    tm = e1.shape[1]
    ri_ref[0, 0] = jnp.zeros((8, tm), jnp.int32)
    ri_ref[0, 0, 0:1, :] = e1
    ri_ref[0, 0, 1:2, :] = e2
    rw_ref[0] = _rows_to_columns([group_p * p1 / (p1 + p2), group_p * p2 / (p1 + p2)])
    erow = lax.broadcasted_iota(jnp.int32, (N_EXPERTS, tm), 0)
    sel = jnp.where((erow == e1) | (erow == e2), 1.0, 0.0).astype(BF16)
    cnt_ref[0, 0] = _dot(sel, jnp.ones((tm, 128), BF16))


def _post_mixer(ya, yb, yc, x, mod, gain_a, gain_b, gain_c, w_out, norm_ffn, wr_hi, wr_lo, br):
    b, s, d = x.shape
    tm = TM_POST
    tok = lambda width: pl.BlockSpec((1, tm, width), lambda bi, i: (bi, i, 0))
    full = lambda shape: pl.BlockSpec(shape, lambda bi, i: (0,) * len(shape))
    return pl.pallas_call(
        _post_kernel,
        grid=(b, s // tm),
        in_specs=[
            tok(SWA_WIDTH), tok(LRU_WIDTH), tok(FOX_WIDTH), tok(d),
            pl.BlockSpec((1, 6, d), lambda bi, i: (bi, 0, 0)),
            full((1, SWA_WIDTH)), full((1, LRU_WIDTH)), full((1, FOX_WIDTH)),
            full((d, d)), full((1, d)),
            full((ROUTER_ROWS, d)), full((ROUTER_ROWS, d)), full((ROUTER_ROWS, 1)),
        ],
        out_specs=[
            tok(d), tok(d),
            pl.BlockSpec((1, 1, 8, tm), lambda bi, i: (bi, i, 0, 0)),
            tok(128),
            pl.BlockSpec((1, 1, N_EXPERTS, 128), lambda bi, i: (bi, i, 0, 0)),
        ],
        out_shape=[
            jax.ShapeDtypeStruct((b, s, d), F32),
            jax.ShapeDtypeStruct((b, s, d), BF16),
            jax.ShapeDtypeStruct((b, s // tm, 8, tm), jnp.int32),
            jax.ShapeDtypeStruct((b, s, 128), F32),
            jax.ShapeDtypeStruct((b, s // tm, N_EXPERTS, 128), F32),
        ],
        compiler_params=_params("parallel", "parallel"),
        name="post_mixer",
    )(ya, yb, yc, x, mod, gain_a, gain_b, gain_c, w_out, norm_ffn, wr_hi, wr_lo, br)


TMS = TM_POST
SEG = 16
SORT_ROWS = 1280
ET = 1024
SEG_PIECES = (512, 256, 128, 64, 32, 16)
TAIL_PIECES = (512, 256, 128, 64, 32, 16)
TOTAL_PIECES = (1024, 512, 256, 128, 64, 32, 16)
RARE_ROWS = 128
assert SORT_ROWS >= 2 * TMS + N_EXPERTS * (SEG - 1) and SEG_PIECES[0] == TMS and TAIL_PIECES[0] * 2 == ET


def _for_each_piece(n16_ref, lo_ref, goff_ref, tile, fn):
    for e in range(N_EXPERTS):
        n = n16_ref[tile * N_EXPERTS + e]
        lo = lo_ref[tile * N_EXPERTS + e]
        go = goff_ref[tile * N_EXPERTS + e]

        def pieces(sizes, n=n, lo=lo, go=go):
            for rows in sizes:
                done = n & (-2 * rows)

                @pl.when((n & rows) != 0)
                def _(done=done, rows=rows):
                    fn(pl.multiple_of(lo + done, SEG), pl.multiple_of(go + done, SEG), rows)

        @pl.when(n >= RARE_ROWS)
        def _():
            pieces([r for r in SEG_PIECES if r >= RARE_ROWS])

        pieces([r for r in SEG_PIECES if r < RARE_ROWS])


def _for_each_total_piece(total_ref, tile, fn):
    n = total_ref[tile]
    for rows in TOTAL_PIECES:
        @pl.when((n & rows) != 0)
        def _(rows=rows):
            fn(rows)


def _dispatch_kernel(n16_ref, lo_ref, goff_ref, total_ref, tail_row_ref, tail_n_ref, n_valid_ref,
                     h_ref, ri_ref, tri_ref, xs_ref, dest_ref, sorted_ref, zero_ref, sem):
    i = pl.program_id(0)
    last = pl.num_programs(0) - 1
    slot = i & 1
    tms = h_ref.shape[0]

    ri = ri_ref[0]
    e1 = ri[0:1, :]
    e2 = ri[1:2, :]
    erow = lax.broadcasted_iota(jnp.int32, (N_EXPERTS, tms), 0)
    m1 = erow == e1
    m2 = erow == e2
    sel = jnp.where(m1 | m2, 1.0, 0.0).astype(BF16)
    rank = _dot(sel, tri_ref[...]).astype(jnp.int32)
    start = jnp.zeros((N_EXPERTS, tms), jnp.int32)
    for e in range(N_EXPERTS):
        start = jnp.where(erow == e, lo_ref[i * N_EXPERTS + e], start)
    dest = rank + start
    r1 = jnp.sum(jnp.where(m1, dest, 0), axis=0, keepdims=True)
    r2 = jnp.sum(jnp.where(m2, dest, 0), axis=0, keepdims=True)
    dest_ref[...] = _rows_to_columns([r1.astype(F32), r2.astype(F32)])

    rows = lax.broadcasted_iota(jnp.int32, (SORT_ROWS, tms), 0)
    perm = jnp.where((rows == r1) | (rows == r2), 1.0, 0.0).astype(BF16)
    sorted_ref[slot] = _dot(perm, h_ref[...]).astype(BF16)

    def copy(src_slot, local_row, global_row, n):
        return pltpu.make_async_copy(sorted_ref.at[src_slot, pl.ds(local_row, n)],
                                     xs_ref.at[pl.ds(global_row, n)], sem)

    @pl.when(i > 0)
    def _():
        _for_each_total_piece(total_ref, i - 1, lambda n: copy(1 - slot, 0, 0, n).wait())

    _for_each_piece(n16_ref, lo_ref, goff_ref, i, lambda lr, gr, n: copy(slot, lr, gr, n).start())

    @pl.when(i == last)
    def _():
        _for_each_total_piece(total_ref, i, lambda n: copy(slot, 0, 0, n).wait())
        zero_ref[...] = jnp.zeros_like(zero_ref)

        def tail_copies(fn):
            for e in range(N_EXPERTS):
                n = tail_n_ref[e]
                row = tail_row_ref[e]
                for rows_ in TAIL_PIECES:
                    done = n & (-2 * rows_)

                    @pl.when((n & rows_) != 0)
                    def _(row=row, done=done, rows_=rows_):
                        fn(pltpu.make_async_copy(
                            zero_ref.at[pl.ds(0, rows_)],
                            xs_ref.at[pl.ds(pl.multiple_of(row + done, SEG), rows_)], sem))

        tail_copies(lambda c: c.start())
        tail_copies(lambda c: c.wait())

        def unused_tiles(fn):
            def body(j, carry):
                for part in range(ET // TAIL_PIECES[0]):
                    row = pl.multiple_of(j * ET + part * TAIL_PIECES[0], SEG)
                    fn(pltpu.make_async_copy(zero_ref, xs_ref.at[pl.ds(row, TAIL_PIECES[0])], sem))
                return carry
            lax.fori_loop(n_valid_ref[0], xs_ref.shape[0] // ET, body, 0)

        unused_tiles(lambda c: c.start())
        unused_tiles(lambda c: c.wait())


def _dispatch(h2, route_i, tri, tables, rows_max):
    t, d = h2.shape
    nt = t // TMS
    grid_spec = pltpu.PrefetchScalarGridSpec(
        num_scalar_prefetch=7,
        grid=(nt,),
        in_specs=[
            pl.BlockSpec((TMS, d), lambda i, *_: (i, 0)),
            pl.BlockSpec((1, 8, TMS), lambda i, *_: (i, 0, 0)),
            pl.BlockSpec((TMS, TMS), lambda i, *_: (0, 0)),
        ],
        out_specs=[
            pl.BlockSpec(memory_space=pl.ANY),
            pl.BlockSpec((TMS, 128), lambda i, *_: (i, 0)),
        ],
        scratch_shapes=[
            pltpu.VMEM((2, SORT_ROWS, d), BF16),
            pltpu.VMEM((TAIL_PIECES[0], d), BF16),
            pltpu.SemaphoreType.DMA,
        ],
    )
    return pl.pallas_call(
        _dispatch_kernel,
        grid_spec=grid_spec,
        out_shape=[
            jax.ShapeDtypeStruct((rows_max, d), BF16),
            jax.ShapeDtypeStruct((t, 128), F32),
        ],
        compiler_params=_params("arbitrary"),
        name="moe_dispatch",
    )(*tables, h2, route_i, tri)


def _experts_kernel(te_ref, nv_ref, xs_ref, wg_ref, wu_ref, wd_ref, ys_ref, wg_s, wu_s, wd_s):
    j = pl.program_id(0)

    @pl.when((j == 0) | (te_ref[j] != te_ref[jnp.maximum(j - 1, 0)]))
    def _():
        wg_s[...] = wg_ref[0].astype(BF16)
        wu_s[...] = wu_ref[0].astype(BF16)
        wd_s[...] = wd_ref[0].astype(BF16)

    @pl.when(j < nv_ref[0])
    def _():
        h = xs_ref[...]
        gate = _dot(h, wg_s[...])
        hidden = (gate * jax.nn.sigmoid(gate)) * _dot(h, wu_s[...])
        ys_ref[...] = _dot(hidden.astype(BF16), wd_s[...]).astype(BF16)

    @pl.when(j >= nv_ref[0])
    def _():
        ys_ref[...] = jnp.zeros_like(ys_ref)


def _experts(xs, tile_expert, n_valid, w_gate, w_up, w_down, layer):
    rows, d = xs.shape
    row_tile = lambda j, te, nv: (jnp.minimum(j, nv[0] - 1), 0)
    weight = lambda j, te, nv: (layer, te[j], 0, 0)
    grid_spec = pltpu.PrefetchScalarGridSpec(
        num_scalar_prefetch=2,
        grid=(rows // ET,),
        in_specs=[
            pl.BlockSpec((ET, d), row_tile),
            pl.BlockSpec((None, 1, d, D_EXPERT), weight),
            pl.BlockSpec((None, 1, d, D_EXPERT), weight),
            pl.BlockSpec((None, 1, D_EXPERT, d), weight),
        ],
        out_specs=pl.BlockSpec((ET, d), lambda j, te, nv: (j, 0)),
        scratch_shapes=[pltpu.VMEM((d, D_EXPERT), BF16), pltpu.VMEM((d, D_EXPERT), BF16),
                        pltpu.VMEM((D_EXPERT, d), BF16)],
    )
    return pl.pallas_call(
        _experts_kernel,
        grid_spec=grid_spec,
        out_shape=jax.ShapeDtypeStruct((rows, d), BF16),
        compiler_params=_params("arbitrary"),
        name="moe_experts",
    )(tile_expert, n_valid, xs, w_gate, w_up, w_down)


def _combine_kernel(n16_ref, lo_ref, goff_ref, total_ref, ys_ref, dest_ref, w_ref, x_ref, mod_ref,
                    nf_ref, o_ref, buf_ref, sems, *, final):
    i = pl.program_id(0)
    nt = pl.num_programs(0)
    slot = i & 1

    def copy(dst_slot, local_row, global_row, n):
        return pltpu.make_async_copy(ys_ref.at[pl.ds(global_row, n)],
                                     buf_ref.at[dst_slot, pl.ds(local_row, n)], sems.at[dst_slot])

    @pl.when(i == 0)
    def _():
        buf_ref[...] = jnp.zeros_like(buf_ref)
        _for_each_piece(n16_ref, lo_ref, goff_ref, 0, lambda lr, gr, n: copy(0, lr, gr, n).start())

    @pl.when(i + 1 < nt)
    def _():
        _for_each_piece(n16_ref, lo_ref, goff_ref, i + 1,
                        lambda lr, gr, n: copy(1 - slot, lr, gr, n).start())

    _for_each_total_piece(total_ref, i, lambda n: copy(slot, 0, 0, n).wait())

    dest = dest_ref[:, 0:2].astype(jnp.int32)
    w = w_ref[:, 0:2]
    col = lax.broadcasted_iota(jnp.int32, (dest.shape[0], SORT_ROWS), 1)
    wmat = (jnp.where(col == dest[:, 0:1], w[:, 0:1], 0.0)
            + jnp.where(col == dest[:, 1:2], w[:, 1:2], 0.0))
    y = _dot(wmat.astype(BF16), buf_ref[slot])
    x = x_ref[...] + mod_ref[0, 5:6, :] * y
    if final:
        x = _rms_normalize(x) * nf_ref[...]
    o_ref[...] = x


def _combine(ys, dest_col, w_col, x, mod, norm_final, tables, tiles_per_batch, final):
    t, d = x.shape
    nt = t // TMS
    grid_spec = pltpu.PrefetchScalarGridSpec(
        num_scalar_prefetch=4,
        grid=(nt,),
        in_specs=[
            pl.BlockSpec(memory_space=pl.ANY),
            pl.BlockSpec((TMS, 128), lambda i, *_: (i, 0)),
            pl.BlockSpec((TMS, 128), lambda i, *_: (i, 0)),
            pl.BlockSpec((TMS, d), lambda i, *_: (i, 0)),
            pl.BlockSpec((1, 6, d), lambda i, *_: (i // tiles_per_batch, 0, 0)),
            pl.BlockSpec((1, d), lambda i, *_: (0, 0)),
        ],
        out_specs=pl.BlockSpec((TMS, d), lambda i, *_: (i, 0)),
        scratch_shapes=[pltpu.VMEM((2, SORT_ROWS, d), BF16), pltpu.SemaphoreType.DMA((2,))],
    )
    return pl.pallas_call(
        functools.partial(_combine_kernel, final=final),
        grid_spec=grid_spec,
        out_shape=jax.ShapeDtypeStruct((t, d), F32),
        compiler_params=_params("arbitrary"),
        name="moe_combine_final" if final else "moe_combine",
    )(*tables, ys, dest_col, w_col, x, mod, norm_final)


def _route_tables(counts, n_expert_tiles):
    n16 = (counts + (SEG - 1)) // SEG * SEG
    lo = jnp.cumsum(n16, axis=1) - n16
    total = jnp.sum(n16, axis=0)
    region = (total + (ET - 1)) // ET * ET
    ends = jnp.cumsum(region)
    base = ends - region
    goff = base[None, :] + jnp.cumsum(n16, axis=0) - n16
    tile_row = jnp.arange(n_expert_tiles, dtype=jnp.int32) * ET
    tile_expert = jnp.minimum(jnp.sum(tile_row[:, None] >= ends[None, :], axis=1), N_EXPERTS - 1)
    n_valid = (ends[-1] // ET).reshape(1)
    i32 = lambda a: a.astype(jnp.int32).reshape(-1)
    return (i32(n16), i32(lo), i32(goff), i32(jnp.sum(n16, axis=1))), \
        (i32(base + total), i32(region - total)), i32(tile_expert), i32(n_valid)


def _moe(h2, route_i, route_w, cnt, w_gate, w_up, w_down, layer, x, mod, norm_final, final):
    b, s, d = x.shape
    t = b * s
    nt = t // TMS
    n_expert_tiles = -(-(2 * t + nt * N_EXPERTS * (SEG - 1) + N_EXPERTS * (ET - 1)) // ET)
    counts = jnp.round(cnt[:, :, :, 0]).astype(jnp.int32).reshape(nt, N_EXPERTS)
    seg_tables, tail_tables, tile_expert, n_valid = _route_tables(counts, n_expert_tiles)
    tri = jnp.triu(jnp.ones((TMS, TMS), BF16), k=1)
    xs, dest = _dispatch(h2.reshape(t, d), route_i.reshape(nt, 8, TMS), tri,
                         seg_tables + tail_tables + (n_valid,), n_expert_tiles * ET)
    ys = _experts(xs, tile_expert, n_valid, w_gate, w_up, w_down, layer)
    out = _combine(ys, dest, route_w.reshape(t, 128), x.reshape(t, d), mod, norm_final, seg_tables,
                   s // TMS, final)
    return out.reshape(b, s, d)


def _block_diag(w):
    nb, c, _ = w.shape
    eye = jnp.eye(nb, dtype=w.dtype)
    return (eye[:, None, :, None] * w[:, :, None, :]).reshape(nb * c, nb * c)


def _pair_swa_heads(w, axis):
    half = SWA_Q_HEADS // 2
    shape = w.shape
    w = w.reshape(shape[:axis] + (2, half, HEAD_DIM) + shape[axis + 1:])
    return jnp.swapaxes(w, axis, axis + 1).reshape(shape)


def kernel(x, c, w_mod, b_mod, norm_mix, norm_ffn, w_in, w_out, out_gain, sinks, conv_w, conv_b,
           lru_wa, lru_ba, lru_wx, lru_bx, lru_lam, fox_bf, w_router_group, b_router_group,
           w_router_expert, b_router_expert, w_gate, w_up, w_down, norm_final):
    depth = w_mod.shape[0]
    b, s, d = x.shape
    assert d == D_MODEL and s % max(TM_PRE, TQ_SWA, TS_LRU, T_FOX, TM_POST) == 0
    assert TM_PRE == T_FOX
    c_rows = jnp.zeros((16, d), F32).at[:b].set(c)
    mod_all = _modulation(c_rows, w_mod, b_mod)[:, :b]

    for l in range(depth):
        mod = mod_all[l].reshape(b, 6, d)
        w_in_l = w_in[l]
        w_tok = jnp.concatenate([_pair_swa_heads(w_in_l[:, :SWA_WIDTH], 1), w_in_l[:, SWA_WIDTH:OFF_QC],
                                 w_in_l[:, OFF_KC:OFF_VC]], axis=1).astype(BF16)
        w_t = jnp.concatenate([w_in_l[:, OFF_QC:OFF_KC].T, w_in_l[:, OFF_VC:OFF_FC].T,
                               jnp.zeros((FORGET_ROWS, d), F32).at[:FOX_HEADS].set(w_in_l[:, OFF_FC:].T)],
                              axis=0).astype(BF16)
        fb = jnp.zeros((FORGET_ROWS, 1), F32).at[:FOX_HEADS, 0].set(fox_bf[l])
        gain = out_gain[l]
        gain_a = _pair_swa_heads(gain[:SWA_WIDTH], 0).reshape(1, SWA_WIDTH)
        gain_b = gain[SWA_WIDTH:SWA_WIDTH + LRU_WIDTH].reshape(1, LRU_WIDTH)
        gain_c = gain[SWA_WIDTH + LRU_WIDTH:].reshape(1, FOX_WIDTH)
        w_out_l = jnp.concatenate([_pair_swa_heads(w_out[l][:SWA_WIDTH], 0), w_out[l][SWA_WIDTH:]],
                                  axis=0).astype(BF16)
        wr = jnp.zeros((ROUTER_ROWS, d), F32)
        wr = wr.at[:N_GROUPS].set(w_router_group[l].T).at[8:8 + N_EXPERTS].set(w_router_expert[l].T)
        wr_hi = wr.astype(BF16)
        wr_lo = (wr - wr_hi.astype(F32)).astype(BF16)
        br = jnp.zeros((ROUTER_ROWS, 1), F32)
        br = br.at[:N_GROUPS, 0].set(b_router_group[l]).at[8:8 + N_EXPERTS, 0].set(b_router_expert[l])

        qa, ka, va, xb, gb, kc, qct, vaug, lf = _pre_mixer(
            x, mod, norm_mix[l].reshape(1, d), w_tok, w_t, fb)
        cum, cum_col = _cumsum(lf)
        kaug = _fox_prep(kc, cum_col)
        base = cum[:, :FOX_HEADS, ::T_FOX].reshape(-1)
        ya = _swa(qa, ka, va, sinks[l].reshape(1, SWA_Q_HEADS))
        yb = _lru(xb, gb, conv_w[l], conv_b[l].reshape(1, -1),
                  _block_diag(lru_wa[l]).astype(BF16), lru_ba[l].reshape(1, -1),
                  _block_diag(lru_wx[l]).astype(BF16), lru_bx[l].reshape(1, -1),
                  lru_lam[l].reshape(1, -1))
        yc = _fox(qct, kaug, vaug, base)
        x, h2, route_i, route_w, cnt = _post_mixer(
            ya, yb, yc, x, mod, gain_a, gain_b, gain_c, w_out_l,
            norm_ffn[l].reshape(1, d), wr_hi, wr_lo, br)
        x = _moe(h2, route_i, route_w, cnt, w_gate, w_up, w_down, l, x, mod,
                 norm_final.reshape(1, d), final=(l == depth - 1))
    return x
```

```python
import functools

import jax
import jax.numpy as jnp
from jax import lax
from jax.experimental import pallas as pl
from jax.experimental.pallas import tpu as pltpu

F32 = jnp.float32
BF16 = jnp.bfloat16

D_MODEL = 1024
HEAD_DIM = 64
PAIR = 2 * HEAD_DIM
BLOCK = 128
SWA_Q_HEADS = 8
SWA_WIDTH = SWA_Q_HEADS * HEAD_DIM
KV_WIDTH = 2 * HEAD_DIM
LRU_WIDTH = 256
LRU_BLOCKS = 8
CONV_WIDTH = 4
LRU_C = 8.0
FOX_HEADS = 4
FOX_WIDTH = FOX_HEADS * HEAD_DIM
N_GROUPS = 4
EXPERTS_PER_GROUP = 4
N_EXPERTS = 16
D_EXPERT = 256
EPS = 1e-6
NEG = -1e30
LOG2E = 1.4426950408889634

OFF_KA = SWA_WIDTH
OFF_VA = OFF_KA + KV_WIDTH
OFF_XB = OFF_VA + KV_WIDTH
OFF_GB = OFF_XB + LRU_WIDTH
OFF_QC = OFF_GB + LRU_WIDTH
OFF_KC = OFF_QC + FOX_WIDTH
OFF_VC = OFF_KC + FOX_WIDTH
OFF_FC = OFF_VC + FOX_WIDTH

ROUTER_ROWS = 32
FORGET_ROWS = 16
V_ROWS = HEAD_DIM + 16
VMEM_LIMIT_BYTES = 48 * 1024 * 1024

TM_PRE = 512
TQ_SWA = 1024
TS_LRU = 1024
T_FOX = 512
TM_POST = 512
CUMSUM_CHUNK = 256


def _params(*semantics):
    return pltpu.CompilerParams(dimension_semantics=semantics, vmem_limit_bytes=VMEM_LIMIT_BYTES)


def _split_bf16(v):
    hi = v.astype(BF16)
    lo = (v - hi.astype(F32)).astype(BF16)
    return hi, lo


def _dot(a, b):
    return jnp.dot(a, b, preferred_element_type=F32)


def _dot_nt(a, b):
    return lax.dot_general(a, b, (((1,), (1,)), ((), ())), preferred_element_type=F32)


def _rms_normalize(v):
    return v * lax.rsqrt(jnp.mean(v * v, axis=-1, keepdims=True) + EPS)


def _mod_kernel(c_ref, w_ref, b_ref, o_ref):
    c = c_ref[...]
    ca = c * jax.nn.sigmoid(c)
    w = w_ref[0]
    a_hi, a_lo = _split_bf16(ca)
    w_hi, w_lo = _split_bf16(w)
    o_ref[0] = _dot(a_hi, w_hi) + _dot(a_lo, w_hi) + _dot(a_hi, w_lo) + b_ref[0]


def _modulation(c, w_mod, b_mod):
    depth, d, d6 = w_mod.shape
    b = c.shape[0]
    n = d6 // d
    return pl.pallas_call(
        _mod_kernel,
        grid=(depth, n),
        in_specs=[
            pl.BlockSpec((b, d), lambda l, j: (0, 0)),
            pl.BlockSpec((1, d, d), lambda l, j: (l, 0, j)),
            pl.BlockSpec((1, 1, d), lambda l, j: (l, 0, j)),
        ],
        out_specs=pl.BlockSpec((1, b, d), lambda l, j: (l, 0, j)),
        out_shape=jax.ShapeDtypeStruct((depth, b, d6), F32),
        compiler_params=_params("arbitrary", "arbitrary"),
        name="modulation",
    )(c, w_mod, b_mod.reshape(depth, 1, d6))


def _log_sigmoid(z):
    return jnp.minimum(z, 0.0) - jnp.log1p(jnp.exp(-jnp.abs(z)))


def _other_half(a):
    return HEAD_DIM * (1 - a)


def _pre_kernel(x_ref, mod_ref, g_ref, w_ref, wt_ref, fb_ref,
                qa_ref, ka_ref, va_ref, xb_ref, gb_ref, kc_ref, qct_ref, vct_ref, lf_ref):
    x = x_ref[0]
    h = (_rms_normalize(x) * g_ref[...]) * (1.0 + mod_ref[0, 1:2, :]) + mod_ref[0, 0:1, :]
    hb = h.astype(BF16)

    def proj(lo, width):
        return _dot(hb, w_ref[:, lo:lo + width])

    scale = HEAD_DIM ** -0.5
    qa_ref[0] = (proj(0, SWA_WIDTH) * (scale * LOG2E)).astype(BF16)
    ka_ref[0] = proj(OFF_KA, KV_WIDTH).astype(BF16)
    va_ref[0] = proj(OFF_VA, KV_WIDTH).astype(BF16)
    xb_ref[0] = proj(OFF_XB, LRU_WIDTH).astype(BF16)
    gb_ref[0] = proj(OFF_GB, LRU_WIDTH).astype(BF16)
    kc_ref[0] = proj(OFF_QC, FOX_WIDTH).astype(BF16)

    t_all = _dot_nt(wt_ref[...], hb)
    qct_ref[0, 0] = (t_all[0:FOX_WIDTH] * (scale * LOG2E)).astype(BF16)
    ones_row = jnp.where(lax.broadcasted_iota(jnp.int32, (V_ROWS - HEAD_DIM, t_all.shape[1]), 0) == 0,
                         1.0, 0.0)
    for hd in range(FOX_HEADS):
        v_head = t_all[FOX_WIDTH + hd * HEAD_DIM:FOX_WIDTH + (hd + 1) * HEAD_DIM]
        vct_ref[0, 0, hd] = jnp.concatenate([v_head, ones_row], axis=0).astype(BF16)
    lf_ref[0] = _log_sigmoid(t_all[2 * FOX_WIDTH:] + fb_ref[...])


def _pre_mixer(x, mod, gain, w_tok, w_t, fb):
    b, s, d = x.shape
    tm = TM_PRE
    nt = s // tm

    def tok(width):
        return pl.BlockSpec((1, tm, width), lambda bi, i: (bi, i, 0))

    def tok_shape(width):
        return jax.ShapeDtypeStruct((b, s, width), BF16)

    return pl.pallas_call(
        _pre_kernel,
        grid=(b, nt),
        in_specs=[
            tok(d),
            pl.BlockSpec((1, 6, d), lambda bi, i: (bi, 0, 0)),
            pl.BlockSpec((1, d), lambda bi, i: (0, 0)),
            pl.BlockSpec(w_tok.shape, lambda bi, i: (0, 0)),
            pl.BlockSpec(w_t.shape, lambda bi, i: (0, 0)),
            pl.BlockSpec((FORGET_ROWS, 1), lambda bi, i: (0, 0)),
        ],
        out_specs=[
            tok(SWA_WIDTH), tok(KV_WIDTH), tok(KV_WIDTH), tok(LRU_WIDTH), tok(LRU_WIDTH),
            tok(FOX_WIDTH),
            pl.BlockSpec((1, 1, FOX_WIDTH, tm), lambda bi, i: (bi, i, 0, 0)),
            pl.BlockSpec((1, 1, FOX_HEADS, V_ROWS, tm), lambda bi, i: (bi, i, 0, 0, 0)),
            pl.BlockSpec((1, FORGET_ROWS, tm), lambda bi, i: (bi, 0, i)),
        ],
        out_shape=[
            tok_shape(SWA_WIDTH), tok_shape(KV_WIDTH), tok_shape(KV_WIDTH), tok_shape(LRU_WIDTH),
            tok_shape(LRU_WIDTH), tok_shape(FOX_WIDTH),
            jax.ShapeDtypeStruct((b, nt, FOX_WIDTH, tm), BF16),
            jax.ShapeDtypeStruct((b, nt, FOX_HEADS, V_ROWS, tm), BF16),
            jax.ShapeDtypeStruct((b, FORGET_ROWS, s), F32),
        ],
        compiler_params=_params("parallel", "parallel"),
        name="pre_mixer",
    )(x, mod, gain, w_tok, w_t, fb)


def _cumsum_kernel(lf_ref, o_ref, col_ref):
    n = CUMSUM_CHUNK
    rows = lf_ref.shape[1]
    s = lf_ref.shape[2]
    r = lax.broadcasted_iota(jnp.int32, (n, n), 0)
    c = lax.broadcasted_iota(jnp.int32, (n, n), 1)
    tri = jnp.where(r <= c, 1.0, 0.0).astype(BF16)
    carry = jnp.zeros((rows, 1), F32)
    for j in range(s // n):
        seg = lf_ref[0, :, j * n:(j + 1) * n]
        hi = seg.astype(BF16)
        rem = seg - hi.astype(F32)
        mid = rem.astype(BF16)
        lo = (rem - mid.astype(F32)).astype(BF16)
        cs = (_dot(hi, tri) + _dot(mid, tri)) + _dot(lo, tri) + carry
        o_ref[0, :, j * n:(j + 1) * n] = cs
        parts = _split3(cs[0:FOX_HEADS] * LOG2E)
        col_ref[0, j * n:(j + 1) * n, :] = _rows_to_columns(
            [parts[k][hd:hd + 1] for hd in range(FOX_HEADS) for k in range(3)])
        carry = cs[:, n - 1:n]


def _cumsum(lf):
    b, r, s = lf.shape
    return pl.pallas_call(
        _cumsum_kernel,
        grid=(b,),
        in_specs=[pl.BlockSpec((1, r, s), lambda bi: (bi, 0, 0))],
        out_specs=[pl.BlockSpec((1, r, s), lambda bi: (bi, 0, 0)),
                   pl.BlockSpec((1, s, 128), lambda bi: (bi, 0, 0))],
        out_shape=[jax.ShapeDtypeStruct((b, r, s), F32), jax.ShapeDtypeStruct((b, s, 128), F32)],
        compiler_params=_params("parallel"),
        name="cumsum",
    )(lf)


def _swa_kernel(sink_ref, q_ref, kc_ref, vc_ref, kp_ref, vp_ref, o_ref, kf_ref, vf_ref, bias_ref):
    i = pl.program_id(1)
    nsub = q_ref.shape[1] // BLOCK
    npair = SWA_Q_HEADS // 2

    @pl.when((pl.program_id(0) == 0) & (i == 0))
    def _():
        qpos = BLOCK + lax.broadcasted_iota(jnp.int32, (BLOCK, 2 * BLOCK), 0)
        kpos = lax.broadcasted_iota(jnp.int32, (BLOCK, 2 * BLOCK), 1)
        dist = qpos - kpos
        valid = (dist >= 0) & (dist < BLOCK)
        distf = dist.astype(F32)
        for hd in range(SWA_Q_HEADS):
            slope = 2.0 ** (-8.0 * (hd + 1) / SWA_Q_HEADS)
            bias_ref[hd] = jnp.where(valid, (-slope * LOG2E) * distf, NEG)

    kf_ref[0:BLOCK, :] = kp_ref[0]
    kf_ref[BLOCK:, :] = kc_ref[0]
    vf_ref[0:BLOCK, :] = vp_ref[0]
    vf_ref[BLOCK:, :] = vc_ref[0]

    lane = lax.broadcasted_iota(jnp.int32, (2 * BLOCK, PAIR), 1)
    lane_o = lax.broadcasted_iota(jnp.int32, (BLOCK, PAIR), 1)
    col = lax.broadcasted_iota(jnp.int32, (BLOCK, 2 * BLOCK), 1)

    for j in range(nsub):
        k2 = kf_ref[j * BLOCK:(j + 2) * BLOCK, :]
        v2 = vf_ref[j * BLOCK:(j + 2) * BLOCK, :]
        k_half = (jnp.where(lane < HEAD_DIM, k2, jnp.zeros_like(k2)),
                  jnp.where(lane >= HEAD_DIM, k2, jnp.zeros_like(k2)))
        for p in range(npair):
            qp = q_ref[0, j * BLOCK:(j + 1) * BLOCK, p * PAIR:(p + 1) * PAIR]
            outs = []
            for a in range(2):
                hd = p + a * npair
                sc = _dot_nt(qp, k_half[a]) + bias_ref[hd]
                if j == 0:
                    sc = sc + jnp.where(col < BLOCK, jnp.where(i == 0, NEG, 0.0), 0.0)
                sink = sink_ref[0, hd] * LOG2E
                m = jnp.maximum(jnp.max(sc, axis=-1, keepdims=True), sink)
                pr = jnp.exp2(sc - m)
                den = jnp.sum(pr, axis=-1, keepdims=True) + jnp.exp2(sink - m)
                outs.append(_dot(pr.astype(BF16), v2) / den)
            o_ref[0, j * BLOCK:(j + 1) * BLOCK, p * PAIR:(p + 1) * PAIR] = jnp.where(
                lane_o < HEAD_DIM, outs[0], outs[1]).astype(BF16)


def _swa(qa, ka, va, sinks):
    b, s, _ = qa.shape
    tq = TQ_SWA
    nsub = tq // BLOCK
    prev = lambda bi, i: (bi, jnp.maximum(i * nsub - 1, 0), 0)
    return pl.pallas_call(
        _swa_kernel,
        grid=(b, s // tq),
        in_specs=[
            pl.BlockSpec(memory_space=pltpu.SMEM),
            pl.BlockSpec((1, tq, SWA_WIDTH), lambda bi, i: (bi, i, 0)),
            pl.BlockSpec((1, tq, KV_WIDTH), lambda bi, i: (bi, i, 0)),
            pl.BlockSpec((1, tq, KV_WIDTH), lambda bi, i: (bi, i, 0)),
            pl.BlockSpec((1, BLOCK, KV_WIDTH), prev),
            pl.BlockSpec((1, BLOCK, KV_WIDTH), prev),
        ],
        out_specs=pl.BlockSpec((1, tq, SWA_WIDTH), lambda bi, i: (bi, i, 0)),
        out_shape=jax.ShapeDtypeStruct((b, s, SWA_WIDTH), BF16),
        scratch_shapes=[
            pltpu.VMEM((tq + BLOCK, KV_WIDTH), BF16),
            pltpu.VMEM((tq + BLOCK, KV_WIDTH), BF16),
            pltpu.VMEM((SWA_Q_HEADS, BLOCK, 2 * BLOCK), F32),
        ],
        compiler_params=_params("arbitrary", "arbitrary"),
        name="swa",
    )(sinks, qa, ka, va, ka, va)


def _gelu_tanh(v):
    return 0.5 * v * (1.0 + jnp.tanh(0.7978845608028654 * (v + 0.044715 * (v * v * v))))


def _lru_kernel(xb_ref, gb_ref, cw_ref, cb_ref, wa_ref, ba_ref, wx_ref, bx_ref, lam_ref,
                o_ref, win_ref, h_ref):
    ts = xb_ref.shape[1]

    @pl.when(pl.program_id(1) == 0)
    def _():
        win_ref[0:8, :] = jnp.zeros((8, LRU_WIDTH), F32)
        h_ref[...] = jnp.zeros_like(h_ref)

    x = xb_ref[0].astype(F32)
    win_ref[8:, :] = x
    conv = cb_ref[...] + cw_ref[CONV_WIDTH - 1:CONV_WIDTH, :] * x
    for k in range(1, CONV_WIDTH):
        conv = conv + cw_ref[CONV_WIDTH - 1 - k:CONV_WIDTH - k, :] * win_ref[8 - k:8 - k + ts, :]
    win_ref[0:8, :] = x[ts - 8:ts]

    cbf = conv.astype(BF16)
    r = jax.nn.sigmoid(_dot(cbf, wa_ref[...]) + ba_ref[...])
    gi = jax.nn.sigmoid(_dot(cbf, wx_ref[...]) + bx_ref[...])
    z = -lam_ref[...]
    softplus = jnp.maximum(z, 0.0) + jnp.log1p(jnp.exp(-jnp.abs(z)))
    log_a = (-LRU_C * softplus) * r
    a = jnp.exp(log_a)
    v = 1.0 - a * a
    u = jnp.where(v == 0.0, 0.0, v * lax.rsqrt(v)) * (gi * conv)

    a = a.reshape(ts // 8, 8, LRU_WIDTH)
    u = u.reshape(ts // 8, 8, LRU_WIDTH)
    sub = lax.broadcasted_iota(jnp.int32, a.shape, 1)
    for step in (1, 2, 4):
        keep = sub >= step
        u = jnp.where(keep, a * pltpu.roll(u, step, 1) + u, u)
        a = jnp.where(keep, a * pltpu.roll(a, step, 1), a)
    carry = h_ref[7:8, :]
    groups = []
    for g in range(ts // 8):
        hg = u[g] + a[g] * carry
        groups.append(hg)
        carry = hg[7:8, :]
    h = jnp.concatenate(groups, axis=0)
    h_ref[...] = groups[-1]
    o_ref[0] = (h * _gelu_tanh(gb_ref[0].astype(F32))).astype(BF16)


def _lru(xb, gb, conv_w, conv_b, wa, ba, wx, bx, lam):
    b, s, w = xb.shape
    ts = TS_LRU
    tok = pl.BlockSpec((1, ts, w), lambda bi, i: (bi, i, 0))
    full = lambda shape: pl.BlockSpec(shape, lambda bi, i: (0,) * len(shape))
    return pl.pallas_call(
        _lru_kernel,
        grid=(b, s // ts),
        in_specs=[tok, tok, full((CONV_WIDTH, w)), full((1, w)), full((w, w)), full((1, w)),
                  full((w, w)), full((1, w)), full((1, w))],
        out_specs=tok,
        out_shape=jax.ShapeDtypeStruct((b, s, w), BF16),
        scratch_shapes=[pltpu.VMEM((8 + ts, w), F32), pltpu.VMEM((8, w), F32)],
        compiler_params=_params("arbitrary", "arbitrary"),
        name="lru",
    )(xb, gb, conv_w, conv_b, wa, ba, wx, bx, lam)


BIAS_ROWS = 6


def _split3(v):
    hi = v.astype(BF16).astype(F32)
    rem = v - hi
    mid = rem.astype(BF16).astype(F32)
    lo = (rem - mid).astype(BF16).astype(F32)
    return hi, mid, lo


def _fox_prep_kernel(k_ref, c_ref, o_ref):
    tm = k_ref.shape[1]
    lane = lax.broadcasted_iota(jnp.int32, (tm, PAIR), 1)
    parts = c_ref[0].astype(BF16)
    src = lax.broadcasted_iota(jnp.int32, (PAIR, PAIR), 0)
    dst = lax.broadcasted_iota(jnp.int32, (PAIR, PAIR), 1)
    for p in range(FOX_WIDTH // PAIR):
        kp = k_ref[0, :, p * PAIR:(p + 1) * PAIR].astype(F32)
        for a in range(2):
            hd = 2 * p + a
            ob = _other_half(a)
            part = src - 3 * hd
            place = jnp.where(part == dst - ob, jnp.where(part >= 0, jnp.where(part < 3, 1.0, 0.0), 0.0),
                              0.0).astype(BF16)
            bias = _dot(parts, place)
            rel = lane - ob
            aug = jnp.where(rel < 3, bias, jnp.where(rel < BIAS_ROWS, 1.0, 0.0))
            in_half = (lane >= HEAD_DIM * a) & (lane < HEAD_DIM * (a + 1))
            o_ref[0, hd] = jnp.where(in_half, kp, aug).astype(BF16)


def _fox_prep(kc, cum_col):
    b, s, w = kc.shape
    tm = TM_PRE
    return pl.pallas_call(
        _fox_prep_kernel,
        grid=(b, s // tm),
        in_specs=[
            pl.BlockSpec((1, tm, w), lambda bi, i: (bi, i, 0)),
            pl.BlockSpec((1, tm, cum_col.shape[2]), lambda bi, i: (bi, i, 0)),
        ],
        out_specs=pl.BlockSpec((1, FOX_HEADS, tm, PAIR), lambda bi, i: (bi, 0, i, 0)),
        out_shape=jax.ShapeDtypeStruct((b, FOX_HEADS, s, PAIR), BF16),
        compiler_params=_params("parallel", "parallel"),
        name="fox_prep",
    )(kc, cum_col)


def _fox_kernel(base_ref, qt_ref, ka_ref, vt_ref, o_ref, acc_ref, s0_ref, s1_ref):
    bi = pl.program_id(0)
    p = pl.program_id(1)
    qi = pl.program_id(2)
    nq = pl.num_programs(2)
    t = qt_ref.shape[3]
    qt = qt_ref[0, 0].astype(F32)
    row = lax.broadcasted_iota(jnp.int32, (PAIR, t), 0)
    causal = (lax.broadcasted_iota(jnp.int32, (t, t), 0) <= lax.broadcasted_iota(jnp.int32, (t, t), 1))
    q_aug = []
    for a in range(2):
        hd = 2 * p + a
        base = jnp.full((1, t), base_ref[(bi * FOX_HEADS + hd) * nq + qi], F32) * LOG2E
        b_hi, b_mid, b_lo = _split3(base)
        rel = row - _other_half(a)
        aug = jnp.where(rel < 3, -1.0, jnp.where(rel == 3, b_hi, jnp.where(
            rel == 4, b_mid, jnp.where(rel == 5, b_lo, 0.0))))
        in_half = (row >= HEAD_DIM * a) & (row < HEAD_DIM * (a + 1))
        q_aug.append(jnp.where(in_half, qt, aug).astype(BF16))
        acc_ref[a] = jnp.zeros((V_ROWS, t), F32)

    slots = (s0_ref, s1_ref)

    def scores(ki, slot, masked):
        tile_max = []
        for a in range(2):
            k = ka_ref[0, a, pl.ds(pl.multiple_of(ki * t, t), t), :]
            st = _dot(k, q_aug[a])
            if masked:
                st = jnp.where(causal, st, NEG)
            slots[slot][a] = st
            tile_max.append(jnp.max(st, axis=0, keepdims=True))
        return tuple(tile_max)

    def consume(ki, slot, m_old, tile_max):
        m_new = []
        for a in range(2):
            m_a = jnp.maximum(m_old[a], tile_max[a])
            alpha = jnp.exp2(m_old[a] - m_a)
            pr = jnp.exp2(slots[slot][a] - m_a).astype(BF16)
            acc_ref[a] = alpha * acc_ref[a] + _dot(vt_ref[0, ki, a], pr)
            m_new.append(m_a)
        return tuple(m_new)

    def advance(j, slot, carry):
        m_old, tile_max, prev = carry
        next_max = scores(j, 1 - slot, False)
        return consume(prev, slot, m_old, tile_max), next_max, j

    def body(i, carry):
        return advance(2 * i + 1, 1, advance(2 * i, 0, carry))

    m0 = jnp.full((1, t), NEG, F32)
    carry = ((m0, m0), scores(qi, 0, True), qi)
    carry = lax.fori_loop(0, qi // 2, body, carry)

    def finish_odd(carry):
        m, tile_max, prev = advance(qi - 1, 0, carry)
        consume(prev, 1, m, tile_max)
        return 0

    def finish_even(carry):
        m, tile_max, prev = carry
        consume(prev, 0, m, tile_max)
        return 0

    lax.cond(qi % 2 == 1, finish_odd, finish_even, carry)
    outs = []
    for a in range(2):
        acc = acc_ref[a]
        outs.append(acc[0:HEAD_DIM] / acc[HEAD_DIM:HEAD_DIM + 1, :])
    o_ref[0] = jnp.concatenate(outs, axis=0).T.astype(BF16)


def _fox(qct, kaug, vaug, base):
    b, nt, w, t = qct.shape
    s = nt * t
    npair = w // PAIR
    return pl.pallas_call(
        _fox_kernel,
        grid=(b, npair, nt),
        in_specs=[
            pl.BlockSpec(memory_space=pltpu.SMEM),
            pl.BlockSpec((1, 1, PAIR, t), lambda bi, p, qi: (bi, qi, p, 0)),
            pl.BlockSpec((1, 2, s, PAIR), lambda bi, p, qi: (bi, p, 0, 0)),
            pl.BlockSpec((1, nt, 2, V_ROWS, t), lambda bi, p, qi: (bi, 0, p, 0, 0)),
        ],
        out_specs=pl.BlockSpec((1, t, PAIR), lambda bi, p, qi: (bi, qi, p)),
        out_shape=jax.ShapeDtypeStruct((b, s, w), BF16),
        scratch_shapes=[pltpu.VMEM((2, V_ROWS, t), F32), pltpu.VMEM((2, t, t), F32),
                        pltpu.VMEM((2, t, t), F32)],
        compiler_params=_params("parallel", "parallel", "arbitrary"),
        name="fox",
    )(base, qct, kaug, vaug)


def _rows_to_columns(rows):
    n = rows[0].shape[1]
    sub = lax.broadcasted_iota(jnp.int32, (128, n), 0)
    block = jnp.zeros((128, n), F32)
    for k, row in enumerate(rows):
        block = jnp.where(sub == k, row, block)
    return block.T


def _first_max(vals):
    best = vals[0]
    for v in vals[1:]:
        best = jnp.maximum(best, v)
    idx = jnp.full(best.shape, len(vals) - 1, jnp.int32)
    for j in range(len(vals) - 2, -1, -1):
        idx = jnp.where(vals[j] == best, j, idx)
    return best, idx


def _softmax_rows(vals):
    top = vals[0]
    for v in vals[1:]:
        top = jnp.maximum(top, v)
    ex = [jnp.exp(v - top) for v in vals]
    tot = ex[0]
    for e in ex[1:]:
        tot = tot + e
    return [e / tot for e in ex]


def _post_kernel(ya_ref, yb_ref, yc_ref, x_ref, mod_ref, ga_ref, gb_ref, gc_ref, wo_ref, nf_ref,
                 wr_hi_ref, wr_lo_ref, br_ref, xo_ref, h2_ref, ri_ref, rw_ref, cnt_ref):
    ya = _rms_normalize(ya_ref[0].astype(F32)) * ga_ref[...]
    yb = _rms_normalize(yb_ref[0].astype(F32)) * gb_ref[...]
    yc = _rms_normalize(yc_ref[0].astype(F32)) * gc_ref[...]
    y = jnp.concatenate([ya, yb, yc], axis=-1).astype(BF16)
    x = x_ref[0] + mod_ref[0, 2:3, :] * _dot(y, wo_ref[...])
    xo_ref[0] = x
    h = (_rms_normalize(x) * nf_ref[...]) * (1.0 + mod_ref[0, 4:5, :]) + mod_ref[0, 3:4, :]
    h_hi, h_lo = _split_bf16(h)
    h2_ref[0] = h_hi

    w_hi = wr_hi_ref[...]
    logits = (_dot_nt(w_hi, h_hi) + _dot_nt(w_hi, h_lo)) + _dot_nt(wr_lo_ref[...], h_hi) + br_ref[...]
    rows = [logits[r:r + 1, :] for r in range(8 + N_EXPERTS)]
    group_prob = _softmax_rows(rows[0:N_GROUPS])
    group_p, group_idx = _first_max(group_prob)
    in_group = []
    for j in range(EXPERTS_PER_GROUP):
        v = rows[8 + (N_GROUPS - 1) * EXPERTS_PER_GROUP + j]
        for g in range(N_GROUPS - 2, -1, -1):
            v = jnp.where(group_idx == g, rows[8 + g * EXPERTS_PER_GROUP + j], v)
        in_group.append(v)
    expert_prob = _softmax_rows(in_group)
    p1, i1 = _first_max(expert_prob)
    rest = [jnp.where(i1 == j, -1.0, expert_prob[j]) for j in range(EXPERTS_PER_GROUP)]
    p2, i2 = _first_max(rest)
    e1 = group_idx * EXPERTS_PER_GROUP + i1
    e2 = group_idx * EXPERTS_PER_GROUP + i2
    tm = e1.shape[1]
    ri_ref[0, 0] = jnp.zeros((8, tm), jnp.int32)
    ri_ref[0, 0, 0:1, :] = e1
    ri_ref[0, 0, 1:2, :] = e2
    rw_ref[0] = _rows_to_columns([group_p * p1 / (p1 + p2), group_p * p2 / (p1 + p2)])
    erow = lax.broadcasted_iota(jnp.int32, (N_EXPERTS, tm), 0)
    sel = jnp.where((erow == e1) | (erow == e2), 1.0, 0.0).astype(BF16)
    cnt_ref[0, 0] = _dot(sel, jnp.ones((tm, 128), BF16))


def _post_mixer(ya, yb, yc, x, mod, gain_a, gain_b, gain_c, w_out, norm_ffn, wr_hi, wr_lo, br):
    b, s, d = x.shape
    tm = TM_POST
    tok = lambda width: pl.BlockSpec((1, tm, width), lambda bi, i: (bi, i, 0))
    full = lambda shape: pl.BlockSpec(shape, lambda bi, i: (0,) * len(shape))
    return pl.pallas_call(
        _post_kernel,
        grid=(b, s // tm),
        in_specs=[
            tok(SWA_WIDTH), tok(LRU_WIDTH), tok(FOX_WIDTH), tok(d),
            pl.BlockSpec((1, 6, d), lambda bi, i: (bi, 0, 0)),
            full((1, SWA_WIDTH)), full((1, LRU_WIDTH)), full((1, FOX_WIDTH)),
            full((d, d)), full((1, d)),
            full((ROUTER_ROWS, d)), full((ROUTER_ROWS, d)), full((ROUTER_ROWS, 1)),
        ],
        out_specs=[
            tok(d), tok(d),
            pl.BlockSpec((1, 1, 8, tm), lambda bi, i: (bi, i, 0, 0)),
            tok(128),
            pl.BlockSpec((1, 1, N_EXPERTS, 128), lambda bi, i: (bi, i, 0, 0)),
        ],
        out_shape=[
            jax.ShapeDtypeStruct((b, s, d), F32),
            jax.ShapeDtypeStruct((b, s, d), BF16),
            jax.ShapeDtypeStruct((b, s // tm, 8, tm), jnp.int32),
            jax.ShapeDtypeStruct((b, s, 128), F32),
            jax.ShapeDtypeStruct((b, s // tm, N_EXPERTS, 128), F32),
        ],
        compiler_params=_params("parallel", "parallel"),
        name="post_mixer",
    )(ya, yb, yc, x, mod, gain_a, gain_b, gain_c, w_out, norm_ffn, wr_hi, wr_lo, br)


TMS = TM_POST
SEG = 16
SORT_ROWS = 1280
ET = 1024
SEG_PIECES = (512, 256, 128, 64, 32, 16)
TAIL_PIECES = (512, 256, 128, 64, 32, 16)
TOTAL_PIECES = (1024, 512, 256, 128, 64, 32, 16)
RARE_ROWS = 128
assert SORT_ROWS >= 2 * TMS + N_EXPERTS * (SEG - 1) and SEG_PIECES[0] == TMS and TAIL_PIECES[0] * 2 == ET


def _for_each_piece(n16_ref, lo_ref, goff_ref, tile, fn):
    for e in range(N_EXPERTS):
        n = n16_ref[tile * N_EXPERTS + e]
        lo = lo_ref[tile * N_EXPERTS + e]
        go = goff_ref[tile * N_EXPERTS + e]

        def pieces(sizes, n=n, lo=lo, go=go):
            for rows in sizes:
                done = n & (-2 * rows)

                @pl.when((n & rows) != 0)
                def _(done=done, rows=rows):
                    fn(pl.multiple_of(lo + done, SEG), pl.multiple_of(go + done, SEG), rows)

        @pl.when(n >= RARE_ROWS)
        def _():
            pieces([r for r in SEG_PIECES if r >= RARE_ROWS])

        pieces([r for r in SEG_PIECES if r < RARE_ROWS])


def _for_each_total_piece(total_ref, tile, fn):
    n = total_ref[tile]
    for rows in TOTAL_PIECES:
        @pl.when((n & rows) != 0)
        def _(rows=rows):
            fn(rows)


def _dispatch_kernel(n16_ref, lo_ref, goff_ref, total_ref, tail_row_ref, tail_n_ref, n_valid_ref,
                     h_ref, ri_ref, tri_ref, xs_ref, dest_ref, sorted_ref, zero_ref, sem):
    i = pl.program_id(0)
    last = pl.num_programs(0) - 1
    slot = i & 1
    tms = h_ref.shape[0]

    ri = ri_ref[0]
    e1 = ri[0:1, :]
    e2 = ri[1:2, :]
    erow = lax.broadcasted_iota(jnp.int32, (N_EXPERTS, tms), 0)
    m1 = erow == e1
    m2 = erow == e2
    sel = jnp.where(m1 | m2, 1.0, 0.0).astype(BF16)
    rank = _dot(sel, tri_ref[...]).astype(jnp.int32)
    start = jnp.zeros((N_EXPERTS, tms), jnp.int32)
    for e in range(N_EXPERTS):
        start = jnp.where(erow == e, lo_ref[i * N_EXPERTS + e], start)
    dest = rank + start
    r1 = jnp.sum(jnp.where(m1, dest, 0), axis=0, keepdims=True)
    r2 = jnp.sum(jnp.where(m2, dest, 0), axis=0, keepdims=True)
    dest_ref[...] = _rows_to_columns([r1.astype(F32), r2.astype(F32)])

    rows = lax.broadcasted_iota(jnp.int32, (SORT_ROWS, tms), 0)
    perm = jnp.where((rows == r1) | (rows == r2), 1.0, 0.0).astype(BF16)
    sorted_ref[slot] = _dot(perm, h_ref[...]).astype(BF16)

    def copy(src_slot, local_row, global_row, n):
        return pltpu.make_async_copy(sorted_ref.at[src_slot, pl.ds(local_row, n)],
                                     xs_ref.at[pl.ds(global_row, n)], sem)

    @pl.when(i > 0)
    def _():
        _for_each_total_piece(total_ref, i - 1, lambda n: copy(1 - slot, 0, 0, n).wait())

    _for_each_piece(n16_ref, lo_ref, goff_ref, i, lambda lr, gr, n: copy(slot, lr, gr, n).start())

    @pl.when(i == last)
    def _():
        _for_each_total_piece(total_ref, i, lambda n: copy(slot, 0, 0, n).wait())
        zero_ref[...] = jnp.zeros_like(zero_ref)

        def tail_copies(fn):
            for e in range(N_EXPERTS):
                n = tail_n_ref[e]
                row = tail_row_ref[e]
                for rows_ in TAIL_PIECES:
                    done = n & (-2 * rows_)

                    @pl.when((n & rows_) != 0)
                    def _(row=row, done=done, rows_=rows_):
                        fn(pltpu.make_async_copy(
                            zero_ref.at[pl.ds(0, rows_)],
                            xs_ref.at[pl.ds(pl.multiple_of(row + done, SEG), rows_)], sem))

        tail_copies(lambda c: c.start())
        tail_copies(lambda c: c.wait())

        def unused_tiles(fn):
            def body(j, carry):
                for part in range(ET // TAIL_PIECES[0]):
                    row = pl.multiple_of(j * ET + part * TAIL_PIECES[0], SEG)
                    fn(pltpu.make_async_copy(zero_ref, xs_ref.at[pl.ds(row, TAIL_PIECES[0])], sem))
                return carry
            lax.fori_loop(n_valid_ref[0], xs_ref.shape[0] // ET, body, 0)

        unused_tiles(lambda c: c.start())
        unused_tiles(lambda c: c.wait())


def _dispatch(h2, route_i, tri, tables, rows_max):
    t, d = h2.shape
    nt = t // TMS
    grid_spec = pltpu.PrefetchScalarGridSpec(
        num_scalar_prefetch=7,
        grid=(nt,),
        in_specs=[
            pl.BlockSpec((TMS, d), lambda i, *_: (i, 0)),
            pl.BlockSpec((1, 8, TMS), lambda i, *_: (i, 0, 0)),
            pl.BlockSpec((TMS, TMS), lambda i, *_: (0, 0)),
        ],
        out_specs=[
            pl.BlockSpec(memory_space=pl.ANY),
            pl.BlockSpec((TMS, 128), lambda i, *_: (i, 0)),
        ],
        scratch_shapes=[
            pltpu.VMEM((2, SORT_ROWS, d), BF16),
            pltpu.VMEM((TAIL_PIECES[0], d), BF16),
            pltpu.SemaphoreType.DMA,
        ],
    )
    return pl.pallas_call(
        _dispatch_kernel,
        grid_spec=grid_spec,
        out_shape=[
            jax.ShapeDtypeStruct((rows_max, d), BF16),
            jax.ShapeDtypeStruct((t, 128), F32),
        ],
        compiler_params=_params("arbitrary"),
        name="moe_dispatch",
    )(*tables, h2, route_i, tri)


def _experts_kernel(te_ref, nv_ref, xs_ref, wg_ref, wu_ref, wd_ref, ys_ref, wg_s, wu_s, wd_s):
    j = pl.program_id(0)

    @pl.when((j == 0) | (te_ref[j] != te_ref[jnp.maximum(j - 1, 0)]))
    def _():
        wg_s[...] = wg_ref[0].astype(BF16)
        wu_s[...] = wu_ref[0].astype(BF16)
        wd_s[...] = wd_ref[0].astype(BF16)

    @pl.when(j < nv_ref[0])
    def _():
        h = xs_ref[...]
        gate = _dot(h, wg_s[...])
        hidden = (gate * jax.nn.sigmoid(gate)) * _dot(h, wu_s[...])
        ys_ref[...] = _dot(hidden.astype(BF16), wd_s[...]).astype(BF16)

    @pl.when(j >= nv_ref[0])
    def _():
        ys_ref[...] = jnp.zeros_like(ys_ref)


def _experts(xs, tile_expert, n_valid, w_gate, w_up, w_down, layer):
    rows, d = xs.shape
    row_tile = lambda j, te, nv: (jnp.minimum(j, nv[0] - 1), 0)
    weight = lambda j, te, nv: (layer, te[j], 0, 0)
    grid_spec = pltpu.PrefetchScalarGridSpec(
        num_scalar_prefetch=2,
        grid=(rows // ET,),
        in_specs=[
            pl.BlockSpec((ET, d), row_tile),
            pl.BlockSpec((None, 1, d, D_EXPERT), weight),
            pl.BlockSpec((None, 1, d, D_EXPERT), weight),
            pl.BlockSpec((None, 1, D_EXPERT, d), weight),
        ],
        out_specs=pl.BlockSpec((ET, d), lambda j, te, nv: (j, 0)),
        scratch_shapes=[pltpu.VMEM((d, D_EXPERT), BF16), pltpu.VMEM((d, D_EXPERT), BF16),
                        pltpu.VMEM((D_EXPERT, d), BF16)],
    )
    return pl.pallas_call(
        _experts_kernel,
        grid_spec=grid_spec,
        out_shape=jax.ShapeDtypeStruct((rows, d), BF16),
        compiler_params=_params("arbitrary"),
        name="moe_experts",
    )(tile_expert, n_valid, xs, w_gate, w_up, w_down)


def _combine_kernel(n16_ref, lo_ref, goff_ref, total_ref, ys_ref, dest_ref, w_ref, x_ref, mod_ref,
                    nf_ref, o_ref, buf_ref, sems, *, final):
    i = pl.program_id(0)
    nt = pl.num_programs(0)
    slot = i & 1

    def copy(dst_slot, local_row, global_row, n):
        return pltpu.make_async_copy(ys_ref.at[pl.ds(global_row, n)],
                                     buf_ref.at[dst_slot, pl.ds(local_row, n)], sems.at[dst_slot])

    @pl.when(i == 0)
    def _():
        buf_ref[...] = jnp.zeros_like(buf_ref)
        _for_each_piece(n16_ref, lo_ref, goff_ref, 0, lambda lr, gr, n: copy(0, lr, gr, n).start())

    @pl.when(i + 1 < nt)
    def _():
        _for_each_piece(n16_ref, lo_ref, goff_ref, i + 1,
                        lambda lr, gr, n: copy(1 - slot, lr, gr, n).start())

    _for_each_total_piece(total_ref, i, lambda n: copy(slot, 0, 0, n).wait())

    dest = dest_ref[:, 0:2].astype(jnp.int32)
    w = w_ref[:, 0:2]
    col = lax.broadcasted_iota(jnp.int32, (dest.shape[0], SORT_ROWS), 1)
    wmat = (jnp.where(col == dest[:, 0:1], w[:, 0:1], 0.0)
            + jnp.where(col == dest[:, 1:2], w[:, 1:2], 0.0))
    y = _dot(wmat.astype(BF16), buf_ref[slot])
    x = x_ref[...] + mod_ref[0, 5:6, :] * y
    if final:
        x = _rms_normalize(x) * nf_ref[...]
    o_ref[...] = x


def _combine(ys, dest_col, w_col, x, mod, norm_final, tables, tiles_per_batch, final):
    t, d = x.shape
    nt = t // TMS
    grid_spec = pltpu.PrefetchScalarGridSpec(
        num_scalar_prefetch=4,
        grid=(nt,),
        in_specs=[
            pl.BlockSpec(memory_space=pl.ANY),
            pl.BlockSpec((TMS, 128), lambda i, *_: (i, 0)),
            pl.BlockSpec((TMS, 128), lambda i, *_: (i, 0)),
            pl.BlockSpec((TMS, d), lambda i, *_: (i, 0)),
            pl.BlockSpec((1, 6, d), lambda i, *_: (i // tiles_per_batch, 0, 0)),
            pl.BlockSpec((1, d), lambda i, *_: (0, 0)),
        ],
        out_specs=pl.BlockSpec((TMS, d), lambda i, *_: (i, 0)),
        scratch_shapes=[pltpu.VMEM((2, SORT_ROWS, d), BF16), pltpu.SemaphoreType.DMA((2,))],
    )
    return pl.pallas_call(
        functools.partial(_combine_kernel, final=final),
        grid_spec=grid_spec,
        out_shape=jax.ShapeDtypeStruct((t, d), F32),
        compiler_params=_params("arbitrary"),
        name="moe_combine_final" if final else "moe_combine",
    )(*tables, ys, dest_col, w_col, x, mod, norm_final)


def _route_tables(counts, n_expert_tiles):
    n16 = (counts + (SEG - 1)) // SEG * SEG
    lo = jnp.cumsum(n16, axis=1) - n16
    total = jnp.sum(n16, axis=0)
    region = (total + (ET - 1)) // ET * ET
    ends = jnp.cumsum(region)
    base = ends - region
    goff = base[None, :] + jnp.cumsum(n16, axis=0) - n16
    tile_row = jnp.arange(n_expert_tiles, dtype=jnp.int32) * ET
    tile_expert = jnp.minimum(jnp.sum(tile_row[:, None] >= ends[None, :], axis=1), N_EXPERTS - 1)
    n_valid = (ends[-1] // ET).reshape(1)
    i32 = lambda a: a.astype(jnp.int32).reshape(-1)
    return (i32(n16), i32(lo), i32(goff), i32(jnp.sum(n16, axis=1))), \
        (i32(base + total), i32(region - total)), i32(tile_expert), i32(n_valid)


def _moe(h2, route_i, route_w, cnt, w_gate, w_up, w_down, layer, x, mod, norm_final, final):
    b, s, d = x.shape
    t = b * s
    nt = t // TMS
    n_expert_tiles = -(-(2 * t + nt * N_EXPERTS * (SEG - 1) + N_EXPERTS * (ET - 1)) // ET)
    counts = jnp.round(cnt[:, :, :, 0]).astype(jnp.int32).reshape(nt, N_EXPERTS)
    seg_tables, tail_tables, tile_expert, n_valid = _route_tables(counts, n_expert_tiles)
    tri = jnp.triu(jnp.ones((TMS, TMS), BF16), k=1)
    xs, dest = _dispatch(h2.reshape(t, d), route_i.reshape(nt, 8, TMS), tri,
                         seg_tables + tail_tables + (n_valid,), n_expert_tiles * ET)
    ys = _experts(xs, tile_expert, n_valid, w_gate, w_up, w_down, layer)
    out = _combine(ys, dest, route_w.reshape(t, 128), x.reshape(t, d), mod, norm_final, seg_tables,
                   s // TMS, final)
    return out.reshape(b, s, d)


def _block_diag(w):
    nb, c, _ = w.shape
    eye = jnp.eye(nb, dtype=w.dtype)
    return (eye[:, None, :, None] * w[:, :, None, :]).reshape(nb * c, nb * c)


def _pair_swa_heads(w, axis):
    half = SWA_Q_HEADS // 2
    shape = w.shape
    w = w.reshape(shape[:axis] + (2, half, HEAD_DIM) + shape[axis + 1:])
    return jnp.swapaxes(w, axis, axis + 1).reshape(shape)


def kernel(x, c, w_mod, b_mod, norm_mix, norm_ffn, w_in, w_out, out_gain, sinks, conv_w, conv_b,
           lru_wa, lru_ba, lru_wx, lru_bx, lru_lam, fox_bf, w_router_group, b_router_group,
           w_router_expert, b_router_expert, w_gate, w_up, w_down, norm_final):
    depth = w_mod.shape[0]
    b, s, d = x.shape
    assert d == D_MODEL and s % max(TM_PRE, TQ_SWA, TS_LRU, T_FOX, TM_POST) == 0
    assert TM_PRE == T_FOX
    c_rows = jnp.zeros((16, d), F32).at[:b].set(c)
    mod_all = _modulation(c_rows, w_mod, b_mod)[:, :b]

    for l in range(depth):
        mod = mod_all[l].reshape(b, 6, d)
        w_in_l = w_in[l]
        w_tok = jnp.concatenate([_pair_swa_heads(w_in_l[:, :SWA_WIDTH], 1), w_in_l[:, SWA_WIDTH:OFF_QC],
                                 w_in_l[:, OFF_KC:OFF_VC]], axis=1).astype(BF16)
        w_t = jnp.concatenate([w_in_l[:, OFF_QC:OFF_KC].T, w_in_l[:, OFF_VC:OFF_FC].T,
                               jnp.zeros((FORGET_ROWS, d), F32).at[:FOX_HEADS].set(w_in_l[:, OFF_FC:].T)],
                              axis=0).astype(BF16)
        fb = jnp.zeros((FORGET_ROWS, 1), F32).at[:FOX_HEADS, 0].set(fox_bf[l])
        gain = out_gain[l]
        gain_a = _pair_swa_heads(gain[:SWA_WIDTH], 0).reshape(1, SWA_WIDTH)
        gain_b = gain[SWA_WIDTH:SWA_WIDTH + LRU_WIDTH].reshape(1, LRU_WIDTH)
        gain_c = gain[SWA_WIDTH + LRU_WIDTH:].reshape(1, FOX_WIDTH)
        w_out_l = jnp.concatenate([_pair_swa_heads(w_out[l][:SWA_WIDTH], 0), w_out[l][SWA_WIDTH:]],
                                  axis=0).astype(BF16)
        wr = jnp.zeros((ROUTER_ROWS, d), F32)
        wr = wr.at[:N_GROUPS].set(w_router_group[l].T).at[8:8 + N_EXPERTS].set(w_router_expert[l].T)
        wr_hi = wr.astype(BF16)
        wr_lo = (wr - wr_hi.astype(F32)).astype(BF16)
        br = jnp.zeros((ROUTER_ROWS, 1), F32)
        br = br.at[:N_GROUPS, 0].set(b_router_group[l]).at[8:8 + N_EXPERTS, 0].set(b_router_expert[l])

        qa, ka, va, xb, gb, kc, qct, vaug, lf = _pre_mixer(
            x, mod, norm_mix[l].reshape(1, d), w_tok, w_t, fb)
        cum, cum_col = _cumsum(lf)
        kaug = _fox_prep(kc, cum_col)
        base = cum[:, :FOX_HEADS, ::T_FOX].reshape(-1)
        ya = _swa(qa, ka, va, sinks[l].reshape(1, SWA_Q_HEADS))
        yb = _lru(xb, gb, conv_w[l], conv_b[l].reshape(1, -1),
                  _block_diag(lru_wa[l]).astype(BF16), lru_ba[l].reshape(1, -1),
                  _block_diag(lru_wx[l]).astype(BF16), lru_bx[l].reshape(1, -1),
                  lru_lam[l].reshape(1, -1))
        yc = _fox(qct, kaug, vaug, base)
        x, h2, route_i, route_w, cnt = _post_mixer(
            ya, yb, yc, x, mod, gain_a, gain_b, gain_c, w_out_l,
            norm_ffn[l].reshape(1, d), wr_hi, wr_lo, br)
        x = _moe(h2, route_i, route_w, cnt, w_gate, w_up, w_down, l, x, mod,
                 norm_final.reshape(1, d), final=(l == depth - 1))
    return x
```

```python
import functools

import jax
import jax.numpy as jnp
from jax import lax
from jax.experimental import pallas as pl
from jax.experimental.pallas import tpu as pltpu

F32 = jnp.float32
BF16 = jnp.bfloat16

D_MODEL = 1024
HEAD_DIM = 64
PAIR = 2 * HEAD_DIM
BLOCK = 128
SWA_Q_HEADS = 8
SWA_WIDTH = SWA_Q_HEADS * HEAD_DIM
KV_WIDTH = 2 * HEAD_DIM
LRU_WIDTH = 256
LRU_BLOCKS = 8
CONV_WIDTH = 4
LRU_C = 8.0
FOX_HEADS = 4
FOX_WIDTH = FOX_HEADS * HEAD_DIM
N_GROUPS = 4
EXPERTS_PER_GROUP = 4
N_EXPERTS = 16
D_EXPERT = 256
EPS = 1e-6
NEG = -1e30
LOG2E = 1.4426950408889634

OFF_KA = SWA_WIDTH
OFF_VA = OFF_KA + KV_WIDTH
OFF_XB = OFF_VA + KV_WIDTH
OFF_GB = OFF_XB + LRU_WIDTH
OFF_QC = OFF_GB + LRU_WIDTH
OFF_KC = OFF_QC + FOX_WIDTH
OFF_VC = OFF_KC + FOX_WIDTH
OFF_FC = OFF_VC + FOX_WIDTH

ROUTER_ROWS = 32
FORGET_ROWS = 16
V_ROWS = HEAD_DIM + 16
VMEM_LIMIT_BYTES = 48 * 1024 * 1024

TM_PRE = 512
TM_PREP = 2048
TQ_SWA = 1024
TS_LRU = 1024
T_FOX = 512
TM_POST = 512
CUMSUM_CHUNK = 256


def _params(*semantics):
    return pltpu.CompilerParams(dimension_semantics=semantics, vmem_limit_bytes=VMEM_LIMIT_BYTES)


def _split_bf16(v):
    hi = v.astype(BF16)
    lo = (v - hi.astype(F32)).astype(BF16)
    return hi, lo


def _dot(a, b):
    return jnp.dot(a, b, preferred_element_type=F32)


def _dot_nt(a, b):
    return lax.dot_general(a, b, (((1,), (1,)), ((), ())), preferred_element_type=F32)


def _rms_normalize(v):
    return v * lax.rsqrt(jnp.mean(v * v, axis=-1, keepdims=True) + EPS)


def _mod_kernel(c_ref, w_ref, b_ref, o_ref):
    c = c_ref[...]
    ca = c * jax.nn.sigmoid(c)
    w = w_ref[0]
    a_hi, a_lo = _split_bf16(ca)
    w_hi, w_lo = _split_bf16(w)
    o_ref[0] = _dot(a_hi, w_hi) + _dot(a_lo, w_hi) + _dot(a_hi, w_lo) + b_ref[0]


def _modulation(c, w_mod, b_mod):
    depth, d, d6 = w_mod.shape
    b = c.shape[0]
    n = d6 // d
    return pl.pallas_call(
        _mod_kernel,
        grid=(depth, n),
        in_specs=[
            pl.BlockSpec((b, d), lambda l, j: (0, 0)),
            pl.BlockSpec((1, d, d), lambda l, j: (l, 0, j)),
            pl.BlockSpec((1, 1, d), lambda l, j: (l, 0, j)),
        ],
        out_specs=pl.BlockSpec((1, b, d), lambda l, j: (l, 0, j)),
        out_shape=jax.ShapeDtypeStruct((depth, b, d6), F32),
        compiler_params=_params("arbitrary", "arbitrary"),
        name="modulation",
    )(c, w_mod, b_mod.reshape(depth, 1, d6))


def _log_sigmoid(z):
    return jnp.minimum(z, 0.0) - jnp.log1p(jnp.exp(-jnp.abs(z)))


def _other_half(a):
    return HEAD_DIM * (1 - a)


def _pre_kernel(x_ref, mod_ref, g_ref, w_ref, wt_ref, fb_ref,
                qa_ref, ka_ref, va_ref, xb_ref, gb_ref, kc_ref, qct_ref, vct_ref, lf_ref):
    x = x_ref[0]
    h = (_rms_normalize(x) * g_ref[...]) * (1.0 + mod_ref[0, 1:2, :]) + mod_ref[0, 0:1, :]
    hb = h.astype(BF16)

    def proj(lo, width):
        return _dot(hb, w_ref[:, lo:lo + width])

    scale = HEAD_DIM ** -0.5
    qa_ref[0] = (proj(0, SWA_WIDTH) * (scale * LOG2E)).astype(BF16)
    ka_ref[0] = proj(OFF_KA, KV_WIDTH).astype(BF16)
    va_ref[0] = proj(OFF_VA, KV_WIDTH).astype(BF16)
    xb_ref[0] = proj(OFF_XB, LRU_WIDTH).astype(BF16)
    gb_ref[0] = proj(OFF_GB, LRU_WIDTH).astype(BF16)
    kc_ref[0] = proj(OFF_QC, FOX_WIDTH).astype(BF16)

    t_all = _dot_nt(wt_ref[...], hb)
    qct_ref[0, 0] = (t_all[0:FOX_WIDTH] * (scale * LOG2E)).astype(BF16)
    ones_row = jnp.where(lax.broadcasted_iota(jnp.int32, (V_ROWS - HEAD_DIM, t_all.shape[1]), 0) == 0,
                         1.0, 0.0)
    for hd in range(FOX_HEADS):
        v_head = t_all[FOX_WIDTH + hd * HEAD_DIM:FOX_WIDTH + (hd + 1) * HEAD_DIM]
        vct_ref[0, 0, hd] = jnp.concatenate([v_head, ones_row], axis=0).astype(BF16)
    lf_ref[0] = _log_sigmoid(t_all[2 * FOX_WIDTH:] + fb_ref[...])


def _pre_mixer(x, mod, gain, w_tok, w_t, fb):
    b, s, d = x.shape
    tm = TM_PRE
    nt = s // tm

    def tok(width):
        return pl.BlockSpec((1, tm, width), lambda bi, i: (bi, i, 0))

    def tok_shape(width):
        return jax.ShapeDtypeStruct((b, s, width), BF16)

    return pl.pallas_call(
        _pre_kernel,
        grid=(b, nt),
        in_specs=[
            tok(d),
            pl.BlockSpec((1, 6, d), lambda bi, i: (bi, 0, 0)),
            pl.BlockSpec((1, d), lambda bi, i: (0, 0)),
            pl.BlockSpec(w_tok.shape, lambda bi, i: (0, 0)),
            pl.BlockSpec(w_t.shape, lambda bi, i: (0, 0)),
            pl.BlockSpec((FORGET_ROWS, 1), lambda bi, i: (0, 0)),
        ],
        out_specs=[
            tok(SWA_WIDTH), tok(KV_WIDTH), tok(KV_WIDTH), tok(LRU_WIDTH), tok(LRU_WIDTH),
            tok(FOX_WIDTH),
            pl.BlockSpec((1, 1, FOX_WIDTH, tm), lambda bi, i: (bi, i, 0, 0)),
            pl.BlockSpec((1, 1, FOX_HEADS, V_ROWS, tm), lambda bi, i: (bi, i, 0, 0, 0)),
            pl.BlockSpec((1, FORGET_ROWS, tm), lambda bi, i: (bi, 0, i)),
        ],
        out_shape=[
            tok_shape(SWA_WIDTH), tok_shape(KV_WIDTH), tok_shape(KV_WIDTH), tok_shape(LRU_WIDTH),
            tok_shape(LRU_WIDTH), tok_shape(FOX_WIDTH),
            jax.ShapeDtypeStruct((b, nt, FOX_WIDTH, tm), BF16),
            jax.ShapeDtypeStruct((b, nt, FOX_HEADS, V_ROWS, tm), BF16),
            jax.ShapeDtypeStruct((b, FORGET_ROWS, s), F32),
        ],
        compiler_params=_params("parallel", "parallel"),
        name="pre_mixer",
    )(x, mod, gain, w_tok, w_t, fb)


def _cumsum_kernel(lf_ref, o_ref, col_ref):
    n = CUMSUM_CHUNK
    rows = lf_ref.shape[1]
    s = lf_ref.shape[2]
    r = lax.broadcasted_iota(jnp.int32, (n, n), 0)
    c = lax.broadcasted_iota(jnp.int32, (n, n), 1)
    tri = jnp.where(r <= c, 1.0, 0.0).astype(BF16)
    carry = jnp.zeros((rows, 1), F32)
    for j in range(s // n):
        seg = lf_ref[0, :, j * n:(j + 1) * n]
        hi = seg.astype(BF16)
        rem = seg - hi.astype(F32)
        mid = rem.astype(BF16)
        lo = (rem - mid.astype(F32)).astype(BF16)
        cs = (_dot(hi, tri) + _dot(mid, tri)) + _dot(lo, tri) + carry
        o_ref[0, :, j * n:(j + 1) * n] = cs
        parts = _split3(cs[0:FOX_HEADS] * LOG2E)
        col_ref[0, j * n:(j + 1) * n, :] = _rows_to_columns(
            [parts[k][hd:hd + 1] for hd in range(FOX_HEADS) for k in range(3)])
        carry = cs[:, n - 1:n]


def _cumsum(lf):
    b, r, s = lf.shape
    return pl.pallas_call(
        _cumsum_kernel,
        grid=(b,),
        in_specs=[pl.BlockSpec((1, r, s), lambda bi: (bi, 0, 0))],
        out_specs=[pl.BlockSpec((1, r, s), lambda bi: (bi, 0, 0)),
                   pl.BlockSpec((1, s, 128), lambda bi: (bi, 0, 0))],
        out_shape=[jax.ShapeDtypeStruct((b, r, s), F32), jax.ShapeDtypeStruct((b, s, 128), F32)],
        compiler_params=_params("parallel"),
        name="cumsum",
    )(lf)


def _swa_kernel(sink_ref, q_ref, kc_ref, vc_ref, kp_ref, vp_ref, o_ref, kf_ref, vf_ref, bias_ref):
    i = pl.program_id(1)
    nsub = q_ref.shape[1] // BLOCK
    npair = SWA_Q_HEADS // 2

    @pl.when((pl.program_id(0) == 0) & (i == 0))
    def _():
        qpos = BLOCK + lax.broadcasted_iota(jnp.int32, (BLOCK, 2 * BLOCK), 0)
        kpos = lax.broadcasted_iota(jnp.int32, (BLOCK, 2 * BLOCK), 1)
        dist = qpos - kpos
        valid = (dist >= 0) & (dist < BLOCK)
        distf = dist.astype(F32)
        for hd in range(SWA_Q_HEADS):
            slope = 2.0 ** (-8.0 * (hd + 1) / SWA_Q_HEADS)
            bias_ref[hd] = jnp.where(valid, (-slope * LOG2E) * distf, NEG)

    kf_ref[0:BLOCK, :] = kp_ref[0]
    kf_ref[BLOCK:, :] = kc_ref[0]
    vf_ref[0:BLOCK, :] = vp_ref[0]
    vf_ref[BLOCK:, :] = vc_ref[0]

    lane = lax.broadcasted_iota(jnp.int32, (2 * BLOCK, PAIR), 1)
    lane_o = lax.broadcasted_iota(jnp.int32, (BLOCK, PAIR), 1)
    col = lax.broadcasted_iota(jnp.int32, (BLOCK, 2 * BLOCK), 1)

    for j in range(nsub):
        k2 = kf_ref[j * BLOCK:(j + 2) * BLOCK, :]
        v2 = vf_ref[j * BLOCK:(j + 2) * BLOCK, :]
        k_half = (jnp.where(lane < HEAD_DIM, k2, jnp.zeros_like(k2)),
                  jnp.where(lane >= HEAD_DIM, k2, jnp.zeros_like(k2)))
        for p in range(npair):
            qp = q_ref[0, j * BLOCK:(j + 1) * BLOCK, p * PAIR:(p + 1) * PAIR]
            outs = []
            for a in range(2):
                hd = p + a * npair
                sc = _dot_nt(qp, k_half[a]) + bias_ref[hd]
                if j == 0:
                    sc = sc + jnp.where(col < BLOCK, jnp.where(i == 0, NEG, 0.0), 0.0)
                sink = sink_ref[0, hd] * LOG2E
                m = jnp.maximum(jnp.max(sc, axis=-1, keepdims=True), sink)
                pr = jnp.exp2(sc - m)
                den = jnp.sum(pr, axis=-1, keepdims=True) + jnp.exp2(sink - m)
                outs.append(_dot(pr.astype(BF16), v2) / den)
            o_ref[0, j * BLOCK:(j + 1) * BLOCK, p * PAIR:(p + 1) * PAIR] = jnp.where(
                lane_o < HEAD_DIM, outs[0], outs[1]).astype(BF16)


def _swa(qa, ka, va, sinks):
    b, s, _ = qa.shape
    tq = TQ_SWA
    nsub = tq // BLOCK
    prev = lambda bi, i: (bi, jnp.maximum(i * nsub - 1, 0), 0)
    return pl.pallas_call(
        _swa_kernel,
        grid=(b, s // tq),
        in_specs=[
            pl.BlockSpec(memory_space=pltpu.SMEM),
            pl.BlockSpec((1, tq, SWA_WIDTH), lambda bi, i: (bi, i, 0)),
            pl.BlockSpec((1, tq, KV_WIDTH), lambda bi, i: (bi, i, 0)),
            pl.BlockSpec((1, tq, KV_WIDTH), lambda bi, i: (bi, i, 0)),
            pl.BlockSpec((1, BLOCK, KV_WIDTH), prev),
            pl.BlockSpec((1, BLOCK, KV_WIDTH), prev),
        ],
        out_specs=pl.BlockSpec((1, tq, SWA_WIDTH), lambda bi, i: (bi, i, 0)),
        out_shape=jax.ShapeDtypeStruct((b, s, SWA_WIDTH), BF16),
        scratch_shapes=[
            pltpu.VMEM((tq + BLOCK, KV_WIDTH), BF16),
            pltpu.VMEM((tq + BLOCK, KV_WIDTH), BF16),
            pltpu.VMEM((SWA_Q_HEADS, BLOCK, 2 * BLOCK), F32),
        ],
        compiler_params=_params("arbitrary", "arbitrary"),
        name="swa",
    )(sinks, qa, ka, va, ka, va)


def _gelu_tanh(v):
    return 0.5 * v * (1.0 + jnp.tanh(0.7978845608028654 * (v + 0.044715 * (v * v * v))))


def _lru_kernel(xb_ref, gb_ref, cw_ref, cb_ref, wa_ref, ba_ref, wx_ref, bx_ref, lam_ref,
                o_ref, win_ref, h_ref):
    ts = xb_ref.shape[1]

    @pl.when(pl.program_id(1) == 0)
    def _():
        win_ref[0:8, :] = jnp.zeros((8, LRU_WIDTH), F32)
        h_ref[...] = jnp.zeros_like(h_ref)

    x = xb_ref[0].astype(F32)
    win_ref[8:, :] = x
    conv = cb_ref[...] + cw_ref[CONV_WIDTH - 1:CONV_WIDTH, :] * x
    for k in range(1, CONV_WIDTH):
        conv = conv + cw_ref[CONV_WIDTH - 1 - k:CONV_WIDTH - k, :] * win_ref[8 - k:8 - k + ts, :]
    win_ref[0:8, :] = x[ts - 8:ts]

    cbf = conv.astype(BF16)
    r = jax.nn.sigmoid(_dot(cbf, wa_ref[...]) + ba_ref[...])
    gi = jax.nn.sigmoid(_dot(cbf, wx_ref[...]) + bx_ref[...])
    z = -lam_ref[...]
    softplus = jnp.maximum(z, 0.0) + jnp.log1p(jnp.exp(-jnp.abs(z)))
    log_a = (-LRU_C * softplus) * r
    a = jnp.exp(log_a)
    v = 1.0 - a * a
    u = jnp.where(v == 0.0, 0.0, v * lax.rsqrt(v)) * (gi * conv)

    a = a.reshape(ts // 8, 8, LRU_WIDTH)
    u = u.reshape(ts // 8, 8, LRU_WIDTH)
    sub = lax.broadcasted_iota(jnp.int32, a.shape, 1)
    for step in (1, 2, 4):
        keep = sub >= step
        u = jnp.where(keep, a * pltpu.roll(u, step, 1) + u, u)
        a = jnp.where(keep, a * pltpu.roll(a, step, 1), a)
    carry = h_ref[7:8, :]
    groups = []
    for g in range(ts // 8):
        hg = u[g] + a[g] * carry
        groups.append(hg)
        carry = hg[7:8, :]
    h = jnp.concatenate(groups, axis=0)
    h_ref[...] = groups[-1]
    o_ref[0] = (h * _gelu_tanh(gb_ref[0].astype(F32))).astype(BF16)


def _lru(xb, gb, conv_w, conv_b, wa, ba, wx, bx, lam):
    b, s, w = xb.shape
    ts = TS_LRU
    tok = pl.BlockSpec((1, ts, w), lambda bi, i: (bi, i, 0))
    full = lambda shape: pl.BlockSpec(shape, lambda bi, i: (0,) * len(shape))
    return pl.pallas_call(
        _lru_kernel,
        grid=(b, s // ts),
        in_specs=[tok, tok, full((CONV_WIDTH, w)), full((1, w)), full((w, w)), full((1, w)),
                  full((w, w)), full((1, w)), full((1, w))],
        out_specs=tok,
        out_shape=jax.ShapeDtypeStruct((b, s, w), BF16),
        scratch_shapes=[pltpu.VMEM((8 + ts, w), F32), pltpu.VMEM((8, w), F32)],
        compiler_params=_params("arbitrary", "arbitrary"),
        name="lru",
    )(xb, gb, conv_w, conv_b, wa, ba, wx, bx, lam)


BIAS_ROWS = 6


def _split3(v):
    hi = v.astype(BF16).astype(F32)
    rem = v - hi
    mid = rem.astype(BF16).astype(F32)
    lo = (rem - mid).astype(BF16).astype(F32)
    return hi, mid, lo


def _fox_prep_kernel(k_ref, c_ref, o_ref):
    tm = k_ref.shape[1]
    lane = lax.broadcasted_iota(jnp.int32, (tm, PAIR), 1)
    parts = c_ref[0].astype(BF16)
    src = lax.broadcasted_iota(jnp.int32, (PAIR, PAIR), 0)
    dst = lax.broadcasted_iota(jnp.int32, (PAIR, PAIR), 1)
    for p in range(FOX_WIDTH // PAIR):
        kp = k_ref[0, :, p * PAIR:(p + 1) * PAIR].astype(F32)
        for a in range(2):
            hd = 2 * p + a
            ob = _other_half(a)
            part = src - 3 * hd
            place = jnp.where(part == dst - ob, jnp.where(part >= 0, jnp.where(part < 3, 1.0, 0.0), 0.0),
                              0.0).astype(BF16)
            bias = _dot(parts, place)
            rel = lane - ob
            aug = jnp.where(rel < 3, bias, jnp.where(rel < BIAS_ROWS, 1.0, 0.0))
            in_half = (lane >= HEAD_DIM * a) & (lane < HEAD_DIM * (a + 1))
            o_ref[0, hd] = jnp.where(in_half, kp, aug).astype(BF16)


def _fox_prep(kc, cum_col):
    b, s, w = kc.shape
    tm = TM_PREP
    return pl.pallas_call(
        _fox_prep_kernel,
        grid=(b, s // tm),
        in_specs=[
            pl.BlockSpec((1, tm, w), lambda bi, i: (bi, i, 0)),
            pl.BlockSpec((1, tm, cum_col.shape[2]), lambda bi, i: (bi, i, 0)),
        ],
        out_specs=pl.BlockSpec((1, FOX_HEADS, tm, PAIR), lambda bi, i: (bi, 0, i, 0)),
        out_shape=jax.ShapeDtypeStruct((b, FOX_HEADS, s, PAIR), BF16),
        compiler_params=_params("parallel", "parallel"),
        name="fox_prep",
    )(kc, cum_col)


def _fox_kernel(base_ref, qt_ref, ka_ref, vt_ref, o_ref, acc_ref, s0_ref, s1_ref):
    bi = pl.program_id(0)
    p = pl.program_id(1)
    qi = pl.program_id(2)
    nq = pl.num_programs(2)
    t = qt_ref.shape[3]
    qt = qt_ref[0, 0].astype(F32)
    row = lax.broadcasted_iota(jnp.int32, (PAIR, t), 0)
    causal = (lax.broadcasted_iota(jnp.int32, (t, t), 0) <= lax.broadcasted_iota(jnp.int32, (t, t), 1))
    q_aug = []
    for a in range(2):
        hd = 2 * p + a
        base = jnp.full((1, t), base_ref[(bi * FOX_HEADS + hd) * nq + qi], F32) * LOG2E
        b_hi, b_mid, b_lo = _split3(base)
        rel = row - _other_half(a)
        aug = jnp.where(rel < 3, -1.0, jnp.where(rel == 3, b_hi, jnp.where(
            rel == 4, b_mid, jnp.where(rel == 5, b_lo, 0.0))))
        in_half = (row >= HEAD_DIM * a) & (row < HEAD_DIM * (a + 1))
        q_aug.append(jnp.where(in_half, qt, aug).astype(BF16))
        acc_ref[a] = jnp.zeros((V_ROWS, t), F32)

    slots = (s0_ref, s1_ref)

    def scores(ki, slot, masked):
        tile_max = []
        for a in range(2):
            k = ka_ref[0, a, pl.ds(pl.multiple_of(ki * t, t), t), :]
            st = _dot(k, q_aug[a])
            if masked:
                st = jnp.where(causal, st, NEG)
            slots[slot][a] = st
            tile_max.append(jnp.max(st, axis=0, keepdims=True))
        return tuple(tile_max)

    def consume(ki, slot, m_old, tile_max):
        m_new = []
        for a in range(2):
            m_a = jnp.maximum(m_old[a], tile_max[a])
            alpha = jnp.exp2(m_old[a] - m_a)
            pr = jnp.exp2(slots[slot][a] - m_a).astype(BF16)
            acc_ref[a] = alpha * acc_ref[a] + _dot(vt_ref[0, ki, a], pr)
            m_new.append(m_a)
        return tuple(m_new)

    def advance(j, slot, carry):
        m_old, tile_max, prev = carry
        next_max = scores(j, 1 - slot, False)
        return consume(prev, slot, m_old, tile_max), next_max, j

    def body(i, carry):
        return advance(2 * i + 1, 1, advance(2 * i, 0, carry))

    m0 = jnp.full((1, t), NEG, F32)
    carry = ((m0, m0), scores(qi, 0, True), qi)
    carry = lax.fori_loop(0, qi // 2, body, carry)

    def finish_odd(carry):
        m, tile_max, prev = advance(qi - 1, 0, carry)
        consume(prev, 1, m, tile_max)
        return 0

    def finish_even(carry):
        m, tile_max, prev = carry
        consume(prev, 0, m, tile_max)
        return 0

    lax.cond(qi % 2 == 1, finish_odd, finish_even, carry)
    outs = []
    for a in range(2):
        acc = acc_ref[a]
        outs.append(acc[0:HEAD_DIM] / acc[HEAD_DIM:HEAD_DIM + 1, :])
    o_ref[0] = jnp.concatenate(outs, axis=0).T.astype(BF16)


def _fox(qct, kaug, vaug, base):
    b, nt, w, t = qct.shape
    s = nt * t
    npair = w // PAIR
    return pl.pallas_call(
        _fox_kernel,
        grid=(b, npair, nt),
        in_specs=[
            pl.BlockSpec(memory_space=pltpu.SMEM),
            pl.BlockSpec((1, 1, PAIR, t), lambda bi, p, qi: (bi, qi, p, 0)),
            pl.BlockSpec((1, 2, s, PAIR), lambda bi, p, qi: (bi, p, 0, 0)),
            pl.BlockSpec((1, nt, 2, V_ROWS, t), lambda bi, p, qi: (bi, 0, p, 0, 0)),
        ],
        out_specs=pl.BlockSpec((1, t, PAIR), lambda bi, p, qi: (bi, qi, p)),
        out_shape=jax.ShapeDtypeStruct((b, s, w), BF16),
        scratch_shapes=[pltpu.VMEM((2, V_ROWS, t), F32), pltpu.VMEM((2, t, t), F32),
                        pltpu.VMEM((2, t, t), F32)],
        compiler_params=_params("parallel", "parallel", "arbitrary"),
        name="fox",
    )(base, qct, kaug, vaug)


def _rows_to_columns(rows):
    n = rows[0].shape[1]
    sub = lax.broadcasted_iota(jnp.int32, (128, n), 0)
    block = jnp.zeros((128, n), F32)
    for k, row in enumerate(rows):
        block = jnp.where(sub == k, row, block)
    return block.T


def _first_max(vals):
    best = vals[0]
    for v in vals[1:]:
        best = jnp.maximum(best, v)
    idx = jnp.full(best.shape, len(vals) - 1, jnp.int32)
    for j in range(len(vals) - 2, -1, -1):
        idx = jnp.where(vals[j] == best, j, idx)
    return best, idx


def _softmax_rows(vals):
    top = vals[0]
    for v in vals[1:]:
        top = jnp.maximum(top, v)
    ex = [jnp.exp(v - top) for v in vals]
    tot = ex[0]
    for e in ex[1:]:
        tot = tot + e
    return [e / tot for e in ex]


def _post_kernel(ya_ref, yb_ref, yc_ref, x_ref, mod_ref, ga_ref, gb_ref, gc_ref, wo_ref, nf_ref,
                 wr_hi_ref, wr_lo_ref, br_ref, xo_ref, h2_ref, ri_ref, rw_ref, cnt_ref):
    ya = _rms_normalize(ya_ref[0].astype(F32)) * ga_ref[...]
    yb = _rms_normalize(yb_ref[0].astype(F32)) * gb_ref[...]
    yc = _rms_normalize(yc_ref[0].astype(F32)) * gc_ref[...]
    y = jnp.concatenate([ya, yb, yc], axis=-1).astype(BF16)
    x = x_ref[0] + mod_ref[0, 2:3, :] * _dot(y, wo_ref[...])
    xo_ref[0] = x
    h = (_rms_normalize(x) * nf_ref[...]) * (1.0 + mod_ref[0, 4:5, :]) + mod_ref[0, 3:4, :]
    h_hi, h_lo = _split_bf16(h)
    h2_ref[0] = h_hi

    w_hi = wr_hi_ref[...]
    logits = (_dot_nt(w_hi, h_hi) + _dot_nt(w_hi, h_lo)) + _dot_nt(wr_lo_ref[...], h_hi) + br_ref[...]
    rows = [logits[r:r + 1, :] for r in range(8 + N_EXPERTS)]
    group_prob = _softmax_rows(rows[0:N_GROUPS])
    group_p, group_idx = _first_max(group_prob)
    in_group = []
    for j in range(EXPERTS_PER_GROUP):
        v = rows[8 + (N_GROUPS - 1) * EXPERTS_PER_GROUP + j]
        for g in range(N_GROUPS - 2, -1, -1):
            v = jnp.where(group_idx == g, rows[8 + g * EXPERTS_PER_GROUP + j], v)
        in_group.append(v)
    expert_prob = _softmax_rows(in_group)
    p1, i1 = _first_max(expert_prob)
    rest = [jnp.where(i1 == j, -1.0, expert_prob[j]) for j in range(EXPERTS_PER_GROUP)]
    p2, i2 = _first_max(rest)
    e1 = group_idx * EXPERTS_PER_GROUP + i1
    e2 = group_idx * EXPERTS_PER_GROUP + i2
    tm = e1.shape[1]
    ri_ref[0, 0] = jnp.zeros((8, tm), jnp.int32)
    ri_ref[0, 0, 0:1, :] = e1
    ri_ref[0, 0, 1:2, :] = e2
    rw_ref[0] = _rows_to_columns([group_p * p1 / (p1 + p2), group_p * p2 / (p1 + p2)])
    erow = lax.broadcasted_iota(jnp.int32, (N_EXPERTS, tm), 0)
    sel = jnp.where((erow == e1) | (erow == e2), 1.0, 0.0).astype(BF16)
    cnt_ref[0, 0] = _dot(sel, jnp.ones((tm, 128), BF16))


def _post_mixer(ya, yb, yc, x, mod, gain_a, gain_b, gain_c, w_out, norm_ffn, wr_hi, wr_lo, br):
    b, s, d = x.shape
    tm = TM_POST
    tok = lambda width: pl.BlockSpec((1, tm, width), lambda bi, i: (bi, i, 0))
    full = lambda shape: pl.BlockSpec(shape, lambda bi, i: (0,) * len(shape))
    return pl.pallas_call(
        _post_kernel,
        grid=(b, s // tm),
        in_specs=[
            tok(SWA_WIDTH), tok(LRU_WIDTH), tok(FOX_WIDTH), tok(d),
            pl.BlockSpec((1, 6, d), lambda bi, i: (bi, 0, 0)),
            full((1, SWA_WIDTH)), full((1, LRU_WIDTH)), full((1, FOX_WIDTH)),
            full((d, d)), full((1, d)),
            full((ROUTER_ROWS, d)), full((ROUTER_ROWS, d)), full((ROUTER_ROWS, 1)),
        ],
        out_specs=[
            tok(d), tok(d),
            pl.BlockSpec((1, 1, 8, tm), lambda bi, i: (bi, i, 0, 0)),
            tok(128),
            pl.BlockSpec((1, 1, N_EXPERTS, 128), lambda bi, i: (bi, i, 0, 0)),
        ],
        out_shape=[
            jax.ShapeDtypeStruct((b, s, d), F32),
            jax.ShapeDtypeStruct((b, s, d), BF16),
            jax.ShapeDtypeStruct((b, s // tm, 8, tm), jnp.int32),
            jax.ShapeDtypeStruct((b, s, 128), F32),
            jax.ShapeDtypeStruct((b, s // tm, N_EXPERTS, 128), F32),
        ],
        compiler_params=_params("parallel", "parallel"),
        name="post_mixer",
    )(ya, yb, yc, x, mod, gain_a, gain_b, gain_c, w_out, norm_ffn, wr_hi, wr_lo, br)


TMS = TM_POST
SEG = 16
SORT_ROWS = 1280
ET = 1024
SEG_PIECES = (512, 256, 128, 64, 32, 16)
TAIL_PIECES = (512, 256, 128, 64, 32, 16)
TOTAL_PIECES = (1024, 512, 256, 128, 64, 32, 16)
RARE_ROWS = 128
assert SORT_ROWS >= 2 * TMS + N_EXPERTS * (SEG - 1) and SEG_PIECES[0] == TMS and TAIL_PIECES[0] * 2 == ET


def _for_each_piece(n16_ref, lo_ref, goff_ref, tile, fn):
    for e in range(N_EXPERTS):
        n = n16_ref[tile * N_EXPERTS + e]
        lo = lo_ref[tile * N_EXPERTS + e]
        go = goff_ref[tile * N_EXPERTS + e]

        def pieces(sizes, n=n, lo=lo, go=go):
            for rows in sizes:
                done = n & (-2 * rows)

                @pl.when((n & rows) != 0)
                def _(done=done, rows=rows):
                    fn(pl.multiple_of(lo + done, SEG), pl.multiple_of(go + done, SEG), rows)

        @pl.when(n >= RARE_ROWS)
        def _():
            pieces([r for r in SEG_PIECES if r >= RARE_ROWS])

        pieces([r for r in SEG_PIECES if r < RARE_ROWS])


def _for_each_total_piece(total_ref, tile, fn):
    n = total_ref[tile]
    for rows in TOTAL_PIECES:
        @pl.when((n & rows) != 0)
        def _(rows=rows):
            fn(rows)


def _dispatch_kernel(n16_ref, lo_ref, goff_ref, total_ref, tail_row_ref, tail_n_ref, n_valid_ref,
                     h_ref, ri_ref, tri_ref, xs_ref, dest_ref, sorted_ref, zero_ref, sem):
    i = pl.program_id(0)
    last = pl.num_programs(0) - 1
    slot = i & 1
    tms = h_ref.shape[0]

    ri = ri_ref[0]
    e1 = ri[0:1, :]
    e2 = ri[1:2, :]
    erow = lax.broadcasted_iota(jnp.int32, (N_EXPERTS, tms), 0)
    m1 = erow == e1
    m2 = erow == e2
    sel = jnp.where(m1 | m2, 1.0, 0.0).astype(BF16)
    rank = _dot(sel, tri_ref[...]).astype(jnp.int32)
    start = jnp.zeros((N_EXPERTS, tms), jnp.int32)
    for e in range(N_EXPERTS):
        start = jnp.where(erow == e, lo_ref[i * N_EXPERTS + e], start)
    dest = rank + start
    r1 = jnp.sum(jnp.where(m1, dest, 0), axis=0, keepdims=True)
    r2 = jnp.sum(jnp.where(m2, dest, 0), axis=0, keepdims=True)
    dest_ref[...] = _rows_to_columns([r1.astype(F32), r2.astype(F32)])

    rows = lax.broadcasted_iota(jnp.int32, (SORT_ROWS, tms), 0)
    perm = jnp.where((rows == r1) | (rows == r2), 1.0, 0.0).astype(BF16)
    sorted_ref[slot] = _dot(perm, h_ref[...]).astype(BF16)

    def copy(src_slot, local_row, global_row, n):
        return pltpu.make_async_copy(sorted_ref.at[src_slot, pl.ds(local_row, n)],
                                     xs_ref.at[pl.ds(global_row, n)], sem)

    @pl.when(i > 0)
    def _():
        _for_each_total_piece(total_ref, i - 1, lambda n: copy(1 - slot, 0, 0, n).wait())

    _for_each_piece(n16_ref, lo_ref, goff_ref, i, lambda lr, gr, n: copy(slot, lr, gr, n).start())

    @pl.when(i == last)
    def _():
        _for_each_total_piece(total_ref, i, lambda n: copy(slot, 0, 0, n).wait())
        zero_ref[...] = jnp.zeros_like(zero_ref)

        def tail_copies(fn):
            for e in range(N_EXPERTS):
                n = tail_n_ref[e]
                row = tail_row_ref[e]
                for rows_ in TAIL_PIECES:
                    done = n & (-2 * rows_)

                    @pl.when((n & rows_) != 0)
                    def _(row=row, done=done, rows_=rows_):
                        fn(pltpu.make_async_copy(
                            zero_ref.at[pl.ds(0, rows_)],
                            xs_ref.at[pl.ds(pl.multiple_of(row + done, SEG), rows_)], sem))

        tail_copies(lambda c: c.start())
        tail_copies(lambda c: c.wait())

        def unused_tiles(fn):
            def body(j, carry):
                for part in range(ET // TAIL_PIECES[0]):
                    row = pl.multiple_of(j * ET + part * TAIL_PIECES[0], SEG)
                    fn(pltpu.make_async_copy(zero_ref, xs_ref.at[pl.ds(row, TAIL_PIECES[0])], sem))
                return carry
            lax.fori_loop(n_valid_ref[0], xs_ref.shape[0] // ET, body, 0)

        unused_tiles(lambda c: c.start())
        unused_tiles(lambda c: c.wait())


def _dispatch(h2, route_i, tri, tables, rows_max):
    t, d = h2.shape
    nt = t // TMS
    grid_spec = pltpu.PrefetchScalarGridSpec(
        num_scalar_prefetch=7,
        grid=(nt,),
        in_specs=[
            pl.BlockSpec((TMS, d), lambda i, *_: (i, 0)),
            pl.BlockSpec((1, 8, TMS), lambda i, *_: (i, 0, 0)),
            pl.BlockSpec((TMS, TMS), lambda i, *_: (0, 0)),
        ],
        out_specs=[
            pl.BlockSpec(memory_space=pl.ANY),
            pl.BlockSpec((TMS, 128), lambda i, *_: (i, 0)),
        ],
        scratch_shapes=[
            pltpu.VMEM((2, SORT_ROWS, d), BF16),
            pltpu.VMEM((TAIL_PIECES[0], d), BF16),
            pltpu.SemaphoreType.DMA,
        ],
    )
    return pl.pallas_call(
        _dispatch_kernel,
        grid_spec=grid_spec,
        out_shape=[
            jax.ShapeDtypeStruct((rows_max, d), BF16),
            jax.ShapeDtypeStruct((t, 128), F32),
        ],
        compiler_params=_params("arbitrary"),
        name="moe_dispatch",
    )(*tables, h2, route_i, tri)


def _experts_kernel(te_ref, nv_ref, xs_ref, wg_ref, wu_ref, wd_ref, ys_ref, wg_s, wu_s, wd_s):
    j = pl.program_id(0)

    @pl.when((j == 0) | (te_ref[j] != te_ref[jnp.maximum(j - 1, 0)]))
    def _():
        wg_s[...] = wg_ref[0].astype(BF16)
        wu_s[...] = wu_ref[0].astype(BF16)
        wd_s[...] = wd_ref[0].astype(BF16)

    @pl.when(j < nv_ref[0])
    def _():
        h = xs_ref[...]
        gate = _dot(h, wg_s[...])
        hidden = (gate * jax.nn.sigmoid(gate)) * _dot(h, wu_s[...])
        ys_ref[...] = _dot(hidden.astype(BF16), wd_s[...]).astype(BF16)

    @pl.when(j >= nv_ref[0])
    def _():
        ys_ref[...] = jnp.zeros_like(ys_ref)


def _experts(xs, tile_expert, n_valid, w_gate, w_up, w_down, layer):
    rows, d = xs.shape
    row_tile = lambda j, te, nv: (jnp.minimum(j, nv[0] - 1), 0)
    weight = lambda j, te, nv: (layer, te[j], 0, 0)
    grid_spec = pltpu.PrefetchScalarGridSpec(
        num_scalar_prefetch=2,
        grid=(rows // ET,),
        in_specs=[
            pl.BlockSpec((ET, d), row_tile),
            pl.BlockSpec((None, 1, d, D_EXPERT), weight),
            pl.BlockSpec((None, 1, d, D_EXPERT), weight),
            pl.BlockSpec((None, 1, D_EXPERT, d), weight),
        ],
        out_specs=pl.BlockSpec((ET, d), lambda j, te, nv: (j, 0)),
        scratch_shapes=[pltpu.VMEM((d, D_EXPERT), BF16), pltpu.VMEM((d, D_EXPERT), BF16),
                        pltpu.VMEM((D_EXPERT, d), BF16)],
    )
    return pl.pallas_call(
        _experts_kernel,
        grid_spec=grid_spec,
        out_shape=jax.ShapeDtypeStruct((rows, d), BF16),
        compiler_params=_params("arbitrary"),
        name="moe_experts",
    )(tile_expert, n_valid, xs, w_gate, w_up, w_down)


def _combine_kernel(n16_ref, lo_ref, goff_ref, total_ref, ys_ref, dest_ref, w_ref, x_ref, mod_ref,
                    nf_ref, o_ref, buf_ref, sems, *, final):
    i = pl.program_id(0)
    nt = pl.num_programs(0)
    slot = i & 1

    def copy(dst_slot, local_row, global_row, n):
        return pltpu.make_async_copy(ys_ref.at[pl.ds(global_row, n)],
                                     buf_ref.at[dst_slot, pl.ds(local_row, n)], sems.at[dst_slot])

    @pl.when(i == 0)
    def _():
        buf_ref[...] = jnp.zeros_like(buf_ref)
        _for_each_piece(n16_ref, lo_ref, goff_ref, 0, lambda lr, gr, n: copy(0, lr, gr, n).start())

    @pl.when(i + 1 < nt)
    def _():
        _for_each_piece(n16_ref, lo_ref, goff_ref, i + 1,
                        lambda lr, gr, n: copy(1 - slot, lr, gr, n).start())

    _for_each_total_piece(total_ref, i, lambda n: copy(slot, 0, 0, n).wait())

    dest = dest_ref[:, 0:2].astype(jnp.int32)
    w = w_ref[:, 0:2]
    col = lax.broadcasted_iota(jnp.int32, (dest.shape[0], SORT_ROWS), 1)
    wmat = (jnp.where(col == dest[:, 0:1], w[:, 0:1], 0.0)
            + jnp.where(col == dest[:, 1:2], w[:, 1:2], 0.0))
    y = _dot(wmat.astype(BF16), buf_ref[slot])
    x = x_ref[...] + mod_ref[0, 5:6, :] * y
    if final:
        x = _rms_normalize(x) * nf_ref[...]
    o_ref[...] = x


def _combine(ys, dest_col, w_col, x, mod, norm_final, tables, tiles_per_batch, final):
    t, d = x.shape
    nt = t // TMS
    grid_spec = pltpu.PrefetchScalarGridSpec(
        num_scalar_prefetch=4,
        grid=(nt,),
        in_specs=[
            pl.BlockSpec(memory_space=pl.ANY),
            pl.BlockSpec((TMS, 128), lambda i, *_: (i, 0)),
            pl.BlockSpec((TMS, 128), lambda i, *_: (i, 0)),
            pl.BlockSpec((TMS, d), lambda i, *_: (i, 0)),
            pl.BlockSpec((1, 6, d), lambda i, *_: (i // tiles_per_batch, 0, 0)),
            pl.BlockSpec((1, d), lambda i, *_: (0, 0)),
        ],
        out_specs=pl.BlockSpec((TMS, d), lambda i, *_: (i, 0)),
        scratch_shapes=[pltpu.VMEM((2, SORT_ROWS, d), BF16), pltpu.SemaphoreType.DMA((2,))],
    )
    return pl.pallas_call(
        functools.partial(_combine_kernel, final=final),
        grid_spec=grid_spec,
        out_shape=jax.ShapeDtypeStruct((t, d), F32),
        compiler_params=_params("arbitrary"),
        name="moe_combine_final" if final else "moe_combine",
    )(*tables, ys, dest_col, w_col, x, mod, norm_final)


def _route_tables(counts, n_expert_tiles):
    n16 = (counts + (SEG - 1)) // SEG * SEG
    lo = jnp.cumsum(n16, axis=1) - n16
    total = jnp.sum(n16, axis=0)
    region = (total + (ET - 1)) // ET * ET
    ends = jnp.cumsum(region)
    base = ends - region
    goff = base[None, :] + jnp.cumsum(n16, axis=0) - n16
    tile_row = jnp.arange(n_expert_tiles, dtype=jnp.int32) * ET
    tile_expert = jnp.minimum(jnp.sum(tile_row[:, None] >= ends[None, :], axis=1), N_EXPERTS - 1)
    n_valid = (ends[-1] // ET).reshape(1)
    i32 = lambda a: a.astype(jnp.int32).reshape(-1)
    return (i32(n16), i32(lo), i32(goff), i32(jnp.sum(n16, axis=1))), \
        (i32(base + total), i32(region - total)), i32(tile_expert), i32(n_valid)


def _moe(h2, route_i, route_w, cnt, w_gate, w_up, w_down, layer, x, mod, norm_final, final):
    b, s, d = x.shape
    t = b * s
    nt = t // TMS
    n_expert_tiles = -(-(2 * t + nt * N_EXPERTS * (SEG - 1) + N_EXPERTS * (ET - 1)) // ET)
    counts = jnp.round(cnt[:, :, :, 0]).astype(jnp.int32).reshape(nt, N_EXPERTS)
    seg_tables, tail_tables, tile_expert, n_valid = _route_tables(counts, n_expert_tiles)
    tri = jnp.triu(jnp.ones((TMS, TMS), BF16), k=1)
    xs, dest = _dispatch(h2.reshape(t, d), route_i.reshape(nt, 8, TMS), tri,
                         seg_tables + tail_tables + (n_valid,), n_expert_tiles * ET)
    ys = _experts(xs, tile_expert, n_valid, w_gate, w_up, w_down, layer)
    out = _combine(ys, dest, route_w.reshape(t, 128), x.reshape(t, d), mod, norm_final, seg_tables,
                   s // TMS, final)
    return out.reshape(b, s, d)


def _block_diag(w):
    nb, c, _ = w.shape
    eye = jnp.eye(nb, dtype=w.dtype)
    return (eye[:, None, :, None] * w[:, :, None, :]).reshape(nb * c, nb * c)


def _pair_swa_heads(w, axis):
    half = SWA_Q_HEADS // 2
    shape = w.shape
    w = w.reshape(shape[:axis] + (2, half, HEAD_DIM) + shape[axis + 1:])
    return jnp.swapaxes(w, axis, axis + 1).reshape(shape)


def kernel(x, c, w_mod, b_mod, norm_mix, norm_ffn, w_in, w_out, out_gain, sinks, conv_w, conv_b,
           lru_wa, lru_ba, lru_wx, lru_bx, lru_lam, fox_bf, w_router_group, b_router_group,
           w_router_expert, b_router_expert, w_gate, w_up, w_down, norm_final):
    depth = w_mod.shape[0]
    b, s, d = x.shape
    assert d == D_MODEL and s % max(TM_PRE, TM_PREP, TQ_SWA, TS_LRU, T_FOX, TM_POST) == 0
    assert TM_PRE == T_FOX
    c_rows = jnp.zeros((16, d), F32).at[:b].set(c)
    mod_all = _modulation(c_rows, w_mod, b_mod)[:, :b]

    for l in range(depth):
        mod = mod_all[l].reshape(b, 6, d)
        w_in_l = w_in[l]
        w_tok = jnp.concatenate([_pair_swa_heads(w_in_l[:, :SWA_WIDTH], 1), w_in_l[:, SWA_WIDTH:OFF_QC],
                                 w_in_l[:, OFF_KC:OFF_VC]], axis=1).astype(BF16)
        w_t = jnp.concatenate([w_in_l[:, OFF_QC:OFF_KC].T, w_in_l[:, OFF_VC:OFF_FC].T,
                               jnp.zeros((FORGET_ROWS, d), F32).at[:FOX_HEADS].set(w_in_l[:, OFF_FC:].T)],
                              axis=0).astype(BF16)
        fb = jnp.zeros((FORGET_ROWS, 1), F32).at[:FOX_HEADS, 0].set(fox_bf[l])
        gain = out_gain[l]
        gain_a = _pair_swa_heads(gain[:SWA_WIDTH], 0).reshape(1, SWA_WIDTH)
        gain_b = gain[SWA_WIDTH:SWA_WIDTH + LRU_WIDTH].reshape(1, LRU_WIDTH)
        gain_c = gain[SWA_WIDTH + LRU_WIDTH:].reshape(1, FOX_WIDTH)
        w_out_l = jnp.concatenate([_pair_swa_heads(w_out[l][:SWA_WIDTH], 0), w_out[l][SWA_WIDTH:]],
                                  axis=0).astype(BF16)
        wr = jnp.zeros((ROUTER_ROWS, d), F32)
        wr = wr.at[:N_GROUPS].set(w_router_group[l].T).at[8:8 + N_EXPERTS].set(w_router_expert[l].T)
        wr_hi = wr.astype(BF16)
        wr_lo = (wr - wr_hi.astype(F32)).astype(BF16)
        br = jnp.zeros((ROUTER_ROWS, 1), F32)
        br = br.at[:N_GROUPS, 0].set(b_router_group[l]).at[8:8 + N_EXPERTS, 0].set(b_router_expert[l])

        qa, ka, va, xb, gb, kc, qct, vaug, lf = _pre_mixer(
            x, mod, norm_mix[l].reshape(1, d), w_tok, w_t, fb)
        cum, cum_col = _cumsum(lf)
        kaug = _fox_prep(kc, cum_col)
        base = cum[:, :FOX_HEADS, ::T_FOX].reshape(-1)
        ya = _swa(qa, ka, va, sinks[l].reshape(1, SWA_Q_HEADS))
        yb = _lru(xb, gb, conv_w[l], conv_b[l].reshape(1, -1),
                  _block_diag(lru_wa[l]).astype(BF16), lru_ba[l].reshape(1, -1),
                  _block_diag(lru_wx[l]).astype(BF16), lru_bx[l].reshape(1, -1),
                  lru_lam[l].reshape(1, -1))
        yc = _fox(qct, kaug, vaug, base)
        x, h2, route_i, route_w, cnt = _post_mixer(
            ya, yb, yc, x, mod, gain_a, gain_b, gain_c, w_out_l,
            norm_ffn[l].reshape(1, d), wr_hi, wr_lo, br)
        x = _moe(h2, route_i, route_w, cnt, w_gate, w_up, w_down, l, x, mod,
                 norm_final.reshape(1, d), final=(l == depth - 1))
    return x
```

```python
import functools

import jax
import jax.numpy as jnp
from jax import lax
from jax.experimental import pallas as pl
from jax.experimental.pallas import tpu as pltpu

F32 = jnp.float32
BF16 = jnp.bfloat16

D_MODEL = 1024
HEAD_DIM = 64
PAIR = 2 * HEAD_DIM
BLOCK = 128
SWA_Q_HEADS = 8
SWA_WIDTH = SWA_Q_HEADS * HEAD_DIM
KV_WIDTH = 2 * HEAD_DIM
LRU_WIDTH = 256
LRU_BLOCKS = 8
CONV_WIDTH = 4
LRU_C = 8.0
FOX_HEADS = 4
FOX_WIDTH = FOX_HEADS * HEAD_DIM
N_GROUPS = 4
EXPERTS_PER_GROUP = 4
N_EXPERTS = 16
D_EXPERT = 256
EPS = 1e-6
NEG = -1e30
LOG2E = 1.4426950408889634

OFF_KA = SWA_WIDTH
OFF_VA = OFF_KA + KV_WIDTH
OFF_XB = OFF_VA + KV_WIDTH
OFF_GB = OFF_XB + LRU_WIDTH
OFF_QC = OFF_GB + LRU_WIDTH
OFF_KC = OFF_QC + FOX_WIDTH
OFF_VC = OFF_KC + FOX_WIDTH
OFF_FC = OFF_VC + FOX_WIDTH

ROUTER_ROWS = 32
FORGET_ROWS = 16
V_ROWS = HEAD_DIM + 16
VMEM_LIMIT_BYTES = 48 * 1024 * 1024

TM_PRE = 1024
TM_PREP = 2048
TQ_SWA = 1024
TS_LRU = 1024
T_FOX = 512
TM_POST = 1024
CUMSUM_CHUNK = 256


def _params(*semantics):
    return pltpu.CompilerParams(dimension_semantics=semantics, vmem_limit_bytes=VMEM_LIMIT_BYTES)


def _split_bf16(v):
    hi = v.astype(BF16)
    lo = (v - hi.astype(F32)).astype(BF16)
    return hi, lo


def _dot(a, b):
    return jnp.dot(a, b, preferred_element_type=F32)


def _dot_nt(a, b):
    return lax.dot_general(a, b, (((1,), (1,)), ((), ())), preferred_element_type=F32)


def _rms_normalize(v):
    return v * lax.rsqrt(jnp.mean(v * v, axis=-1, keepdims=True) + EPS)


def _mod_kernel(c_ref, w_ref, b_ref, o_ref):
    c = c_ref[...]
    ca = c * jax.nn.sigmoid(c)
    w = w_ref[0]
    a_hi, a_lo = _split_bf16(ca)
    w_hi, w_lo = _split_bf16(w)
    o_ref[0] = _dot(a_hi, w_hi) + _dot(a_lo, w_hi) + _dot(a_hi, w_lo) + b_ref[0]


def _modulation(c, w_mod, b_mod):
    depth, d, d6 = w_mod.shape
    b = c.shape[0]
    n = d6 // d
    return pl.pallas_call(
        _mod_kernel,
        grid=(depth, n),
        in_specs=[
            pl.BlockSpec((b, d), lambda l, j: (0, 0)),
            pl.BlockSpec((1, d, d), lambda l, j: (l, 0, j)),
            pl.BlockSpec((1, 1, d), lambda l, j: (l, 0, j)),
        ],
        out_specs=pl.BlockSpec((1, b, d), lambda l, j: (l, 0, j)),
        out_shape=jax.ShapeDtypeStruct((depth, b, d6), F32),
        compiler_params=_params("arbitrary", "arbitrary"),
        name="modulation",
    )(c, w_mod, b_mod.reshape(depth, 1, d6))


def _log_sigmoid(z):
    return jnp.minimum(z, 0.0) - jnp.log1p(jnp.exp(-jnp.abs(z)))


def _other_half(a):
    return HEAD_DIM * (1 - a)


def _pre_kernel(x_ref, mod_ref, g_ref, w_ref, wt_ref, fb_ref,
                qa_ref, ka_ref, va_ref, xb_ref, gb_ref, kc_ref, qct_ref, vct_ref, lf_ref):
    x = x_ref[0]
    h = (_rms_normalize(x) * g_ref[...]) * (1.0 + mod_ref[0, 1:2, :]) + mod_ref[0, 0:1, :]
    hb = h.astype(BF16)

    def proj(lo, width):
        return _dot(hb, w_ref[:, lo:lo + width])

    scale = HEAD_DIM ** -0.5
    qa_ref[0] = (proj(0, SWA_WIDTH) * (scale * LOG2E)).astype(BF16)
    ka_ref[0] = proj(OFF_KA, KV_WIDTH).astype(BF16)
    va_ref[0] = proj(OFF_VA, KV_WIDTH).astype(BF16)
    xb_ref[0] = proj(OFF_XB, LRU_WIDTH).astype(BF16)
    gb_ref[0] = proj(OFF_GB, LRU_WIDTH).astype(BF16)
    kc_ref[0] = proj(OFF_QC, FOX_WIDTH).astype(BF16)

    t_all = _dot_nt(wt_ref[...], hb)
    tf = qct_ref.shape[3]
    ones_row = jnp.where(lax.broadcasted_iota(jnp.int32, (V_ROWS - HEAD_DIM, tf), 0) == 0, 1.0, 0.0)
    for part in range(t_all.shape[1] // tf):
        cols = slice(part * tf, (part + 1) * tf)
        qct_ref[0, part] = (t_all[0:FOX_WIDTH, cols] * (scale * LOG2E)).astype(BF16)
        for hd in range(FOX_HEADS):
            v_head = t_all[FOX_WIDTH + hd * HEAD_DIM:FOX_WIDTH + (hd + 1) * HEAD_DIM, cols]
            vct_ref[0, part, hd] = jnp.concatenate([v_head, ones_row], axis=0).astype(BF16)
    lf_ref[0] = _log_sigmoid(t_all[2 * FOX_WIDTH:] + fb_ref[...])


def _pre_mixer(x, mod, gain, w_tok, w_t, fb):
    b, s, d = x.shape
    tm = TM_PRE
    tf = T_FOX
    nf = tm // tf

    def tok(width):
        return pl.BlockSpec((1, tm, width), lambda bi, i: (bi, i, 0))

    def tok_shape(width):
        return jax.ShapeDtypeStruct((b, s, width), BF16)

    return pl.pallas_call(
        _pre_kernel,
        grid=(b, s // tm),
        in_specs=[
            tok(d),
            pl.BlockSpec((1, 6, d), lambda bi, i: (bi, 0, 0)),
            pl.BlockSpec((1, d), lambda bi, i: (0, 0)),
            pl.BlockSpec(w_tok.shape, lambda bi, i: (0, 0)),
            pl.BlockSpec(w_t.shape, lambda bi, i: (0, 0)),
            pl.BlockSpec((FORGET_ROWS, 1), lambda bi, i: (0, 0)),
        ],
        out_specs=[
            tok(SWA_WIDTH), tok(KV_WIDTH), tok(KV_WIDTH), tok(LRU_WIDTH), tok(LRU_WIDTH),
            tok(FOX_WIDTH),
            pl.BlockSpec((1, nf, FOX_WIDTH, tf), lambda bi, i: (bi, i, 0, 0)),
            pl.BlockSpec((1, nf, FOX_HEADS, V_ROWS, tf), lambda bi, i: (bi, i, 0, 0, 0)),
            pl.BlockSpec((1, FORGET_ROWS, tm), lambda bi, i: (bi, 0, i)),
        ],
        out_shape=[
            tok_shape(SWA_WIDTH), tok_shape(KV_WIDTH), tok_shape(KV_WIDTH), tok_shape(LRU_WIDTH),
            tok_shape(LRU_WIDTH), tok_shape(FOX_WIDTH),
            jax.ShapeDtypeStruct((b, s // tf, FOX_WIDTH, tf), BF16),
            jax.ShapeDtypeStruct((b, s // tf, FOX_HEADS, V_ROWS, tf), BF16),
            jax.ShapeDtypeStruct((b, FORGET_ROWS, s), F32),
        ],
        compiler_params=_params("parallel", "parallel"),
        name="pre_mixer",
    )(x, mod, gain, w_tok, w_t, fb)


def _cumsum_kernel(lf_ref, o_ref, col_ref):
    n = CUMSUM_CHUNK
    rows = lf_ref.shape[1]
    s = lf_ref.shape[2]
    r = lax.broadcasted_iota(jnp.int32, (n, n), 0)
    c = lax.broadcasted_iota(jnp.int32, (n, n), 1)
    tri = jnp.where(r <= c, 1.0, 0.0).astype(BF16)
    carry = jnp.zeros((rows, 1), F32)
    for j in range(s // n):
        seg = lf_ref[0, :, j * n:(j + 1) * n]
        hi = seg.astype(BF16)
        rem = seg - hi.astype(F32)
        mid = rem.astype(BF16)
        lo = (rem - mid.astype(F32)).astype(BF16)
        cs = (_dot(hi, tri) + _dot(mid, tri)) + _dot(lo, tri) + carry
        o_ref[0, :, j * n:(j + 1) * n] = cs
        parts = _split3(cs[0:FOX_HEADS] * LOG2E)
        col_ref[0, j * n:(j + 1) * n, :] = _rows_to_columns(
            [parts[k][hd:hd + 1] for hd in range(FOX_HEADS) for k in range(3)])
        carry = cs[:, n - 1:n]


def _cumsum(lf):
    b, r, s = lf.shape
    return pl.pallas_call(
        _cumsum_kernel,
        grid=(b,),
        in_specs=[pl.BlockSpec((1, r, s), lambda bi: (bi, 0, 0))],
        out_specs=[pl.BlockSpec((1, r, s), lambda bi: (bi, 0, 0)),
                   pl.BlockSpec((1, s, 128), lambda bi: (bi, 0, 0))],
        out_shape=[jax.ShapeDtypeStruct((b, r, s), F32), jax.ShapeDtypeStruct((b, s, 128), F32)],
        compiler_params=_params("parallel"),
        name="cumsum",
    )(lf)


def _swa_kernel(sink_ref, q_ref, kc_ref, vc_ref, kp_ref, vp_ref, o_ref, kf_ref, vf_ref, bias_ref):
    i = pl.program_id(1)
    nsub = q_ref.shape[1] // BLOCK
    npair = SWA_Q_HEADS // 2

    @pl.when((pl.program_id(0) == 0) & (i == 0))
    def _():
        qpos = BLOCK + lax.broadcasted_iota(jnp.int32, (BLOCK, 2 * BLOCK), 0)
        kpos = lax.broadcasted_iota(jnp.int32, (BLOCK, 2 * BLOCK), 1)
        dist = qpos - kpos
        valid = (dist >= 0) & (dist < BLOCK)
        distf = dist.astype(F32)
        for hd in range(SWA_Q_HEADS):
            slope = 2.0 ** (-8.0 * (hd + 1) / SWA_Q_HEADS)
            bias_ref[hd] = jnp.where(valid, (-slope * LOG2E) * distf, NEG)

    kf_ref[0:BLOCK, :] = kp_ref[0]
    kf_ref[BLOCK:, :] = kc_ref[0]
    vf_ref[0:BLOCK, :] = vp_ref[0]
    vf_ref[BLOCK:, :] = vc_ref[0]

    lane = lax.broadcasted_iota(jnp.int32, (2 * BLOCK, PAIR), 1)
    lane_o = lax.broadcasted_iota(jnp.int32, (BLOCK, PAIR), 1)
    col = lax.broadcasted_iota(jnp.int32, (BLOCK, 2 * BLOCK), 1)

    for j in range(nsub):
        k2 = kf_ref[j * BLOCK:(j + 2) * BLOCK, :]
        v2 = vf_ref[j * BLOCK:(j + 2) * BLOCK, :]
        k_half = (jnp.where(lane < HEAD_DIM, k2, jnp.zeros_like(k2)),
                  jnp.where(lane >= HEAD_DIM, k2, jnp.zeros_like(k2)))
        for p in range(npair):
            qp = q_ref[0, j * BLOCK:(j + 1) * BLOCK, p * PAIR:(p + 1) * PAIR]
            outs = []
            for a in range(2):
                hd = p + a * npair
                sc = _dot_nt(qp, k_half[a]) + bias_ref[hd]
                if j == 0:
                    sc = sc + jnp.where(col < BLOCK, jnp.where(i == 0, NEG, 0.0), 0.0)
                sink = sink_ref[0, hd] * LOG2E
                m = jnp.maximum(jnp.max(sc, axis=-1, keepdims=True), sink)
                pr = jnp.exp2(sc - m)
                den = jnp.sum(pr, axis=-1, keepdims=True) + jnp.exp2(sink - m)
                outs.append(_dot(pr.astype(BF16), v2) / den)
            o_ref[0, j * BLOCK:(j + 1) * BLOCK, p * PAIR:(p + 1) * PAIR] = jnp.where(
                lane_o < HEAD_DIM, outs[0], outs[1]).astype(BF16)


def _swa(qa, ka, va, sinks):
    b, s, _ = qa.shape
    tq = TQ_SWA
    nsub = tq // BLOCK
    prev = lambda bi, i: (bi, jnp.maximum(i * nsub - 1, 0), 0)
    return pl.pallas_call(
        _swa_kernel,
        grid=(b, s // tq),
        in_specs=[
            pl.BlockSpec(memory_space=pltpu.SMEM),
            pl.BlockSpec((1, tq, SWA_WIDTH), lambda bi, i: (bi, i, 0)),
            pl.BlockSpec((1, tq, KV_WIDTH), lambda bi, i: (bi, i, 0)),
            pl.BlockSpec((1, tq, KV_WIDTH), lambda bi, i: (bi, i, 0)),
            pl.BlockSpec((1, BLOCK, KV_WIDTH), prev),
            pl.BlockSpec((1, BLOCK, KV_WIDTH), prev),
        ],
        out_specs=pl.BlockSpec((1, tq, SWA_WIDTH), lambda bi, i: (bi, i, 0)),
        out_shape=jax.ShapeDtypeStruct((b, s, SWA_WIDTH), BF16),
        scratch_shapes=[
            pltpu.VMEM((tq + BLOCK, KV_WIDTH), BF16),
            pltpu.VMEM((tq + BLOCK, KV_WIDTH), BF16),
            pltpu.VMEM((SWA_Q_HEADS, BLOCK, 2 * BLOCK), F32),
        ],
        compiler_params=_params("arbitrary", "arbitrary"),
        name="swa",
    )(sinks, qa, ka, va, ka, va)


def _gelu_tanh(v):
    return 0.5 * v * (1.0 + jnp.tanh(0.7978845608028654 * (v + 0.044715 * (v * v * v))))


def _lru_kernel(xb_ref, gb_ref, cw_ref, cb_ref, wa_ref, ba_ref, wx_ref, bx_ref, lam_ref,
                o_ref, win_ref, h_ref):
    ts = xb_ref.shape[1]

    @pl.when(pl.program_id(1) == 0)
    def _():
        win_ref[0:8, :] = jnp.zeros((8, LRU_WIDTH), F32)
        h_ref[...] = jnp.zeros_like(h_ref)

    x = xb_ref[0].astype(F32)
    win_ref[8:, :] = x
    conv = cb_ref[...] + cw_ref[CONV_WIDTH - 1:CONV_WIDTH, :] * x
    for k in range(1, CONV_WIDTH):
        conv = conv + cw_ref[CONV_WIDTH - 1 - k:CONV_WIDTH - k, :] * win_ref[8 - k:8 - k + ts, :]
    win_ref[0:8, :] = x[ts - 8:ts]

    cbf = conv.astype(BF16)
    r = jax.nn.sigmoid(_dot(cbf, wa_ref[...]) + ba_ref[...])
    gi = jax.nn.sigmoid(_dot(cbf, wx_ref[...]) + bx_ref[...])
    z = -lam_ref[...]
    softplus = jnp.maximum(z, 0.0) + jnp.log1p(jnp.exp(-jnp.abs(z)))
    log_a = (-LRU_C * softplus) * r
    a = jnp.exp(log_a)
    v = 1.0 - a * a
    u = jnp.where(v == 0.0, 0.0, v * lax.rsqrt(v)) * (gi * conv)

    a = a.reshape(ts // 8, 8, LRU_WIDTH)
    u = u.reshape(ts // 8, 8, LRU_WIDTH)
    sub = lax.broadcasted_iota(jnp.int32, a.shape, 1)
    for step in (1, 2, 4):
        keep = sub >= step
        u = jnp.where(keep, a * pltpu.roll(u, step, 1) + u, u)
        a = jnp.where(keep, a * pltpu.roll(a, step, 1), a)
    carry = h_ref[7:8, :]
    groups = []
    for g in range(ts // 8):
        hg = u[g] + a[g] * carry
        groups.append(hg)
        carry = hg[7:8, :]
    h = jnp.concatenate(groups, axis=0)
    h_ref[...] = groups[-1]
    o_ref[0] = (h * _gelu_tanh(gb_ref[0].astype(F32))).astype(BF16)


def _lru(xb, gb, conv_w, conv_b, wa, ba, wx, bx, lam):
    b, s, w = xb.shape
    ts = TS_LRU
    tok = pl.BlockSpec((1, ts, w), lambda bi, i: (bi, i, 0))
    full = lambda shape: pl.BlockSpec(shape, lambda bi, i: (0,) * len(shape))
    return pl.pallas_call(
        _lru_kernel,
        grid=(b, s // ts),
        in_specs=[tok, tok, full((CONV_WIDTH, w)), full((1, w)), full((w, w)), full((1, w)),
                  full((w, w)), full((1, w)), full((1, w))],
        out_specs=tok,
        out_shape=jax.ShapeDtypeStruct((b, s, w), BF16),
        scratch_shapes=[pltpu.VMEM((8 + ts, w), F32), pltpu.VMEM((8, w), F32)],
        compiler_params=_params("arbitrary", "arbitrary"),
        name="lru",
    )(xb, gb, conv_w, conv_b, wa, ba, wx, bx, lam)


BIAS_ROWS = 6


def _split3(v):
    hi = v.astype(BF16).astype(F32)
    rem = v - hi
    mid = rem.astype(BF16).astype(F32)
    lo = (rem - mid).astype(BF16).astype(F32)
    return hi, mid, lo


def _fox_prep_kernel(k_ref, c_ref, o_ref):
    tm = k_ref.shape[1]
    lane = lax.broadcasted_iota(jnp.int32, (tm, PAIR), 1)
    parts = c_ref[0].astype(BF16)
    src = lax.broadcasted_iota(jnp.int32, (PAIR, PAIR), 0)
    dst = lax.broadcasted_iota(jnp.int32, (PAIR, PAIR), 1)
    for p in range(FOX_WIDTH // PAIR):
        kp = k_ref[0, :, p * PAIR:(p + 1) * PAIR].astype(F32)
        for a in range(2):
            hd = 2 * p + a
            ob = _other_half(a)
            part = src - 3 * hd
            place = jnp.where(part == dst - ob, jnp.where(part >= 0, jnp.where(part < 3, 1.0, 0.0), 0.0),
                              0.0).astype(BF16)
            bias = _dot(parts, place)
            rel = lane - ob
            aug = jnp.where(rel < 3, bias, jnp.where(rel < BIAS_ROWS, 1.0, 0.0))
            in_half = (lane >= HEAD_DIM * a) & (lane < HEAD_DIM * (a + 1))
            o_ref[0, hd] = jnp.where(in_half, kp, aug).astype(BF16)


def _fox_prep(kc, cum_col):
    b, s, w = kc.shape
    tm = TM_PREP
    return pl.pallas_call(
        _fox_prep_kernel,
        grid=(b, s // tm),
        in_specs=[
            pl.BlockSpec((1, tm, w), lambda bi, i: (bi, i, 0)),
            pl.BlockSpec((1, tm, cum_col.shape[2]), lambda bi, i: (bi, i, 0)),
        ],
        out_specs=pl.BlockSpec((1, FOX_HEADS, tm, PAIR), lambda bi, i: (bi, 0, i, 0)),
        out_shape=jax.ShapeDtypeStruct((b, FOX_HEADS, s, PAIR), BF16),
        compiler_params=_params("parallel", "parallel"),
        name="fox_prep",
    )(kc, cum_col)


def _fox_kernel(base_ref, qt_ref, ka_ref, vt_ref, o_ref, acc_ref, s0_ref, s1_ref):
    bi = pl.program_id(0)
    p = pl.program_id(1)
    qi = pl.program_id(2)
    nq = pl.num_programs(2)
    t = qt_ref.shape[3]
    qt = qt_ref[0, 0].astype(F32)
    row = lax.broadcasted_iota(jnp.int32, (PAIR, t), 0)
    causal = (lax.broadcasted_iota(jnp.int32, (t, t), 0) <= lax.broadcasted_iota(jnp.int32, (t, t), 1))
    q_aug = []
    for a in range(2):
        hd = 2 * p + a
        base = jnp.full((1, t), base_ref[(bi * FOX_HEADS + hd) * nq + qi], F32) * LOG2E
        b_hi, b_mid, b_lo = _split3(base)
        rel = row - _other_half(a)
        aug = jnp.where(rel < 3, -1.0, jnp.where(rel == 3, b_hi, jnp.where(
            rel == 4, b_mid, jnp.where(rel == 5, b_lo, 0.0))))
        in_half = (row >= HEAD_DIM * a) & (row < HEAD_DIM * (a + 1))
        q_aug.append(jnp.where(in_half, qt, aug).astype(BF16))
        acc_ref[a] = jnp.zeros((V_ROWS, t), F32)

    slots = (s0_ref, s1_ref)

    def scores(ki, slot, masked):
        tile_max = []
        for a in range(2):
            k = ka_ref[0, a, pl.ds(pl.multiple_of(ki * t, t), t), :]
            st = _dot(k, q_aug[a])
            if masked:
                st = jnp.where(causal, st, NEG)
            slots[slot][a] = st
            tile_max.append(jnp.max(st, axis=0, keepdims=True))
        return tuple(tile_max)

    def consume(ki, slot, m_old, tile_max):
        m_new = []
        for a in range(2):
            m_a = jnp.maximum(m_old[a], tile_max[a])
            alpha = jnp.exp2(m_old[a] - m_a)
            pr = jnp.exp2(slots[slot][a] - m_a).astype(BF16)
            acc_ref[a] = alpha * acc_ref[a] + _dot(vt_ref[0, ki, a], pr)
            m_new.append(m_a)
        return tuple(m_new)

    def advance(j, slot, carry):
        m_old, tile_max, prev = carry
        next_max = scores(j, 1 - slot, False)
        return consume(prev, slot, m_old, tile_max), next_max, j

    def body(i, carry):
        return advance(2 * i + 1, 1, advance(2 * i, 0, carry))

    m0 = jnp.full((1, t), NEG, F32)
    carry = ((m0, m0), scores(qi, 0, True), qi)
    carry = lax.fori_loop(0, qi // 2, body, carry)

    def finish_odd(carry):
        m, tile_max, prev = advance(qi - 1, 0, carry)
        consume(prev, 1, m, tile_max)
        return 0

    def finish_even(carry):
        m, tile_max, prev = carry
        consume(prev, 0, m, tile_max)
        return 0

    lax.cond(qi % 2 == 1, finish_odd, finish_even, carry)
    outs = []
    for a in range(2):
        acc = acc_ref[a]
        outs.append(acc[0:HEAD_DIM] / acc[HEAD_DIM:HEAD_DIM + 1, :])
    o_ref[0] = jnp.concatenate(outs, axis=0).T.astype(BF16)


def _fox(qct, kaug, vaug, base):
    b, nt, w, t = qct.shape
    s = nt * t
    npair = w // PAIR
    return pl.pallas_call(
        _fox_kernel,
        grid=(b, npair, nt),
        in_specs=[
            pl.BlockSpec(memory_space=pltpu.SMEM),
            pl.BlockSpec((1, 1, PAIR, t), lambda bi, p, qi: (bi, qi, p, 0)),
            pl.BlockSpec((1, 2, s, PAIR), lambda bi, p, qi: (bi, p, 0, 0)),
            pl.BlockSpec((1, nt, 2, V_ROWS, t), lambda bi, p, qi: (bi, 0, p, 0, 0)),
        ],
        out_specs=pl.BlockSpec((1, t, PAIR), lambda bi, p, qi: (bi, qi, p)),
        out_shape=jax.ShapeDtypeStruct((b, s, w), BF16),
        scratch_shapes=[pltpu.VMEM((2, V_ROWS, t), F32), pltpu.VMEM((2, t, t), F32),
                        pltpu.VMEM((2, t, t), F32)],
        compiler_params=_params("parallel", "parallel", "arbitrary"),
        name="fox",
    )(base, qct, kaug, vaug)


def _rows_to_columns(rows):
    n = rows[0].shape[1]
    sub = lax.broadcasted_iota(jnp.int32, (128, n), 0)
    block = jnp.zeros((128, n), F32)
    for k, row in enumerate(rows):
        block = jnp.where(sub == k, row, block)
    return block.T


def _first_max(vals):
    best = vals[0]
    for v in vals[1:]:
        best = jnp.maximum(best, v)
    idx = jnp.full(best.shape, len(vals) - 1, jnp.int32)
    for j in range(len(vals) - 2, -1, -1):
        idx = jnp.where(vals[j] == best, j, idx)
    return best, idx


def _softmax_rows(vals):
    top = vals[0]
    for v in vals[1:]:
        top = jnp.maximum(top, v)
    ex = [jnp.exp(v - top) for v in vals]
    tot = ex[0]
    for e in ex[1:]:
        tot = tot + e
    return [e / tot for e in ex]


def _post_kernel(ya_ref, yb_ref, yc_ref, x_ref, mod_ref, ga_ref, gb_ref, gc_ref, wo_ref, nf_ref,
                 wr_hi_ref, wr_lo_ref, br_ref, xo_ref, h2_ref, ri_ref, rw_ref, cnt_ref):
    ya = _rms_normalize(ya_ref[0].astype(F32)) * ga_ref[...]
    yb = _rms_normalize(yb_ref[0].astype(F32)) * gb_ref[...]
    yc = _rms_normalize(yc_ref[0].astype(F32)) * gc_ref[...]
    y = jnp.concatenate([ya, yb, yc], axis=-1).astype(BF16)
    x = x_ref[0] + mod_ref[0, 2:3, :] * _dot(y, wo_ref[...])
    xo_ref[0] = x
    h = (_rms_normalize(x) * nf_ref[...]) * (1.0 + mod_ref[0, 4:5, :]) + mod_ref[0, 3:4, :]
    h_hi, h_lo = _split_bf16(h)
    h2_ref[0] = h_hi

    w_hi = wr_hi_ref[...]
    logits = (_dot_nt(w_hi, h_hi) + _dot_nt(w_hi, h_lo)) + _dot_nt(wr_lo_ref[...], h_hi) + br_ref[...]
    rows = [logits[r:r + 1, :] for r in range(8 + N_EXPERTS)]
    group_prob = _softmax_rows(rows[0:N_GROUPS])
    group_p, group_idx = _first_max(group_prob)
    in_group = []
    for j in range(EXPERTS_PER_GROUP):
        v = rows[8 + (N_GROUPS - 1) * EXPERTS_PER_GROUP + j]
        for g in range(N_GROUPS - 2, -1, -1):
            v = jnp.where(group_idx == g, rows[8 + g * EXPERTS_PER_GROUP + j], v)
        in_group.append(v)
    expert_prob = _softmax_rows(in_group)
    p1, i1 = _first_max(expert_prob)
    rest = [jnp.where(i1 == j, -1.0, expert_prob[j]) for j in range(EXPERTS_PER_GROUP)]
    p2, i2 = _first_max(rest)
    e1 = group_idx * EXPERTS_PER_GROUP + i1
    e2 = group_idx * EXPERTS_PER_GROUP + i2
    rw_ref[0] = _rows_to_columns([group_p * p1 / (p1 + p2), group_p * p2 / (p1 + p2)])
    ts = ri_ref.shape[3]
    erow = lax.broadcasted_iota(jnp.int32, (N_EXPERTS, ts), 0)
    for part in range(e1.shape[1] // ts):
        e1p = e1[:, part * ts:(part + 1) * ts]
        e2p = e2[:, part * ts:(part + 1) * ts]
        ri_ref[0, part] = jnp.zeros((8, ts), jnp.int32)
        ri_ref[0, part, 0:1, :] = e1p
        ri_ref[0, part, 1:2, :] = e2p
        sel = jnp.where((erow == e1p) | (erow == e2p), 1.0, 0.0).astype(BF16)
        cnt_ref[0, part] = _dot(sel, jnp.ones((ts, 128), BF16))


def _post_mixer(ya, yb, yc, x, mod, gain_a, gain_b, gain_c, w_out, norm_ffn, wr_hi, wr_lo, br):
    b, s, d = x.shape
    tm = TM_POST
    ts = TMS
    nsort = tm // ts
    tok = lambda width: pl.BlockSpec((1, tm, width), lambda bi, i: (bi, i, 0))
    full = lambda shape: pl.BlockSpec(shape, lambda bi, i: (0,) * len(shape))
    return pl.pallas_call(
        _post_kernel,
        grid=(b, s // tm),
        in_specs=[
            tok(SWA_WIDTH), tok(LRU_WIDTH), tok(FOX_WIDTH), tok(d),
            pl.BlockSpec((1, 6, d), lambda bi, i: (bi, 0, 0)),
            full((1, SWA_WIDTH)), full((1, LRU_WIDTH)), full((1, FOX_WIDTH)),
            full((d, d)), full((1, d)),
            full((ROUTER_ROWS, d)), full((ROUTER_ROWS, d)), full((ROUTER_ROWS, 1)),
        ],
        out_specs=[
            tok(d), tok(d),
            pl.BlockSpec((1, nsort, 8, ts), lambda bi, i: (bi, i, 0, 0)),
            tok(128),
            pl.BlockSpec((1, nsort, N_EXPERTS, 128), lambda bi, i: (bi, i, 0, 0)),
        ],
        out_shape=[
            jax.ShapeDtypeStruct((b, s, d), F32),
            jax.ShapeDtypeStruct((b, s, d), BF16),
            jax.ShapeDtypeStruct((b, s // ts, 8, ts), jnp.int32),
            jax.ShapeDtypeStruct((b, s, 128), F32),
            jax.ShapeDtypeStruct((b, s // ts, N_EXPERTS, 128), F32),
        ],
        compiler_params=_params("parallel", "parallel"),
        name="post_mixer",
    )(ya, yb, yc, x, mod, gain_a, gain_b, gain_c, w_out, norm_ffn, wr_hi, wr_lo, br)


TMS = 512
SEG = 16
SORT_ROWS = 1280
ET = 1024
SEG_PIECES = (512, 256, 128, 64, 32, 16)
TAIL_PIECES = (512, 256, 128, 64, 32, 16)
TOTAL_PIECES = (1024, 512, 256, 128, 64, 32, 16)
RARE_ROWS = 128
assert SORT_ROWS >= 2 * TMS + N_EXPERTS * (SEG - 1) and SEG_PIECES[0] == TMS and TAIL_PIECES[0] * 2 == ET


def _for_each_piece(n16_ref, lo_ref, goff_ref, tile, fn):
    for e in range(N_EXPERTS):
        n = n16_ref[tile * N_EXPERTS + e]
        lo = lo_ref[tile * N_EXPERTS + e]
        go = goff_ref[tile * N_EXPERTS + e]

        def pieces(sizes, n=n, lo=lo, go=go):
            for rows in sizes:
                done = n & (-2 * rows)

                @pl.when((n & rows) != 0)
                def _(done=done, rows=rows):
                    fn(pl.multiple_of(lo + done, SEG), pl.multiple_of(go + done, SEG), rows)

        @pl.when(n >= RARE_ROWS)
        def _():
            pieces([r for r in SEG_PIECES if r >= RARE_ROWS])

        pieces([r for r in SEG_PIECES if r < RARE_ROWS])


def _for_each_total_piece(total_ref, tile, fn):
    n = total_ref[tile]
    for rows in TOTAL_PIECES:
        @pl.when((n & rows) != 0)
        def _(rows=rows):
            fn(rows)


def _dispatch_kernel(n16_ref, lo_ref, goff_ref, total_ref, tail_row_ref, tail_n_ref, n_valid_ref,
                     h_ref, ri_ref, tri_ref, xs_ref, dest_ref, sorted_ref, zero_ref, sem):
    i = pl.program_id(0)
    last = pl.num_programs(0) - 1
    slot = i & 1
    tms = h_ref.shape[0]

    ri = ri_ref[0]
    e1 = ri[0:1, :]
    e2 = ri[1:2, :]
    erow = lax.broadcasted_iota(jnp.int32, (N_EXPERTS, tms), 0)
    m1 = erow == e1
    m2 = erow == e2
    sel = jnp.where(m1 | m2, 1.0, 0.0).astype(BF16)
    rank = _dot(sel, tri_ref[...]).astype(jnp.int32)
    start = jnp.zeros((N_EXPERTS, tms), jnp.int32)
    for e in range(N_EXPERTS):
        start = jnp.where(erow == e, lo_ref[i * N_EXPERTS + e], start)
    dest = rank + start
    r1 = jnp.sum(jnp.where(m1, dest, 0), axis=0, keepdims=True)
    r2 = jnp.sum(jnp.where(m2, dest, 0), axis=0, keepdims=True)
    dest_ref[...] = _rows_to_columns([r1.astype(F32), r2.astype(F32)])

    rows = lax.broadcasted_iota(jnp.int32, (SORT_ROWS, tms), 0)
    perm = jnp.where((rows == r1) | (rows == r2), 1.0, 0.0).astype(BF16)
    sorted_ref[slot] = _dot(perm, h_ref[...]).astype(BF16)

    def copy(src_slot, local_row, global_row, n):
        return pltpu.make_async_copy(sorted_ref.at[src_slot, pl.ds(local_row, n)],
                                     xs_ref.at[pl.ds(global_row, n)], sem)

    @pl.when(i > 0)
    def _():
        _for_each_total_piece(total_ref, i - 1, lambda n: copy(1 - slot, 0, 0, n).wait())

    _for_each_piece(n16_ref, lo_ref, goff_ref, i, lambda lr, gr, n: copy(slot, lr, gr, n).start())

    @pl.when(i == last)
    def _():
        _for_each_total_piece(total_ref, i, lambda n: copy(slot, 0, 0, n).wait())
        zero_ref[...] = jnp.zeros_like(zero_ref)

        def tail_copies(fn):
            for e in range(N_EXPERTS):
                n = tail_n_ref[e]
                row = tail_row_ref[e]
                for rows_ in TAIL_PIECES:
                    done = n & (-2 * rows_)

                    @pl.when((n & rows_) != 0)
                    def _(row=row, done=done, rows_=rows_):
                        fn(pltpu.make_async_copy(
                            zero_ref.at[pl.ds(0, rows_)],
                            xs_ref.at[pl.ds(pl.multiple_of(row + done, SEG), rows_)], sem))

        tail_copies(lambda c: c.start())
        tail_copies(lambda c: c.wait())

        def unused_tiles(fn):
            def body(j, carry):
                for part in range(ET // TAIL_PIECES[0]):
                    row = pl.multiple_of(j * ET + part * TAIL_PIECES[0], SEG)
                    fn(pltpu.make_async_copy(zero_ref, xs_ref.at[pl.ds(row, TAIL_PIECES[0])], sem))
                return carry
            lax.fori_loop(n_valid_ref[0], xs_ref.shape[0] // ET, body, 0)

        unused_tiles(lambda c: c.start())
        unused_tiles(lambda c: c.wait())


def _dispatch(h2, route_i, tri, tables, rows_max):
    t, d = h2.shape
    nt = t // TMS
    grid_spec = pltpu.PrefetchScalarGridSpec(
        num_scalar_prefetch=7,
        grid=(nt,),
        in_specs=[
            pl.BlockSpec((TMS, d), lambda i, *_: (i, 0)),
            pl.BlockSpec((1, 8, TMS), lambda i, *_: (i, 0, 0)),
            pl.BlockSpec((TMS, TMS), lambda i, *_: (0, 0)),
        ],
        out_specs=[
            pl.BlockSpec(memory_space=pl.ANY),
            pl.BlockSpec((TMS, 128), lambda i, *_: (i, 0)),
        ],
        scratch_shapes=[
            pltpu.VMEM((2, SORT_ROWS, d), BF16),
            pltpu.VMEM((TAIL_PIECES[0], d), BF16),
            pltpu.SemaphoreType.DMA,
        ],
    )
    return pl.pallas_call(
        _dispatch_kernel,
        grid_spec=grid_spec,
        out_shape=[
            jax.ShapeDtypeStruct((rows_max, d), BF16),
            jax.ShapeDtypeStruct((t, 128), F32),
        ],
        compiler_params=_params("arbitrary"),
        name="moe_dispatch",
    )(*tables, h2, route_i, tri)


def _experts_kernel(te_ref, nv_ref, xs_ref, wg_ref, wu_ref, wd_ref, ys_ref, wg_s, wu_s, wd_s):
    j = pl.program_id(0)

    @pl.when((j == 0) | (te_ref[j] != te_ref[jnp.maximum(j - 1, 0)]))
    def _():
        wg_s[...] = wg_ref[0].astype(BF16)
        wu_s[...] = wu_ref[0].astype(BF16)
        wd_s[...] = wd_ref[0].astype(BF16)

    @pl.when(j < nv_ref[0])
    def _():
        h = xs_ref[...]
        gate = _dot(h, wg_s[...])
        hidden = (gate * jax.nn.sigmoid(gate)) * _dot(h, wu_s[...])
        ys_ref[...] = _dot(hidden.astype(BF16), wd_s[...]).astype(BF16)

    @pl.when(j >= nv_ref[0])
    def _():
        ys_ref[...] = jnp.zeros_like(ys_ref)


def _experts(xs, tile_expert, n_valid, w_gate, w_up, w_down, layer):
    rows, d = xs.shape
    row_tile = lambda j, te, nv: (jnp.minimum(j, nv[0] - 1), 0)
    weight = lambda j, te, nv: (layer, te[j], 0, 0)
    grid_spec = pltpu.PrefetchScalarGridSpec(
        num_scalar_prefetch=2,
        grid=(rows // ET,),
        in_specs=[
            pl.BlockSpec((ET, d), row_tile),
            pl.BlockSpec((None, 1, d, D_EXPERT), weight),
            pl.BlockSpec((None, 1, d, D_EXPERT), weight),
            pl.BlockSpec((None, 1, D_EXPERT, d), weight),
        ],
        out_specs=pl.BlockSpec((ET, d), lambda j, te, nv: (j, 0)),
        scratch_shapes=[pltpu.VMEM((d, D_EXPERT), BF16), pltpu.VMEM((d, D_EXPERT), BF16),
                        pltpu.VMEM((D_EXPERT, d), BF16)],
    )
    return pl.pallas_call(
        _experts_kernel,
        grid_spec=grid_spec,
        out_shape=jax.ShapeDtypeStruct((rows, d), BF16),
        compiler_params=_params("arbitrary"),
        name="moe_experts",
    )(tile_expert, n_valid, xs, w_gate, w_up, w_down)


def _combine_kernel(n16_ref, lo_ref, goff_ref, total_ref, ys_ref, dest_ref, w_ref, x_ref, mod_ref,
                    nf_ref, o_ref, buf_ref, sems, *, final):
    i = pl.program_id(0)
    nt = pl.num_programs(0)
    slot = i & 1

    def copy(dst_slot, local_row, global_row, n):
        return pltpu.make_async_copy(ys_ref.at[pl.ds(global_row, n)],
                                     buf_ref.at[dst_slot, pl.ds(local_row, n)], sems.at[dst_slot])

    @pl.when(i == 0)
    def _():
        buf_ref[...] = jnp.zeros_like(buf_ref)
        _for_each_piece(n16_ref, lo_ref, goff_ref, 0, lambda lr, gr, n: copy(0, lr, gr, n).start())

    @pl.when(i + 1 < nt)
    def _():
        _for_each_piece(n16_ref, lo_ref, goff_ref, i + 1,
                        lambda lr, gr, n: copy(1 - slot, lr, gr, n).start())

    _for_each_total_piece(total_ref, i, lambda n: copy(slot, 0, 0, n).wait())

    dest = dest_ref[:, 0:2].astype(jnp.int32)
    w = w_ref[:, 0:2]
    col = lax.broadcasted_iota(jnp.int32, (dest.shape[0], SORT_ROWS), 1)
    wmat = (jnp.where(col == dest[:, 0:1], w[:, 0:1], 0.0)
            + jnp.where(col == dest[:, 1:2], w[:, 1:2], 0.0))
    y = _dot(wmat.astype(BF16), buf_ref[slot])
    x = x_ref[...] + mod_ref[0, 5:6, :] * y
    if final:
        x = _rms_normalize(x) * nf_ref[...]
    o_ref[...] = x


def _combine(ys, dest_col, w_col, x, mod, norm_final, tables, tiles_per_batch, final):
    t, d = x.shape
    nt = t // TMS
    grid_spec = pltpu.PrefetchScalarGridSpec(
        num_scalar_prefetch=4,
        grid=(nt,),
        in_specs=[
            pl.BlockSpec(memory_space=pl.ANY),
            pl.BlockSpec((TMS, 128), lambda i, *_: (i, 0)),
            pl.BlockSpec((TMS, 128), lambda i, *_: (i, 0)),
            pl.BlockSpec((TMS, d), lambda i, *_: (i, 0)),
            pl.BlockSpec((1, 6, d), lambda i, *_: (i // tiles_per_batch, 0, 0)),
            pl.BlockSpec((1, d), lambda i, *_: (0, 0)),
        ],
        out_specs=pl.BlockSpec((TMS, d), lambda i, *_: (i, 0)),
        scratch_shapes=[pltpu.VMEM((2, SORT_ROWS, d), BF16), pltpu.SemaphoreType.DMA((2,))],
    )
    return pl.pallas_call(
        functools.partial(_combine_kernel, final=final),
        grid_spec=grid_spec,
        out_shape=jax.ShapeDtypeStruct((t, d), F32),
        compiler_params=_params("arbitrary"),
        name="moe_combine_final" if final else "moe_combine",
    )(*tables, ys, dest_col, w_col, x, mod, norm_final)


def _route_tables(counts, n_expert_tiles):
    n16 = (counts + (SEG - 1)) // SEG * SEG
    lo = jnp.cumsum(n16, axis=1) - n16
    total = jnp.sum(n16, axis=0)
    region = (total + (ET - 1)) // ET * ET
    ends = jnp.cumsum(region)
    base = ends - region
    goff = base[None, :] + jnp.cumsum(n16, axis=0) - n16
    tile_row = jnp.arange(n_expert_tiles, dtype=jnp.int32) * ET
    tile_expert = jnp.minimum(jnp.sum(tile_row[:, None] >= ends[None, :], axis=1), N_EXPERTS - 1)
    n_valid = (ends[-1] // ET).reshape(1)
    i32 = lambda a: a.astype(jnp.int32).reshape(-1)
    return (i32(n16), i32(lo), i32(goff), i32(jnp.sum(n16, axis=1))), \
        (i32(base + total), i32(region - total)), i32(tile_expert), i32(n_valid)


def _moe(h2, route_i, route_w, cnt, w_gate, w_up, w_down, layer, x, mod, norm_final, final):
    b, s, d = x.shape
    t = b * s
    nt = t // TMS
    n_expert_tiles = -(-(2 * t + nt * N_EXPERTS * (SEG - 1) + N_EXPERTS * (ET - 1)) // ET)
    counts = jnp.round(cnt[:, :, :, 0]).astype(jnp.int32).reshape(nt, N_EXPERTS)
    seg_tables, tail_tables, tile_expert, n_valid = _route_tables(counts, n_expert_tiles)
    tri = jnp.triu(jnp.ones((TMS, TMS), BF16), k=1)
    xs, dest = _dispatch(h2.reshape(t, d), route_i.reshape(nt, 8, TMS), tri,
                         seg_tables + tail_tables + (n_valid,), n_expert_tiles * ET)
    ys = _experts(xs, tile_expert, n_valid, w_gate, w_up, w_down, layer)
    out = _combine(ys, dest, route_w.reshape(t, 128), x.reshape(t, d), mod, norm_final, seg_tables,
                   s // TMS, final)
    return out.reshape(b, s, d)


def _block_diag(w):
    nb, c, _ = w.shape
    eye = jnp.eye(nb, dtype=w.dtype)
    return (eye[:, None, :, None] * w[:, :, None, :]).reshape(nb * c, nb * c)


def _pair_swa_heads(w, axis):
    half = SWA_Q_HEADS // 2
    shape = w.shape
    w = w.reshape(shape[:axis] + (2, half, HEAD_DIM) + shape[axis + 1:])
    return jnp.swapaxes(w, axis, axis + 1).reshape(shape)


def kernel(x, c, w_mod, b_mod, norm_mix, norm_ffn, w_in, w_out, out_gain, sinks, conv_w, conv_b,
           lru_wa, lru_ba, lru_wx, lru_bx, lru_lam, fox_bf, w_router_group, b_router_group,
           w_router_expert, b_router_expert, w_gate, w_up, w_down, norm_final):
    depth = w_mod.shape[0]
    b, s, d = x.shape
    assert d == D_MODEL and s % max(TM_PRE, TM_PREP, TQ_SWA, TS_LRU, T_FOX, TM_POST) == 0
    assert TM_PRE % T_FOX == 0
    c_rows = jnp.zeros((16, d), F32).at[:b].set(c)
    mod_all = _modulation(c_rows, w_mod, b_mod)[:, :b]

    for l in range(depth):
        mod = mod_all[l].reshape(b, 6, d)
        w_in_l = w_in[l]
        w_tok = jnp.concatenate([_pair_swa_heads(w_in_l[:, :SWA_WIDTH], 1), w_in_l[:, SWA_WIDTH:OFF_QC],
                                 w_in_l[:, OFF_KC:OFF_VC]], axis=1).astype(BF16)
        w_t = jnp.concatenate([w_in_l[:, OFF_QC:OFF_KC].T, w_in_l[:, OFF_VC:OFF_FC].T,
                               jnp.zeros((FORGET_ROWS, d), F32).at[:FOX_HEADS].set(w_in_l[:, OFF_FC:].T)],
                              axis=0).astype(BF16)
        fb = jnp.zeros((FORGET_ROWS, 1), F32).at[:FOX_HEADS, 0].set(fox_bf[l])
        gain = out_gain[l]
        gain_a = _pair_swa_heads(gain[:SWA_WIDTH], 0).reshape(1, SWA_WIDTH)
        gain_b = gain[SWA_WIDTH:SWA_WIDTH + LRU_WIDTH].reshape(1, LRU_WIDTH)
        gain_c = gain[SWA_WIDTH + LRU_WIDTH:].reshape(1, FOX_WIDTH)
        w_out_l = jnp.concatenate([_pair_swa_heads(w_out[l][:SWA_WIDTH], 0), w_out[l][SWA_WIDTH:]],
                                  axis=0).astype(BF16)
        wr = jnp.zeros((ROUTER_ROWS, d), F32)
        wr = wr.at[:N_GROUPS].set(w_router_group[l].T).at[8:8 + N_EXPERTS].set(w_router_expert[l].T)
        wr_hi = wr.astype(BF16)
        wr_lo = (wr - wr_hi.astype(F32)).astype(BF16)
        br = jnp.zeros((ROUTER_ROWS, 1), F32)
        br = br.at[:N_GROUPS, 0].set(b_router_group[l]).at[8:8 + N_EXPERTS, 0].set(b_router_expert[l])

        qa, ka, va, xb, gb, kc, qct, vaug, lf = _pre_mixer(
            x, mod, norm_mix[l].reshape(1, d), w_tok, w_t, fb)
        cum, cum_col = _cumsum(lf)
        kaug = _fox_prep(kc, cum_col)
        base = cum[:, :FOX_HEADS, ::T_FOX].reshape(-1)
        ya = _swa(qa, ka, va, sinks[l].reshape(1, SWA_Q_HEADS))
        yb = _lru(xb, gb, conv_w[l], conv_b[l].reshape(1, -1),
                  _block_diag(lru_wa[l]).astype(BF16), lru_ba[l].reshape(1, -1),
                  _block_diag(lru_wx[l]).astype(BF16), lru_bx[l].reshape(1, -1),
                  lru_lam[l].reshape(1, -1))
        yc = _fox(qct, kaug, vaug, base)
        x, h2, route_i, route_w, cnt = _post_mixer(
            ya, yb, yc, x, mod, gain_a, gain_b, gain_c, w_out_l,
            norm_ffn[l].reshape(1, d), wr_hi, wr_lo, br)
        x = _moe(h2, route_i, route_w, cnt, w_gate, w_up, w_down, l, x, mod,
                 norm_final.reshape(1, d), final=(l == depth - 1))
    return x
```

```python
import functools

import jax
import jax.numpy as jnp
from jax import lax
from jax.experimental import pallas as pl
from jax.experimental.pallas import tpu as pltpu

F32 = jnp.float32
BF16 = jnp.bfloat16

D_MODEL = 1024
HEAD_DIM = 64
PAIR = 2 * HEAD_DIM
BLOCK = 128
SWA_Q_HEADS = 8
SWA_WIDTH = SWA_Q_HEADS * HEAD_DIM
KV_WIDTH = 2 * HEAD_DIM
LRU_WIDTH = 256
LRU_BLOCKS = 8
CONV_WIDTH = 4
LRU_C = 8.0
FOX_HEADS = 4
FOX_WIDTH = FOX_HEADS * HEAD_DIM
N_GROUPS = 4
EXPERTS_PER_GROUP = 4
N_EXPERTS = 16
D_EXPERT = 256
EPS = 1e-6
NEG = -1e30
LOG2E = 1.4426950408889634

OFF_KA = SWA_WIDTH
OFF_VA = OFF_KA + KV_WIDTH
OFF_XB = OFF_VA + KV_WIDTH
OFF_GB = OFF_XB + LRU_WIDTH
OFF_QC = OFF_GB + LRU_WIDTH
OFF_KC = OFF_QC + FOX_WIDTH
OFF_VC = OFF_KC + FOX_WIDTH
OFF_FC = OFF_VC + FOX_WIDTH

ROUTER_ROWS = 32
FORGET_ROWS = 16
V_ROWS = HEAD_DIM + 16
VMEM_LIMIT_BYTES = 48 * 1024 * 1024

TM_PRE = 1024
TM_PREP = 2048
TQ_SWA = 1024
TS_LRU = 1024
T_FOX = 512
TM_POST = 1024
CUMSUM_CHUNK = 256


def _params(*semantics):
    return pltpu.CompilerParams(dimension_semantics=semantics, vmem_limit_bytes=VMEM_LIMIT_BYTES)


def _split_bf16(v):
    hi = v.astype(BF16)
    lo = (v - hi.astype(F32)).astype(BF16)
    return hi, lo


def _dot(a, b):
    return jnp.dot(a, b, preferred_element_type=F32)


def _dot_nt(a, b):
    return lax.dot_general(a, b, (((1,), (1,)), ((), ())), preferred_element_type=F32)


def _rms_normalize(v):
    return v * lax.rsqrt(jnp.mean(v * v, axis=-1, keepdims=True) + EPS)


def _mod_kernel(c_ref, w_ref, b_ref, o_ref):
    c = c_ref[...]
    ca = c * jax.nn.sigmoid(c)
    w = w_ref[0]
    a_hi, a_lo = _split_bf16(ca)
    w_hi, w_lo = _split_bf16(w)
    o_ref[0] = _dot(a_hi, w_hi) + _dot(a_lo, w_hi) + _dot(a_hi, w_lo) + b_ref[0]


def _modulation(c, w_mod, b_mod):
    depth, d, d6 = w_mod.shape
    b = c.shape[0]
    n = d6 // d
    return pl.pallas_call(
        _mod_kernel,
        grid=(depth, n),
        in_specs=[
            pl.BlockSpec((b, d), lambda l, j: (0, 0)),
            pl.BlockSpec((1, d, d), lambda l, j: (l, 0, j)),
            pl.BlockSpec((1, 1, d), lambda l, j: (l, 0, j)),
        ],
        out_specs=pl.BlockSpec((1, b, d), lambda l, j: (l, 0, j)),
        out_shape=jax.ShapeDtypeStruct((depth, b, d6), F32),
        compiler_params=_params("arbitrary", "arbitrary"),
        name="modulation",
    )(c, w_mod, b_mod.reshape(depth, 1, d6))


def _log_sigmoid(z):
    return jnp.minimum(z, 0.0) - jnp.log1p(jnp.exp(-jnp.abs(z)))


def _other_half(a):
    return HEAD_DIM * (1 - a)


def _pre_kernel(x_ref, mod_ref, g_ref, w_ref, wt_ref, fb_ref,
                qa_ref, ka_ref, va_ref, xb_ref, gb_ref, kc_ref, qct_ref, vct_ref, lf_ref):
    x = x_ref[0]
    h = (_rms_normalize(x) * g_ref[...]) * (1.0 + mod_ref[0, 1:2, :]) + mod_ref[0, 0:1, :]
    hb = h.astype(BF16)

    def proj(lo, width):
        return _dot(hb, w_ref[:, lo:lo + width])

    scale = HEAD_DIM ** -0.5
    qa_ref[0] = (proj(0, SWA_WIDTH) * (scale * LOG2E)).astype(BF16)
    ka_ref[0] = proj(OFF_KA, KV_WIDTH).astype(BF16)
    va_ref[0] = proj(OFF_VA, KV_WIDTH).astype(BF16)
    xb_ref[0] = proj(OFF_XB, LRU_WIDTH).astype(BF16)
    gb_ref[0] = proj(OFF_GB, LRU_WIDTH).astype(BF16)
    kc_ref[0] = proj(OFF_QC, FOX_WIDTH).astype(BF16)

    t_all = _dot_nt(wt_ref[...], hb)
    tf = qct_ref.shape[3]
    ones_row = jnp.where(lax.broadcasted_iota(jnp.int32, (V_ROWS - HEAD_DIM, tf), 0) == 0, 1.0, 0.0)
    for part in range(t_all.shape[1] // tf):
        cols = slice(part * tf, (part + 1) * tf)
        qct_ref[0, part] = (t_all[0:FOX_WIDTH, cols] * (scale * LOG2E)).astype(BF16)
        for hd in range(FOX_HEADS):
            v_head = t_all[FOX_WIDTH + hd * HEAD_DIM:FOX_WIDTH + (hd + 1) * HEAD_DIM, cols]
            vct_ref[0, part, hd] = jnp.concatenate([v_head, ones_row], axis=0).astype(BF16)
    lf_ref[0] = _log_sigmoid(t_all[2 * FOX_WIDTH:] + fb_ref[...])


def _pre_mixer(x, mod, gain, w_tok, w_t, fb):
    b, s, d = x.shape
    tm = TM_PRE
    tf = T_FOX
    nf = tm // tf

    def tok(width):
        return pl.BlockSpec((1, tm, width), lambda bi, i: (bi, i, 0))

    def tok_shape(width):
        return jax.ShapeDtypeStruct((b, s, width), BF16)

    return pl.pallas_call(
        _pre_kernel,
        grid=(b, s // tm),
        in_specs=[
            tok(d),
            pl.BlockSpec((1, 6, d), lambda bi, i: (bi, 0, 0)),
            pl.BlockSpec((1, d), lambda bi, i: (0, 0)),
            pl.BlockSpec(w_tok.shape, lambda bi, i: (0, 0)),
            pl.BlockSpec(w_t.shape, lambda bi, i: (0, 0)),
            pl.BlockSpec((FORGET_ROWS, 1), lambda bi, i: (0, 0)),
        ],
        out_specs=[
            tok(SWA_WIDTH), tok(KV_WIDTH), tok(KV_WIDTH), tok(LRU_WIDTH), tok(LRU_WIDTH),
            tok(FOX_WIDTH),
            pl.BlockSpec((1, nf, FOX_WIDTH, tf), lambda bi, i: (bi, i, 0, 0)),
            pl.BlockSpec((1, nf, FOX_HEADS, V_ROWS, tf), lambda bi, i: (bi, i, 0, 0, 0)),
            pl.BlockSpec((1, FORGET_ROWS, tm), lambda bi, i: (bi, 0, i)),
        ],
        out_shape=[
            tok_shape(SWA_WIDTH), tok_shape(KV_WIDTH), tok_shape(KV_WIDTH), tok_shape(LRU_WIDTH),
            tok_shape(LRU_WIDTH), tok_shape(FOX_WIDTH),
            jax.ShapeDtypeStruct((b, s // tf, FOX_WIDTH, tf), BF16),
            jax.ShapeDtypeStruct((b, s // tf, FOX_HEADS, V_ROWS, tf), BF16),
            jax.ShapeDtypeStruct((b, FORGET_ROWS, s), F32),
        ],
        compiler_params=_params("parallel", "parallel"),
        name="pre_mixer",
    )(x, mod, gain, w_tok, w_t, fb)


def _cumsum_kernel(lf_ref, o_ref, col_ref):
    n = CUMSUM_CHUNK
    rows = lf_ref.shape[1]
    s = lf_ref.shape[2]
    r = lax.broadcasted_iota(jnp.int32, (n, n), 0)
    c = lax.broadcasted_iota(jnp.int32, (n, n), 1)
    tri = jnp.where(r <= c, 1.0, 0.0).astype(BF16)
    carry = jnp.zeros((rows, 1), F32)
    for j in range(s // n):
        seg = lf_ref[0, :, j * n:(j + 1) * n]
        hi = seg.astype(BF16)
        rem = seg - hi.astype(F32)
        mid = rem.astype(BF16)
        lo = (rem - mid.astype(F32)).astype(BF16)
        cs = (_dot(hi, tri) + _dot(mid, tri)) + _dot(lo, tri) + carry
        o_ref[0, :, j * n:(j + 1) * n] = cs
        parts = _split3(cs[0:FOX_HEADS] * LOG2E)
        col_ref[0, j * n:(j + 1) * n, :] = _rows_to_columns(
            [parts[k][hd:hd + 1] for hd in range(FOX_HEADS) for k in range(3)])
        carry = cs[:, n - 1:n]


def _cumsum(lf):
    b, r, s = lf.shape
    return pl.pallas_call(
        _cumsum_kernel,
        grid=(b,),
        in_specs=[pl.BlockSpec((1, r, s), lambda bi: (bi, 0, 0))],
        out_specs=[pl.BlockSpec((1, r, s), lambda bi: (bi, 0, 0)),
                   pl.BlockSpec((1, s, 128), lambda bi: (bi, 0, 0))],
        out_shape=[jax.ShapeDtypeStruct((b, r, s), F32), jax.ShapeDtypeStruct((b, s, 128), F32)],
        compiler_params=_params("parallel"),
        name="cumsum",
    )(lf)


def _swa_kernel(sink_ref, q_ref, kc_ref, vc_ref, kp_ref, vp_ref, o_ref, kf_ref, vf_ref, bias_ref):
    i = pl.program_id(1)
    nsub = q_ref.shape[1] // BLOCK
    npair = SWA_Q_HEADS // 2

    @pl.when((pl.program_id(0) == 0) & (i == 0))
    def _():
        qpos = BLOCK + lax.broadcasted_iota(jnp.int32, (BLOCK, 2 * BLOCK), 0)
        kpos = lax.broadcasted_iota(jnp.int32, (BLOCK, 2 * BLOCK), 1)
        dist = qpos - kpos
        valid = (dist >= 0) & (dist < BLOCK)
        distf = dist.astype(F32)
        for hd in range(SWA_Q_HEADS):
            slope = 2.0 ** (-8.0 * (hd + 1) / SWA_Q_HEADS)
            bias_ref[hd] = jnp.where(valid, (-slope * LOG2E) * distf, NEG)

    kf_ref[0:BLOCK, :] = kp_ref[0]
    kf_ref[BLOCK:, :] = kc_ref[0]
    vf_ref[0:BLOCK, :] = vp_ref[0]
    vf_ref[BLOCK:, :] = vc_ref[0]

    lane = lax.broadcasted_iota(jnp.int32, (2 * BLOCK, PAIR), 1)
    lane_o = lax.broadcasted_iota(jnp.int32, (BLOCK, PAIR), 1)
    col = lax.broadcasted_iota(jnp.int32, (BLOCK, 2 * BLOCK), 1)

    for j in range(nsub):
        k2 = kf_ref[j * BLOCK:(j + 2) * BLOCK, :]
        v2 = vf_ref[j * BLOCK:(j + 2) * BLOCK, :]
        k_half = (jnp.where(lane < HEAD_DIM, k2, jnp.zeros_like(k2)),
                  jnp.where(lane >= HEAD_DIM, k2, jnp.zeros_like(k2)))
        for p in range(npair):
            qp = q_ref[0, j * BLOCK:(j + 1) * BLOCK, p * PAIR:(p + 1) * PAIR]
            outs = []
            for a in range(2):
                hd = p + a * npair
                sc = _dot_nt(qp, k_half[a]) + bias_ref[hd]
                if j == 0:
                    sc = sc + jnp.where(col < BLOCK, jnp.where(i == 0, NEG, 0.0), 0.0)
                sink = sink_ref[0, hd] * LOG2E
                m = jnp.maximum(jnp.max(sc, axis=-1, keepdims=True), sink)
                pr = jnp.exp2(sc - m)
                den = jnp.sum(pr, axis=-1, keepdims=True) + jnp.exp2(sink - m)
                outs.append(_dot(pr.astype(BF16), v2) / den)
            o_ref[0, j * BLOCK:(j + 1) * BLOCK, p * PAIR:(p + 1) * PAIR] = jnp.where(
                lane_o < HEAD_DIM, outs[0], outs[1]).astype(BF16)


def _swa(qa, ka, va, sinks):
    b, s, _ = qa.shape
    tq = TQ_SWA
    nsub = tq // BLOCK
    prev = lambda bi, i: (bi, jnp.maximum(i * nsub - 1, 0), 0)
    return pl.pallas_call(
        _swa_kernel,
        grid=(b, s // tq),
        in_specs=[
            pl.BlockSpec(memory_space=pltpu.SMEM),
            pl.BlockSpec((1, tq, SWA_WIDTH), lambda bi, i: (bi, i, 0)),
            pl.BlockSpec((1, tq, KV_WIDTH), lambda bi, i: (bi, i, 0)),
            pl.BlockSpec((1, tq, KV_WIDTH), lambda bi, i: (bi, i, 0)),
            pl.BlockSpec((1, BLOCK, KV_WIDTH), prev),
            pl.BlockSpec((1, BLOCK, KV_WIDTH), prev),
        ],
        out_specs=pl.BlockSpec((1, tq, SWA_WIDTH), lambda bi, i: (bi, i, 0)),
        out_shape=jax.ShapeDtypeStruct((b, s, SWA_WIDTH), BF16),
        scratch_shapes=[
            pltpu.VMEM((tq + BLOCK, KV_WIDTH), BF16),
            pltpu.VMEM((tq + BLOCK, KV_WIDTH), BF16),
            pltpu.VMEM((SWA_Q_HEADS, BLOCK, 2 * BLOCK), F32),
        ],
        compiler_params=_params("arbitrary", "arbitrary"),
        name="swa",
    )(sinks, qa, ka, va, ka, va)


def _gelu_tanh(v):
    return 0.5 * v * (1.0 + jnp.tanh(0.7978845608028654 * (v + 0.044715 * (v * v * v))))


def _lru_kernel(xb_ref, gb_ref, cw_ref, cb_ref, wa_ref, ba_ref, wx_ref, bx_ref, lam_ref,
                o_ref, win_ref, h_ref):
    ts = xb_ref.shape[1]

    @pl.when(pl.program_id(1) == 0)
    def _():
        win_ref[0:8, :] = jnp.zeros((8, LRU_WIDTH), F32)
        h_ref[...] = jnp.zeros_like(h_ref)

    x = xb_ref[0].astype(F32)
    win_ref[8:, :] = x
    conv = cb_ref[...] + cw_ref[CONV_WIDTH - 1:CONV_WIDTH, :] * x
    for k in range(1, CONV_WIDTH):
        conv = conv + cw_ref[CONV_WIDTH - 1 - k:CONV_WIDTH - k, :] * win_ref[8 - k:8 - k + ts, :]
    win_ref[0:8, :] = x[ts - 8:ts]

    cbf = conv.astype(BF16)
    r = jax.nn.sigmoid(_dot(cbf, wa_ref[...]) + ba_ref[...])
    gi = jax.nn.sigmoid(_dot(cbf, wx_ref[...]) + bx_ref[...])
    z = -lam_ref[...]
    softplus = jnp.maximum(z, 0.0) + jnp.log1p(jnp.exp(-jnp.abs(z)))
    log_a = (-LRU_C * softplus) * r
    a = jnp.exp(log_a)
    v = 1.0 - a * a
    u = jnp.where(v == 0.0, 0.0, v * lax.rsqrt(v)) * (gi * conv)

    a = a.reshape(ts // 8, 8, LRU_WIDTH)
    u = u.reshape(ts // 8, 8, LRU_WIDTH)
    sub = lax.broadcasted_iota(jnp.int32, a.shape, 1)
    for step in (1, 2, 4):
        keep = sub >= step
        u = jnp.where(keep, a * pltpu.roll(u, step, 1) + u, u)
        a = jnp.where(keep, a * pltpu.roll(a, step, 1), a)
    carry = h_ref[7:8, :]
    groups = []
    for g in range(ts // 8):
        hg = u[g] + a[g] * carry
        groups.append(hg)
        carry = hg[7:8, :]
    h = jnp.concatenate(groups, axis=0)
    h_ref[...] = groups[-1]
    o_ref[0] = (h * _gelu_tanh(gb_ref[0].astype(F32))).astype(BF16)


def _lru(xb, gb, conv_w, conv_b, wa, ba, wx, bx, lam):
    b, s, w = xb.shape
    ts = TS_LRU
    tok = pl.BlockSpec((1, ts, w), lambda bi, i: (bi, i, 0))
    full = lambda shape: pl.BlockSpec(shape, lambda bi, i: (0,) * len(shape))
    return pl.pallas_call(
        _lru_kernel,
        grid=(b, s // ts),
        in_specs=[tok, tok, full((CONV_WIDTH, w)), full((1, w)), full((w, w)), full((1, w)),
                  full((w, w)), full((1, w)), full((1, w))],
        out_specs=tok,
        out_shape=jax.ShapeDtypeStruct((b, s, w), BF16),
        scratch_shapes=[pltpu.VMEM((8 + ts, w), F32), pltpu.VMEM((8, w), F32)],
        compiler_params=_params("arbitrary", "arbitrary"),
        name="lru",
    )(xb, gb, conv_w, conv_b, wa, ba, wx, bx, lam)


BIAS_ROWS = 6


def _split3(v):
    hi = v.astype(BF16).astype(F32)
    rem = v - hi
    mid = rem.astype(BF16).astype(F32)
    lo = (rem - mid).astype(BF16).astype(F32)
    return hi, mid, lo


def _fox_prep_kernel(k_ref, c_ref, o_ref):
    tm = k_ref.shape[1]
    lane = lax.broadcasted_iota(jnp.int32, (tm, PAIR), 1)
    parts = c_ref[0].astype(BF16)
    src = lax.broadcasted_iota(jnp.int32, (PAIR, PAIR), 0)
    dst = lax.broadcasted_iota(jnp.int32, (PAIR, PAIR), 1)
    for p in range(FOX_WIDTH // PAIR):
        kp = k_ref[0, :, p * PAIR:(p + 1) * PAIR].astype(F32)
        for a in range(2):
            hd = 2 * p + a
            ob = _other_half(a)
            part = src - 3 * hd
            place = jnp.where(part == dst - ob, jnp.where(part >= 0, jnp.where(part < 3, 1.0, 0.0), 0.0),
                              0.0).astype(BF16)
            bias = _dot(parts, place)
            rel = lane - ob
            aug = jnp.where(rel < 3, bias, jnp.where(rel < BIAS_ROWS, 1.0, 0.0))
            in_half = (lane >= HEAD_DIM * a) & (lane < HEAD_DIM * (a + 1))
            o_ref[0, hd] = jnp.where(in_half, kp, aug).astype(BF16)


def _fox_prep(kc, cum_col):
    b, s, w = kc.shape
    tm = TM_PREP
    return pl.pallas_call(
        _fox_prep_kernel,
        grid=(b, s // tm),
        in_specs=[
            pl.BlockSpec((1, tm, w), lambda bi, i: (bi, i, 0)),
            pl.BlockSpec((1, tm, cum_col.shape[2]), lambda bi, i: (bi, i, 0)),
        ],
        out_specs=pl.BlockSpec((1, FOX_HEADS, tm, PAIR), lambda bi, i: (bi, 0, i, 0)),
        out_shape=jax.ShapeDtypeStruct((b, FOX_HEADS, s, PAIR), BF16),
        compiler_params=_params("parallel", "parallel"),
        name="fox_prep",
    )(kc, cum_col)


def _fox_kernel(base_ref, qt_ref, ka_ref, vt_ref, o_ref, acc_ref, s0_ref, s1_ref):
    bi = pl.program_id(0)
    p = pl.program_id(1)
    qi = pl.program_id(2)
    nq = pl.num_programs(2)
    tk = qt_ref.shape[3]
    tq = 2 * tk
    qt = jnp.concatenate([qt_ref[0, 0], qt_ref[0, 1]], axis=1).astype(F32)
    row = lax.broadcasted_iota(jnp.int32, (PAIR, tq), 0)
    key_off = (lax.broadcasted_iota(jnp.int32, (tk, tq), 1)
               - lax.broadcasted_iota(jnp.int32, (tk, tq), 0))
    q_aug = []
    for a in range(2):
        hd = 2 * p + a
        base = jnp.full((1, tq), base_ref[(bi * FOX_HEADS + hd) * nq + qi], F32) * LOG2E
        b_hi, b_mid, b_lo = _split3(base)
        rel = row - _other_half(a)
        aug = jnp.where(rel < 3, -1.0, jnp.where(rel == 3, b_hi, jnp.where(
            rel == 4, b_mid, jnp.where(rel == 5, b_lo, 0.0))))
        in_half = (row >= HEAD_DIM * a) & (row < HEAD_DIM * (a + 1))
        q_aug.append(jnp.where(in_half, qt, aug).astype(BF16))
        acc_ref[a] = jnp.zeros((V_ROWS, tq), F32)

    slots = (s0_ref, s1_ref)

    def scores(ki, slot, diag=None):
        tile_max = []
        for a in range(2):
            k = ka_ref[0, a, pl.ds(pl.multiple_of(ki * tk, tk), tk), :]
            st = _dot(k, q_aug[a])
            if diag is not None:
                st = jnp.where(key_off >= diag * tk, st, NEG)
            slots[slot][a] = st
            tile_max.append(jnp.max(st, axis=0, keepdims=True))
        return tuple(tile_max)

    def consume(ki, slot, m_old, tile_max):
        m_new = []
        for a in range(2):
            m_a = jnp.maximum(m_old[a], tile_max[a])
            alpha = jnp.exp2(m_old[a] - m_a)
            pr = jnp.exp2(slots[slot][a] - m_a).astype(BF16)
            acc_ref[a] = alpha * acc_ref[a] + _dot(vt_ref[0, ki, a], pr)
            m_new.append(m_a)
        return tuple(m_new)

    def advance(j, slot, carry, diag=None):
        m_old, tile_max, prev = carry
        next_max = scores(j, 1 - slot, diag)
        return consume(prev, slot, m_old, tile_max), next_max, j

    def body(i, carry):
        return advance(2 * i + 1, 0, advance(2 * i, 1, carry))

    m0 = jnp.full((1, tq), NEG, F32)
    carry = ((m0, m0), scores(2 * qi, 0, diag=0), 2 * qi)
    carry = advance(2 * qi + 1, 0, carry, diag=1)
    m, tile_max, prev = lax.fori_loop(0, qi, body, carry)
    consume(prev, 1, m, tile_max)
    outs = []
    for a in range(2):
        acc = acc_ref[a]
        outs.append(acc[0:HEAD_DIM] / acc[HEAD_DIM:HEAD_DIM + 1, :])
    o_ref[0] = jnp.concatenate(outs, axis=0).T.astype(BF16)


def _fox(qct, kaug, vaug, base):
    b, nt, w, tk = qct.shape
    s = nt * tk
    tq = 2 * tk
    npair = w // PAIR
    return pl.pallas_call(
        _fox_kernel,
        grid=(b, npair, s // tq),
        in_specs=[
            pl.BlockSpec(memory_space=pltpu.SMEM),
            pl.BlockSpec((1, 2, PAIR, tk), lambda bi, p, qi: (bi, qi, p, 0)),
            pl.BlockSpec((1, 2, s, PAIR), lambda bi, p, qi: (bi, p, 0, 0)),
            pl.BlockSpec((1, nt, 2, V_ROWS, tk), lambda bi, p, qi: (bi, 0, p, 0, 0)),
        ],
        out_specs=pl.BlockSpec((1, tq, PAIR), lambda bi, p, qi: (bi, qi, p)),
        out_shape=jax.ShapeDtypeStruct((b, s, w), BF16),
        scratch_shapes=[pltpu.VMEM((2, V_ROWS, tq), F32), pltpu.VMEM((2, tk, tq), F32),
                        pltpu.VMEM((2, tk, tq), F32)],
        compiler_params=_params("parallel", "parallel", "arbitrary"),
        name="fox",
    )(base, qct, kaug, vaug)


def _rows_to_columns(rows):
    n = rows[0].shape[1]
    sub = lax.broadcasted_iota(jnp.int32, (128, n), 0)
    block = jnp.zeros((128, n), F32)
    for k, row in enumerate(rows):
        block = jnp.where(sub == k, row, block)
    return block.T


def _first_max(vals):
    best = vals[0]
    for v in vals[1:]:
        best = jnp.maximum(best, v)
    idx = jnp.full(best.shape, len(vals) - 1, jnp.int32)
    for j in range(len(vals) - 2, -1, -1):
        idx = jnp.where(vals[j] == best, j, idx)
    return best, idx


def _softmax_rows(vals):
    top = vals[0]
    for v in vals[1:]:
        top = jnp.maximum(top, v)
    ex = [jnp.exp(v - top) for v in vals]
    tot = ex[0]
    for e in ex[1:]:
        tot = tot + e
    return [e / tot for e in ex]


def _post_kernel(ya_ref, yb_ref, yc_ref, x_ref, mod_ref, ga_ref, gb_ref, gc_ref, wo_ref, nf_ref,
                 wr_hi_ref, wr_lo_ref, br_ref, xo_ref, h2_ref, ri_ref, rw_ref, cnt_ref):
    ya = _rms_normalize(ya_ref[0].astype(F32)) * ga_ref[...]
    yb = _rms_normalize(yb_ref[0].astype(F32)) * gb_ref[...]
    yc = _rms_normalize(yc_ref[0].astype(F32)) * gc_ref[...]
    y = jnp.concatenate([ya, yb, yc], axis=-1).astype(BF16)
    x = x_ref[0] + mod_ref[0, 2:3, :] * _dot(y, wo_ref[...])
    xo_ref[0] = x
    h = (_rms_normalize(x) * nf_ref[...]) * (1.0 + mod_ref[0, 4:5, :]) + mod_ref[0, 3:4, :]
    h_hi, h_lo = _split_bf16(h)
    h2_ref[0] = h_hi

    w_hi = wr_hi_ref[...]
    logits = (_dot_nt(w_hi, h_hi) + _dot_nt(w_hi, h_lo)) + _dot_nt(wr_lo_ref[...], h_hi) + br_ref[...]
    rows = [logits[r:r + 1, :] for r in range(8 + N_EXPERTS)]
    group_prob = _softmax_rows(rows[0:N_GROUPS])
    group_p, group_idx = _first_max(group_prob)
    in_group = []
    for j in range(EXPERTS_PER_GROUP):
        v = rows[8 + (N_GROUPS - 1) * EXPERTS_PER_GROUP + j]
        for g in range(N_GROUPS - 2, -1, -1):
            v = jnp.where(group_idx == g, rows[8 + g * EXPERTS_PER_GROUP + j], v)
        in_group.append(v)
    expert_prob = _softmax_rows(in_group)
    p1, i1 = _first_max(expert_prob)
    rest = [jnp.where(i1 == j, -1.0, expert_prob[j]) for j in range(EXPERTS_PER_GROUP)]
    p2, i2 = _first_max(rest)
    e1 = group_idx * EXPERTS_PER_GROUP + i1
    e2 = group_idx * EXPERTS_PER_GROUP + i2
    rw_ref[0] = _rows_to_columns([group_p * p1 / (p1 + p2), group_p * p2 / (p1 + p2)])
    ts = ri_ref.shape[3]
    erow = lax.broadcasted_iota(jnp.int32, (N_EXPERTS, ts), 0)
    for part in range(e1.shape[1] // ts):
        e1p = e1[:, part * ts:(part + 1) * ts]
        e2p = e2[:, part * ts:(part + 1) * ts]
        ri_ref[0, part] = jnp.zeros((8, ts), jnp.int32)
        ri_ref[0, part, 0:1, :] = e1p
        ri_ref[0, part, 1:2, :] = e2p
        sel = jnp.where((erow == e1p) | (erow == e2p), 1.0, 0.0).astype(BF16)
        cnt_ref[0, part] = _dot(sel, jnp.ones((ts, 128), BF16))


def _post_mixer(ya, yb, yc, x, mod, gain_a, gain_b, gain_c, w_out, norm_ffn, wr_hi, wr_lo, br):
    b, s, d = x.shape
    tm = TM_POST
    ts = TMS
    nsort = tm // ts
    tok = lambda width: pl.BlockSpec((1, tm, width), lambda bi, i: (bi, i, 0))
    full = lambda shape: pl.BlockSpec(shape, lambda bi, i: (0,) * len(shape))
    return pl.pallas_call(
        _post_kernel,
        grid=(b, s // tm),
        in_specs=[
            tok(SWA_WIDTH), tok(LRU_WIDTH), tok(FOX_WIDTH), tok(d),
            pl.BlockSpec((1, 6, d), lambda bi, i: (bi, 0, 0)),
            full((1, SWA_WIDTH)), full((1, LRU_WIDTH)), full((1, FOX_WIDTH)),
            full((d, d)), full((1, d)),
            full((ROUTER_ROWS, d)), full((ROUTER_ROWS, d)), full((ROUTER_ROWS, 1)),
        ],
        out_specs=[
            tok(d), tok(d),
            pl.BlockSpec((1, nsort, 8, ts), lambda bi, i: (bi, i, 0, 0)),
            tok(128),
            pl.BlockSpec((1, nsort, N_EXPERTS, 128), lambda bi, i: (bi, i, 0, 0)),
        ],
        out_shape=[
            jax.ShapeDtypeStruct((b, s, d), F32),
            jax.ShapeDtypeStruct((b, s, d), BF16),
            jax.ShapeDtypeStruct((b, s // ts, 8, ts), jnp.int32),
            jax.ShapeDtypeStruct((b, s, 128), F32),
            jax.ShapeDtypeStruct((b, s // ts, N_EXPERTS, 128), F32),
        ],
        compiler_params=_params("parallel", "parallel"),
        name="post_mixer",
    )(ya, yb, yc, x, mod, gain_a, gain_b, gain_c, w_out, norm_ffn, wr_hi, wr_lo, br)


TMS = 512
SEG = 16
SORT_ROWS = 1280
ET = 1024
SEG_PIECES = (512, 256, 128, 64, 32, 16)
TAIL_PIECES = (512, 256, 128, 64, 32, 16)
TOTAL_PIECES = (1024, 512, 256, 128, 64, 32, 16)
RARE_ROWS = 128
assert SORT_ROWS >= 2 * TMS + N_EXPERTS * (SEG - 1) and SEG_PIECES[0] == TMS and TAIL_PIECES[0] * 2 == ET


def _for_each_piece(n16_ref, lo_ref, goff_ref, tile, fn):
    for e in range(N_EXPERTS):
        n = n16_ref[tile * N_EXPERTS + e]
        lo = lo_ref[tile * N_EXPERTS + e]
        go = goff_ref[tile * N_EXPERTS + e]

        def pieces(sizes, n=n, lo=lo, go=go):
            for rows in sizes:
                done = n & (-2 * rows)

                @pl.when((n & rows) != 0)
                def _(done=done, rows=rows):
                    fn(pl.multiple_of(lo + done, SEG), pl.multiple_of(go + done, SEG), rows)

        @pl.when(n >= RARE_ROWS)
        def _():
            pieces([r for r in SEG_PIECES if r >= RARE_ROWS])

        pieces([r for r in SEG_PIECES if r < RARE_ROWS])


def _for_each_total_piece(total_ref, tile, fn):
    n = total_ref[tile]
    for rows in TOTAL_PIECES:
        @pl.when((n & rows) != 0)
        def _(rows=rows):
            fn(rows)


def _dispatch_kernel(n16_ref, lo_ref, goff_ref, total_ref, tail_row_ref, tail_n_ref, n_valid_ref,
                     h_ref, ri_ref, tri_ref, xs_ref, dest_ref, sorted_ref, zero_ref, sem):
    i = pl.program_id(0)
    last = pl.num_programs(0) - 1
    slot = i & 1
    tms = h_ref.shape[0]

    ri = ri_ref[0]
    e1 = ri[0:1, :]
    e2 = ri[1:2, :]
    erow = lax.broadcasted_iota(jnp.int32, (N_EXPERTS, tms), 0)
    m1 = erow == e1
    m2 = erow == e2
    sel = jnp.where(m1 | m2, 1.0, 0.0).astype(BF16)
    rank = _dot(sel, tri_ref[...]).astype(jnp.int32)
    start = jnp.zeros((N_EXPERTS, tms), jnp.int32)
    for e in range(N_EXPERTS):
        start = jnp.where(erow == e, lo_ref[i * N_EXPERTS + e], start)
    dest = rank + start
    r1 = jnp.sum(jnp.where(m1, dest, 0), axis=0, keepdims=True)
    r2 = jnp.sum(jnp.where(m2, dest, 0), axis=0, keepdims=True)
    dest_ref[...] = _rows_to_columns([r1.astype(F32), r2.astype(F32)])

    rows = lax.broadcasted_iota(jnp.int32, (SORT_ROWS, tms), 0)
    perm = jnp.where((rows == r1) | (rows == r2), 1.0, 0.0).astype(BF16)
    sorted_ref[slot] = _dot(perm, h_ref[...]).astype(BF16)

    def copy(src_slot, local_row, global_row, n):
        return pltpu.make_async_copy(sorted_ref.at[src_slot, pl.ds(local_row, n)],
                                     xs_ref.at[pl.ds(global_row, n)], sem)

    @pl.when(i > 0)
    def _():
        _for_each_total_piece(total_ref, i - 1, lambda n: copy(1 - slot, 0, 0, n).wait())

    _for_each_piece(n16_ref, lo_ref, goff_ref, i, lambda lr, gr, n: copy(slot, lr, gr, n).start())

    @pl.when(i == last)
    def _():
        _for_each_total_piece(total_ref, i, lambda n: copy(slot, 0, 0, n).wait())
        zero_ref[...] = jnp.zeros_like(zero_ref)

        def tail_copies(fn):
            for e in range(N_EXPERTS):
                n = tail_n_ref[e]
                row = tail_row_ref[e]
                for rows_ in TAIL_PIECES:
                    done = n & (-2 * rows_)

                    @pl.when((n & rows_) != 0)
                    def _(row=row, done=done, rows_=rows_):
                        fn(pltpu.make_async_copy(
                            zero_ref.at[pl.ds(0, rows_)],
                            xs_ref.at[pl.ds(pl.multiple_of(row + done, SEG), rows_)], sem))

        tail_copies(lambda c: c.start())
        tail_copies(lambda c: c.wait())

        def unused_tiles(fn):
            def body(j, carry):
                for part in range(ET // TAIL_PIECES[0]):
                    row = pl.multiple_of(j * ET + part * TAIL_PIECES[0], SEG)
                    fn(pltpu.make_async_copy(zero_ref, xs_ref.at[pl.ds(row, TAIL_PIECES[0])], sem))
                return carry
            lax.fori_loop(n_valid_ref[0], xs_ref.shape[0] // ET, body, 0)

        unused_tiles(lambda c: c.start())
        unused_tiles(lambda c: c.wait())


def _dispatch(h2, route_i, tri, tables, rows_max):
    t, d = h2.shape
    nt = t // TMS
    grid_spec = pltpu.PrefetchScalarGridSpec(
        num_scalar_prefetch=7,
        grid=(nt,),
        in_specs=[
            pl.BlockSpec((TMS, d), lambda i, *_: (i, 0)),
            pl.BlockSpec((1, 8, TMS), lambda i, *_: (i, 0, 0)),
            pl.BlockSpec((TMS, TMS), lambda i, *_: (0, 0)),
        ],
        out_specs=[
            pl.BlockSpec(memory_space=pl.ANY),
            pl.BlockSpec((TMS, 128), lambda i, *_: (i, 0)),
        ],
        scratch_shapes=[
            pltpu.VMEM((2, SORT_ROWS, d), BF16),
            pltpu.VMEM((TAIL_PIECES[0], d), BF16),
            pltpu.SemaphoreType.DMA,
        ],
    )
    return pl.pallas_call(
        _dispatch_kernel,
        grid_spec=grid_spec,
        out_shape=[
            jax.ShapeDtypeStruct((rows_max, d), BF16),
            jax.ShapeDtypeStruct((t, 128), F32),
        ],
        compiler_params=_params("arbitrary"),
        name="moe_dispatch",
    )(*tables, h2, route_i, tri)


def _experts_kernel(te_ref, nv_ref, xs_ref, wg_ref, wu_ref, wd_ref, ys_ref, wg_s, wu_s, wd_s):
    j = pl.program_id(0)

    @pl.when((j == 0) | (te_ref[j] != te_ref[jnp.maximum(j - 1, 0)]))
    def _():
        wg_s[...] = wg_ref[0].astype(BF16)
        wu_s[...] = wu_ref[0].astype(BF16)
        wd_s[...] = wd_ref[0].astype(BF16)

    @pl.when(j < nv_ref[0])
    def _():
        h = xs_ref[...]
        gate = _dot(h, wg_s[...])
        hidden = (gate * jax.nn.sigmoid(gate)) * _dot(h, wu_s[...])
        ys_ref[...] = _dot(hidden.astype(BF16), wd_s[...]).astype(BF16)

    @pl.when(j >= nv_ref[0])
    def _():
        ys_ref[...] = jnp.zeros_like(ys_ref)


def _experts(xs, tile_expert, n_valid, w_gate, w_up, w_down, layer):
    rows, d = xs.shape
    row_tile = lambda j, te, nv: (jnp.minimum(j, nv[0] - 1), 0)
    weight = lambda j, te, nv: (layer, te[j], 0, 0)
    grid_spec = pltpu.PrefetchScalarGridSpec(
        num_scalar_prefetch=2,
        grid=(rows // ET,),
        in_specs=[
            pl.BlockSpec((ET, d), row_tile),
            pl.BlockSpec((None, 1, d, D_EXPERT), weight),
            pl.BlockSpec((None, 1, d, D_EXPERT), weight),
            pl.BlockSpec((None, 1, D_EXPERT, d), weight),
        ],
        out_specs=pl.BlockSpec((ET, d), lambda j, te, nv: (j, 0)),
        scratch_shapes=[pltpu.VMEM((d, D_EXPERT), BF16), pltpu.VMEM((d, D_EXPERT), BF16),
                        pltpu.VMEM((D_EXPERT, d), BF16)],
    )
    return pl.pallas_call(
        _experts_kernel,
        grid_spec=grid_spec,
        out_shape=jax.ShapeDtypeStruct((rows, d), BF16),
        compiler_params=_params("arbitrary"),
        name="moe_experts",
    )(tile_expert, n_valid, xs, w_gate, w_up, w_down)


def _combine_kernel(n16_ref, lo_ref, goff_ref, total_ref, ys_ref, dest_ref, w_ref, x_ref, mod_ref,
                    nf_ref, o_ref, buf_ref, sems, *, final):
    i = pl.program_id(0)
    nt = pl.num_programs(0)
    slot = i & 1

    def copy(dst_slot, local_row, global_row, n):
        return pltpu.make_async_copy(ys_ref.at[pl.ds(global_row, n)],
                                     buf_ref.at[dst_slot, pl.ds(local_row, n)], sems.at[dst_slot])

    @pl.when(i == 0)
    def _():
        buf_ref[...] = jnp.zeros_like(buf_ref)
        _for_each_piece(n16_ref, lo_ref, goff_ref, 0, lambda lr, gr, n: copy(0, lr, gr, n).start())

    @pl.when(i + 1 < nt)
    def _():
        _for_each_piece(n16_ref, lo_ref, goff_ref, i + 1,
                        lambda lr, gr, n: copy(1 - slot, lr, gr, n).start())

    _for_each_total_piece(total_ref, i, lambda n: copy(slot, 0, 0, n).wait())

    dest = dest_ref[:, 0:2].astype(jnp.int32)
    w = w_ref[:, 0:2]
    col = lax.broadcasted_iota(jnp.int32, (dest.shape[0], SORT_ROWS), 1)
    wmat = (jnp.where(col == dest[:, 0:1], w[:, 0:1], 0.0)
            + jnp.where(col == dest[:, 1:2], w[:, 1:2], 0.0))
    y = _dot(wmat.astype(BF16), buf_ref[slot])
    x = x_ref[...] + mod_ref[0, 5:6, :] * y
    if final:
        x = _rms_normalize(x) * nf_ref[...]
    o_ref[...] = x


def _combine(ys, dest_col, w_col, x, mod, norm_final, tables, tiles_per_batch, final):
    t, d = x.shape
    nt = t // TMS
    grid_spec = pltpu.PrefetchScalarGridSpec(
        num_scalar_prefetch=4,
        grid=(nt,),
        in_specs=[
            pl.BlockSpec(memory_space=pl.ANY),
            pl.BlockSpec((TMS, 128), lambda i, *_: (i, 0)),
            pl.BlockSpec((TMS, 128), lambda i, *_: (i, 0)),
            pl.BlockSpec((TMS, d), lambda i, *_: (i, 0)),
            pl.BlockSpec((1, 6, d), lambda i, *_: (i // tiles_per_batch, 0, 0)),
            pl.BlockSpec((1, d), lambda i, *_: (0, 0)),
        ],
        out_specs=pl.BlockSpec((TMS, d), lambda i, *_: (i, 0)),
        scratch_shapes=[pltpu.VMEM((2, SORT_ROWS, d), BF16), pltpu.SemaphoreType.DMA((2,))],
    )
    return pl.pallas_call(
        functools.partial(_combine_kernel, final=final),
        grid_spec=grid_spec,
        out_shape=jax.ShapeDtypeStruct((t, d), F32),
        compiler_params=_params("arbitrary"),
        name="moe_combine_final" if final else "moe_combine",
    )(*tables, ys, dest_col, w_col, x, mod, norm_final)


def _route_tables(counts, n_expert_tiles):
    n16 = (counts + (SEG - 1)) // SEG * SEG
    lo = jnp.cumsum(n16, axis=1) - n16
    total = jnp.sum(n16, axis=0)
    region = (total + (ET - 1)) // ET * ET
    ends = jnp.cumsum(region)
    base = ends - region
    goff = base[None, :] + jnp.cumsum(n16, axis=0) - n16
    tile_row = jnp.arange(n_expert_tiles, dtype=jnp.int32) * ET
    tile_expert = jnp.minimum(jnp.sum(tile_row[:, None] >= ends[None, :], axis=1), N_EXPERTS - 1)
    n_valid = (ends[-1] // ET).reshape(1)
    i32 = lambda a: a.astype(jnp.int32).reshape(-1)
    return (i32(n16), i32(lo), i32(goff), i32(jnp.sum(n16, axis=1))), \
        (i32(base + total), i32(region - total)), i32(tile_expert), i32(n_valid)


def _moe(h2, route_i, route_w, cnt, w_gate, w_up, w_down, layer, x, mod, norm_final, final):
    b, s, d = x.shape
    t = b * s
    nt = t // TMS
    n_expert_tiles = -(-(2 * t + nt * N_EXPERTS * (SEG - 1) + N_EXPERTS * (ET - 1)) // ET)
    counts = jnp.round(cnt[:, :, :, 0]).astype(jnp.int32).reshape(nt, N_EXPERTS)
    seg_tables, tail_tables, tile_expert, n_valid = _route_tables(counts, n_expert_tiles)
    tri = jnp.triu(jnp.ones((TMS, TMS), BF16), k=1)
    xs, dest = _dispatch(h2.reshape(t, d), route_i.reshape(nt, 8, TMS), tri,
                         seg_tables + tail_tables + (n_valid,), n_expert_tiles * ET)
    ys = _experts(xs, tile_expert, n_valid, w_gate, w_up, w_down, layer)
    out = _combine(ys, dest, route_w.reshape(t, 128), x.reshape(t, d), mod, norm_final, seg_tables,
                   s // TMS, final)
    return out.reshape(b, s, d)


def _block_diag(w):
    nb, c, _ = w.shape
    eye = jnp.eye(nb, dtype=w.dtype)
    return (eye[:, None, :, None] * w[:, :, None, :]).reshape(nb * c, nb * c)


def _pair_swa_heads(w, axis):
    half = SWA_Q_HEADS // 2
    shape = w.shape
    w = w.reshape(shape[:axis] + (2, half, HEAD_DIM) + shape[axis + 1:])
    return jnp.swapaxes(w, axis, axis + 1).reshape(shape)


def kernel(x, c, w_mod, b_mod, norm_mix, norm_ffn, w_in, w_out, out_gain, sinks, conv_w, conv_b,
           lru_wa, lru_ba, lru_wx, lru_bx, lru_lam, fox_bf, w_router_group, b_router_group,
           w_router_expert, b_router_expert, w_gate, w_up, w_down, norm_final):
    depth = w_mod.shape[0]
    b, s, d = x.shape
    assert d == D_MODEL and s % max(TM_PRE, TM_PREP, TQ_SWA, TS_LRU, 2 * T_FOX, TM_POST) == 0
    assert TM_PRE % T_FOX == 0
    c_rows = jnp.zeros((16, d), F32).at[:b].set(c)
    mod_all = _modulation(c_rows, w_mod, b_mod)[:, :b]

    for l in range(depth):
        mod = mod_all[l].reshape(b, 6, d)
        w_in_l = w_in[l]
        w_tok = jnp.concatenate([_pair_swa_heads(w_in_l[:, :SWA_WIDTH], 1), w_in_l[:, SWA_WIDTH:OFF_QC],
                                 w_in_l[:, OFF_KC:OFF_VC]], axis=1).astype(BF16)
        w_t = jnp.concatenate([w_in_l[:, OFF_QC:OFF_KC].T, w_in_l[:, OFF_VC:OFF_FC].T,
                               jnp.zeros((FORGET_ROWS, d), F32).at[:FOX_HEADS].set(w_in_l[:, OFF_FC:].T)],
                              axis=0).astype(BF16)
        fb = jnp.zeros((FORGET_ROWS, 1), F32).at[:FOX_HEADS, 0].set(fox_bf[l])
        gain = out_gain[l]
        gain_a = _pair_swa_heads(gain[:SWA_WIDTH], 0).reshape(1, SWA_WIDTH)
        gain_b = gain[SWA_WIDTH:SWA_WIDTH + LRU_WIDTH].reshape(1, LRU_WIDTH)
        gain_c = gain[SWA_WIDTH + LRU_WIDTH:].reshape(1, FOX_WIDTH)
        w_out_l = jnp.concatenate([_pair_swa_heads(w_out[l][:SWA_WIDTH], 0), w_out[l][SWA_WIDTH:]],
                                  axis=0).astype(BF16)
        wr = jnp.zeros((ROUTER_ROWS, d), F32)
        wr = wr.at[:N_GROUPS].set(w_router_group[l].T).at[8:8 + N_EXPERTS].set(w_router_expert[l].T)
        wr_hi = wr.astype(BF16)
        wr_lo = (wr - wr_hi.astype(F32)).astype(BF16)
        br = jnp.zeros((ROUTER_ROWS, 1), F32)
        br = br.at[:N_GROUPS, 0].set(b_router_group[l]).at[8:8 + N_EXPERTS, 0].set(b_router_expert[l])

        qa, ka, va, xb, gb, kc, qct, vaug, lf = _pre_mixer(
            x, mod, norm_mix[l].reshape(1, d), w_tok, w_t, fb)
        cum, cum_col = _cumsum(lf)
        kaug = _fox_prep(kc, cum_col)
        base = cum[:, :FOX_HEADS, ::2 * T_FOX].reshape(-1)
        ya = _swa(qa, ka, va, sinks[l].reshape(1, SWA_Q_HEADS))
        yb = _lru(xb, gb, conv_w[l], conv_b[l].reshape(1, -1),
                  _block_diag(lru_wa[l]).astype(BF16), lru_ba[l].reshape(1, -1),
                  _block_diag(lru_wx[l]).astype(BF16), lru_bx[l].reshape(1, -1),
                  lru_lam[l].reshape(1, -1))
        yc = _fox(qct, kaug, vaug, base)
        x, h2, route_i, route_w, cnt = _post_mixer(
            ya, yb, yc, x, mod, gain_a, gain_b, gain_c, w_out_l,
            norm_ffn[l].reshape(1, d), wr_hi, wr_lo, br)
        x = _moe(h2, route_i, route_w, cnt, w_gate, w_up, w_down, l, x, mod,
                 norm_final.reshape(1, d), final=(l == depth - 1))
    return x
```

```python
import functools

import jax
import jax.numpy as jnp
from jax import lax
from jax.experimental import pallas as pl
from jax.experimental.pallas import tpu as pltpu

F32 = jnp.float32
BF16 = jnp.bfloat16

D_MODEL = 1024
HEAD_DIM = 64
PAIR = 2 * HEAD_DIM
BLOCK = 128
SWA_Q_HEADS = 8
SWA_WIDTH = SWA_Q_HEADS * HEAD_DIM
KV_WIDTH = 2 * HEAD_DIM
LRU_WIDTH = 256
LRU_BLOCKS = 8
CONV_WIDTH = 4
LRU_C = 8.0
FOX_HEADS = 4
FOX_WIDTH = FOX_HEADS * HEAD_DIM
N_GROUPS = 4
EXPERTS_PER_GROUP = 4
N_EXPERTS = 16
D_EXPERT = 256
EPS = 1e-6
NEG = -1e30
LOG2E = 1.4426950408889634

OFF_KA = SWA_WIDTH
OFF_VA = OFF_KA + KV_WIDTH
OFF_XB = OFF_VA + KV_WIDTH
OFF_GB = OFF_XB + LRU_WIDTH
OFF_QC = OFF_GB + LRU_WIDTH
OFF_KC = OFF_QC + FOX_WIDTH
OFF_VC = OFF_KC + FOX_WIDTH
OFF_FC = OFF_VC + FOX_WIDTH

ROUTER_ROWS = 32
FORGET_ROWS = 16
V_ROWS = HEAD_DIM + 16
VMEM_LIMIT_BYTES = 48 * 1024 * 1024

TM_PRE = 1024
TM_PREP = 2048
TQ_SWA = 2048
TS_LRU = 2048
T_FOX = 512
TM_POST = 1024
CUMSUM_CHUNK = 256


def _params(*semantics):
    return pltpu.CompilerParams(dimension_semantics=semantics, vmem_limit_bytes=VMEM_LIMIT_BYTES)


def _split_bf16(v):
    hi = v.astype(BF16)
    lo = (v - hi.astype(F32)).astype(BF16)
    return hi, lo


def _dot(a, b):
    return jnp.dot(a, b, preferred_element_type=F32)


def _dot_nt(a, b):
    return lax.dot_general(a, b, (((1,), (1,)), ((), ())), preferred_element_type=F32)


def _rms_normalize(v):
    return v * lax.rsqrt(jnp.mean(v * v, axis=-1, keepdims=True) + EPS)


def _mod_kernel(c_ref, w_ref, b_ref, o_ref):
    c = c_ref[...]
    ca = c * jax.nn.sigmoid(c)
    w = w_ref[0]
    a_hi, a_lo = _split_bf16(ca)
    w_hi, w_lo = _split_bf16(w)
    o_ref[0] = _dot(a_hi, w_hi) + _dot(a_lo, w_hi) + _dot(a_hi, w_lo) + b_ref[0]


def _modulation(c, w_mod, b_mod):
    depth, d, d6 = w_mod.shape
    b = c.shape[0]
    n = d6 // d
    return pl.pallas_call(
        _mod_kernel,
        grid=(depth, n),
        in_specs=[
            pl.BlockSpec((b, d), lambda l, j: (0, 0)),
            pl.BlockSpec((1, d, d), lambda l, j: (l, 0, j)),
            pl.BlockSpec((1, 1, d), lambda l, j: (l, 0, j)),
        ],
        out_specs=pl.BlockSpec((1, b, d), lambda l, j: (l, 0, j)),
        out_shape=jax.ShapeDtypeStruct((depth, b, d6), F32),
        compiler_params=_params("arbitrary", "arbitrary"),
        name="modulation",
    )(c, w_mod, b_mod.reshape(depth, 1, d6))


def _log_sigmoid(z):
    return jnp.minimum(z, 0.0) - jnp.log1p(jnp.exp(-jnp.abs(z)))


def _other_half(a):
    return HEAD_DIM * (1 - a)


def _pre_kernel(x_ref, mod_ref, g_ref, w_ref, wt_ref, fb_ref,
                qa_ref, ka_ref, va_ref, xb_ref, gb_ref, kc_ref, qct_ref, vct_ref, lf_ref):
    x = x_ref[0]
    h = (_rms_normalize(x) * g_ref[...]) * (1.0 + mod_ref[0, 1:2, :]) + mod_ref[0, 0:1, :]
    hb = h.astype(BF16)

    def proj(lo, width):
        return _dot(hb, w_ref[:, lo:lo + width])

    scale = HEAD_DIM ** -0.5
    qa_ref[0] = (proj(0, SWA_WIDTH) * (scale * LOG2E)).astype(BF16)
    ka_ref[0] = proj(OFF_KA, KV_WIDTH).astype(BF16)
    va_ref[0] = proj(OFF_VA, KV_WIDTH).astype(BF16)
    xb_ref[0] = proj(OFF_XB, LRU_WIDTH).astype(BF16)
    gb_ref[0] = proj(OFF_GB, LRU_WIDTH).astype(BF16)
    kc_ref[0] = proj(OFF_QC, FOX_WIDTH).astype(BF16)

    t_all = _dot_nt(wt_ref[...], hb)
    tf = qct_ref.shape[3]
    ones_row = jnp.where(lax.broadcasted_iota(jnp.int32, (V_ROWS - HEAD_DIM, tf), 0) == 0, 1.0, 0.0)
    for part in range(t_all.shape[1] // tf):
        cols = slice(part * tf, (part + 1) * tf)
        qct_ref[0, part] = (t_all[0:FOX_WIDTH, cols] * (scale * LOG2E)).astype(BF16)
        for hd in range(FOX_HEADS):
            v_head = t_all[FOX_WIDTH + hd * HEAD_DIM:FOX_WIDTH + (hd + 1) * HEAD_DIM, cols]
            vct_ref[0, part, hd] = jnp.concatenate([v_head, ones_row], axis=0).astype(BF16)
    lf_ref[0] = _log_sigmoid(t_all[2 * FOX_WIDTH:] + fb_ref[...])


def _pre_mixer(x, mod, gain, w_tok, w_t, fb):
    b, s, d = x.shape
    tm = TM_PRE
    tf = T_FOX
    nf = tm // tf

    def tok(width):
        return pl.BlockSpec((1, tm, width), lambda bi, i: (bi, i, 0))

    def tok_shape(width):
        return jax.ShapeDtypeStruct((b, s, width), BF16)

    return pl.pallas_call(
        _pre_kernel,
        grid=(b, s // tm),
        in_specs=[
            tok(d),
            pl.BlockSpec((1, 6, d), lambda bi, i: (bi, 0, 0)),
            pl.BlockSpec((1, d), lambda bi, i: (0, 0)),
            pl.BlockSpec(w_tok.shape, lambda bi, i: (0, 0)),
            pl.BlockSpec(w_t.shape, lambda bi, i: (0, 0)),
            pl.BlockSpec((FORGET_ROWS, 1), lambda bi, i: (0, 0)),
        ],
        out_specs=[
            tok(SWA_WIDTH), tok(KV_WIDTH), tok(KV_WIDTH), tok(LRU_WIDTH), tok(LRU_WIDTH),
            tok(FOX_WIDTH),
            pl.BlockSpec((1, nf, FOX_WIDTH, tf), lambda bi, i: (bi, i, 0, 0)),
            pl.BlockSpec((1, nf, FOX_HEADS, V_ROWS, tf), lambda bi, i: (bi, i, 0, 0, 0)),
            pl.BlockSpec((1, FORGET_ROWS, tm), lambda bi, i: (bi, 0, i)),
        ],
        out_shape=[
            tok_shape(SWA_WIDTH), tok_shape(KV_WIDTH), tok_shape(KV_WIDTH), tok_shape(LRU_WIDTH),
            tok_shape(LRU_WIDTH), tok_shape(FOX_WIDTH),
            jax.ShapeDtypeStruct((b, s // tf, FOX_WIDTH, tf), BF16),
            jax.ShapeDtypeStruct((b, s // tf, FOX_HEADS, V_ROWS, tf), BF16),
            jax.ShapeDtypeStruct((b, FORGET_ROWS, s), F32),
        ],
        compiler_params=_params("parallel", "parallel"),
        name="pre_mixer",
    )(x, mod, gain, w_tok, w_t, fb)


def _cumsum_kernel(lf_ref, o_ref, col_ref):
    n = CUMSUM_CHUNK
    rows = lf_ref.shape[1]
    s = lf_ref.shape[2]
    r = lax.broadcasted_iota(jnp.int32, (n, n), 0)
    c = lax.broadcasted_iota(jnp.int32, (n, n), 1)
    tri = jnp.where(r <= c, 1.0, 0.0).astype(BF16)
    carry = jnp.zeros((rows, 1), F32)
    for j in range(s // n):
        seg = lf_ref[0, :, j * n:(j + 1) * n]
        hi = seg.astype(BF16)
        rem = seg - hi.astype(F32)
        mid = rem.astype(BF16)
        lo = (rem - mid.astype(F32)).astype(BF16)
        cs = (_dot(hi, tri) + _dot(mid, tri)) + _dot(lo, tri) + carry
        o_ref[0, :, j * n:(j + 1) * n] = cs
        parts = _split3(cs[0:FOX_HEADS] * LOG2E)
        col_ref[0, j * n:(j + 1) * n, :] = _rows_to_columns(
            [parts[k][hd:hd + 1] for hd in range(FOX_HEADS) for k in range(3)])
        carry = cs[:, n - 1:n]


def _cumsum(lf):
    b, r, s = lf.shape
    return pl.pallas_call(
        _cumsum_kernel,
        grid=(b,),
        in_specs=[pl.BlockSpec((1, r, s), lambda bi: (bi, 0, 0))],
        out_specs=[pl.BlockSpec((1, r, s), lambda bi: (bi, 0, 0)),
                   pl.BlockSpec((1, s, 128), lambda bi: (bi, 0, 0))],
        out_shape=[jax.ShapeDtypeStruct((b, r, s), F32), jax.ShapeDtypeStruct((b, s, 128), F32)],
        compiler_params=_params("parallel"),
        name="cumsum",
    )(lf)


def _swa_kernel(sink_ref, q_ref, kc_ref, vc_ref, kp_ref, vp_ref, o_ref, kf_ref, vf_ref, bias_ref):
    i = pl.program_id(1)
    nsub = q_ref.shape[1] // BLOCK
    npair = SWA_Q_HEADS // 2

    @pl.when((pl.program_id(0) == 0) & (i == 0))
    def _():
        qpos = BLOCK + lax.broadcasted_iota(jnp.int32, (BLOCK, 2 * BLOCK), 0)
        kpos = lax.broadcasted_iota(jnp.int32, (BLOCK, 2 * BLOCK), 1)
        dist = qpos - kpos
        valid = (dist >= 0) & (dist < BLOCK)
        distf = dist.astype(F32)
        for hd in range(SWA_Q_HEADS):
            slope = 2.0 ** (-8.0 * (hd + 1) / SWA_Q_HEADS)
            bias_ref[hd] = jnp.where(valid, (-slope * LOG2E) * distf, NEG)

    kf_ref[0:BLOCK, :] = kp_ref[0]
    kf_ref[BLOCK:, :] = kc_ref[0]
    vf_ref[0:BLOCK, :] = vp_ref[0]
    vf_ref[BLOCK:, :] = vc_ref[0]

    lane = lax.broadcasted_iota(jnp.int32, (2 * BLOCK, PAIR), 1)
    lane_o = lax.broadcasted_iota(jnp.int32, (BLOCK, PAIR), 1)
    col = lax.broadcasted_iota(jnp.int32, (BLOCK, 2 * BLOCK), 1)

    for j in range(nsub):
        k2 = kf_ref[j * BLOCK:(j + 2) * BLOCK, :]
        v2 = vf_ref[j * BLOCK:(j + 2) * BLOCK, :]
        k_half = (jnp.where(lane < HEAD_DIM, k2, jnp.zeros_like(k2)),
                  jnp.where(lane >= HEAD_DIM, k2, jnp.zeros_like(k2)))
        for p in range(npair):
            qp = q_ref[0, j * BLOCK:(j + 1) * BLOCK, p * PAIR:(p + 1) * PAIR]
            outs = []
            for a in range(2):
                hd = p + a * npair
                sc = _dot_nt(qp, k_half[a]) + bias_ref[hd]
                if j == 0:
                    sc = sc + jnp.where(col < BLOCK, jnp.where(i == 0, NEG, 0.0), 0.0)
                sink = sink_ref[0, hd] * LOG2E
                m = jnp.maximum(jnp.max(sc, axis=-1, keepdims=True), sink)
                pr = jnp.exp2(sc - m)
                den = jnp.sum(pr, axis=-1, keepdims=True) + jnp.exp2(sink - m)
                outs.append(_dot(pr.astype(BF16), v2) / den)
            o_ref[0, j * BLOCK:(j + 1) * BLOCK, p * PAIR:(p + 1) * PAIR] = jnp.where(
                lane_o < HEAD_DIM, outs[0], outs[1]).astype(BF16)


def _swa(qa, ka, va, sinks):
    b, s, _ = qa.shape
    tq = TQ_SWA
    nsub = tq // BLOCK
    prev = lambda bi, i: (bi, jnp.maximum(i * nsub - 1, 0), 0)
    return pl.pallas_call(
        _swa_kernel,
        grid=(b, s // tq),
        in_specs=[
            pl.BlockSpec(memory_space=pltpu.SMEM),
            pl.BlockSpec((1, tq, SWA_WIDTH), lambda bi, i: (bi, i, 0)),
            pl.BlockSpec((1, tq, KV_WIDTH), lambda bi, i: (bi, i, 0)),
            pl.BlockSpec((1, tq, KV_WIDTH), lambda bi, i: (bi, i, 0)),
            pl.BlockSpec((1, BLOCK, KV_WIDTH), prev),
            pl.BlockSpec((1, BLOCK, KV_WIDTH), prev),
        ],
        out_specs=pl.BlockSpec((1, tq, SWA_WIDTH), lambda bi, i: (bi, i, 0)),
        out_shape=jax.ShapeDtypeStruct((b, s, SWA_WIDTH), BF16),
        scratch_shapes=[
            pltpu.VMEM((tq + BLOCK, KV_WIDTH), BF16),
            pltpu.VMEM((tq + BLOCK, KV_WIDTH), BF16),
            pltpu.VMEM((SWA_Q_HEADS, BLOCK, 2 * BLOCK), F32),
        ],
        compiler_params=_params("arbitrary", "arbitrary"),
        name="swa",
    )(sinks, qa, ka, va, ka, va)


def _gelu_tanh(v):
    return 0.5 * v * (1.0 + jnp.tanh(0.7978845608028654 * (v + 0.044715 * (v * v * v))))


def _lru_kernel(xb_ref, gb_ref, cw_ref, cb_ref, wa_ref, ba_ref, wx_ref, bx_ref, lam_ref,
                o_ref, win_ref, h_ref):
    ts = xb_ref.shape[1]

    @pl.when(pl.program_id(1) == 0)
    def _():
        win_ref[0:8, :] = jnp.zeros((8, LRU_WIDTH), F32)
        h_ref[...] = jnp.zeros_like(h_ref)

    x = xb_ref[0].astype(F32)
    win_ref[8:, :] = x
    conv = cb_ref[...] + cw_ref[CONV_WIDTH - 1:CONV_WIDTH, :] * x
    for k in range(1, CONV_WIDTH):
        conv = conv + cw_ref[CONV_WIDTH - 1 - k:CONV_WIDTH - k, :] * win_ref[8 - k:8 - k + ts, :]
    win_ref[0:8, :] = x[ts - 8:ts]

    cbf = conv.astype(BF16)
    r = jax.nn.sigmoid(_dot(cbf, wa_ref[...]) + ba_ref[...])
    gi = jax.nn.sigmoid(_dot(cbf, wx_ref[...]) + bx_ref[...])
    z = -lam_ref[...]
    softplus = jnp.maximum(z, 0.0) + jnp.log1p(jnp.exp(-jnp.abs(z)))
    log_a = (-LRU_C * softplus) * r
    a = jnp.exp(log_a)
    v = 1.0 - a * a
    u = jnp.where(v == 0.0, 0.0, v * lax.rsqrt(v)) * (gi * conv)

    a = a.reshape(ts // 8, 8, LRU_WIDTH)
    u = u.reshape(ts // 8, 8, LRU_WIDTH)
    sub = lax.broadcasted_iota(jnp.int32, a.shape, 1)
    for step in (1, 2, 4):
        keep = sub >= step
        u = jnp.where(keep, a * pltpu.roll(u, step, 1) + u, u)
        a = jnp.where(keep, a * pltpu.roll(a, step, 1), a)
    carry = h_ref[7:8, :]
    groups = []
    for g in range(ts // 8):
        hg = u[g] + a[g] * carry
        groups.append(hg)
        carry = hg[7:8, :]
    h = jnp.concatenate(groups, axis=0)
    h_ref[...] = groups[-1]
    o_ref[0] = (h * _gelu_tanh(gb_ref[0].astype(F32))).astype(BF16)


def _lru(xb, gb, conv_w, conv_b, wa, ba, wx, bx, lam):
    b, s, w = xb.shape
    ts = TS_LRU
    tok = pl.BlockSpec((1, ts, w), lambda bi, i: (bi, i, 0))
    full = lambda shape: pl.BlockSpec(shape, lambda bi, i: (0,) * len(shape))
    return pl.pallas_call(
        _lru_kernel,
        grid=(b, s // ts),
        in_specs=[tok, tok, full((CONV_WIDTH, w)), full((1, w)), full((w, w)), full((1, w)),
                  full((w, w)), full((1, w)), full((1, w))],
        out_specs=tok,
        out_shape=jax.ShapeDtypeStruct((b, s, w), BF16),
        scratch_shapes=[pltpu.VMEM((8 + ts, w), F32), pltpu.VMEM((8, w), F32)],
        compiler_params=_params("arbitrary", "arbitrary"),
        name="lru",
    )(xb, gb, conv_w, conv_b, wa, ba, wx, bx, lam)


BIAS_ROWS = 6


def _split3(v):
    hi = v.astype(BF16).astype(F32)
    rem = v - hi
    mid = rem.astype(BF16).astype(F32)
    lo = (rem - mid).astype(BF16).astype(F32)
    return hi, mid, lo


def _fox_prep_kernel(k_ref, c_ref, o_ref):
    tm = k_ref.shape[1]
    lane = lax.broadcasted_iota(jnp.int32, (tm, PAIR), 1)
    parts = c_ref[0].astype(BF16)
    src = lax.broadcasted_iota(jnp.int32, (PAIR, PAIR), 0)
    dst = lax.broadcasted_iota(jnp.int32, (PAIR, PAIR), 1)
    for p in range(FOX_WIDTH // PAIR):
        kp = k_ref[0, :, p * PAIR:(p + 1) * PAIR].astype(F32)
        for a in range(2):
            hd = 2 * p + a
            ob = _other_half(a)
            part = src - 3 * hd
            place = jnp.where(part == dst - ob, jnp.where(part >= 0, jnp.where(part < 3, 1.0, 0.0), 0.0),
                              0.0).astype(BF16)
            bias = _dot(parts, place)
            rel = lane - ob
            aug = jnp.where(rel < 3, bias, jnp.where(rel < BIAS_ROWS, 1.0, 0.0))
            in_half = (lane >= HEAD_DIM * a) & (lane < HEAD_DIM * (a + 1))
            o_ref[0, hd] = jnp.where(in_half, kp, aug).astype(BF16)


def _fox_prep(kc, cum_col):
    b, s, w = kc.shape
    tm = TM_PREP
    return pl.pallas_call(
        _fox_prep_kernel,
        grid=(b, s // tm),
        in_specs=[
            pl.BlockSpec((1, tm, w), lambda bi, i: (bi, i, 0)),
            pl.BlockSpec((1, tm, cum_col.shape[2]), lambda bi, i: (bi, i, 0)),
        ],
        out_specs=pl.BlockSpec((1, FOX_HEADS, tm, PAIR), lambda bi, i: (bi, 0, i, 0)),
        out_shape=jax.ShapeDtypeStruct((b, FOX_HEADS, s, PAIR), BF16),
        compiler_params=_params("parallel", "parallel"),
        name="fox_prep",
    )(kc, cum_col)


def _fox_kernel(base_ref, qt_ref, ka_ref, vt_ref, o_ref, acc_ref, s0_ref, s1_ref):
    bi = pl.program_id(0)
    p = pl.program_id(1)
    qi = pl.program_id(2)
    nq = pl.num_programs(2)
    tk = qt_ref.shape[3]
    tq = 2 * tk
    qt = jnp.concatenate([qt_ref[0, 0], qt_ref[0, 1]], axis=1).astype(F32)
    row = lax.broadcasted_iota(jnp.int32, (PAIR, tq), 0)
    key_off = (lax.broadcasted_iota(jnp.int32, (tk, tq), 1)
               - lax.broadcasted_iota(jnp.int32, (tk, tq), 0))
    q_aug = []
    for a in range(2):
        hd = 2 * p + a
        base = jnp.full((1, tq), base_ref[(bi * FOX_HEADS + hd) * nq + qi], F32) * LOG2E
        b_hi, b_mid, b_lo = _split3(base)
        rel = row - _other_half(a)
        aug = jnp.where(rel < 3, -1.0, jnp.where(rel == 3, b_hi, jnp.where(
            rel == 4, b_mid, jnp.where(rel == 5, b_lo, 0.0))))
        in_half = (row >= HEAD_DIM * a) & (row < HEAD_DIM * (a + 1))
        q_aug.append(jnp.where(in_half, qt, aug).astype(BF16))
        acc_ref[a] = jnp.zeros((V_ROWS, tq), F32)

    slots = (s0_ref, s1_ref)

    def scores(ki, slot, diag=None):
        tile_max = []
        for a in range(2):
            k = ka_ref[0, a, pl.ds(pl.multiple_of(ki * tk, tk), tk), :]
            st = _dot(k, q_aug[a])
            if diag is not None:
                st = jnp.where(key_off >= diag * tk, st, NEG)
            slots[slot][a] = st
            tile_max.append(jnp.max(st, axis=0, keepdims=True))
        return tuple(tile_max)

    def consume(ki, slot, m_old, tile_max):
        m_new = []
        for a in range(2):
            m_a = jnp.maximum(m_old[a], tile_max[a])
            alpha = jnp.exp2(m_old[a] - m_a)
            pr = jnp.exp2(slots[slot][a] - m_a).astype(BF16)
            acc_ref[a] = alpha * acc_ref[a] + _dot(vt_ref[0, ki, a], pr)
            m_new.append(m_a)
        return tuple(m_new)

    def advance(j, slot, carry, diag=None):
        m_old, tile_max, prev = carry
        next_max = scores(j, 1 - slot, diag)
        return consume(prev, slot, m_old, tile_max), next_max, j

    def body(i, carry):
        return advance(2 * i + 1, 0, advance(2 * i, 1, carry))

    m0 = jnp.full((1, tq), NEG, F32)
    carry = ((m0, m0), scores(2 * qi, 0, diag=0), 2 * qi)
    carry = advance(2 * qi + 1, 0, carry, diag=1)
    m, tile_max, prev = lax.fori_loop(0, qi, body, carry)
    consume(prev, 1, m, tile_max)
    outs = []
    for a in range(2):
        acc = acc_ref[a]
        outs.append(acc[0:HEAD_DIM] / acc[HEAD_DIM:HEAD_DIM + 1, :])
    o_ref[0] = jnp.concatenate(outs, axis=0).T.astype(BF16)


def _fox(qct, kaug, vaug, base):
    b, nt, w, tk = qct.shape
    s = nt * tk
    tq = 2 * tk
    npair = w // PAIR
    return pl.pallas_call(
        _fox_kernel,
        grid=(b, npair, s // tq),
        in_specs=[
            pl.BlockSpec(memory_space=pltpu.SMEM),
            pl.BlockSpec((1, 2, PAIR, tk), lambda bi, p, qi: (bi, qi, p, 0)),
            pl.BlockSpec((1, 2, s, PAIR), lambda bi, p, qi: (bi, p, 0, 0)),
            pl.BlockSpec((1, nt, 2, V_ROWS, tk), lambda bi, p, qi: (bi, 0, p, 0, 0)),
        ],
        out_specs=pl.BlockSpec((1, tq, PAIR), lambda bi, p, qi: (bi, qi, p)),
        out_shape=jax.ShapeDtypeStruct((b, s, w), BF16),
        scratch_shapes=[pltpu.VMEM((2, V_ROWS, tq), F32), pltpu.VMEM((2, tk, tq), F32),
                        pltpu.VMEM((2, tk, tq), F32)],
        compiler_params=_params("parallel", "parallel", "arbitrary"),
        name="fox",
    )(base, qct, kaug, vaug)


def _rows_to_columns(rows):
    n = rows[0].shape[1]
    sub = lax.broadcasted_iota(jnp.int32, (128, n), 0)
    block = jnp.zeros((128, n), F32)
    for k, row in enumerate(rows):
        block = jnp.where(sub == k, row, block)
    return block.T


def _first_max(vals):
    best = vals[0]
    for v in vals[1:]:
        best = jnp.maximum(best, v)
    idx = jnp.full(best.shape, len(vals) - 1, jnp.int32)
    for j in range(len(vals) - 2, -1, -1):
        idx = jnp.where(vals[j] == best, j, idx)
    return best, idx


def _softmax_rows(vals):
    top = vals[0]
    for v in vals[1:]:
        top = jnp.maximum(top, v)
    ex = [jnp.exp(v - top) for v in vals]
    tot = ex[0]
    for e in ex[1:]:
        tot = tot + e
    return [e / tot for e in ex]


def _post_kernel(ya_ref, yb_ref, yc_ref, x_ref, mod_ref, ga_ref, gb_ref, gc_ref, wo_ref, nf_ref,
                 wr_hi_ref, wr_lo_ref, br_ref, xo_ref, h2_ref, ri_ref, rw_ref, cnt_ref):
    ya = _rms_normalize(ya_ref[0].astype(F32)) * ga_ref[...]
    yb = _rms_normalize(yb_ref[0].astype(F32)) * gb_ref[...]
    yc = _rms_normalize(yc_ref[0].astype(F32)) * gc_ref[...]
    y = jnp.concatenate([ya, yb, yc], axis=-1).astype(BF16)
    x = x_ref[0] + mod_ref[0, 2:3, :] * _dot(y, wo_ref[...])
    xo_ref[0] = x
    h = (_rms_normalize(x) * nf_ref[...]) * (1.0 + mod_ref[0, 4:5, :]) + mod_ref[0, 3:4, :]
    h_hi, h_lo = _split_bf16(h)
    h2_ref[0] = h_hi

    w_hi = wr_hi_ref[...]
    logits = (_dot_nt(w_hi, h_hi) + _dot_nt(w_hi, h_lo)) + _dot_nt(wr_lo_ref[...], h_hi) + br_ref[...]
    rows = [logits[r:r + 1, :] for r in range(8 + N_EXPERTS)]
    group_prob = _softmax_rows(rows[0:N_GROUPS])
    group_p, group_idx = _first_max(group_prob)
    in_group = []
    for j in range(EXPERTS_PER_GROUP):
        v = rows[8 + (N_GROUPS - 1) * EXPERTS_PER_GROUP + j]
        for g in range(N_GROUPS - 2, -1, -1):
            v = jnp.where(group_idx == g, rows[8 + g * EXPERTS_PER_GROUP + j], v)
        in_group.append(v)
    expert_prob = _softmax_rows(in_group)
    p1, i1 = _first_max(expert_prob)
    rest = [jnp.where(i1 == j, -1.0, expert_prob[j]) for j in range(EXPERTS_PER_GROUP)]
    p2, i2 = _first_max(rest)
    e1 = group_idx * EXPERTS_PER_GROUP + i1
    e2 = group_idx * EXPERTS_PER_GROUP + i2
    rw_ref[0] = _rows_to_columns([group_p * p1 / (p1 + p2), group_p * p2 / (p1 + p2)])
    ts = ri_ref.shape[3]
    erow = lax.broadcasted_iota(jnp.int32, (N_EXPERTS, ts), 0)
    for part in range(e1.shape[1] // ts):
        e1p = e1[:, part * ts:(part + 1) * ts]
        e2p = e2[:, part * ts:(part + 1) * ts]
        ri_ref[0, part] = jnp.zeros((8, ts), jnp.int32)
        ri_ref[0, part, 0:1, :] = e1p
        ri_ref[0, part, 1:2, :] = e2p
        sel = jnp.where((erow == e1p) | (erow == e2p), 1.0, 0.0).astype(BF16)
        cnt_ref[0, part] = _dot(sel, jnp.ones((ts, 128), BF16))


def _post_mixer(ya, yb, yc, x, mod, gain_a, gain_b, gain_c, w_out, norm_ffn, wr_hi, wr_lo, br):
    b, s, d = x.shape
    tm = TM_POST
    ts = TMS
    nsort = tm // ts
    tok = lambda width: pl.BlockSpec((1, tm, width), lambda bi, i: (bi, i, 0))
    full = lambda shape: pl.BlockSpec(shape, lambda bi, i: (0,) * len(shape))
    return pl.pallas_call(
        _post_kernel,
        grid=(b, s // tm),
        in_specs=[
            tok(SWA_WIDTH), tok(LRU_WIDTH), tok(FOX_WIDTH), tok(d),
            pl.BlockSpec((1, 6, d), lambda bi, i: (bi, 0, 0)),
            full((1, SWA_WIDTH)), full((1, LRU_WIDTH)), full((1, FOX_WIDTH)),
            full((d, d)), full((1, d)),
            full((ROUTER_ROWS, d)), full((ROUTER_ROWS, d)), full((ROUTER_ROWS, 1)),
        ],
        out_specs=[
            tok(d), tok(d),
            pl.BlockSpec((1, nsort, 8, ts), lambda bi, i: (bi, i, 0, 0)),
            tok(128),
            pl.BlockSpec((1, nsort, N_EXPERTS, 128), lambda bi, i: (bi, i, 0, 0)),
        ],
        out_shape=[
            jax.ShapeDtypeStruct((b, s, d), F32),
            jax.ShapeDtypeStruct((b, s, d), BF16),
            jax.ShapeDtypeStruct((b, s // ts, 8, ts), jnp.int32),
            jax.ShapeDtypeStruct((b, s, 128), F32),
            jax.ShapeDtypeStruct((b, s // ts, N_EXPERTS, 128), F32),
        ],
        compiler_params=_params("parallel", "parallel"),
        name="post_mixer",
    )(ya, yb, yc, x, mod, gain_a, gain_b, gain_c, w_out, norm_ffn, wr_hi, wr_lo, br)


TMS = 512
SEG = 16
SORT_ROWS = 1280
ET = 1024
SEG_PIECES = (512, 256, 128, 64, 32, 16)
TAIL_PIECES = (512, 256, 128, 64, 32, 16)
TOTAL_PIECES = (1024, 512, 256, 128, 64, 32, 16)
RARE_ROWS = 128
assert SORT_ROWS >= 2 * TMS + N_EXPERTS * (SEG - 1) and SEG_PIECES[0] == TMS and TAIL_PIECES[0] * 2 == ET


def _for_each_piece(n16_ref, lo_ref, goff_ref, tile, fn):
    for e in range(N_EXPERTS):
        n = n16_ref[tile * N_EXPERTS + e]
        lo = lo_ref[tile * N_EXPERTS + e]
        go = goff_ref[tile * N_EXPERTS + e]

        def pieces(sizes, n=n, lo=lo, go=go):
            for rows in sizes:
                done = n & (-2 * rows)

                @pl.when((n & rows) != 0)
                def _(done=done, rows=rows):
                    fn(pl.multiple_of(lo + done, SEG), pl.multiple_of(go + done, SEG), rows)

        @pl.when(n >= RARE_ROWS)
        def _():
            pieces([r for r in SEG_PIECES if r >= RARE_ROWS])

        pieces([r for r in SEG_PIECES if r < RARE_ROWS])


def _for_each_total_piece(total_ref, tile, fn):
    n = total_ref[tile]
    for rows in TOTAL_PIECES:
        @pl.when((n & rows) != 0)
        def _(rows=rows):
            fn(rows)


def _dispatch_kernel(n16_ref, lo_ref, goff_ref, total_ref, tail_row_ref, tail_n_ref, n_valid_ref,
                     h_ref, ri_ref, tri_ref, xs_ref, dest_ref, sorted_ref, zero_ref, sem):
    i = pl.program_id(0)
    last = pl.num_programs(0) - 1
    slot = i & 1
    tms = h_ref.shape[0]

    ri = ri_ref[0]
    e1 = ri[0:1, :]
    e2 = ri[1:2, :]
    erow = lax.broadcasted_iota(jnp.int32, (N_EXPERTS, tms), 0)
    m1 = erow == e1
    m2 = erow == e2
    sel = jnp.where(m1 | m2, 1.0, 0.0).astype(BF16)
    rank = _dot(sel, tri_ref[...]).astype(jnp.int32)
    start = jnp.zeros((N_EXPERTS, tms), jnp.int32)
    for e in range(N_EXPERTS):
        start = jnp.where(erow == e, lo_ref[i * N_EXPERTS + e], start)
    dest = rank + start
    r1 = jnp.sum(jnp.where(m1, dest, 0), axis=0, keepdims=True)
    r2 = jnp.sum(jnp.where(m2, dest, 0), axis=0, keepdims=True)
    dest_ref[...] = _rows_to_columns([r1.astype(F32), r2.astype(F32)])

    rows = lax.broadcasted_iota(jnp.int32, (SORT_ROWS, tms), 0)
    perm = jnp.where((rows == r1) | (rows == r2), 1.0, 0.0).astype(BF16)
    sorted_ref[slot] = _dot(perm, h_ref[...]).astype(BF16)

    def copy(src_slot, local_row, global_row, n):
        return pltpu.make_async_copy(sorted_ref.at[src_slot, pl.ds(local_row, n)],
                                     xs_ref.at[pl.ds(global_row, n)], sem)

    @pl.when(i > 0)
    def _():
        _for_each_total_piece(total_ref, i - 1, lambda n: copy(1 - slot, 0, 0, n).wait())

    _for_each_piece(n16_ref, lo_ref, goff_ref, i, lambda lr, gr, n: copy(slot, lr, gr, n).start())

    @pl.when(i == last)
    def _():
        _for_each_total_piece(total_ref, i, lambda n: copy(slot, 0, 0, n).wait())
        zero_ref[...] = jnp.zeros_like(zero_ref)

        def tail_copies(fn):
            for e in range(N_EXPERTS):
                n = tail_n_ref[e]
                row = tail_row_ref[e]
                for rows_ in TAIL_PIECES:
                    done = n & (-2 * rows_)

                    @pl.when((n & rows_) != 0)
                    def _(row=row, done=done, rows_=rows_):
                        fn(pltpu.make_async_copy(
                            zero_ref.at[pl.ds(0, rows_)],
                            xs_ref.at[pl.ds(pl.multiple_of(row + done, SEG), rows_)], sem))

        tail_copies(lambda c: c.start())
        tail_copies(lambda c: c.wait())

        def unused_tiles(fn):
            def body(j, carry):
                for part in range(ET // TAIL_PIECES[0]):
                    row = pl.multiple_of(j * ET + part * TAIL_PIECES[0], SEG)
                    fn(pltpu.make_async_copy(zero_ref, xs_ref.at[pl.ds(row, TAIL_PIECES[0])], sem))
                return carry
            lax.fori_loop(n_valid_ref[0], xs_ref.shape[0] // ET, body, 0)

        unused_tiles(lambda c: c.start())
        unused_tiles(lambda c: c.wait())


def _dispatch(h2, route_i, tri, tables, rows_max):
    t, d = h2.shape
    nt = t // TMS
    grid_spec = pltpu.PrefetchScalarGridSpec(
        num_scalar_prefetch=7,
        grid=(nt,),
        in_specs=[
            pl.BlockSpec((TMS, d), lambda i, *_: (i, 0)),
            pl.BlockSpec((1, 8, TMS), lambda i, *_: (i, 0, 0)),
            pl.BlockSpec((TMS, TMS), lambda i, *_: (0, 0)),
        ],
        out_specs=[
            pl.BlockSpec(memory_space=pl.ANY),
            pl.BlockSpec((TMS, 128), lambda i, *_: (i, 0)),
        ],
        scratch_shapes=[
            pltpu.VMEM((2, SORT_ROWS, d), BF16),
            pltpu.VMEM((TAIL_PIECES[0], d), BF16),
            pltpu.SemaphoreType.DMA,
        ],
    )
    return pl.pallas_call(
        _dispatch_kernel,
        grid_spec=grid_spec,
        out_shape=[
            jax.ShapeDtypeStruct((rows_max, d), BF16),
            jax.ShapeDtypeStruct((t, 128), F32),
        ],
        compiler_params=_params("arbitrary"),
        name="moe_dispatch",
    )(*tables, h2, route_i, tri)


def _experts_kernel(te_ref, nv_ref, xs_ref, wg_ref, wu_ref, wd_ref, ys_ref, wg_s, wu_s, wd_s):
    j = pl.program_id(0)

    @pl.when((j == 0) | (te_ref[j] != te_ref[jnp.maximum(j - 1, 0)]))
    def _():
        wg_s[...] = wg_ref[0].astype(BF16)
        wu_s[...] = wu_ref[0].astype(BF16)
        wd_s[...] = wd_ref[0].astype(BF16)

    @pl.when(j < nv_ref[0])
    def _():
        h = xs_ref[...]
        gate = _dot(h, wg_s[...])
        hidden = (gate * jax.nn.sigmoid(gate)) * _dot(h, wu_s[...])
        ys_ref[...] = _dot(hidden.astype(BF16), wd_s[...]).astype(BF16)

    @pl.when(j >= nv_ref[0])
    def _():
        ys_ref[...] = jnp.zeros_like(ys_ref)


def _experts(xs, tile_expert, n_valid, w_gate, w_up, w_down, layer):
    rows, d = xs.shape
    row_tile = lambda j, te, nv: (jnp.minimum(j, nv[0] - 1), 0)
    weight = lambda j, te, nv: (layer, te[j], 0, 0)
    grid_spec = pltpu.PrefetchScalarGridSpec(
        num_scalar_prefetch=2,
        grid=(rows // ET,),
        in_specs=[
            pl.BlockSpec((ET, d), row_tile),
            pl.BlockSpec((None, 1, d, D_EXPERT), weight),
            pl.BlockSpec((None, 1, d, D_EXPERT), weight),
            pl.BlockSpec((None, 1, D_EXPERT, d), weight),
        ],
        out_specs=pl.BlockSpec((ET, d), lambda j, te, nv: (j, 0)),
        scratch_shapes=[pltpu.VMEM((d, D_EXPERT), BF16), pltpu.VMEM((d, D_EXPERT), BF16),
                        pltpu.VMEM((D_EXPERT, d), BF16)],
    )
    return pl.pallas_call(
        _experts_kernel,
        grid_spec=grid_spec,
        out_shape=jax.ShapeDtypeStruct((rows, d), BF16),
        compiler_params=_params("arbitrary"),
        name="moe_experts",
    )(tile_expert, n_valid, xs, w_gate, w_up, w_down)


def _combine_kernel(n16_ref, lo_ref, goff_ref, total_ref, ys_ref, dest_ref, w_ref, x_ref, mod_ref,
                    nf_ref, o_ref, buf_ref, sems, *, final):
    i = pl.program_id(0)
    nt = pl.num_programs(0)
    slot = i & 1

    def copy(dst_slot, local_row, global_row, n):
        return pltpu.make_async_copy(ys_ref.at[pl.ds(global_row, n)],
                                     buf_ref.at[dst_slot, pl.ds(local_row, n)], sems.at[dst_slot])

    @pl.when(i == 0)
    def _():
        buf_ref[...] = jnp.zeros_like(buf_ref)
        _for_each_piece(n16_ref, lo_ref, goff_ref, 0, lambda lr, gr, n: copy(0, lr, gr, n).start())

    @pl.when(i + 1 < nt)
    def _():
        _for_each_piece(n16_ref, lo_ref, goff_ref, i + 1,
                        lambda lr, gr, n: copy(1 - slot, lr, gr, n).start())

    _for_each_total_piece(total_ref, i, lambda n: copy(slot, 0, 0, n).wait())

    dest = dest_ref[:, 0:2].astype(jnp.int32)
    w = w_ref[:, 0:2]
    col = lax.broadcasted_iota(jnp.int32, (dest.shape[0], SORT_ROWS), 1)
    wmat = (jnp.where(col == dest[:, 0:1], w[:, 0:1], 0.0)
            + jnp.where(col == dest[:, 1:2], w[:, 1:2], 0.0))
    y = _dot(wmat.astype(BF16), buf_ref[slot])
    x = x_ref[...] + mod_ref[0, 5:6, :] * y
    if final:
        x = _rms_normalize(x) * nf_ref[...]
    o_ref[...] = x


def _combine(ys, dest_col, w_col, x, mod, norm_final, tables, tiles_per_batch, final):
    t, d = x.shape
    nt = t // TMS
    grid_spec = pltpu.PrefetchScalarGridSpec(
        num_scalar_prefetch=4,
        grid=(nt,),
        in_specs=[
            pl.BlockSpec(memory_space=pl.ANY),
            pl.BlockSpec((TMS, 128), lambda i, *_: (i, 0)),
            pl.BlockSpec((TMS, 128), lambda i, *_: (i, 0)),
            pl.BlockSpec((TMS, d), lambda i, *_: (i, 0)),
            pl.BlockSpec((1, 6, d), lambda i, *_: (i // tiles_per_batch, 0, 0)),
            pl.BlockSpec((1, d), lambda i, *_: (0, 0)),
        ],
        out_specs=pl.BlockSpec((TMS, d), lambda i, *_: (i, 0)),
        scratch_shapes=[pltpu.VMEM((2, SORT_ROWS, d), BF16), pltpu.SemaphoreType.DMA((2,))],
    )
    return pl.pallas_call(
        functools.partial(_combine_kernel, final=final),
        grid_spec=grid_spec,
        out_shape=jax.ShapeDtypeStruct((t, d), F32),
        compiler_params=_params("arbitrary"),
        name="moe_combine_final" if final else "moe_combine",
    )(*tables, ys, dest_col, w_col, x, mod, norm_final)


def _route_tables(counts, n_expert_tiles):
    n16 = (counts + (SEG - 1)) // SEG * SEG
    lo = jnp.cumsum(n16, axis=1) - n16
    total = jnp.sum(n16, axis=0)
    region = (total + (ET - 1)) // ET * ET
    ends = jnp.cumsum(region)
    base = ends - region
    goff = base[None, :] + jnp.cumsum(n16, axis=0) - n16
    tile_row = jnp.arange(n_expert_tiles, dtype=jnp.int32) * ET
    tile_expert = jnp.minimum(jnp.sum(tile_row[:, None] >= ends[None, :], axis=1), N_EXPERTS - 1)
    n_valid = (ends[-1] // ET).reshape(1)
    i32 = lambda a: a.astype(jnp.int32).reshape(-1)
    return (i32(n16), i32(lo), i32(goff), i32(jnp.sum(n16, axis=1))), \
        (i32(base + total), i32(region - total)), i32(tile_expert), i32(n_valid)


def _moe(h2, route_i, route_w, cnt, w_gate, w_up, w_down, layer, x, mod, norm_final, final):
    b, s, d = x.shape
    t = b * s
    nt = t // TMS
    n_expert_tiles = -(-(2 * t + nt * N_EXPERTS * (SEG - 1) + N_EXPERTS * (ET - 1)) // ET)
    counts = jnp.round(cnt[:, :, :, 0]).astype(jnp.int32).reshape(nt, N_EXPERTS)
    seg_tables, tail_tables, tile_expert, n_valid = _route_tables(counts, n_expert_tiles)
    tri = jnp.triu(jnp.ones((TMS, TMS), BF16), k=1)
    xs, dest = _dispatch(h2.reshape(t, d), route_i.reshape(nt, 8, TMS), tri,
                         seg_tables + tail_tables + (n_valid,), n_expert_tiles * ET)
    ys = _experts(xs, tile_expert, n_valid, w_gate, w_up, w_down, layer)
    out = _combine(ys, dest, route_w.reshape(t, 128), x.reshape(t, d), mod, norm_final, seg_tables,
                   s // TMS, final)
    return out.reshape(b, s, d)


def _block_diag(w):
    nb, c, _ = w.shape
    eye = jnp.eye(nb, dtype=w.dtype)
    return (eye[:, None, :, None] * w[:, :, None, :]).reshape(nb * c, nb * c)


def _pair_swa_heads(w, axis):
    half = SWA_Q_HEADS // 2
    shape = w.shape
    w = w.reshape(shape[:axis] + (2, half, HEAD_DIM) + shape[axis + 1:])
    return jnp.swapaxes(w, axis, axis + 1).reshape(shape)


def kernel(x, c, w_mod, b_mod, norm_mix, norm_ffn, w_in, w_out, out_gain, sinks, conv_w, conv_b,
           lru_wa, lru_ba, lru_wx, lru_bx, lru_lam, fox_bf, w_router_group, b_router_group,
           w_router_expert, b_router_expert, w_gate, w_up, w_down, norm_final):
    depth = w_mod.shape[0]
    b, s, d = x.shape
    assert d == D_MODEL and s % max(TM_PRE, TM_PREP, TQ_SWA, TS_LRU, 2 * T_FOX, TM_POST) == 0
    assert TM_PRE % T_FOX == 0
    c_rows = jnp.zeros((16, d), F32).at[:b].set(c)
    mod_all = _modulation(c_rows, w_mod, b_mod)[:, :b]

    for l in range(depth):
        mod = mod_all[l].reshape(b, 6, d)
        w_in_l = w_in[l]
        w_tok = jnp.concatenate([_pair_swa_heads(w_in_l[:, :SWA_WIDTH], 1), w_in_l[:, SWA_WIDTH:OFF_QC],
                                 w_in_l[:, OFF_KC:OFF_VC]], axis=1).astype(BF16)
        w_t = jnp.concatenate([w_in_l[:, OFF_QC:OFF_KC].T, w_in_l[:, OFF_VC:OFF_FC].T,
                               jnp.zeros((FORGET_ROWS, d), F32).at[:FOX_HEADS].set(w_in_l[:, OFF_FC:].T)],
                              axis=0).astype(BF16)
        fb = jnp.zeros((FORGET_ROWS, 1), F32).at[:FOX_HEADS, 0].set(fox_bf[l])
        gain = out_gain[l]
        gain_a = _pair_swa_heads(gain[:SWA_WIDTH], 0).reshape(1, SWA_WIDTH)
        gain_b = gain[SWA_WIDTH:SWA_WIDTH + LRU_WIDTH].reshape(1, LRU_WIDTH)
        gain_c = gain[SWA_WIDTH + LRU_WIDTH:].reshape(1, FOX_WIDTH)
        w_out_l = jnp.concatenate([_pair_swa_heads(w_out[l][:SWA_WIDTH], 0), w_out[l][SWA_WIDTH:]],
                                  axis=0).astype(BF16)
        wr = jnp.zeros((ROUTER_ROWS, d), F32)
        wr = wr.at[:N_GROUPS].set(w_router_group[l].T).at[8:8 + N_EXPERTS].set(w_router_expert[l].T)
        wr_hi = wr.astype(BF16)
        wr_lo = (wr - wr_hi.astype(F32)).astype(BF16)
        br = jnp.zeros((ROUTER_ROWS, 1), F32)
        br = br.at[:N_GROUPS, 0].set(b_router_group[l]).at[8:8 + N_EXPERTS, 0].set(b_router_expert[l])

        qa, ka, va, xb, gb, kc, qct, vaug, lf = _pre_mixer(
            x, mod, norm_mix[l].reshape(1, d), w_tok, w_t, fb)
        cum, cum_col = _cumsum(lf)
        kaug = _fox_prep(kc, cum_col)
        base = cum[:, :FOX_HEADS, ::2 * T_FOX].reshape(-1)
        ya = _swa(qa, ka, va, sinks[l].reshape(1, SWA_Q_HEADS))
        yb = _lru(xb, gb, conv_w[l], conv_b[l].reshape(1, -1),
                  _block_diag(lru_wa[l]).astype(BF16), lru_ba[l].reshape(1, -1),
                  _block_diag(lru_wx[l]).astype(BF16), lru_bx[l].reshape(1, -1),
                  lru_lam[l].reshape(1, -1))
        yc = _fox(qct, kaug, vaug, base)
        x, h2, route_i, route_w, cnt = _post_mixer(
            ya, yb, yc, x, mod, gain_a, gain_b, gain_c, w_out_l,
            norm_ffn[l].reshape(1, d), wr_hi, wr_lo, br)
        x = _moe(h2, route_i, route_w, cnt, w_gate, w_up, w_down, l, x, mod,
                 norm_final.reshape(1, d), final=(l == depth - 1))
    return x
```

```python
import functools

import jax
import jax.numpy as jnp
from jax import lax
from jax.experimental import pallas as pl
from jax.experimental.pallas import tpu as pltpu

F32 = jnp.float32
BF16 = jnp.bfloat16

D_MODEL = 1024
HEAD_DIM = 64
PAIR = 2 * HEAD_DIM
BLOCK = 128
SWA_Q_HEADS = 8
SWA_WIDTH = SWA_Q_HEADS * HEAD_DIM
KV_WIDTH = 2 * HEAD_DIM
LRU_WIDTH = 256
LRU_BLOCKS = 8
CONV_WIDTH = 4
LRU_C = 8.0
FOX_HEADS = 4
FOX_WIDTH = FOX_HEADS * HEAD_DIM
N_GROUPS = 4
EXPERTS_PER_GROUP = 4
N_EXPERTS = 16
D_EXPERT = 256
EPS = 1e-6
NEG = -1e30
LOG2E = 1.4426950408889634

OFF_KA = SWA_WIDTH
OFF_VA = OFF_KA + KV_WIDTH
OFF_XB = OFF_VA + KV_WIDTH
OFF_GB = OFF_XB + LRU_WIDTH
OFF_QC = OFF_GB + LRU_WIDTH
OFF_KC = OFF_QC + FOX_WIDTH
OFF_VC = OFF_KC + FOX_WIDTH
OFF_FC = OFF_VC + FOX_WIDTH

ROUTER_ROWS = 32
FORGET_ROWS = 16
V_ROWS = HEAD_DIM + 16
VMEM_LIMIT_BYTES = 48 * 1024 * 1024

TM_PRE = 1024
TM_PREP = 2048
TQ_SWA = 2048
TS_LRU = 2048
T_FOX = 512
TM_POST = 1024
CUMSUM_CHUNK = 256


def _params(*semantics):
    return pltpu.CompilerParams(dimension_semantics=semantics, vmem_limit_bytes=VMEM_LIMIT_BYTES)


def _split_bf16(v):
    hi = v.astype(BF16)
    lo = (v - hi.astype(F32)).astype(BF16)
    return hi, lo


def _dot(a, b):
    return jnp.dot(a, b, preferred_element_type=F32)


def _dot_nt(a, b):
    return lax.dot_general(a, b, (((1,), (1,)), ((), ())), preferred_element_type=F32)


def _rms_normalize(v):
    return v * lax.rsqrt(jnp.mean(v * v, axis=-1, keepdims=True) + EPS)


def _mod_kernel(c_ref, w_ref, b_ref, o_ref):
    c = c_ref[...]
    ca = c * jax.nn.sigmoid(c)
    w = w_ref[0]
    a_hi, a_lo = _split_bf16(ca)
    w_hi, w_lo = _split_bf16(w)
    o_ref[0] = _dot(a_hi, w_hi) + _dot(a_lo, w_hi) + _dot(a_hi, w_lo) + b_ref[0]


def _modulation(c, w_mod, b_mod):
    depth, d, d6 = w_mod.shape
    b = c.shape[0]
    n = d6 // d
    return pl.pallas_call(
        _mod_kernel,
        grid=(depth, n),
        in_specs=[
            pl.BlockSpec((b, d), lambda l, j: (0, 0)),
            pl.BlockSpec((1, d, d), lambda l, j: (l, 0, j)),
            pl.BlockSpec((1, 1, d), lambda l, j: (l, 0, j)),
        ],
        out_specs=pl.BlockSpec((1, b, d), lambda l, j: (l, 0, j)),
        out_shape=jax.ShapeDtypeStruct((depth, b, d6), F32),
        compiler_params=_params("arbitrary", "arbitrary"),
        name="modulation",
    )(c, w_mod, b_mod.reshape(depth, 1, d6))


def _log_sigmoid(z):
    return jnp.minimum(z, 0.0) - jnp.log1p(jnp.exp(-jnp.abs(z)))


def _other_half(a):
    return HEAD_DIM * (1 - a)


def _pre_kernel(x_ref, mod_ref, g_ref, w_ref, wt_ref, fb_ref,
                qa_ref, ka_ref, va_ref, xb_ref, gb_ref, kc_ref, qct_ref, vct_ref, lf_ref):
    x = x_ref[0]
    h = (_rms_normalize(x) * g_ref[...]) * (1.0 + mod_ref[0, 1:2, :]) + mod_ref[0, 0:1, :]
    hb = h.astype(BF16)

    def proj(lo, width):
        return _dot(hb, w_ref[:, lo:lo + width])

    scale = HEAD_DIM ** -0.5
    qa_ref[0] = (proj(0, SWA_WIDTH) * (scale * LOG2E)).astype(BF16)
    ka_ref[0] = proj(OFF_KA, KV_WIDTH).astype(BF16)
    va_ref[0] = proj(OFF_VA, KV_WIDTH).astype(BF16)
    xb_ref[0] = proj(OFF_XB, LRU_WIDTH).astype(BF16)
    gb_ref[0] = proj(OFF_GB, LRU_WIDTH).astype(BF16)
    kc_ref[0] = proj(OFF_QC, FOX_WIDTH).astype(BF16)

    t_all = _dot_nt(wt_ref[...], hb)
    tf = qct_ref.shape[3]
    ones_row = jnp.where(lax.broadcasted_iota(jnp.int32, (V_ROWS - HEAD_DIM, tf), 0) == 0, 1.0, 0.0)
    for part in range(t_all.shape[1] // tf):
        cols = slice(part * tf, (part + 1) * tf)
        qct_ref[0, part] = (t_all[0:FOX_WIDTH, cols] * (scale * LOG2E)).astype(BF16)
        for hd in range(FOX_HEADS):
            v_head = t_all[FOX_WIDTH + hd * HEAD_DIM:FOX_WIDTH + (hd + 1) * HEAD_DIM, cols]
            vct_ref[0, part, hd] = jnp.concatenate([v_head, ones_row], axis=0).astype(BF16)
    lf_ref[0] = _log_sigmoid(t_all[2 * FOX_WIDTH:] + fb_ref[...])


def _pre_mixer(x, mod, gain, w_tok, w_t, fb):
    b, s, d = x.shape
    tm = TM_PRE
    tf = T_FOX
    nf = tm // tf

    def tok(width):
        return pl.BlockSpec((1, tm, width), lambda bi, i: (bi, i, 0))

    def tok_shape(width):
        return jax.ShapeDtypeStruct((b, s, width), BF16)

    return pl.pallas_call(
        _pre_kernel,
        grid=(b, s // tm),
        in_specs=[
            tok(d),
            pl.BlockSpec((1, 6, d), lambda bi, i: (bi, 0, 0)),
            pl.BlockSpec((1, d), lambda bi, i: (0, 0)),
            pl.BlockSpec(w_tok.shape, lambda bi, i: (0, 0)),
            pl.BlockSpec(w_t.shape, lambda bi, i: (0, 0)),
            pl.BlockSpec((FORGET_ROWS, 1), lambda bi, i: (0, 0)),
        ],
        out_specs=[
            tok(SWA_WIDTH), tok(KV_WIDTH), tok(KV_WIDTH), tok(LRU_WIDTH), tok(LRU_WIDTH),
            tok(FOX_WIDTH),
            pl.BlockSpec((1, nf, FOX_WIDTH, tf), lambda bi, i: (bi, i, 0, 0)),
            pl.BlockSpec((1, nf, FOX_HEADS, V_ROWS, tf), lambda bi, i: (bi, i, 0, 0, 0)),
            pl.BlockSpec((1, FORGET_ROWS, tm), lambda bi, i: (bi, 0, i)),
        ],
        out_shape=[
            tok_shape(SWA_WIDTH), tok_shape(KV_WIDTH), tok_shape(KV_WIDTH), tok_shape(LRU_WIDTH),
            tok_shape(LRU_WIDTH), tok_shape(FOX_WIDTH),
            jax.ShapeDtypeStruct((b, s // tf, FOX_WIDTH, tf), BF16),
            jax.ShapeDtypeStruct((b, s // tf, FOX_HEADS, V_ROWS, tf), BF16),
            jax.ShapeDtypeStruct((b, FORGET_ROWS, s), F32),
        ],
        compiler_params=_params("parallel", "parallel"),
        name="pre_mixer",
    )(x, mod, gain, w_tok, w_t, fb)


def _cumsum_kernel(lf_ref, o_ref, col_ref):
    n = CUMSUM_CHUNK
    rows = lf_ref.shape[1]
    s = lf_ref.shape[2]
    r = lax.broadcasted_iota(jnp.int32, (n, n), 0)
    c = lax.broadcasted_iota(jnp.int32, (n, n), 1)
    tri = jnp.where(r <= c, 1.0, 0.0).astype(BF16)
    carry = jnp.zeros((rows, 1), F32)
    for j in range(s // n):
        seg = lf_ref[0, :, j * n:(j + 1) * n]
        hi = seg.astype(BF16)
        rem = seg - hi.astype(F32)
        mid = rem.astype(BF16)
        lo = (rem - mid.astype(F32)).astype(BF16)
        cs = (_dot(hi, tri) + _dot(mid, tri)) + _dot(lo, tri) + carry
        o_ref[0, :, j * n:(j + 1) * n] = cs
        parts = _split3(cs[0:FOX_HEADS] * LOG2E)
        col_ref[0, j * n:(j + 1) * n, :] = _rows_to_columns(
            [parts[k][hd:hd + 1] for hd in range(FOX_HEADS) for k in range(3)])
        carry = cs[:, n - 1:n]


def _cumsum(lf):
    b, r, s = lf.shape
    return pl.pallas_call(
        _cumsum_kernel,
        grid=(b,),
        in_specs=[pl.BlockSpec((1, r, s), lambda bi: (bi, 0, 0))],
        out_specs=[pl.BlockSpec((1, r, s), lambda bi: (bi, 0, 0)),
                   pl.BlockSpec((1, s, 128), lambda bi: (bi, 0, 0))],
        out_shape=[jax.ShapeDtypeStruct((b, r, s), F32), jax.ShapeDtypeStruct((b, s, 128), F32)],
        compiler_params=_params("parallel"),
        name="cumsum",
    )(lf)


def _swa_kernel(sink_ref, q_ref, kc_ref, vc_ref, kp_ref, vp_ref, o_ref, kf_ref, vf_ref, bias_ref):
    i = pl.program_id(1)
    nsub = q_ref.shape[1] // BLOCK
    npair = SWA_Q_HEADS // 2

    @pl.when((pl.program_id(0) == 0) & (i == 0))
    def _():
        qpos = BLOCK + lax.broadcasted_iota(jnp.int32, (BLOCK, 2 * BLOCK), 0)
        kpos = lax.broadcasted_iota(jnp.int32, (BLOCK, 2 * BLOCK), 1)
        dist = qpos - kpos
        valid = (dist >= 0) & (dist < BLOCK)
        distf = dist.astype(F32)
        for hd in range(SWA_Q_HEADS):
            slope = 2.0 ** (-8.0 * (hd + 1) / SWA_Q_HEADS)
            bias_ref[hd] = jnp.where(valid, (-slope * LOG2E) * distf, NEG)

    kf_ref[0:BLOCK, :] = kp_ref[0]
    kf_ref[BLOCK:, :] = kc_ref[0]
    vf_ref[0:BLOCK, :] = vp_ref[0]
    vf_ref[BLOCK:, :] = vc_ref[0]

    lane = lax.broadcasted_iota(jnp.int32, (2 * BLOCK, PAIR), 1)
    lane_o = lax.broadcasted_iota(jnp.int32, (BLOCK, PAIR), 1)
    col = lax.broadcasted_iota(jnp.int32, (BLOCK, 2 * BLOCK), 1)

    for j in range(nsub):
        k2 = kf_ref[j * BLOCK:(j + 2) * BLOCK, :]
        v2 = vf_ref[j * BLOCK:(j + 2) * BLOCK, :]
        k_half = (jnp.where(lane < HEAD_DIM, k2, jnp.zeros_like(k2)),
                  jnp.where(lane >= HEAD_DIM, k2, jnp.zeros_like(k2)))
        for p in range(npair):
            qp = q_ref[0, j * BLOCK:(j + 1) * BLOCK, p * PAIR:(p + 1) * PAIR]
            outs = []
            for a in range(2):
                hd = p + a * npair
                sc = _dot_nt(qp, k_half[a]) + bias_ref[hd]
                if j == 0:
                    sc = sc + jnp.where(col < BLOCK, jnp.where(i == 0, NEG, 0.0), 0.0)
                sink = sink_ref[0, hd] * LOG2E
                m = jnp.maximum(jnp.max(sc, axis=-1, keepdims=True), sink)
                pr = jnp.exp2(sc - m)
                den = jnp.sum(pr, axis=-1, keepdims=True) + jnp.exp2(sink - m)
                outs.append(_dot(pr.astype(BF16), v2) / den)
            o_ref[0, j * BLOCK:(j + 1) * BLOCK, p * PAIR:(p + 1) * PAIR] = jnp.where(
                lane_o < HEAD_DIM, outs[0], outs[1]).astype(BF16)


def _swa(qa, ka, va, sinks):
    b, s, _ = qa.shape
    tq = TQ_SWA
    nsub = tq // BLOCK
    prev = lambda bi, i: (bi, jnp.maximum(i * nsub - 1, 0), 0)
    return pl.pallas_call(
        _swa_kernel,
        grid=(b, s // tq),
        in_specs=[
            pl.BlockSpec(memory_space=pltpu.SMEM),
            pl.BlockSpec((1, tq, SWA_WIDTH), lambda bi, i: (bi, i, 0)),
            pl.BlockSpec((1, tq, KV_WIDTH), lambda bi, i: (bi, i, 0)),
            pl.BlockSpec((1, tq, KV_WIDTH), lambda bi, i: (bi, i, 0)),
            pl.BlockSpec((1, BLOCK, KV_WIDTH), prev),
            pl.BlockSpec((1, BLOCK, KV_WIDTH), prev),
        ],
        out_specs=pl.BlockSpec((1, tq, SWA_WIDTH), lambda bi, i: (bi, i, 0)),
        out_shape=jax.ShapeDtypeStruct((b, s, SWA_WIDTH), BF16),
        scratch_shapes=[
            pltpu.VMEM((tq + BLOCK, KV_WIDTH), BF16),
            pltpu.VMEM((tq + BLOCK, KV_WIDTH), BF16),
            pltpu.VMEM((SWA_Q_HEADS, BLOCK, 2 * BLOCK), F32),
        ],
        compiler_params=_params("arbitrary", "arbitrary"),
        name="swa",
    )(sinks, qa, ka, va, ka, va)


def _gelu_tanh(v):
    return 0.5 * v * (1.0 + jnp.tanh(0.7978845608028654 * (v + 0.044715 * (v * v * v))))


def _lru_kernel(xb_ref, gb_ref, cw_ref, cb_ref, wa_ref, ba_ref, wx_ref, bx_ref, lam_ref,
                o_ref, win_ref, h_ref):
    ts = xb_ref.shape[1]

    @pl.when(pl.program_id(1) == 0)
    def _():
        win_ref[0:8, :] = jnp.zeros((8, LRU_WIDTH), F32)
        h_ref[...] = jnp.zeros_like(h_ref)

    x = xb_ref[0].astype(F32)
    win_ref[8:, :] = x
    conv = cb_ref[...] + cw_ref[CONV_WIDTH - 1:CONV_WIDTH, :] * x
    for k in range(1, CONV_WIDTH):
        conv = conv + cw_ref[CONV_WIDTH - 1 - k:CONV_WIDTH - k, :] * win_ref[8 - k:8 - k + ts, :]
    win_ref[0:8, :] = x[ts - 8:ts]

    cbf = conv.astype(BF16)
    r = jax.nn.sigmoid(_dot(cbf, wa_ref[...]) + ba_ref[...])
    gi = jax.nn.sigmoid(_dot(cbf, wx_ref[...]) + bx_ref[...])
    z = -lam_ref[...]
    softplus = jnp.maximum(z, 0.0) + jnp.log1p(jnp.exp(-jnp.abs(z)))
    log_a = (-LRU_C * softplus) * r
    a = jnp.exp(log_a)
    v = 1.0 - a * a
    u = jnp.where(v == 0.0, 0.0, v * lax.rsqrt(v)) * (gi * conv)

    a = a.reshape(ts // 8, 8, LRU_WIDTH)
    u = u.reshape(ts // 8, 8, LRU_WIDTH)
    sub = lax.broadcasted_iota(jnp.int32, a.shape, 1)
    for step in (1, 2, 4):
        keep = sub >= step
        u = jnp.where(keep, a * pltpu.roll(u, step, 1) + u, u)
        a = jnp.where(keep, a * pltpu.roll(a, step, 1), a)
    carry = h_ref[7:8, :]
    groups = []
    for g in range(ts // 8):
        hg = u[g] + a[g] * carry
        groups.append(hg)
        carry = hg[7:8, :]
    h = jnp.concatenate(groups, axis=0)
    h_ref[...] = groups[-1]
    o_ref[0] = (h * _gelu_tanh(gb_ref[0].astype(F32))).astype(BF16)


def _lru(xb, gb, conv_w, conv_b, wa, ba, wx, bx, lam):
    b, s, w = xb.shape
    ts = TS_LRU
    tok = pl.BlockSpec((1, ts, w), lambda bi, i: (bi, i, 0))
    full = lambda shape: pl.BlockSpec(shape, lambda bi, i: (0,) * len(shape))
    return pl.pallas_call(
        _lru_kernel,
        grid=(b, s // ts),
        in_specs=[tok, tok, full((CONV_WIDTH, w)), full((1, w)), full((w, w)), full((1, w)),
                  full((w, w)), full((1, w)), full((1, w))],
        out_specs=tok,
        out_shape=jax.ShapeDtypeStruct((b, s, w), BF16),
        scratch_shapes=[pltpu.VMEM((8 + ts, w), F32), pltpu.VMEM((8, w), F32)],
        compiler_params=_params("arbitrary", "arbitrary"),
        name="lru",
    )(xb, gb, conv_w, conv_b, wa, ba, wx, bx, lam)


BIAS_ROWS = 6


def _split3(v):
    hi = v.astype(BF16).astype(F32)
    rem = v - hi
    mid = rem.astype(BF16).astype(F32)
    lo = (rem - mid).astype(BF16).astype(F32)
    return hi, mid, lo


def _fox_prep_kernel(k_ref, c_ref, o_ref):
    tm = k_ref.shape[1]
    lane = lax.broadcasted_iota(jnp.int32, (tm, PAIR), 1)
    parts = c_ref[0].astype(BF16)
    src = lax.broadcasted_iota(jnp.int32, (PAIR, PAIR), 0)
    dst = lax.broadcasted_iota(jnp.int32, (PAIR, PAIR), 1)
    for p in range(FOX_WIDTH // PAIR):
        kp = k_ref[0, :, p * PAIR:(p + 1) * PAIR].astype(F32)
        for a in range(2):
            hd = 2 * p + a
            ob = _other_half(a)
            part = src - 3 * hd
            place = jnp.where(part == dst - ob, jnp.where(part >= 0, jnp.where(part < 3, 1.0, 0.0), 0.0),
                              0.0).astype(BF16)
            bias = _dot(parts, place)
            rel = lane - ob
            aug = jnp.where(rel < 3, bias, jnp.where(rel < BIAS_ROWS, 1.0, 0.0))
            in_half = (lane >= HEAD_DIM * a) & (lane < HEAD_DIM * (a + 1))
            o_ref[0, hd] = jnp.where(in_half, kp, aug).astype(BF16)


def _fox_prep(kc, cum_col):
    b, s, w = kc.shape
    tm = TM_PREP
    return pl.pallas_call(
        _fox_prep_kernel,
        grid=(b, s // tm),
        in_specs=[
            pl.BlockSpec((1, tm, w), lambda bi, i: (bi, i, 0)),
            pl.BlockSpec((1, tm, cum_col.shape[2]), lambda bi, i: (bi, i, 0)),
        ],
        out_specs=pl.BlockSpec((1, FOX_HEADS, tm, PAIR), lambda bi, i: (bi, 0, i, 0)),
        out_shape=jax.ShapeDtypeStruct((b, FOX_HEADS, s, PAIR), BF16),
        compiler_params=_params("parallel", "parallel"),
        name="fox_prep",
    )(kc, cum_col)


def _fox_kernel(base_ref, qt_ref, ka_ref, vt_ref, o_ref, acc_ref, s0_ref, s1_ref):
    bi = pl.program_id(0)
    p = pl.program_id(1)
    qi = pl.program_id(2)
    nq = pl.num_programs(2)
    tk = qt_ref.shape[3]
    tq = 2 * tk
    qt = jnp.concatenate([qt_ref[0, 0], qt_ref[0, 1]], axis=1).astype(F32)
    row = lax.broadcasted_iota(jnp.int32, (PAIR, tq), 0)
    key_off = (lax.broadcasted_iota(jnp.int32, (tk, tq), 1)
               - lax.broadcasted_iota(jnp.int32, (tk, tq), 0))
    q_aug = []
    for a in range(2):
        hd = 2 * p + a
        base = jnp.full((1, tq), base_ref[(bi * FOX_HEADS + hd) * nq + qi], F32) * LOG2E
        b_hi, b_mid, b_lo = _split3(base)
        rel = row - _other_half(a)
        aug = jnp.where(rel < 3, -1.0, jnp.where(rel == 3, b_hi, jnp.where(
            rel == 4, b_mid, jnp.where(rel == 5, b_lo, 0.0))))
        in_half = (row >= HEAD_DIM * a) & (row < HEAD_DIM * (a + 1))
        q_aug.append(jnp.where(in_half, qt, aug).astype(BF16))
        acc_ref[a] = jnp.zeros((V_ROWS, tq), F32)

    slots = (s0_ref, s1_ref)

    def scores(ki, slot, diag=None):
        tile_max = []
        for a in range(2):
            k = ka_ref[0, a, pl.ds(pl.multiple_of(ki * tk, tk), tk), :]
            st = _dot(k, q_aug[a])
            if diag is not None:
                st = jnp.where(key_off >= diag * tk, st, NEG)
            slots[slot][a] = st
            tile_max.append(jnp.max(st, axis=0, keepdims=True))
        return tuple(tile_max)

    def consume(ki, slot, m_old, tile_max):
        m_new = []
        for a in range(2):
            m_a = jnp.maximum(m_old[a], tile_max[a])
            alpha = jnp.exp2(m_old[a] - m_a)
            pr = jnp.exp2(slots[slot][a] - m_a).astype(BF16)
            acc_ref[a] = alpha * acc_ref[a] + _dot(vt_ref[0, ki, a], pr)
            m_new.append(m_a)
        return tuple(m_new)

    def advance(j, slot, carry, diag=None):
        m_old, tile_max, prev = carry
        next_max = scores(j, 1 - slot, diag)
        return consume(prev, slot, m_old, tile_max), next_max, j

    def body(i, carry):
        return advance(2 * i + 1, 0, advance(2 * i, 1, carry))

    m0 = jnp.full((1, tq), NEG, F32)
    carry = ((m0, m0), scores(2 * qi, 0, diag=0), 2 * qi)
    carry = advance(2 * qi + 1, 0, carry, diag=1)
    m, tile_max, prev = lax.fori_loop(0, qi, body, carry)
    consume(prev, 1, m, tile_max)
    outs = []
    for a in range(2):
        acc = acc_ref[a]
        outs.append(acc[0:HEAD_DIM] / acc[HEAD_DIM:HEAD_DIM + 1, :])
    o_ref[0] = jnp.concatenate(outs, axis=0).T.astype(BF16)


def _fox(qct, kaug, vaug, base):
    b, nt, w, tk = qct.shape
    s = nt * tk
    tq = 2 * tk
    npair = w // PAIR
    return pl.pallas_call(
        _fox_kernel,
        grid=(b, npair, s // tq),
        in_specs=[
            pl.BlockSpec(memory_space=pltpu.SMEM),
            pl.BlockSpec((1, 2, PAIR, tk), lambda bi, p, qi: (bi, qi, p, 0)),
            pl.BlockSpec((1, 2, s, PAIR), lambda bi, p, qi: (bi, p, 0, 0)),
            pl.BlockSpec((1, nt, 2, V_ROWS, tk), lambda bi, p, qi: (bi, 0, p, 0, 0)),
        ],
        out_specs=pl.BlockSpec((1, tq, PAIR), lambda bi, p, qi: (bi, qi, p)),
        out_shape=jax.ShapeDtypeStruct((b, s, w), BF16),
        scratch_shapes=[pltpu.VMEM((2, V_ROWS, tq), F32), pltpu.VMEM((2, tk, tq), F32),
                        pltpu.VMEM((2, tk, tq), F32)],
        compiler_params=_params("parallel", "parallel", "arbitrary"),
        name="fox",
    )(base, qct, kaug, vaug)


def _rows_to_columns(rows):
    n = rows[0].shape[1]
    sub = lax.broadcasted_iota(jnp.int32, (128, n), 0)
    block = jnp.zeros((128, n), F32)
    for k, row in enumerate(rows):
        block = jnp.where(sub == k, row, block)
    return block.T


def _first_max(vals):
    best = vals[0]
    for v in vals[1:]:
        best = jnp.maximum(best, v)
    idx = jnp.full(best.shape, len(vals) - 1, jnp.int32)
    for j in range(len(vals) - 2, -1, -1):
        idx = jnp.where(vals[j] == best, j, idx)
    return best, idx


def _softmax_rows(vals):
    top = vals[0]
    for v in vals[1:]:
        top = jnp.maximum(top, v)
    ex = [jnp.exp(v - top) for v in vals]
    tot = ex[0]
    for e in ex[1:]:
        tot = tot + e
    return [e / tot for e in ex]


def _post_kernel(ya_ref, yb_ref, yc_ref, x_ref, mod_ref, ga_ref, gb_ref, gc_ref, wo_ref, nf_ref,
                 wr_hi_ref, wr_lo_ref, br_ref, xo_ref, h2_ref, ri_ref, rw_ref, cnt_ref):
    ya = _rms_normalize(ya_ref[0].astype(F32)) * ga_ref[...]
    yb = _rms_normalize(yb_ref[0].astype(F32)) * gb_ref[...]
    yc = _rms_normalize(yc_ref[0].astype(F32)) * gc_ref[...]
    y = jnp.concatenate([ya, yb, yc], axis=-1).astype(BF16)
    x = x_ref[0] + mod_ref[0, 2:3, :] * _dot(y, wo_ref[...])
    xo_ref[0] = x
    h = (_rms_normalize(x) * nf_ref[...]) * (1.0 + mod_ref[0, 4:5, :]) + mod_ref[0, 3:4, :]
    h_hi, h_lo = _split_bf16(h)
    h2_ref[0] = h_hi

    w_hi = wr_hi_ref[...]
    logits = (_dot_nt(w_hi, h_hi) + _dot_nt(w_hi, h_lo)) + _dot_nt(wr_lo_ref[...], h_hi) + br_ref[...]
    rows = [logits[r:r + 1, :] for r in range(8 + N_EXPERTS)]
    group_prob = _softmax_rows(rows[0:N_GROUPS])
    group_p, group_idx = _first_max(group_prob)
    in_group = []
    for j in range(EXPERTS_PER_GROUP):
        v = rows[8 + (N_GROUPS - 1) * EXPERTS_PER_GROUP + j]
        for g in range(N_GROUPS - 2, -1, -1):
            v = jnp.where(group_idx == g, rows[8 + g * EXPERTS_PER_GROUP + j], v)
        in_group.append(v)
    expert_prob = _softmax_rows(in_group)
    p1, i1 = _first_max(expert_prob)
    rest = [jnp.where(i1 == j, -1.0, expert_prob[j]) for j in range(EXPERTS_PER_GROUP)]
    p2, i2 = _first_max(rest)
    e1 = group_idx * EXPERTS_PER_GROUP + i1
    e2 = group_idx * EXPERTS_PER_GROUP + i2
    rw_ref[0] = _rows_to_columns([group_p * p1 / (p1 + p2), group_p * p2 / (p1 + p2)])
    ts = ri_ref.shape[3]
    erow = lax.broadcasted_iota(jnp.int32, (N_EXPERTS, ts), 0)
    for part in range(e1.shape[1] // ts):
        e1p = e1[:, part * ts:(part + 1) * ts]
        e2p = e2[:, part * ts:(part + 1) * ts]
        ri_ref[0, part] = jnp.zeros((8, ts), jnp.int32)
        ri_ref[0, part, 0:1, :] = e1p
        ri_ref[0, part, 1:2, :] = e2p
        sel = jnp.where((erow == e1p) | (erow == e2p), 1.0, 0.0).astype(BF16)
        cnt_ref[0, part] = _dot(sel, jnp.ones((ts, 128), BF16))


def _post_mixer(ya, yb, yc, x, mod, gain_a, gain_b, gain_c, w_out, norm_ffn, wr_hi, wr_lo, br):
    b, s, d = x.shape
    tm = TM_POST
    ts = TMS
    nsort = tm // ts
    tok = lambda width: pl.BlockSpec((1, tm, width), lambda bi, i: (bi, i, 0))
    full = lambda shape: pl.BlockSpec(shape, lambda bi, i: (0,) * len(shape))
    return pl.pallas_call(
        _post_kernel,
        grid=(b, s // tm),
        in_specs=[
            tok(SWA_WIDTH), tok(LRU_WIDTH), tok(FOX_WIDTH), tok(d),
            pl.BlockSpec((1, 6, d), lambda bi, i: (bi, 0, 0)),
            full((1, SWA_WIDTH)), full((1, LRU_WIDTH)), full((1, FOX_WIDTH)),
            full((d, d)), full((1, d)),
            full((ROUTER_ROWS, d)), full((ROUTER_ROWS, d)), full((ROUTER_ROWS, 1)),
        ],
        out_specs=[
            tok(d), tok(d),
            pl.BlockSpec((1, nsort, 8, ts), lambda bi, i: (bi, i, 0, 0)),
            tok(128),
            pl.BlockSpec((1, nsort, N_EXPERTS, 128), lambda bi, i: (bi, i, 0, 0)),
        ],
        out_shape=[
            jax.ShapeDtypeStruct((b, s, d), F32),
            jax.ShapeDtypeStruct((b, s, d), BF16),
            jax.ShapeDtypeStruct((b, s // ts, 8, ts), jnp.int32),
            jax.ShapeDtypeStruct((b, s, 128), F32),
            jax.ShapeDtypeStruct((b, s // ts, N_EXPERTS, 128), F32),
        ],
        compiler_params=_params("parallel", "parallel"),
        name="post_mixer",
    )(ya, yb, yc, x, mod, gain_a, gain_b, gain_c, w_out, norm_ffn, wr_hi, wr_lo, br)


TMS = 512
SEG = 16
SORT_ROWS = 1280
ET = 1024
SEG_PIECES = (512, 256, 128, 64, 32, 16)
TAIL_PIECES = (512, 256, 128, 64, 32, 16)
TOTAL_PIECES = (1024, 512, 256, 128, 64, 32, 16)
RARE_ROWS = 128
assert SORT_ROWS >= 2 * TMS + N_EXPERTS * (SEG - 1) and SEG_PIECES[0] == TMS and TAIL_PIECES[0] * 2 == ET


def _for_each_piece(n16_ref, lo_ref, goff_ref, tile, fn):
    for e in range(N_EXPERTS):
        n = n16_ref[tile * N_EXPERTS + e]
        lo = lo_ref[tile * N_EXPERTS + e]
        go = goff_ref[tile * N_EXPERTS + e]

        def pieces(sizes, n=n, lo=lo, go=go):
            for rows in sizes:
                done = n & (-2 * rows)

                @pl.when((n & rows) != 0)
                def _(done=done, rows=rows):
                    fn(pl.multiple_of(lo + done, SEG), pl.multiple_of(go + done, SEG), rows)

        @pl.when(n >= RARE_ROWS)
        def _():
            pieces([r for r in SEG_PIECES if r >= RARE_ROWS])

        pieces([r for r in SEG_PIECES if r < RARE_ROWS])


def _alternating_start():
    count = [0]

    def start(copy):
        copy.start(priority=count[0] % 2)
        count[0] += 1

    return start


def _for_each_total_piece(total_ref, tile, fn):
    n = total_ref[tile]
    for rows in TOTAL_PIECES:
        @pl.when((n & rows) != 0)
        def _(rows=rows):
            fn(rows)


def _dispatch_kernel(n16_ref, lo_ref, goff_ref, total_ref, tail_row_ref, tail_n_ref, n_valid_ref,
                     h_ref, ri_ref, tri_ref, xs_ref, dest_ref, sorted_ref, zero_ref, sem):
    i = pl.program_id(0)
    last = pl.num_programs(0) - 1
    slot = i & 1
    tms = h_ref.shape[0]

    ri = ri_ref[0]
    e1 = ri[0:1, :]
    e2 = ri[1:2, :]
    erow = lax.broadcasted_iota(jnp.int32, (N_EXPERTS, tms), 0)
    m1 = erow == e1
    m2 = erow == e2
    sel = jnp.where(m1 | m2, 1.0, 0.0).astype(BF16)
    rank = _dot(sel, tri_ref[...]).astype(jnp.int32)
    start = jnp.zeros((N_EXPERTS, tms), jnp.int32)
    for e in range(N_EXPERTS):
        start = jnp.where(erow == e, lo_ref[i * N_EXPERTS + e], start)
    dest = rank + start
    r1 = jnp.sum(jnp.where(m1, dest, 0), axis=0, keepdims=True)
    r2 = jnp.sum(jnp.where(m2, dest, 0), axis=0, keepdims=True)
    dest_ref[...] = _rows_to_columns([r1.astype(F32), r2.astype(F32)])

    rows = lax.broadcasted_iota(jnp.int32, (SORT_ROWS, tms), 0)
    perm = jnp.where((rows == r1) | (rows == r2), 1.0, 0.0).astype(BF16)
    sorted_ref[slot] = _dot(perm, h_ref[...]).astype(BF16)

    def copy(src_slot, local_row, global_row, n):
        return pltpu.make_async_copy(sorted_ref.at[src_slot, pl.ds(local_row, n)],
                                     xs_ref.at[pl.ds(global_row, n)], sem)

    @pl.when(i > 0)
    def _():
        _for_each_total_piece(total_ref, i - 1, lambda n: copy(1 - slot, 0, 0, n).wait())

    start = _alternating_start()
    _for_each_piece(n16_ref, lo_ref, goff_ref, i, lambda lr, gr, n: start(copy(slot, lr, gr, n)))

    @pl.when(i == last)
    def _():
        _for_each_total_piece(total_ref, i, lambda n: copy(slot, 0, 0, n).wait())
        zero_ref[...] = jnp.zeros_like(zero_ref)

        def tail_copies(fn):
            for e in range(N_EXPERTS):
                n = tail_n_ref[e]
                row = tail_row_ref[e]
                for rows_ in TAIL_PIECES:
                    done = n & (-2 * rows_)

                    @pl.when((n & rows_) != 0)
                    def _(row=row, done=done, rows_=rows_):
                        fn(pltpu.make_async_copy(
                            zero_ref.at[pl.ds(0, rows_)],
                            xs_ref.at[pl.ds(pl.multiple_of(row + done, SEG), rows_)], sem))

        tail_copies(lambda c: c.start())
        tail_copies(lambda c: c.wait())

        def unused_tiles(fn):
            def body(j, carry):
                for part in range(ET // TAIL_PIECES[0]):
                    row = pl.multiple_of(j * ET + part * TAIL_PIECES[0], SEG)
                    fn(pltpu.make_async_copy(zero_ref, xs_ref.at[pl.ds(row, TAIL_PIECES[0])], sem))
                return carry
            lax.fori_loop(n_valid_ref[0], xs_ref.shape[0] // ET, body, 0)

        unused_tiles(lambda c: c.start())
        unused_tiles(lambda c: c.wait())


def _dispatch(h2, route_i, tri, tables, rows_max):
    t, d = h2.shape
    nt = t // TMS
    grid_spec = pltpu.PrefetchScalarGridSpec(
        num_scalar_prefetch=7,
        grid=(nt,),
        in_specs=[
            pl.BlockSpec((TMS, d), lambda i, *_: (i, 0)),
            pl.BlockSpec((1, 8, TMS), lambda i, *_: (i, 0, 0)),
            pl.BlockSpec((TMS, TMS), lambda i, *_: (0, 0)),
        ],
        out_specs=[
            pl.BlockSpec(memory_space=pl.ANY),
            pl.BlockSpec((TMS, 128), lambda i, *_: (i, 0)),
        ],
        scratch_shapes=[
            pltpu.VMEM((2, SORT_ROWS, d), BF16),
            pltpu.VMEM((TAIL_PIECES[0], d), BF16),
            pltpu.SemaphoreType.DMA,
        ],
    )
    return pl.pallas_call(
        _dispatch_kernel,
        grid_spec=grid_spec,
        out_shape=[
            jax.ShapeDtypeStruct((rows_max, d), BF16),
            jax.ShapeDtypeStruct((t, 128), F32),
        ],
        compiler_params=_params("arbitrary"),
        name="moe_dispatch",
    )(*tables, h2, route_i, tri)


def _experts_kernel(te_ref, nv_ref, xs_ref, wg_ref, wu_ref, wd_ref, ys_ref, wg_s, wu_s, wd_s):
    j = pl.program_id(0)

    @pl.when((j == 0) | (te_ref[j] != te_ref[jnp.maximum(j - 1, 0)]))
    def _():
        wg_s[...] = wg_ref[0].astype(BF16)
        wu_s[...] = wu_ref[0].astype(BF16)
        wd_s[...] = wd_ref[0].astype(BF16)

    @pl.when(j < nv_ref[0])
    def _():
        h = xs_ref[...]
        gate = _dot(h, wg_s[...])
        hidden = (gate * jax.nn.sigmoid(gate)) * _dot(h, wu_s[...])
        ys_ref[...] = _dot(hidden.astype(BF16), wd_s[...]).astype(BF16)

    @pl.when(j >= nv_ref[0])
    def _():
        ys_ref[...] = jnp.zeros_like(ys_ref)


def _experts(xs, tile_expert, n_valid, w_gate, w_up, w_down, layer):
    rows, d = xs.shape
    row_tile = lambda j, te, nv: (jnp.minimum(j, nv[0] - 1), 0)
    weight = lambda j, te, nv: (layer, te[j], 0, 0)
    grid_spec = pltpu.PrefetchScalarGridSpec(
        num_scalar_prefetch=2,
        grid=(rows // ET,),
        in_specs=[
            pl.BlockSpec((ET, d), row_tile),
            pl.BlockSpec((None, 1, d, D_EXPERT), weight),
            pl.BlockSpec((None, 1, d, D_EXPERT), weight),
            pl.BlockSpec((None, 1, D_EXPERT, d), weight),
        ],
        out_specs=pl.BlockSpec((ET, d), lambda j, te, nv: (j, 0)),
        scratch_shapes=[pltpu.VMEM((d, D_EXPERT), BF16), pltpu.VMEM((d, D_EXPERT), BF16),
                        pltpu.VMEM((D_EXPERT, d), BF16)],
    )
    return pl.pallas_call(
        _experts_kernel,
        grid_spec=grid_spec,
        out_shape=jax.ShapeDtypeStruct((rows, d), BF16),
        compiler_params=_params("arbitrary"),
        name="moe_experts",
    )(tile_expert, n_valid, xs, w_gate, w_up, w_down)


def _combine_kernel(n16_ref, lo_ref, goff_ref, total_ref, ys_ref, dest_ref, w_ref, x_ref, mod_ref,
                    nf_ref, o_ref, buf_ref, sems, *, final):
    i = pl.program_id(0)
    nt = pl.num_programs(0)
    slot = i & 1

    def copy(dst_slot, local_row, global_row, n):
        return pltpu.make_async_copy(ys_ref.at[pl.ds(global_row, n)],
                                     buf_ref.at[dst_slot, pl.ds(local_row, n)], sems.at[dst_slot])

    @pl.when(i == 0)
    def _():
        buf_ref[...] = jnp.zeros_like(buf_ref)
        first = _alternating_start()
        _for_each_piece(n16_ref, lo_ref, goff_ref, 0, lambda lr, gr, n: first(copy(0, lr, gr, n)))

    @pl.when(i + 1 < nt)
    def _():
        ahead = _alternating_start()
        _for_each_piece(n16_ref, lo_ref, goff_ref, i + 1,
                        lambda lr, gr, n: ahead(copy(1 - slot, lr, gr, n)))

    _for_each_total_piece(total_ref, i, lambda n: copy(slot, 0, 0, n).wait())

    dest = dest_ref[:, 0:2].astype(jnp.int32)
    w = w_ref[:, 0:2]
    col = lax.broadcasted_iota(jnp.int32, (dest.shape[0], SORT_ROWS), 1)
    wmat = (jnp.where(col == dest[:, 0:1], w[:, 0:1], 0.0)
            + jnp.where(col == dest[:, 1:2], w[:, 1:2], 0.0))
    y = _dot(wmat.astype(BF16), buf_ref[slot])
    x = x_ref[...] + mod_ref[0, 5:6, :] * y
    if final:
        x = _rms_normalize(x) * nf_ref[...]
    o_ref[...] = x


def _combine(ys, dest_col, w_col, x, mod, norm_final, tables, tiles_per_batch, final):
    t, d = x.shape
    nt = t // TMS
    grid_spec = pltpu.PrefetchScalarGridSpec(
        num_scalar_prefetch=4,
        grid=(nt,),
        in_specs=[
            pl.BlockSpec(memory_space=pl.ANY),
            pl.BlockSpec((TMS, 128), lambda i, *_: (i, 0)),
            pl.BlockSpec((TMS, 128), lambda i, *_: (i, 0)),
            pl.BlockSpec((TMS, d), lambda i, *_: (i, 0)),
            pl.BlockSpec((1, 6, d), lambda i, *_: (i // tiles_per_batch, 0, 0)),
            pl.BlockSpec((1, d), lambda i, *_: (0, 0)),
        ],
        out_specs=pl.BlockSpec((TMS, d), lambda i, *_: (i, 0)),
        scratch_shapes=[pltpu.VMEM((2, SORT_ROWS, d), BF16), pltpu.SemaphoreType.DMA((2,))],
    )
    return pl.pallas_call(
        functools.partial(_combine_kernel, final=final),
        grid_spec=grid_spec,
        out_shape=jax.ShapeDtypeStruct((t, d), F32),
        compiler_params=_params("arbitrary"),
        name="moe_combine_final" if final else "moe_combine",
    )(*tables, ys, dest_col, w_col, x, mod, norm_final)


def _route_tables(counts, n_expert_tiles):
    n16 = (counts + (SEG - 1)) // SEG * SEG
    lo = jnp.cumsum(n16, axis=1) - n16
    total = jnp.sum(n16, axis=0)
    region = (total + (ET - 1)) // ET * ET
    ends = jnp.cumsum(region)
    base = ends - region
    goff = base[None, :] + jnp.cumsum(n16, axis=0) - n16
    tile_row = jnp.arange(n_expert_tiles, dtype=jnp.int32) * ET
    tile_expert = jnp.minimum(jnp.sum(tile_row[:, None] >= ends[None, :], axis=1), N_EXPERTS - 1)
    n_valid = (ends[-1] // ET).reshape(1)
    i32 = lambda a: a.astype(jnp.int32).reshape(-1)
    return (i32(n16), i32(lo), i32(goff), i32(jnp.sum(n16, axis=1))), \
        (i32(base + total), i32(region - total)), i32(tile_expert), i32(n_valid)


def _moe(h2, route_i, route_w, cnt, w_gate, w_up, w_down, layer, x, mod, norm_final, final):
    b, s, d = x.shape
    t = b * s
    nt = t // TMS
    n_expert_tiles = -(-(2 * t + nt * N_EXPERTS * (SEG - 1) + N_EXPERTS * (ET - 1)) // ET)
    counts = jnp.round(cnt[:, :, :, 0]).astype(jnp.int32).reshape(nt, N_EXPERTS)
    seg_tables, tail_tables, tile_expert, n_valid = _route_tables(counts, n_expert_tiles)
    tri = jnp.triu(jnp.ones((TMS, TMS), BF16), k=1)
    xs, dest = _dispatch(h2.reshape(t, d), route_i.reshape(nt, 8, TMS), tri,
                         seg_tables + tail_tables + (n_valid,), n_expert_tiles * ET)
    ys = _experts(xs, tile_expert, n_valid, w_gate, w_up, w_down, layer)
    out = _combine(ys, dest, route_w.reshape(t, 128), x.reshape(t, d), mod, norm_final, seg_tables,
                   s // TMS, final)
    return out.reshape(b, s, d)


def _block_diag(w):
    nb, c, _ = w.shape
    eye = jnp.eye(nb, dtype=w.dtype)
    return (eye[:, None, :, None] * w[:, :, None, :]).reshape(nb * c, nb * c)


def _pair_swa_heads(w, axis):
    half = SWA_Q_HEADS // 2
    shape = w.shape
    w = w.reshape(shape[:axis] + (2, half, HEAD_DIM) + shape[axis + 1:])
    return jnp.swapaxes(w, axis, axis + 1).reshape(shape)


def kernel(x, c, w_mod, b_mod, norm_mix, norm_ffn, w_in, w_out, out_gain, sinks, conv_w, conv_b,
           lru_wa, lru_ba, lru_wx, lru_bx, lru_lam, fox_bf, w_router_group, b_router_group,
           w_router_expert, b_router_expert, w_gate, w_up, w_down, norm_final):
    depth = w_mod.shape[0]
    b, s, d = x.shape
    assert d == D_MODEL and s % max(TM_PRE, TM_PREP, TQ_SWA, TS_LRU, 2 * T_FOX, TM_POST) == 0
    assert TM_PRE % T_FOX == 0
    c_rows = jnp.zeros((16, d), F32).at[:b].set(c)
    mod_all = _modulation(c_rows, w_mod, b_mod)[:, :b]

    for l in range(depth):
        mod = mod_all[l].reshape(b, 6, d)
        w_in_l = w_in[l]
        w_tok = jnp.concatenate([_pair_swa_heads(w_in_l[:, :SWA_WIDTH], 1), w_in_l[:, SWA_WIDTH:OFF_QC],
                                 w_in_l[:, OFF_KC:OFF_VC]], axis=1).astype(BF16)
        w_t = jnp.concatenate([w_in_l[:, OFF_QC:OFF_KC].T, w_in_l[:, OFF_VC:OFF_FC].T,
                               jnp.zeros((FORGET_ROWS, d), F32).at[:FOX_HEADS].set(w_in_l[:, OFF_FC:].T)],
                              axis=0).astype(BF16)
        fb = jnp.zeros((FORGET_ROWS, 1), F32).at[:FOX_HEADS, 0].set(fox_bf[l])
        gain = out_gain[l]
        gain_a = _pair_swa_heads(gain[:SWA_WIDTH], 0).reshape(1, SWA_WIDTH)
        gain_b = gain[SWA_WIDTH:SWA_WIDTH + LRU_WIDTH].reshape(1, LRU_WIDTH)
        gain_c = gain[SWA_WIDTH + LRU_WIDTH:].reshape(1, FOX_WIDTH)
        w_out_l = jnp.concatenate([_pair_swa_heads(w_out[l][:SWA_WIDTH], 0), w_out[l][SWA_WIDTH:]],
                                  axis=0).astype(BF16)
        wr = jnp.zeros((ROUTER_ROWS, d), F32)
        wr = wr.at[:N_GROUPS].set(w_router_group[l].T).at[8:8 + N_EXPERTS].set(w_router_expert[l].T)
        wr_hi = wr.astype(BF16)
        wr_lo = (wr - wr_hi.astype(F32)).astype(BF16)
        br = jnp.zeros((ROUTER_ROWS, 1), F32)
        br = br.at[:N_GROUPS, 0].set(b_router_group[l]).at[8:8 + N_EXPERTS, 0].set(b_router_expert[l])

        qa, ka, va, xb, gb, kc, qct, vaug, lf = _pre_mixer(
            x, mod, norm_mix[l].reshape(1, d), w_tok, w_t, fb)
        cum, cum_col = _cumsum(lf)
        kaug = _fox_prep(kc, cum_col)
        base = cum[:, :FOX_HEADS, ::2 * T_FOX].reshape(-1)
        ya = _swa(qa, ka, va, sinks[l].reshape(1, SWA_Q_HEADS))
        yb = _lru(xb, gb, conv_w[l], conv_b[l].reshape(1, -1),
                  _block_diag(lru_wa[l]).astype(BF16), lru_ba[l].reshape(1, -1),
                  _block_diag(lru_wx[l]).astype(BF16), lru_bx[l].reshape(1, -1),
                  lru_lam[l].reshape(1, -1))
        yc = _fox(qct, kaug, vaug, base)
        x, h2, route_i, route_w, cnt = _post_mixer(
            ya, yb, yc, x, mod, gain_a, gain_b, gain_c, w_out_l,
            norm_ffn[l].reshape(1, d), wr_hi, wr_lo, br)
        x = _moe(h2, route_i, route_w, cnt, w_gate, w_up, w_down, l, x, mod,
                 norm_final.reshape(1, d), final=(l == depth - 1))
    return x
```
